```python
import math
import jax, jax.numpy as jnp
from jax import lax
import numpy as np

D_MODEL = 2048
BATCH = 1
SEQ = 8192
DEPTH = 2

GRID_W = 64
CTX_LEN = 256
HEAD_DIM = 128
HG_HEADS = D_MODEL // (4 * HEAD_DIM)
HG_WIDTH = HG_HEADS * HEAD_DIM
HG_CHUNK = 64
GQA_HEADS = D_MODEL // (2 * HEAD_DIM)
GQA_KV_HEADS = GQA_HEADS // 4
GQA_WIDTH = GQA_HEADS * HEAD_DIM
GQA_KV_WIDTH = GQA_KV_HEADS * HEAD_DIM
DIFF_HEADS = D_MODEL // (4 * HEAD_DIM)
DIFF_QK_DIM = HEAD_DIM // 2
DIFF_WIDTH = DIFF_HEADS * HEAD_DIM
MIX_WIDTH = HG_WIDTH + GQA_WIDTH + DIFF_WIDTH
IN_SPLITS = (HG_WIDTH,) * 6 + (GQA_WIDTH, GQA_KV_WIDTH, GQA_KV_WIDTH) + (DIFF_WIDTH,) * 3
IN_WIDTH = sum(IN_SPLITS)
Q_BLOCK = 128
ROPE_THETA = 10000.0
N_GROUPS = 4
EXPERTS_PER_GROUP = 8
N_EXPERTS = N_GROUPS * EXPERTS_PER_GROUP
EXPERT_FF = 512
TOP_K = 2
N_MOD = 6
EPS = 1e-6
F32 = jnp.float32

kernel_name = "hybrid_hgrn2_gqa_diffattn_hmoe_dit"


def rms_norm(x, g):
    xf = x.astype(F32)
    y = xf * lax.rsqrt(jnp.mean(xf * xf, axis=-1, keepdims=True) + EPS)
    return (y * g.astype(F32)).astype(x.dtype)


def modulate(h, shift, scale):
    return h * (1 + scale) + shift


def split_cols(p):
    return jnp.split(p, np.cumsum(IN_SPLITS)[:-1].tolist(), axis=-1)


def to_heads(a, n):
    B, L, _ = a.shape
    return a.reshape(B, L, n, -1).transpose(0, 2, 1, 3)


def from_heads(a):
    B, H, L, d = a.shape
    return a.transpose(0, 2, 1, 3).reshape(B, L, H * d)


def axial_rope(n_tokens, dim):
    rows = n_tokens // GRID_W
    row = jnp.repeat(jnp.arange(rows, dtype=F32), GRID_W)
    col = jnp.tile(jnp.arange(GRID_W, dtype=F32), rows)
    axis_dim = dim // 2
    inv_freq = ROPE_THETA ** (-jnp.arange(0, axis_dim, 2, dtype=F32) / axis_dim)
    ang = jnp.concatenate([row[:, None] * inv_freq, col[:, None] * inv_freq], axis=-1)
    return jnp.cos(ang), jnp.sin(ang)


def apply_rope(x, cos, sin):
    xf = x.astype(F32).reshape(*x.shape[:-1], -1, 2)
    x0, x1 = xf[..., 0], xf[..., 1]
    out = jnp.stack([x0 * cos - x1 * sin, x0 * sin + x1 * cos], axis=-1)
    return out.reshape(x.shape).astype(x.dtype)


def hgrn_log_forget(z, lb):
    lbf = lb.astype(F32)
    return jnp.logaddexp(jnp.log(lbf), jnp.log1p(-lbf) + jax.nn.log_sigmoid(z.astype(F32)))


def hgrn2_bidir_inputs(qf, qb, ff, fb, iv, lb):
    flip = lambda a: jnp.flip(a, axis=2)
    logf = jnp.concatenate([to_heads(hgrn_log_forget(ff, lb), HG_HEADS),
                            flip(to_heads(hgrn_log_forget(fb, lb), HG_HEADS))], axis=1)
    q = jnp.concatenate([to_heads(qf, HG_HEADS), flip(to_heads(qb, HG_HEADS))], axis=1)
    vh = to_heads(iv, HG_HEADS)
    v = jnp.concatenate([vh, flip(vh)], axis=1)
    k = -jnp.expm1(logf)
    return q, k, v, logf


def chunk_gated_scan(q, k, v, logf, s0):
    B, H, T, dk = q.shape
    dv = v.shape[-1]
    n = T // HG_CHUNK
    chunks = lambda a: a.reshape(B, H, n, HG_CHUNK, a.shape[-1]).transpose(2, 0, 1, 3, 4)
    order_mask = jnp.tril(jnp.ones((HG_CHUNK, HG_CHUNK), dtype=bool))[:, :, None]

    def step(state, inp):
        qc, kc, vc, gc = inp
        b = jnp.cumsum(gc.astype(F32), axis=-2)
        rel = jnp.where(order_mask, b[..., :, None, :] - b[..., None, :, :], -jnp.inf)
        intra = jnp.einsum('bhtd,bhsd,bhtsd->bhts', qc.astype(F32), kc.astype(F32), jnp.exp(rel))
        o = (jnp.einsum('bhts,bhse->bhte', intra, vc.astype(F32))
             + jnp.einsum('bhtd,bhde->bhte', qc * jnp.exp(b), state))
        b_end = b[..., -1:, :]
        new_state = (jnp.exp(b_end[..., 0, :])[..., None] * state
                     + jnp.einsum('bhsd,bhse->bhde', kc * jnp.exp(b_end - b), vc.astype(F32)))
        return new_state, o

    state, o = lax.scan(step, s0, (chunks(q), chunks(k), chunks(v), chunks(logf)))
    return o.transpose(1, 2, 0, 3, 4).reshape(B, H, T, dv), state


def hgrn2_readout(o, gate, norm_g):
    o = o[:, :HG_HEADS] + jnp.flip(o[:, HG_HEADS:], axis=2)
    return from_heads(rms_norm(o, norm_g)).astype(gate.dtype) * jax.nn.silu(gate)


def gqa_heads(q, k, v, qn, kn):
    return (rms_norm(to_heads(q, GQA_HEADS), qn), rms_norm(to_heads(k, GQA_KV_HEADS), kn),
            to_heads(v, GQA_KV_HEADS))


def gqa_attention(q, k, v):
    B, H, Tq, d = q.shape
    KV = k.shape[1]
    nb = Tq // Q_BLOCK
    qb = q.reshape(B, KV, H // KV, nb, Q_BLOCK, d).transpose(3, 0, 1, 2, 4, 5)

    def one_block(qblk):
        s = jnp.einsum('bkgqd,bksd->bkgqs', qblk, k).astype(F32) * (d ** -0.5)
        p = jax.nn.softmax(s, axis=-1)
        return jnp.einsum('bkgqs,bksd->bkgqd', p.astype(v.dtype), v)

    o = lax.map(one_block, qb)
    return from_heads(o.transpose(1, 2, 3, 0, 4, 5).reshape(B, H, Tq, d))


def diff_heads(q, k, v):
    B, L, _ = q.shape
    q = q.reshape(B, L, DIFF_HEADS, 2, DIFF_QK_DIM).transpose(3, 0, 2, 1, 4)
    k = k.reshape(B, L, DIFF_HEADS, 2, DIFF_QK_DIM).transpose(3, 0, 2, 1, 4)
    return q, k, to_heads(v, DIFF_HEADS)


def diff_attention(q, k, v, lam):
    _, B, H, Tq, d = q.shape
    nb = Tq // Q_BLOCK
    qb = q.reshape(2, B, H, nb, Q_BLOCK, d).transpose(3, 0, 1, 2, 4, 5)

    def one_block(qblk):
        s = jnp.einsum('jbhqd,jbhkd->jbhqk', qblk, k).astype(F32) * (d ** -0.5)
        p = jax.nn.softmax(s, axis=-1)
        w = p[0] - lam * p[1]
        return jnp.einsum('bhqk,bhkd->bhqd', w.astype(v.dtype), v)

    o = lax.map(one_block, qb)
    return o.transpose(1, 2, 0, 3, 4).reshape(B, H, Tq, v.shape[-1])


def diff_readout(o, norm_g, lam_init):
    return from_heads(rms_norm(o, norm_g) * (1 - lam_init))


def hier_moe(h, rgw, rgb, rew, reb, w_gate, w_up, w_down):
    N = h.shape[0]
    g_prob = jax.nn.softmax((h @ rgw + rgb).astype(F32), axis=-1)
    g_top, g_idx = lax.top_k(g_prob, 1)
    e_logits = (h @ rew + reb).astype(F32).reshape(N, N_GROUPS, EXPERTS_PER_GROUP)
    e_sel = jnp.take_along_axis(e_logits, g_idx[:, :, None], axis=1)[:, 0]
    e_top, e_idx = lax.top_k(jax.nn.softmax(e_sel, axis=-1), TOP_K)
    weights = g_top * e_top / jnp.sum(e_top, axis=-1, keepdims=True)
    expert_id = g_idx * EXPERTS_PER_GROUP + e_idx
    combine = jnp.sum(jax.nn.one_hot(expert_id, N_EXPERTS, dtype=F32) * weights[..., None], axis=1)
    y = jnp.zeros_like(h)
    for gi in range(N_GROUPS):
        sl = slice(gi * EXPERTS_PER_GROUP, (gi + 1) * EXPERTS_PER_GROUP)
        a = jax.nn.silu(jnp.einsum('nd,edf->enf', h, w_gate[sl])) * jnp.einsum('nd,edf->enf', h, w_up[sl])
        a = a * combine[:, sl].T[:, :, None].astype(a.dtype)
        y = y + jnp.einsum('enf,efd->nd', a, w_down[sl])
    return y


def layer(xl, xc, c, c_ctx, w_mod, b_mod, n1, n2, w_in, w_out, lb, hg_ng, qn, kn,
          lq1, lk1, lq2, lk2, dn, rgw, rgb, rew, reb, w_gate, w_up, w_down,
          layer_idx, ropes, update_ctx):
    B = xl.shape[0]
    D = xl.shape[-1]
    sh1, sc1, gt1, sh2, sc2, gt2 = jnp.split((jax.nn.silu(c) @ w_mod + b_mod)[:, None, :], N_MOD, axis=-1)
    csh1, csc1, cgt1, csh2, csc2, cgt2 = jnp.split(jax.nn.silu(c_ctx) @ w_mod + b_mod, N_MOD, axis=-1)
    (cos_a, sin_a), (cos_d, sin_d) = ropes

    pl = split_cols(modulate(rms_norm(xl, n1), sh1, sc1) @ w_in)
    pc = split_cols(modulate(rms_norm(xc, n1), csh1, csc1) @ w_in)

    s0 = jnp.zeros((B, 2 * HG_HEADS, HEAD_DIM, HEAD_DIM), F32)
    o_c, s_c = chunk_gated_scan(*hgrn2_bidir_inputs(*pc[:5], lb), s0)
    o_l, _ = chunk_gated_scan(*hgrn2_bidir_inputs(*pl[:5], lb), s_c)
    hg_l = hgrn2_readout(o_l, pl[5], hg_ng)

    qa_l, ka_l, va_l = gqa_heads(*pl[6:9], qn, kn)
    qa_l = apply_rope(qa_l, cos_a, sin_a)
    ka_l = apply_rope(ka_l, cos_a, sin_a)
    qa_c, ka_c, va_c = gqa_heads(*pc[6:9], qn, kn)
    att_l = gqa_attention(qa_l, jnp.concatenate([ka_c, ka_l], axis=2), jnp.concatenate([va_c, va_l], axis=2))

    qd_l, kd_l, vd_l = diff_heads(*pl[9:12])
    qd_l = apply_rope(qd_l, cos_d, sin_d)
    kd_l = apply_rope(kd_l, cos_d, sin_d)
    qd_c, kd_c, vd_c = diff_heads(*pc[9:12])
    lam_init = 0.8 - 0.6 * math.exp(-0.3 * layer_idx)
    lam = (jnp.exp(jnp.sum(lq1.astype(F32) * lk1.astype(F32)))
           - jnp.exp(jnp.sum(lq2.astype(F32) * lk2.astype(F32))) + lam_init)
    diff_l = diff_readout(diff_attention(qd_l, jnp.concatenate([kd_c, kd_l], axis=3),
                                         jnp.concatenate([vd_c, vd_l], axis=2), lam), dn, lam_init)

    xl = xl + gt1 * (jnp.concatenate([hg_l, att_l, diff_l], axis=-1) @ w_out)

    hl2 = modulate(rms_norm(xl, n2), sh2, sc2)
    if update_ctx:
        hg_c = hgrn2_readout(o_c, pc[5], hg_ng)
        att_c = gqa_attention(qa_c, ka_c, va_c)
        diff_c = diff_readout(diff_attention(qd_c, kd_c, vd_c, lam), dn, lam_init)
        xc = xc + cgt1 * (jnp.concatenate([hg_c, att_c, diff_c], axis=-1) @ w_out)
        hc2 = modulate(rms_norm(xc, n2), csh2, csc2)
        Lc = xc.shape[1]
        tokens = jnp.concatenate([hc2, hl2], axis=1)
        y = hier_moe(tokens.reshape(-1, D), rgw, rgb, rew, reb, w_gate, w_up, w_down).reshape(tokens.shape)
        xc = xc + cgt2 * y[:, :Lc]
        xl = xl + gt2 * y[:, Lc:]
    else:
        y = hier_moe(hl2.reshape(-1, D), rgw, rgb, rew, reb, w_gate, w_up, w_down).reshape(hl2.shape)
        xl = xl + gt2 * y
    return xl, xc


def setup_inputs(seed: int = 0) -> dict:
    key = jax.random.key(seed)
    ks = jax.random.split(key, 27)
    D = D_MODEL
    nrm = lambda k, shape, s: s * jax.random.normal(k, shape, F32)
    return {
        "x": nrm(ks[0], (BATCH, SEQ, D), 1.0),
        "c": nrm(ks[1], (BATCH, D), 1.0),
        "ctx": nrm(ks[2], (BATCH, CTX_LEN, D), 1.0),
        "c_ctx": nrm(ks[3], (D,), 1.0),
        "w_mod": nrm(ks[4], (DEPTH, D, N_MOD * D), 0.5 * D ** -0.5),
        "b_mod": nrm(ks[5], (DEPTH, N_MOD * D), 0.02),
        "norm1_g": 1.0 + nrm(ks[6], (DEPTH, D), 0.02),
        "norm2_g": 1.0 + nrm(ks[7], (DEPTH, D), 0.02),
        "w_in": nrm(ks[8], (DEPTH, D, IN_WIDTH), D ** -0.5),
        "w_out": nrm(ks[9], (DEPTH, MIX_WIDTH, D), MIX_WIDTH ** -0.5),
        "hg_lb_logits": nrm(ks[10], (DEPTH, HG_WIDTH), 0.1),
        "hg_norm_g": 1.0 + nrm(ks[11], (DEPTH, HEAD_DIM), 0.02),
        "q_norm_g": 1.0 + nrm(ks[12], (DEPTH, HEAD_DIM), 0.02),
        "k_norm_g": 1.0 + nrm(ks[13], (DEPTH, HEAD_DIM), 0.02),
        "lam_q1": nrm(ks[14], (DEPTH, DIFF_QK_DIM), 0.1),
        "lam_k1": nrm(ks[15], (DEPTH, DIFF_QK_DIM), 0.1),
        "lam_q2": nrm(ks[16], (DEPTH, DIFF_QK_DIM), 0.1),
        "lam_k2": nrm(ks[17], (DEPTH, DIFF_QK_DIM), 0.1),
        "diff_norm_g": 1.0 + nrm(ks[18], (DEPTH, HEAD_DIM), 0.02),
        "router_group_w": nrm(ks[19], (DEPTH, D, N_GROUPS), D ** -0.5),
        "router_group_b": nrm(ks[20], (DEPTH, N_GROUPS), 0.01),
        "router_expert_w": nrm(ks[21], (DEPTH, D, N_EXPERTS), D ** -0.5),
        "router_expert_b": nrm(ks[22], (DEPTH, N_EXPERTS), 0.01),
        "w_gate": nrm(ks[23], (DEPTH, N_EXPERTS, D, EXPERT_FF), D ** -0.5),
        "w_up": nrm(ks[24], (DEPTH, N_EXPERTS, D, EXPERT_FF), D ** -0.5),
        "w_down": nrm(ks[25], (DEPTH, N_EXPERTS, EXPERT_FF, D), EXPERT_FF ** -0.5),
        "final_norm_g": 1.0 + nrm(ks[26], (D,), 0.02),
    }


def reference(x, c, ctx, c_ctx, w_mod, b_mod, norm1_g, norm2_g, w_in, w_out, hg_lb_logits,
              hg_norm_g, q_norm_g, k_norm_g, lam_q1, lam_k1, lam_q2, lam_k2, diff_norm_g,
              router_group_w, router_group_b, router_expert_w, router_expert_b,
              w_gate, w_up, w_down, final_norm_g):
    n_tokens = x.shape[1]
    ropes = (axial_rope(n_tokens, HEAD_DIM), axial_rope(n_tokens, DIFF_QK_DIM))
    lb_all = jnp.cumsum(jax.nn.softmax(hg_lb_logits.astype(F32), axis=0), axis=0)
    lb_all = lb_all - lb_all[:1]
    xl, xc = x, ctx
    for l in range(DEPTH):
        xl, xc = layer(xl, xc, c, c_ctx, w_mod[l], b_mod[l], norm1_g[l], norm2_g[l], w_in[l], w_out[l],
                       lb_all[l], hg_norm_g[l], q_norm_g[l], k_norm_g[l],
                       lam_q1[l], lam_k1[l], lam_q2[l], lam_k2[l], diff_norm_g[l],
                       router_group_w[l], router_group_b[l], router_expert_w[l], router_expert_b[l],
                       w_gate[l], w_up[l], w_down[l], l, ropes, l < DEPTH - 1)
    return rms_norm(xl, final_norm_g)
```

```python
import functools
import math

import numpy as np
import jax
import jax.numpy as jnp
from jax import lax
from jax.experimental import pallas as pl
from jax.experimental.pallas import tpu as pltpu

F32 = jnp.float32
BF16 = jnp.bfloat16
I32 = jnp.int32

HEAD_DIM = 128
LANES = 128
GRID_W = 64
ROPE_THETA = 10000.0
EPS = 1e-6
HG_HEADS = 4
HG_CHUNK = 64
GQA_HEADS = 8
GQA_KV_HEADS = 2
DIFF_HEADS = 4
N_GROUPS = 4
EXPERTS_PER_GROUP = 8
N_EXPERTS = N_GROUPS * EXPERTS_PER_GROUP
N_MOD = 6
ROW_TILE = 256
MOE_TILE = 256
MIB = 1024 * 1024

_NT = (((1,), (1,)), ((), ()))
_TN = (((0,), (0,)), ((), ()))


def _params(semantics, vmem_mib):
    return pltpu.CompilerParams(dimension_semantics=semantics, vmem_limit_bytes=vmem_mib * MIB)


def _split_bf16(x, parts):
    out = []
    for _ in range(parts - 1):
        p = x.astype(BF16)
        out.append(p)
        x = x - p.astype(F32)
    out.append(x.astype(BF16))
    return out


def _mod_kernel(a_ref, w_ref, b_ref, o_ref):
    a = a_ref[...]
    a = a * (1.0 / (1.0 + jnp.exp(-a)))
    hi, lo = _split_bf16(a, 2)
    w = w_ref[0].astype(BF16)
    o_ref[0] = (jnp.dot(hi, w, preferred_element_type=F32)
                + jnp.dot(lo, w, preferred_element_type=F32) + b_ref[0])


def _mod_call(cc, w_mod, b_mod):
    depth, d, n = w_mod.shape
    tn = 1024
    return pl.pallas_call(
        _mod_kernel,
        out_shape=jax.ShapeDtypeStruct((depth, 8, n), F32),
        grid=(depth, n // tn),
        in_specs=[pl.BlockSpec((8, d), lambda l, j: (0, 0)),
                  pl.BlockSpec((1, d, tn), lambda l, j: (l, 0, j)),
                  pl.BlockSpec((1, 1, tn), lambda l, j: (l, 0, j))],
        out_specs=pl.BlockSpec((1, 8, tn), lambda l, j: (l, 0, j)),
        compiler_params=_params(("arbitrary", "arbitrary"), 40),
        name="mod_vectors",
    )(cc, w_mod, b_mod.reshape(depth, 1, n))


def _norm_mod(x, g, shift, scale):
    y = x * lax.rsqrt(jnp.mean(x * x, axis=-1, keepdims=True) + EPS) * g
    return y * (1.0 + scale) + shift


def _prenorm_kernel(x_ref, g_ref, m_ref, o_ref):
    m = m_ref[0]
    o_ref[...] = _norm_mod(x_ref[...], g_ref[...], m[0:1], m[1:2]).astype(BF16)


def _mod_spec(d, nct, tile_off=0):
    return pl.BlockSpec((1, N_MOD, d), lambda i: (jnp.where(i + tile_off < nct, 1, 0), 0, 0))


def _prenorm_call(x, g, mods, nct):
    t, d = x.shape
    return pl.pallas_call(
        _prenorm_kernel,
        out_shape=jax.ShapeDtypeStruct((t, d), BF16),
        grid=(t // ROW_TILE,),
        in_specs=[pl.BlockSpec((ROW_TILE, d), lambda i: (i, 0)),
                  pl.BlockSpec((1, d), lambda i: (0, 0)),
                  _mod_spec(d, nct)],
        out_specs=pl.BlockSpec((ROW_TILE, d), lambda i: (i, 0)),
        compiler_params=_params(("arbitrary",), 24),
        name="prenorm",
    )(x, g.reshape(1, d), mods)


def _mm_kernel(a_ref, b_ref, o_ref):
    o_ref[...] = jnp.dot(a_ref[...], b_ref[...], preferred_element_type=F32)


def _matmul_call(a, b, tm, tn):
    m, k = a.shape
    n = b.shape[1]
    return pl.pallas_call(
        _mm_kernel,
        out_shape=jax.ShapeDtypeStruct((m, n), F32),
        grid=(m // tm, n // tn),
        in_specs=[pl.BlockSpec((tm, k), lambda i, j: (i, 0)),
                  pl.BlockSpec((k, tn), lambda i, j: (0, j))],
        out_specs=pl.BlockSpec((tm, tn), lambda i, j: (i, j)),
        compiler_params=_params(("arbitrary", "arbitrary"), 40),
        name="in_proj",
    )(a, b)


_HG_LEVELS = (1, 2, 4, 8, 16, 32)
_HG_TOT_ROW = HG_CHUNK * (len(_HG_LEVELS) + 1)
_HG_W_ROWS = _HG_TOT_ROW + 16


def _hgrn_consts():
    c = HG_CHUNK
    w = np.zeros((2, _HG_W_ROWS, c), np.float32)
    msk = np.zeros((2, len(_HG_LEVELS) + 1, c, c), np.float32)
    for d in range(2):
        u = np.arange(c) if d == 0 else c - 1 - np.arange(c)
        ut, us = u[:, None], u[None, :]
        w[d, :c] = us <= ut
        for li, lv in enumerate(_HG_LEVELS):
            blk = u // (2 * lv)
            qside = (u % (2 * lv)) >= lv
            bnd = (blk * 2 * lv + lv - 1)[:, None]
            wq = (us > bnd) & (us <= ut)
            wk = (us > ut) & (us <= bnd)
            w[d, c * (li + 1):c * (li + 2)] = np.where(qside[:, None], wq, -1.0 * wk)
            msk[d, li] = (blk[:, None] == blk[None, :]) & qside[:, None] & ~qside[None, :]
        msk[d, len(_HG_LEVELS)] = np.eye(c)
        w[d, _HG_TOT_ROW:] = 1.0
    return jnp.asarray(w, BF16), jnp.asarray(msk, F32)


def _hgrn_kernel(lbl_ref, q_ref, z_ref, v_ref, w_ref, msk_ref, o_ref, st_ref, *, layer, chunks):
    c = HG_CHUNK
    d = pl.program_id(0)
    j = pl.program_id(2)

    @pl.when(j == 0)
    def _():
        st_ref[...] = jnp.zeros_like(st_ref)

    lbl = lbl_ref[...]
    rows = [lbl[i:i + 1] for i in range(lbl.shape[0])]
    mx = functools.reduce(jnp.maximum, rows)
    ex = [jnp.exp(r - mx) for r in rows]
    tot = functools.reduce(lambda a, b: a + b, ex)
    lb = jnp.zeros_like(mx)
    for i in range(1, layer + 1):
        lb = lb + ex[i] / tot
    log_lb = jnp.log(lb)
    log_1m_lb = jnp.log1p(-lb)

    wmat = w_ref[0]
    t_idx = lax.broadcasted_iota(I32, (c, HEAD_DIM), 0)
    u_idx = t_idx + d * (c - 1 - 2 * t_idx)

    for ci in range(chunks):
        cc = ci + d * (chunks - 1 - 2 * ci)
        r0 = pl.multiple_of(cc * c, c)
        q = q_ref[pl.ds(r0, c), :]
        z = z_ref[pl.ds(r0, c), :]
        v = v_ref[pl.ds(r0, c), :]
        l1p = jnp.log1p(jnp.exp(-jnp.abs(z)))
        ls_pos = jnp.minimum(z, 0.0) - l1p
        ls_neg = jnp.minimum(-z, 0.0) - l1p
        a2 = log_1m_lb + ls_pos
        logf = jnp.maximum(log_lb, a2) + jnp.log1p(jnp.exp(-jnp.abs(log_lb - a2)))
        k = (1.0 - lb) * jnp.exp(ls_neg)

        parts = jnp.concatenate(_split_bf16(logf, 3), axis=1)
        sums = jnp.dot(wmat, parts, preferred_element_type=F32)
        sums = sums[:, :HEAD_DIM] + sums[:, HEAD_DIM:2 * HEAD_DIM] + sums[:, 2 * HEAD_DIM:]
        b = sums[0:c]
        btot = sums[_HG_TOT_ROW:_HG_TOT_ROW + 1]

        q_bf = q.astype(BF16)
        k_bf = k.astype(BF16)
        v_bf = v.astype(BF16)
        nlev = len(_HG_LEVELS)
        amat = lax.dot_general(q_bf, k_bf, _NT, preferred_element_type=F32) * msk_ref[0, nlev]
        for li, lv in enumerate(_HG_LEVELS):
            fl = jnp.exp(-jnp.abs(sums[c * (li + 1):c * (li + 2)]))
            qside = (u_idx & (2 * lv - 1)) >= lv
            ql = jnp.where(qside, q * fl, 0.0).astype(BF16)
            kl = jnp.where(qside, 0.0, k * fl).astype(BF16)
            amat = amat + lax.dot_general(ql, kl, _NT, preferred_element_type=F32) * msk_ref[0, li]

        st = st_ref[...]
        qe = (q * jnp.exp(b)).astype(BF16)
        o = (jnp.dot(amat.astype(BF16), v_bf, preferred_element_type=F32)
             + lax.dot_general(qe, st.astype(BF16), _NT, preferred_element_type=F32))
        o_ref[0, pl.ds(r0, c), :] = o
        kd = (k * jnp.exp(btot - b)).astype(BF16)
        st_ref[...] = st * jnp.exp(btot) + lax.dot_general(v_bf, kd, _TN, preferred_element_type=F32)


def _hgrn_call(p, lb_logits, layer, nct):
    t = p.shape[0]
    nblk = t // ROW_TILE
    chunks = ROW_TILE // HG_CHUNK
    wmat, msk = _hgrn_consts()

    def blk(d, j):
        back = jnp.where(j < nct, nct - 1 - j, nblk - 1 - (j - nct))
        return jnp.where(d == 0, j, back)

    return pl.pallas_call(
        functools.partial(_hgrn_kernel, layer=layer, chunks=chunks),
        out_shape=jax.ShapeDtypeStruct((2, t, HG_HEADS * HEAD_DIM), F32),
        grid=(2, HG_HEADS, nblk),
        in_specs=[pl.BlockSpec((lb_logits.shape[0], HEAD_DIM), lambda d, h, j: (0, h)),
                  pl.BlockSpec((ROW_TILE, HEAD_DIM), lambda d, h, j: (blk(d, j), d * HG_HEADS + h)),
                  pl.BlockSpec((ROW_TILE, HEAD_DIM), lambda d, h, j: (blk(d, j), (2 + d) * HG_HEADS + h)),
                  pl.BlockSpec((ROW_TILE, HEAD_DIM), lambda d, h, j: (blk(d, j), 4 * HG_HEADS + h)),
                  pl.BlockSpec((1, _HG_W_ROWS, HG_CHUNK), lambda d, h, j: (d, 0, 0)),
                  pl.BlockSpec((1, len(_HG_LEVELS) + 1, HG_CHUNK, HG_CHUNK), lambda d, h, j: (d, 0, 0, 0))],
        out_specs=pl.BlockSpec((1, ROW_TILE, HEAD_DIM), lambda d, h, j: (d, blk(d, j), h)),
        scratch_shapes=[pltpu.VMEM((HEAD_DIM, HEAD_DIM), F32)],
        compiler_params=_params(("arbitrary", "arbitrary", "arbitrary"), 24),
        name="hgrn_scan",
    )(lb_logits, p, p, p, wmat, msk)


def _rope_tables(n_lat, n_ctx, dim):
    rows = n_lat // GRID_W
    row = jnp.repeat(jnp.arange(rows, dtype=F32), GRID_W)
    col = jnp.tile(jnp.arange(GRID_W, dtype=F32), rows)
    axis_dim = dim // 2
    inv_freq = ROPE_THETA ** (-jnp.arange(0, axis_dim, 2, dtype=F32) / axis_dim)
    ang = jnp.concatenate([row[:, None] * inv_freq, col[:, None] * inv_freq], axis=-1)
    cos = jnp.repeat(jnp.cos(ang), 2, axis=1)
    sin = jnp.repeat(jnp.sin(ang), 2, axis=1) * jnp.tile(jnp.asarray([-1.0, 1.0], F32), dim // 2)
    reps = LANES // dim
    cos = jnp.concatenate([jnp.ones((n_ctx, dim), F32), cos], axis=0)
    sin = jnp.concatenate([jnp.zeros((n_ctx, dim), F32), sin], axis=0)
    return jnp.tile(cos, (1, reps)), jnp.tile(sin, (1, reps))


def _rope(x, cos, sin):
    lane = lax.broadcasted_iota(I32, x.shape, 1)
    swapped = jnp.where((lane & 1) == 0, pltpu.roll(x, LANES - 1, 1), pltpu.roll(x, 1, 1))
    return x * cos + swapped * sin


def _head_norm(x, g):
    return x * lax.rsqrt(jnp.mean(x * x, axis=-1, keepdims=True) + EPS) * g


def _attn_prep_kernel(pq_ref, pkv_ref, pd_ref, ca_ref, sa_ref, cd_ref, sd_ref, qn_ref, kn_ref,
                      qa_ref, ka_ref, va_ref, qd_ref, kd_ref, vd_ref):
    hd = HEAD_DIM
    ca, sa, cd, sd = ca_ref[...], sa_ref[...], cd_ref[...], sd_ref[...]
    qn, kn = qn_ref[...], kn_ref[...]
    for h in range(GQA_HEADS):
        xq = _rope(_head_norm(pq_ref[:, h * hd:(h + 1) * hd], qn), ca, sa)
        qa_ref[:, h * hd:(h + 1) * hd] = (xq * hd ** -0.5).astype(BF16)
    for h in range(GQA_KV_HEADS):
        xk = _rope(_head_norm(pkv_ref[:, h * hd:(h + 1) * hd], kn), ca, sa)
        ka_ref[:, h * hd:(h + 1) * hd] = xk.astype(BF16)
    va_ref[...] = pkv_ref[:, GQA_KV_HEADS * hd:].astype(BF16)
    first = lax.broadcasted_iota(I32, (pq_ref.shape[0], hd), 1) < hd // 2
    dw = DIFF_HEADS * hd
    for h in range(DIFF_HEADS):
        xq = _rope(pd_ref[:, h * hd:(h + 1) * hd], cd, sd) * (hd // 2) ** -0.5
        qd_ref[:, (2 * h) * hd:(2 * h + 1) * hd] = jnp.where(first, xq, 0.0).astype(BF16)
        qd_ref[:, (2 * h + 1) * hd:(2 * h + 2) * hd] = jnp.where(first, 0.0, xq).astype(BF16)
        xk = _rope(pd_ref[:, dw + h * hd:dw + (h + 1) * hd], cd, sd)
        kd_ref[:, h * hd:(h + 1) * hd] = xk.astype(BF16)
    vd_ref[...] = pd_ref[:, 2 * dw:].astype(BF16)


def _attn_prep_call(p, tabs, qn, kn):
    t = p.shape[0]
    hd = HEAD_DIM
    gq, gkv, dw = GQA_HEADS * hd, 2 * GQA_KV_HEADS * hd, DIFF_HEADS * hd
    q_off = 6 * HG_HEADS * hd
    assert q_off % gq == 0 and (q_off + gq) % gkv == 0 and (q_off + gq + gkv) % (3 * dw) == 0
    row = lambda w: pl.BlockSpec((ROW_TILE, w), lambda i: (i, 0))
    vec = pl.BlockSpec((1, hd), lambda i: (0, 0))
    return pl.pallas_call(
        _attn_prep_kernel,
        out_shape=[jax.ShapeDtypeStruct((t, gq), BF16),
                   jax.ShapeDtypeStruct((t, gkv // 2), BF16),
                   jax.ShapeDtypeStruct((t, gkv // 2), BF16),
                   jax.ShapeDtypeStruct((t, 2 * dw), BF16),
                   jax.ShapeDtypeStruct((t, dw), BF16),
                   jax.ShapeDtypeStruct((t, dw), BF16)],
        grid=(t // ROW_TILE,),
        in_specs=[pl.BlockSpec((ROW_TILE, gq), lambda i: (i, q_off // gq)),
                  pl.BlockSpec((ROW_TILE, gkv), lambda i: (i, (q_off + gq) // gkv)),
                  pl.BlockSpec((ROW_TILE, 3 * dw), lambda i: (i, (q_off + gq + gkv) // (3 * dw))),
                  row(hd), row(hd), row(hd), row(hd), vec, vec],
        out_specs=[row(gq), row(gkv // 2), row(gkv // 2), row(2 * dw), row(dw), row(dw)],
        compiler_params=_params(("arbitrary",), 24),
        name="attn_prep",
    )(p, p, p, *tabs, qn.reshape(1, hd), kn.reshape(1, hd))


def _flash_kernel(*refs, g, n_ctx, n_lat, tkl, nct, tile_off, diff, lam_init):
    if diff:
        q_ref, k_ref, v_ref, lq1, lk1, lq2, lk2, dn_ref, o_ref, m_sc, l_sc, acc_sc = refs
    else:
        q_ref, k_ref, v_ref, o_ref, m_sc, l_sc, acc_sc = refs
    hd = HEAD_DIM
    tq = q_ref.shape[0]
    i = pl.program_id(1) + tile_off
    q = jnp.concatenate([q_ref[:, h * hd:(h + 1) * hd] for h in range(g)], axis=0)

    def scores(kb):
        return lax.dot_general(q, kb, _NT, preferred_element_type=F32)

    s = scores(k_ref[0:n_ctx, :])
    m0 = jnp.max(s, axis=1, keepdims=True)
    p = jnp.exp(s - m0)
    m_sc[...] = jnp.broadcast_to(m0, m_sc.shape)
    l_sc[...] = jnp.broadcast_to(jnp.sum(p, axis=1, keepdims=True), l_sc.shape)
    acc_sc[...] = jnp.dot(p.astype(BF16), v_ref[0:n_ctx, :], preferred_element_type=F32)

    @pl.when(i >= nct)
    def _():
        def body(jb, carry):
            r0 = pl.multiple_of(n_ctx + jb * tkl, LANES)
            s = scores(k_ref[pl.ds(r0, tkl), :])
            m_prev = m_sc[...]
            m_new = jnp.maximum(m_prev, jnp.max(s, axis=1, keepdims=True))
            p = jnp.exp(s - jnp.tile(m_new, (1, tkl // LANES)))
            alpha = jnp.exp(m_prev - m_new)
            l_sc[...] = alpha * l_sc[...] + jnp.sum(p, axis=1, keepdims=True)
            acc_sc[...] = alpha * acc_sc[...] + jnp.dot(p.astype(BF16), v_ref[pl.ds(r0, tkl), :],
                                                       preferred_element_type=F32)
            m_sc[...] = m_new
            return carry
        lax.fori_loop(0, n_lat // tkl, body, 0)

    o = acc_sc[...] / l_sc[...]
    if diff:
        lam = (jnp.exp(jnp.sum(lq1[...] * lk1[...], axis=1, keepdims=True))
               - jnp.exp(jnp.sum(lq2[...] * lk2[...], axis=1, keepdims=True)) + lam_init)
        dd = o[0:tq] - lam * o[tq:2 * tq]
        o_ref[...] = (_head_norm(dd, dn_ref[...]) * (1.0 - lam_init)).astype(BF16)
    else:
        for h in range(g):
            o_ref[:, h * hd:(h + 1) * hd] = o[h * tq:(h + 1) * tq].astype(BF16)


def _flash_call(q, k, v, n_ctx, g, tile_off, extra=None, lam_init=0.0):
    t = k.shape[0]
    hd = HEAD_DIM
    n_kv = k.shape[1] // hd
    n_lat = t - n_ctx
    nct = n_ctx // ROW_TILE
    tkl = 1024
    diff = extra is not None
    rows = g * ROW_TILE
    ow = hd if diff else g * hd
    in_specs = [pl.BlockSpec((ROW_TILE, g * hd), lambda kv, i: (i + tile_off, kv)),
                pl.BlockSpec((t, hd), lambda kv, i: (0, kv)),
                pl.BlockSpec((t, hd), lambda kv, i: (0, kv))]
    args = [q, k, v]
    if diff:
        for a in extra:
            a = a.reshape(1, -1)
            in_specs.append(pl.BlockSpec(a.shape, lambda kv, i: (0, 0)))
            args.append(a)
    return pl.pallas_call(
        functools.partial(_flash_kernel, g=g, n_ctx=n_ctx, n_lat=n_lat, tkl=tkl, nct=nct,
                          tile_off=tile_off, diff=diff, lam_init=lam_init),
        out_shape=jax.ShapeDtypeStruct((t - tile_off * ROW_TILE, n_kv * ow), BF16),
        grid=(n_kv, t // ROW_TILE - tile_off),
        in_specs=in_specs,
        out_specs=pl.BlockSpec((ROW_TILE, ow), lambda kv, i: (i, kv)),
        scratch_shapes=[pltpu.VMEM((rows, LANES), F32), pltpu.VMEM((rows, LANES), F32),
                        pltpu.VMEM((rows, hd), F32)],
        compiler_params=_params(("arbitrary", "arbitrary"), 48),
        name="diff_attention" if diff else "gqa_attention",
    )(*args)


def _outproj_kernel(of_ref, ob_ref, gate_ref, ng_ref, att_ref, dif_ref, wo_ref, x_ref, m_ref, n2_ref,
                    rw_ref, xo_ref, h2_ref, lg_ref):
    hd = HEAD_DIM
    hgw = HG_HEADS * hd
    ng = ng_ref[...]
    gate = gate_ref[...]
    silu_gate = gate * (1.0 / (1.0 + jnp.exp(-gate)))
    acc = jnp.dot(att_ref[...], wo_ref[hgw:hgw + att_ref.shape[1], :], preferred_element_type=F32)
    acc = acc + jnp.dot(dif_ref[...], wo_ref[hgw + att_ref.shape[1]:, :], preferred_element_type=F32)
    hg = []
    for h in range(HG_HEADS):
        o = of_ref[0, :, h * hd:(h + 1) * hd] + ob_ref[0, :, h * hd:(h + 1) * hd]
        hg.append((_head_norm(o, ng) * silu_gate[:, h * hd:(h + 1) * hd]).astype(BF16))
    acc = acc + jnp.dot(jnp.concatenate(hg, axis=1), wo_ref[0:hgw, :], preferred_element_type=F32)
    m = m_ref[0]
    xn = x_ref[...] + m[2:3] * acc
    xo_ref[...] = xn
    h2 = _norm_mod(xn, n2_ref[...], m[3:4], m[4:5])
    h2_ref[...] = h2
    hs = _split_bf16(h2, 3)
    ws = _split_bf16(rw_ref[...], 3)
    lg = jnp.zeros(lg_ref.shape, F32)
    for a, b in ((2, 0), (1, 1), (0, 2), (1, 0), (0, 1), (0, 0)):
        lg = lg + jnp.dot(hs[a], ws[b], preferred_element_type=F32)
    lg_ref[...] = lg


def _outproj_call(o, p, ng, att, dif, wo, x, mods, n2, rw, nct, tile_off):
    d = x.shape[1]
    hd = HEAD_DIM
    hgw = HG_HEADS * hd
    n_rows = att.shape[0]
    row = lambda w: pl.BlockSpec((ROW_TILE, w), lambda i: (i, 0))
    full = lambda a: pl.BlockSpec(a.shape, lambda i: (0,) * a.ndim)
    ng, n2 = ng.reshape(1, hd), n2.reshape(1, d)
    x_off = (x.shape[0] - n_rows) // ROW_TILE
    return pl.pallas_call(
        _outproj_kernel,
        out_shape=[jax.ShapeDtypeStruct((n_rows, d), F32),
                   jax.ShapeDtypeStruct((n_rows, d), F32),
                   jax.ShapeDtypeStruct((n_rows, LANES), F32)],
        grid=(n_rows // ROW_TILE,),
        in_specs=[pl.BlockSpec((1, ROW_TILE, hgw), lambda i: (0, i + tile_off, 0)),
                  pl.BlockSpec((1, ROW_TILE, hgw), lambda i: (1, i + tile_off, 0)),
                  pl.BlockSpec((ROW_TILE, hgw), lambda i: (i + tile_off, 5)),
                  full(ng), row(att.shape[1]), row(dif.shape[1]), full(wo),
                  pl.BlockSpec((ROW_TILE, d), lambda i: (i + x_off, 0)),
                  _mod_spec(d, nct, tile_off), full(n2), full(rw)],
        out_specs=[row(d), row(d), row(LANES)],
        compiler_params=_params(("arbitrary",), 56),
        name="out_proj",
    )(o, o, p, ng, att, dif, wo, x, mods, n2, rw)


def _router_kernel(lg_ref, bias_ref, r_ref, cnt_ref, run_ref):
    i = pl.program_id(0)
    tm = lg_ref.shape[0]

    @pl.when(i == 0)
    def _():
        run_ref[...] = jnp.zeros_like(run_ref)

    lane = lax.broadcasted_iota(I32, (tm, LANES), 1).astype(F32)
    lg = lg_ref[...] + bias_ref[...]
    ninf = -jnp.inf

    def first_max(vals):
        mx = jnp.max(vals, axis=1, keepdims=True)
        idx = jnp.min(jnp.where(vals == mx, lane, float(LANES)), axis=1, keepdims=True)
        return mx, idx

    gl = jnp.where(lane < N_GROUPS, lg, ninf)
    gmax, gidx = first_max(gl)
    g_top = 1.0 / jnp.sum(jnp.exp(gl - gmax), axis=1, keepdims=True)
    lo = N_GROUPS + EXPERTS_PER_GROUP * gidx
    el = jnp.where((lane >= lo) & (lane < lo + EXPERTS_PER_GROUP), lg, ninf)
    m1, e1 = first_max(el)
    m2, e2 = first_max(jnp.where(lane == e1, ninf, el))
    r = jnp.exp(m2 - m1)
    w1 = g_top / (1.0 + r)
    w2 = g_top * r / (1.0 + r)

    hit = ((lane == e1) | (lane == e2)).astype(BF16)
    ti = lax.broadcasted_iota(I32, (tm, tm), 0)
    si = lax.broadcasted_iota(I32, (tm, tm), 1)
    before = (si < ti).astype(BF16)
    pos = jnp.dot(before, hit, preferred_element_type=F32) + run_ref[0:1, :]
    p1 = jnp.sum(jnp.where(lane == e1, pos, 0.0), axis=1, keepdims=True)
    p2 = jnp.sum(jnp.where(lane == e2, pos, 0.0), axis=1, keepdims=True)
    total = run_ref[0:1, :] + jnp.sum(hit.astype(F32), axis=0, keepdims=True)
    run_ref[...] = jnp.broadcast_to(total, run_ref.shape)
    cnt_ref[...] = jnp.broadcast_to(total, cnt_ref.shape)

    fields = (e1 - N_GROUPS, e2 - N_GROUPS, w1, w2, p1, p2)
    out = jnp.zeros((tm, LANES), F32)
    for n, f in enumerate(fields):
        out = jnp.where(lane == n, f, out)
    r_ref[...] = out


def _router_call(logits, bias):
    n = logits.shape[0]
    return pl.pallas_call(
        _router_kernel,
        out_shape=[jax.ShapeDtypeStruct((n, LANES), F32), jax.ShapeDtypeStruct((8, LANES), F32)],
        grid=(n // ROW_TILE,),
        in_specs=[pl.BlockSpec((ROW_TILE, LANES), lambda i: (i, 0)),
                  pl.BlockSpec((1, LANES), lambda i: (0, 0))],
        out_specs=[pl.BlockSpec((ROW_TILE, LANES), lambda i: (i, 0)),
                   pl.BlockSpec((8, LANES), lambda i: (0, 0))],
        scratch_shapes=[pltpu.VMEM((8, LANES), F32)],
        compiler_params=_params(("arbitrary",), 16),
        name="router",
    )(logits, bias)


def _moe_kernel(src_ref, te_ref, nt_ref, h_hbm, wg_ref, wu_ref, wd_ref, y_ref,
                xbuf, sem, wg_bf, wu_bf, wd_bf):
    i = pl.program_id(0)
    n_tiles = nt_ref[0]
    tm = xbuf.shape[1]

    def row_copy(tile, slot, r):
        return pltpu.make_async_copy(h_hbm.at[pl.ds(src_ref[tile * tm + r], 1), :],
                                     xbuf.at[slot, pl.ds(r, 1), :], sem.at[slot])

    def start_tile(tile, slot):
        def body(r, carry):
            row_copy(tile, slot, r).start()
            return carry
        lax.fori_loop(0, tm, body, 0)

    def wait_tile(tile, slot):
        def body(r, carry):
            row_copy(tile, slot, r).wait()
            return carry
        lax.fori_loop(0, tm, body, 0)

    @pl.when(i == 0)
    def _():
        start_tile(0, 0)

    @pl.when(i < n_tiles)
    def _():
        slot = i % 2

        @pl.when(i + 1 < n_tiles)
        def _():
            start_tile(i + 1, 1 - slot)

        changed = jnp.logical_or(i == 0, te_ref[i] != te_ref[jnp.maximum(i - 1, 0)])

        @pl.when(changed)
        def _():
            wg_bf[...] = wg_ref[0].astype(BF16)
            wu_bf[...] = wu_ref[0].astype(BF16)
            wd_bf[...] = wd_ref[0].astype(BF16)

        wait_tile(i, slot)
        xb = xbuf[slot].astype(BF16)
        gt = jnp.dot(xb, wg_bf[...], preferred_element_type=F32)
        up = jnp.dot(xb, wu_bf[...], preferred_element_type=F32)
        act = (gt * (1.0 / (1.0 + jnp.exp(-gt))) * up).astype(BF16)
        y_ref[...] = jnp.dot(act, wd_bf[...], preferred_element_type=F32)

    @pl.when(i >= n_tiles)
    def _():
        y_ref[...] = jnp.zeros_like(y_ref)


def _moe_call(h2, src, tile_expert, n_tiles, wg, wu, wd):
    d = h2.shape[1]
    ff = wg.shape[2]
    max_tiles = tile_expert.shape[0]
    tm = MOE_TILE
    wspec = lambda s: pl.BlockSpec((1,) + s, lambda i, src, te, nt: (te[i], 0, 0))
    grid_spec = pltpu.PrefetchScalarGridSpec(
        num_scalar_prefetch=3,
        grid=(max_tiles,),
        in_specs=[pl.BlockSpec(memory_space=pl.ANY), wspec((d, ff)), wspec((d, ff)), wspec((ff, d))],
        out_specs=pl.BlockSpec((tm, d), lambda i, src, te, nt: (i, 0)),
        scratch_shapes=[pltpu.VMEM((2, tm, d), F32), pltpu.SemaphoreType.DMA((2,)),
                        pltpu.VMEM((d, ff), BF16), pltpu.VMEM((d, ff), BF16), pltpu.VMEM((ff, d), BF16)])
    return pl.pallas_call(
        _moe_kernel,
        out_shape=jax.ShapeDtypeStruct((max_tiles * tm, d), F32),
        grid_spec=grid_spec,
        compiler_params=_params(("arbitrary",), 56),
        name="moe_experts",
    )(src, tile_expert, n_tiles, h2, wg, wu, wd)


def _combine_kernel(dst_ref, x_ref, rt_ref, m_ref, g_ref, mn_ref, y_hbm, *rest, last, n_out):
    outs, (ybuf, sem) = rest[:n_out], rest[n_out:]
    i = pl.program_id(0)
    tm = x_ref.shape[0]

    def row_copy(r, j):
        return pltpu.make_async_copy(y_hbm.at[pl.ds(dst_ref[(i * tm + r) * 2 + j], 1), :],
                                     ybuf.at[j, pl.ds(r, 1), :], sem.at[0])

    def start(r, carry):
        row_copy(r, 0).start()
        row_copy(r, 1).start()
        return carry

    def wait(r, carry):
        row_copy(r, 0).wait()
        row_copy(r, 1).wait()
        return carry

    lax.fori_loop(0, tm, start, 0)
    lax.fori_loop(0, tm, wait, 0)
    rt = rt_ref[...]
    y = rt[:, 2:3] * ybuf[0] + rt[:, 3:4] * ybuf[1]
    xn = x_ref[...] + m_ref[0][5:6] * y
    if last:
        outs[0][...] = xn * lax.rsqrt(jnp.mean(xn * xn, axis=-1, keepdims=True) + EPS) * g_ref[...]
    else:
        outs[0][...] = xn
        mn = mn_ref[0]
        outs[1][...] = _norm_mod(xn, g_ref[...], mn[0:1], mn[1:2]).astype(BF16)


def _combine_call(dst, x, route, mods, g, mods_next, ys, nct, last):
    n, d = x.shape
    tm = ROW_TILE
    mspec = pl.BlockSpec((1, N_MOD, d), lambda i, dst: (jnp.where(i < nct, 1, 0), 0, 0))
    out_shape = [jax.ShapeDtypeStruct((n, d), F32)]
    if not last:
        out_shape.append(jax.ShapeDtypeStruct((n, d), BF16))
    row = lambda w: pl.BlockSpec((tm, w), lambda i, dst: (i, 0))
    grid_spec = pltpu.PrefetchScalarGridSpec(
        num_scalar_prefetch=1,
        grid=(n // tm,),
        in_specs=[row(d), row(LANES), mspec, pl.BlockSpec((1, d), lambda i, dst: (0, 0)), mspec,
                  pl.BlockSpec(memory_space=pl.ANY)],
        out_specs=[row(d)] * len(out_shape),
        scratch_shapes=[pltpu.VMEM((2, tm, d), F32), pltpu.SemaphoreType.DMA((1,))])
    return pl.pallas_call(
        functools.partial(_combine_kernel, last=last, n_out=len(out_shape)),
        out_shape=out_shape,
        grid_spec=grid_spec,
        compiler_params=_params(("arbitrary",), 32),
        name="moe_combine",
    )(dst, x, route, mods, g.reshape(1, d), mods_next, ys)


def _dispatch_plan(route, counts, n_tokens):
    tm = MOE_TILE
    max_tiles = (2 * n_tokens) // tm + N_EXPERTS
    cnt = counts[0, N_GROUPS:N_GROUPS + N_EXPERTS].astype(I32)
    tiles_per = (cnt + tm - 1) // tm
    tile_end = jnp.cumsum(tiles_per)
    offs = (tile_end - tiles_per) * tm
    eid = route[:, 0:2].astype(I32)
    dst = offs[eid] + route[:, 4:6].astype(I32)
    tok = jnp.broadcast_to(jnp.arange(n_tokens, dtype=I32)[:, None], (n_tokens, 2))
    src = jnp.zeros((max_tiles * tm,), I32).at[dst.reshape(-1)].set(tok.reshape(-1))
    n_tiles = tile_end[-1:]
    tile_ids = jnp.minimum(jnp.arange(max_tiles, dtype=I32), n_tiles[0] - 1)
    tile_expert = jnp.searchsorted(tile_end, tile_ids, side="right").astype(I32)
    return dst.reshape(-1), src, tile_expert, n_tiles.astype(I32)


def kernel(x, c, ctx, c_ctx, w_mod, b_mod, norm1_g, norm2_g, w_in, w_out, hg_lb_logits, hg_norm_g,
           q_norm_g, k_norm_g, lam_q1, lam_k1, lam_q2, lam_k2, diff_norm_g, router_group_w,
           router_group_b, router_expert_w, router_expert_b, w_gate, w_up, w_down, final_norm_g):
    depth = w_in.shape[0]
    n_lat, d = x.shape[1], x.shape[2]
    n_ctx = ctx.shape[1]
    nct = n_ctx // ROW_TILE
    assert x.shape[0] == 1 and n_ctx % ROW_TILE == 0 and n_lat % 1024 == 0

    xall = jnp.concatenate([ctx[0], x[0]], axis=0)
    cc = jnp.zeros((8, d), F32).at[0].set(c[0]).at[1].set(c_ctx)
    mods_all = _mod_call(cc, w_mod, b_mod)[:, :2].reshape(depth, 2, N_MOD, d)
    tabs = _rope_tables(n_lat, n_ctx, HEAD_DIM) + _rope_tables(n_lat, n_ctx, HEAD_DIM // 2)

    h1 = _prenorm_call(xall, norm1_g[0], mods_all[0], nct)
    out = None
    for l in range(depth):
        last = l == depth - 1
        mods = mods_all[l]
        tile_off = nct if last else 0
        tm_in = 768 if (n_ctx + n_lat) % 768 == 0 else ROW_TILE
        p = _matmul_call(h1, w_in[l].astype(BF16), tm_in, 1024)
        o = _hgrn_call(p, hg_lb_logits, l, nct)
        qa, ka, va, qd, kd, vd = _attn_prep_call(p, tabs, q_norm_g[l], k_norm_g[l])
        att = _flash_call(qa, ka, va, n_ctx, GQA_HEADS // GQA_KV_HEADS, tile_off)
        lam_init = 0.8 - 0.6 * math.exp(-0.3 * l)
        dif = _flash_call(qd, kd, vd, n_ctx, 2, tile_off,
                          extra=(lam_q1[l], lam_k1[l], lam_q2[l], lam_k2[l], diff_norm_g[l]),
                          lam_init=lam_init)
        rw = jnp.concatenate([router_group_w[l], router_expert_w[l],
                              jnp.zeros((d, LANES - N_GROUPS - N_EXPERTS), F32)], axis=1)
        rb = jnp.concatenate([router_group_b[l], router_expert_b[l],
                              jnp.zeros((LANES - N_GROUPS - N_EXPERTS,), F32)]).reshape(1, LANES)
        xn, h2, logits = _outproj_call(o, p, hg_norm_g[l], att, dif, w_out[l].astype(BF16), xall,
                                       mods, norm2_g[l], rw, nct, tile_off)
        n_tok = xn.shape[0]
        route, counts = _router_call(logits, rb)
        dst, src, tile_expert, n_tiles = _dispatch_plan(route, counts, n_tok)
        ys = _moe_call(h2, src, tile_expert, n_tiles, w_gate[l], w_up[l], w_down[l])
        if last:
            (out,) = _combine_call(dst, xn, route, mods, final_norm_g, mods, ys, 0, True)
        else:
            xall, h1 = _combine_call(dst, xn, route, mods, norm1_g[l + 1], mods_all[l + 1], ys, nct, False)
    return out.reshape(1, n_lat, d)
```

```python
import functools
import math

import numpy as np
import jax
import jax.numpy as jnp
from jax import lax
from jax.experimental import pallas as pl
from jax.experimental.pallas import tpu as pltpu

F32 = jnp.float32
BF16 = jnp.bfloat16
I32 = jnp.int32

HEAD_DIM = 128
LANES = 128
GRID_W = 64
ROPE_THETA = 10000.0
EPS = 1e-6
HG_HEADS = 4
HG_CHUNK = 64
GQA_HEADS = 8
GQA_KV_HEADS = 2
DIFF_HEADS = 4
N_GROUPS = 4
EXPERTS_PER_GROUP = 8
N_EXPERTS = N_GROUPS * EXPERTS_PER_GROUP
N_MOD = 6
ROW_TILE = 256
MOE_TILE = 256
MIB = 1024 * 1024

_LOG2E = math.log2(math.e)
_NT = (((1,), (1,)), ((), ()))
_TN = (((0,), (0,)), ((), ()))


def _params(semantics, vmem_mib):
    return pltpu.CompilerParams(dimension_semantics=semantics, vmem_limit_bytes=vmem_mib * MIB)


def _split_bf16(x, parts):
    out = []
    for _ in range(parts - 1):
        p = x.astype(BF16)
        out.append(p)
        x = x - p.astype(F32)
    out.append(x.astype(BF16))
    return out


def _mod_kernel(a_ref, w_ref, b_ref, o_ref):
    a = a_ref[...]
    a = a * (1.0 / (1.0 + jnp.exp(-a)))
    hi, lo = _split_bf16(a, 2)
    w = w_ref[0].astype(BF16)
    o_ref[0] = (jnp.dot(hi, w, preferred_element_type=F32)
                + jnp.dot(lo, w, preferred_element_type=F32) + b_ref[0])


def _mod_call(cc, w_mod, b_mod):
    depth, d, n = w_mod.shape
    tn = 1024
    return pl.pallas_call(
        _mod_kernel,
        out_shape=jax.ShapeDtypeStruct((depth, 8, n), F32),
        grid=(depth, n // tn),
        in_specs=[pl.BlockSpec((8, d), lambda l, j: (0, 0)),
                  pl.BlockSpec((1, d, tn), lambda l, j: (l, 0, j)),
                  pl.BlockSpec((1, 1, tn), lambda l, j: (l, 0, j))],
        out_specs=pl.BlockSpec((1, 8, tn), lambda l, j: (l, 0, j)),
        compiler_params=_params(("arbitrary", "arbitrary"), 40),
        name="mod_vectors",
    )(cc, w_mod, b_mod.reshape(depth, 1, n))


def _norm_mod(x, g, shift, scale):
    y = x * lax.rsqrt(jnp.mean(x * x, axis=-1, keepdims=True) + EPS) * g
    return y * (1.0 + scale) + shift


def _prenorm_kernel(x_ref, g_ref, m_ref, o_ref):
    m = m_ref[0]
    o_ref[...] = _norm_mod(x_ref[...], g_ref[...], m[0:1], m[1:2]).astype(BF16)


def _mod_spec(d, nct, tile_off=0):
    return pl.BlockSpec((1, N_MOD, d), lambda i: (jnp.where(i + tile_off < nct, 1, 0), 0, 0))


def _prenorm_call(x, g, mods, nct):
    t, d = x.shape
    return pl.pallas_call(
        _prenorm_kernel,
        out_shape=jax.ShapeDtypeStruct((t, d), BF16),
        grid=(t // ROW_TILE,),
        in_specs=[pl.BlockSpec((ROW_TILE, d), lambda i: (i, 0)),
                  pl.BlockSpec((1, d), lambda i: (0, 0)),
                  _mod_spec(d, nct)],
        out_specs=pl.BlockSpec((ROW_TILE, d), lambda i: (i, 0)),
        compiler_params=_params(("arbitrary",), 24),
        name="prenorm",
    )(x, g.reshape(1, d), mods)


def _mm_kernel(a_ref, b_ref, o_ref):
    o_ref[...] = jnp.dot(a_ref[...], b_ref[...], preferred_element_type=F32)


def _matmul_call(a, b, tm, tn):
    m, k = a.shape
    n = b.shape[1]
    return pl.pallas_call(
        _mm_kernel,
        out_shape=jax.ShapeDtypeStruct((m, n), F32),
        grid=(m // tm, n // tn),
        in_specs=[pl.BlockSpec((tm, k), lambda i, j: (i, 0)),
                  pl.BlockSpec((k, tn), lambda i, j: (0, j))],
        out_specs=pl.BlockSpec((tm, tn), lambda i, j: (i, j)),
        compiler_params=_params(("arbitrary", "arbitrary"), 40),
        name="in_proj",
    )(a, b)


_HG_LEVELS = (1, 2, 4, 8, 16, 32)
_HG_TOT_ROW = HG_CHUNK * (len(_HG_LEVELS) + 1)
_HG_W_ROWS = _HG_TOT_ROW + 16


def _hgrn_consts():
    c = HG_CHUNK
    w = np.zeros((2, _HG_W_ROWS, c), np.float32)
    msk = np.zeros((2, len(_HG_LEVELS) + 1, c, c), np.float32)
    for d in range(2):
        u = np.arange(c) if d == 0 else c - 1 - np.arange(c)
        ut, us = u[:, None], u[None, :]
        w[d, :c] = us <= ut
        for li, lv in enumerate(_HG_LEVELS):
            blk = u // (2 * lv)
            qside = (u % (2 * lv)) >= lv
            bnd = (blk * 2 * lv + lv - 1)[:, None]
            wq = (us > bnd) & (us <= ut)
            wk = (us > ut) & (us <= bnd)
            w[d, c * (li + 1):c * (li + 2)] = np.where(qside[:, None], wq, -1.0 * wk)
            msk[d, li] = (blk[:, None] == blk[None, :]) & qside[:, None] & ~qside[None, :]
        msk[d, len(_HG_LEVELS)] = np.eye(c)
        w[d, _HG_TOT_ROW:] = 1.0
    return jnp.asarray(np.concatenate([w, w, w], axis=2), BF16), jnp.asarray(msk, F32)


def _hgrn_kernel(lbl_ref, q_ref, z_ref, v_ref, w_ref, msk_ref, o_ref, st_ref, *, layer, chunks):
    c = HG_CHUNK
    d = pl.program_id(0)
    j = pl.program_id(2)

    @pl.when(j == 0)
    def _():
        st_ref[...] = jnp.zeros_like(st_ref)

    lbl = lbl_ref[...]
    rows = [lbl[i:i + 1] for i in range(lbl.shape[0])]
    mx = functools.reduce(jnp.maximum, rows)
    ex = [jnp.exp(r - mx) for r in rows]
    tot = functools.reduce(lambda a, b: a + b, ex)
    lb = jnp.zeros_like(mx)
    for i in range(1, layer + 1):
        lb = lb + ex[i] / tot
    log_lb = jnp.log(lb)
    log_1m_lb = jnp.log1p(-lb)

    wmat = w_ref[0]
    t_idx = lax.broadcasted_iota(I32, (c, HEAD_DIM), 0)
    u_idx = t_idx + d * (c - 1 - 2 * t_idx)

    for ci in range(chunks):
        cc = ci + d * (chunks - 1 - 2 * ci)
        r0 = pl.multiple_of(cc * c, c)
        q = q_ref[pl.ds(r0, c), :]
        z = z_ref[pl.ds(r0, c), :]
        v = v_ref[pl.ds(r0, c), :]
        l1p = jnp.log1p(jnp.exp(-jnp.abs(z)))
        ls_pos = jnp.minimum(z, 0.0) - l1p
        ls_neg = jnp.minimum(-z, 0.0) - l1p
        a2 = log_1m_lb + ls_pos
        logf = jnp.maximum(log_lb, a2) + jnp.log1p(jnp.exp(-jnp.abs(log_lb - a2)))
        k = (1.0 - lb) * jnp.exp(ls_neg)

        parts = jnp.concatenate(_split_bf16(logf, 3), axis=0)
        sums = jnp.dot(wmat, parts, preferred_element_type=F32)
        b = sums[0:c]
        btot = sums[_HG_TOT_ROW:_HG_TOT_ROW + 1]

        q_bf = q.astype(BF16)
        k_bf = k.astype(BF16)
        v_bf = v.astype(BF16)
        nlev = len(_HG_LEVELS)
        amat = lax.dot_general(q_bf, k_bf, _NT, preferred_element_type=F32) * msk_ref[0, nlev]
        for li, lv in enumerate(_HG_LEVELS):
            fl = jnp.exp(-jnp.abs(sums[c * (li + 1):c * (li + 2)]))
            qside = (u_idx & (2 * lv - 1)) >= lv
            ql = jnp.where(qside, q * fl, 0.0).astype(BF16)
            kl = jnp.where(qside, 0.0, k * fl).astype(BF16)
            amat = amat + lax.dot_general(ql, kl, _NT, preferred_element_type=F32) * msk_ref[0, li]

        st = st_ref[...]
        qe = (q * jnp.exp(b)).astype(BF16)
        o = (jnp.dot(amat.astype(BF16), v_bf, preferred_element_type=F32)
             + lax.dot_general(qe, st.astype(BF16), _NT, preferred_element_type=F32))
        o_ref[0, pl.ds(r0, c), :] = o
        kd = (k * jnp.exp(btot - b)).astype(BF16)
        st_ref[...] = st * jnp.exp(btot) + lax.dot_general(v_bf, kd, _TN, preferred_element_type=F32)


def _hgrn_call(p, lb_logits, layer, nct):
    t = p.shape[0]
    nblk = t // ROW_TILE
    chunks = ROW_TILE // HG_CHUNK
    wmat, msk = _hgrn_consts()

    def blk(d, j):
        back = jnp.where(j < nct, nct - 1 - j, nblk - 1 - (j - nct))
        return jnp.where(d == 0, j, back)

    return pl.pallas_call(
        functools.partial(_hgrn_kernel, layer=layer, chunks=chunks),
        out_shape=jax.ShapeDtypeStruct((2, t, HG_HEADS * HEAD_DIM), F32),
        grid=(2, HG_HEADS, nblk),
        in_specs=[pl.BlockSpec((lb_logits.shape[0], HEAD_DIM), lambda d, h, j: (0, h)),
                  pl.BlockSpec((ROW_TILE, HEAD_DIM), lambda d, h, j: (blk(d, j), d * HG_HEADS + h)),
                  pl.BlockSpec((ROW_TILE, HEAD_DIM), lambda d, h, j: (blk(d, j), (2 + d) * HG_HEADS + h)),
                  pl.BlockSpec((ROW_TILE, HEAD_DIM), lambda d, h, j: (blk(d, j), 4 * HG_HEADS + h)),
                  pl.BlockSpec((1, _HG_W_ROWS, 3 * HG_CHUNK), lambda d, h, j: (d, 0, 0)),
                  pl.BlockSpec((1, len(_HG_LEVELS) + 1, HG_CHUNK, HG_CHUNK), lambda d, h, j: (d, 0, 0, 0))],
        out_specs=pl.BlockSpec((1, ROW_TILE, HEAD_DIM), lambda d, h, j: (d, blk(d, j), h)),
        scratch_shapes=[pltpu.VMEM((HEAD_DIM, HEAD_DIM), F32)],
        compiler_params=_params(("arbitrary", "arbitrary", "arbitrary"), 24),
        name="hgrn_scan",
    )(lb_logits, p, p, p, wmat, msk)


def _rope_tables(n_lat, n_ctx, dim):
    rows = n_lat // GRID_W
    row = jnp.repeat(jnp.arange(rows, dtype=F32), GRID_W)
    col = jnp.tile(jnp.arange(GRID_W, dtype=F32), rows)
    axis_dim = dim // 2
    inv_freq = ROPE_THETA ** (-jnp.arange(0, axis_dim, 2, dtype=F32) / axis_dim)
    ang = jnp.concatenate([row[:, None] * inv_freq, col[:, None] * inv_freq], axis=-1)
    cos = jnp.repeat(jnp.cos(ang), 2, axis=1)
    sin = jnp.repeat(jnp.sin(ang), 2, axis=1) * jnp.tile(jnp.asarray([-1.0, 1.0], F32), dim // 2)
    reps = LANES // dim
    cos = jnp.concatenate([jnp.ones((n_ctx, dim), F32), cos], axis=0)
    sin = jnp.concatenate([jnp.zeros((n_ctx, dim), F32), sin], axis=0)
    return jnp.tile(cos, (1, reps)), jnp.tile(sin, (1, reps))


def _rope(x, cos, sin):
    lane = lax.broadcasted_iota(I32, x.shape, 1)
    swapped = jnp.where((lane & 1) == 0, pltpu.roll(x, LANES - 1, 1), pltpu.roll(x, 1, 1))
    return x * cos + swapped * sin


def _head_norm(x, g):
    return x * lax.rsqrt(jnp.mean(x * x, axis=-1, keepdims=True) + EPS) * g


def _attn_prep_kernel(pq_ref, pkv_ref, pd_ref, ca_ref, sa_ref, cd_ref, sd_ref, qn_ref, kn_ref,
                      qa_ref, ka_ref, va_ref, qd_ref, kd_ref, vd_ref):
    hd = HEAD_DIM
    ca, sa, cd, sd = ca_ref[...], sa_ref[...], cd_ref[...], sd_ref[...]
    qn, kn = qn_ref[...], kn_ref[...]
    for h in range(GQA_HEADS):
        xq = _rope(_head_norm(pq_ref[:, h * hd:(h + 1) * hd], qn), ca, sa)
        qa_ref[:, h * hd:(h + 1) * hd] = (xq * (hd ** -0.5 * _LOG2E)).astype(BF16)
    for h in range(GQA_KV_HEADS):
        xk = _rope(_head_norm(pkv_ref[:, h * hd:(h + 1) * hd], kn), ca, sa)
        ka_ref[:, h * hd:(h + 1) * hd] = xk.astype(BF16)
    va_ref[...] = pkv_ref[:, GQA_KV_HEADS * hd:].astype(BF16)
    first = lax.broadcasted_iota(I32, (pq_ref.shape[0], hd), 1) < hd // 2
    dw = DIFF_HEADS * hd
    for h in range(DIFF_HEADS):
        xq = _rope(pd_ref[:, h * hd:(h + 1) * hd], cd, sd) * ((hd // 2) ** -0.5 * _LOG2E)
        qd_ref[:, (2 * h) * hd:(2 * h + 1) * hd] = jnp.where(first, xq, 0.0).astype(BF16)
        qd_ref[:, (2 * h + 1) * hd:(2 * h + 2) * hd] = jnp.where(first, 0.0, xq).astype(BF16)
        xk = _rope(pd_ref[:, dw + h * hd:dw + (h + 1) * hd], cd, sd)
        kd_ref[:, h * hd:(h + 1) * hd] = xk.astype(BF16)
    vd_ref[...] = pd_ref[:, 2 * dw:].astype(BF16)


def _attn_prep_call(p, tabs, qn, kn):
    t = p.shape[0]
    hd = HEAD_DIM
    gq, gkv, dw = GQA_HEADS * hd, 2 * GQA_KV_HEADS * hd, DIFF_HEADS * hd
    q_off = 6 * HG_HEADS * hd
    assert q_off % gq == 0 and (q_off + gq) % gkv == 0 and (q_off + gq + gkv) % (3 * dw) == 0
    row = lambda w: pl.BlockSpec((ROW_TILE, w), lambda i: (i, 0))
    vec = pl.BlockSpec((1, hd), lambda i: (0, 0))
    return pl.pallas_call(
        _attn_prep_kernel,
        out_shape=[jax.ShapeDtypeStruct((t, gq), BF16),
                   jax.ShapeDtypeStruct((t, gkv // 2), BF16),
                   jax.ShapeDtypeStruct((t, gkv // 2), BF16),
                   jax.ShapeDtypeStruct((t, 2 * dw), BF16),
                   jax.ShapeDtypeStruct((t, dw), BF16),
                   jax.ShapeDtypeStruct((t, dw), BF16)],
        grid=(t // ROW_TILE,),
        in_specs=[pl.BlockSpec((ROW_TILE, gq), lambda i: (i, q_off // gq)),
                  pl.BlockSpec((ROW_TILE, gkv), lambda i: (i, (q_off + gq) // gkv)),
                  pl.BlockSpec((ROW_TILE, 3 * dw), lambda i: (i, (q_off + gq + gkv) // (3 * dw))),
                  row(hd), row(hd), row(hd), row(hd), vec, vec],
        out_specs=[row(gq), row(gkv // 2), row(gkv // 2), row(2 * dw), row(dw), row(dw)],
        compiler_params=_params(("arbitrary",), 24),
        name="attn_prep",
    )(p, p, p, *tabs, qn.reshape(1, hd), kn.reshape(1, hd))


def _flash_kernel(*refs, g, nh, n_ctx, n_lat, tkl, nct, tile_off, diff, lam_init):
    if diff:
        q_ref, k_ref, v_ref, lq1, lk1, lq2, lk2, dn_ref, o_ref = refs[:9]
    else:
        q_ref, k_ref, v_ref, o_ref = refs[:4]
    nc = nh * g
    m_sc, l_sc, acc_sc, p_sc = (refs[len(refs) - (4 - n) * nc:len(refs) - (3 - n) * nc] for n in range(4))
    hd = HEAD_DIM
    i = pl.program_id(1) + tile_off
    nblk = n_lat // tkl
    chains = [(c, c // g) for c in range(nh * g)]

    def head(ref, rows, h):
        return ref[rows, h * hd:(h + 1) * hd]

    ctx_rows = slice(0, n_ctx)
    for c, kh in chains:
        s = lax.dot_general(head(q_ref, slice(None), c), head(k_ref, ctx_rows, kh), _NT,
                            preferred_element_type=F32)
        m0 = jnp.max(s, axis=1, keepdims=True)
        p = jnp.exp2(s - m0)
        m_sc[c][...] = jnp.broadcast_to(m0, m_sc[c].shape)
        l_sc[c][...] = jnp.broadcast_to(jnp.sum(p, axis=1, keepdims=True), l_sc[c].shape)
        acc_sc[c][...] = jnp.dot(p.astype(BF16), head(v_ref, ctx_rows, kh), preferred_element_type=F32)

    def probs(c, kh, rows):
        s = lax.dot_general(head(q_ref, slice(None), c), head(k_ref, rows, kh), _NT,
                            preferred_element_type=F32)
        m_prev = m_sc[c][...]
        m_new = jnp.maximum(m_prev, jnp.max(s, axis=1, keepdims=True))
        p = jnp.exp2(s - jnp.tile(m_new, (1, tkl // LANES)))
        alpha = jnp.exp2(m_prev - m_new)
        l_sc[c][...] = alpha * l_sc[c][...] + jnp.sum(p, axis=1, keepdims=True)
        m_sc[c][...] = m_new
        return p.astype(BF16), alpha

    @pl.when(i >= nct)
    def _():
        for c, kh in chains:
            p, alpha = probs(c, kh, pl.ds(n_ctx, tkl))
            p_sc[c][...] = p
            acc_sc[c][...] = alpha * acc_sc[c][...]

        def body(jb, carry):
            r0 = pl.multiple_of(n_ctx + jb * tkl, LANES)
            prev = pl.ds(r0 - tkl, tkl)
            for c, kh in chains:
                pv = jnp.dot(p_sc[c][...], head(v_ref, prev, kh), preferred_element_type=F32)
                p, alpha = probs(c, kh, pl.ds(r0, tkl))
                p_sc[c][...] = p
                acc_sc[c][...] = alpha * (acc_sc[c][...] + pv)
            return carry
        lax.fori_loop(1, nblk, body, 0)

        last = pl.ds(n_ctx + (nblk - 1) * tkl, tkl)
        for c, kh in chains:
            acc_sc[c][...] = acc_sc[c][...] + jnp.dot(p_sc[c][...], head(v_ref, last, kh),
                                                      preferred_element_type=F32)

    if diff:
        lam = (jnp.exp(jnp.sum(lq1[...] * lk1[...], axis=1, keepdims=True))
               - jnp.exp(jnp.sum(lq2[...] * lk2[...], axis=1, keepdims=True)) + lam_init)
        for kh in range(nh):
            dd = (acc_sc[2 * kh][...] / l_sc[2 * kh][...]
                  - lam * (acc_sc[2 * kh + 1][...] / l_sc[2 * kh + 1][...]))
            o_ref[:, kh * hd:(kh + 1) * hd] = (_head_norm(dd, dn_ref[...]) * (1.0 - lam_init)).astype(BF16)
    else:
        for c, _ in chains:
            o_ref[:, c * hd:(c + 1) * hd] = (acc_sc[c][...] / l_sc[c][...]).astype(BF16)


def _flash_call(q, k, v, n_ctx, g, nh, tile_off, extra=None, lam_init=0.0):
    t = k.shape[0]
    hd = HEAD_DIM
    n_kv = k.shape[1] // (nh * hd)
    n_lat = t - n_ctx
    nct = n_ctx // ROW_TILE
    tkl = 1024
    diff = extra is not None
    ow = nh * hd if diff else nh * g * hd
    in_specs = [pl.BlockSpec((ROW_TILE, nh * g * hd), lambda kv, i: (i + tile_off, kv)),
                pl.BlockSpec((t, nh * hd), lambda kv, i: (0, kv)),
                pl.BlockSpec((t, nh * hd), lambda kv, i: (0, kv))]
    args = [q, k, v]
    if diff:
        for a in extra:
            a = a.reshape(1, -1)
            in_specs.append(pl.BlockSpec(a.shape, lambda kv, i: (0, 0)))
            args.append(a)
    return pl.pallas_call(
        functools.partial(_flash_kernel, g=g, nh=nh, n_ctx=n_ctx, n_lat=n_lat, tkl=tkl, nct=nct,
                          tile_off=tile_off, diff=diff, lam_init=lam_init),
        out_shape=jax.ShapeDtypeStruct((t - tile_off * ROW_TILE, n_kv * ow), BF16),
        grid=(n_kv, t // ROW_TILE - tile_off),
        in_specs=in_specs,
        out_specs=pl.BlockSpec((ROW_TILE, ow), lambda kv, i: (i, kv)),
        scratch_shapes=([pltpu.VMEM((ROW_TILE, LANES), F32)] * (2 * nh * g)
                        + [pltpu.VMEM((ROW_TILE, hd), F32)] * (nh * g)
                        + [pltpu.VMEM((ROW_TILE, tkl), BF16)] * (nh * g)),
        compiler_params=_params(("arbitrary", "arbitrary"), 48),
        name="diff_attention" if diff else "gqa_attention",
    )(*args)


def _outproj_kernel(of_ref, ob_ref, gate_ref, ng_ref, att_ref, dif_ref, wo_ref, x_ref, m_ref, n2_ref,
                    rw_ref, xo_ref, h2_ref, lg_ref):
    hd = HEAD_DIM
    hgw = HG_HEADS * hd
    ng = ng_ref[...]
    gate = gate_ref[...]
    silu_gate = gate * (1.0 / (1.0 + jnp.exp(-gate)))
    acc = jnp.dot(att_ref[...], wo_ref[hgw:hgw + att_ref.shape[1], :], preferred_element_type=F32)
    acc = acc + jnp.dot(dif_ref[...], wo_ref[hgw + att_ref.shape[1]:, :], preferred_element_type=F32)
    hg = []
    for h in range(HG_HEADS):
        o = of_ref[0, :, h * hd:(h + 1) * hd] + ob_ref[0, :, h * hd:(h + 1) * hd]
        hg.append((_head_norm(o, ng) * silu_gate[:, h * hd:(h + 1) * hd]).astype(BF16))
    acc = acc + jnp.dot(jnp.concatenate(hg, axis=1), wo_ref[0:hgw, :], preferred_element_type=F32)
    m = m_ref[0]
    xn = x_ref[...] + m[2:3] * acc
    xo_ref[...] = xn
    h2 = _norm_mod(xn, n2_ref[...], m[3:4], m[4:5])
    h2_ref[...] = h2
    hs = _split_bf16(h2, 3)
    ws = _split_bf16(rw_ref[...], 3)
    lg = jnp.zeros(lg_ref.shape, F32)
    for a, b in ((2, 0), (1, 1), (0, 2), (1, 0), (0, 1), (0, 0)):
        lg = lg + jnp.dot(hs[a], ws[b], preferred_element_type=F32)
    lg_ref[...] = lg


def _outproj_call(o, p, ng, att, dif, wo, x, mods, n2, rw, nct, tile_off):
    d = x.shape[1]
    hd = HEAD_DIM
    hgw = HG_HEADS * hd
    n_rows = att.shape[0]
    row = lambda w: pl.BlockSpec((ROW_TILE, w), lambda i: (i, 0))
    full = lambda a: pl.BlockSpec(a.shape, lambda i: (0,) * a.ndim)
    ng, n2 = ng.reshape(1, hd), n2.reshape(1, d)
    x_off = (x.shape[0] - n_rows) // ROW_TILE
    return pl.pallas_call(
        _outproj_kernel,
        out_shape=[jax.ShapeDtypeStruct((n_rows, d), F32),
                   jax.ShapeDtypeStruct((n_rows, d), F32),
                   jax.ShapeDtypeStruct((n_rows, LANES), F32)],
        grid=(n_rows // ROW_TILE,),
        in_specs=[pl.BlockSpec((1, ROW_TILE, hgw), lambda i: (0, i + tile_off, 0)),
                  pl.BlockSpec((1, ROW_TILE, hgw), lambda i: (1, i + tile_off, 0)),
                  pl.BlockSpec((ROW_TILE, hgw), lambda i: (i + tile_off, 5)),
                  full(ng), row(att.shape[1]), row(dif.shape[1]), full(wo),
                  pl.BlockSpec((ROW_TILE, d), lambda i: (i + x_off, 0)),
                  _mod_spec(d, nct, tile_off), full(n2), full(rw)],
        out_specs=[row(d), row(d), row(LANES)],
        compiler_params=_params(("arbitrary",), 56),
        name="out_proj",
    )(o, o, p, ng, att, dif, wo, x, mods, n2, rw)


def _router_kernel(lg_ref, bias_ref, r_ref, cnt_ref, run_ref):
    i = pl.program_id(0)
    tm = lg_ref.shape[0]

    @pl.when(i == 0)
    def _():
        run_ref[...] = jnp.zeros_like(run_ref)

    lane = lax.broadcasted_iota(I32, (tm, LANES), 1).astype(F32)
    lg = lg_ref[...] + bias_ref[...]
    ninf = -jnp.inf

    def first_max(vals):
        mx = jnp.max(vals, axis=1, keepdims=True)
        idx = jnp.min(jnp.where(vals == mx, lane, float(LANES)), axis=1, keepdims=True)
        return mx, idx

    gl = jnp.where(lane < N_GROUPS, lg, ninf)
    gmax, gidx = first_max(gl)
    g_top = 1.0 / jnp.sum(jnp.exp(gl - gmax), axis=1, keepdims=True)
    lo = N_GROUPS + EXPERTS_PER_GROUP * gidx
    el = jnp.where((lane >= lo) & (lane < lo + EXPERTS_PER_GROUP), lg, ninf)
    m1, e1 = first_max(el)
    m2, e2 = first_max(jnp.where(lane == e1, ninf, el))
    r = jnp.exp(m2 - m1)
    w1 = g_top / (1.0 + r)
    w2 = g_top * r / (1.0 + r)

    hit = ((lane == e1) | (lane == e2)).astype(BF16)
    ti = lax.broadcasted_iota(I32, (tm, tm), 0)
    si = lax.broadcasted_iota(I32, (tm, tm), 1)
    before = (si < ti).astype(BF16)
    pos = jnp.dot(before, hit, preferred_element_type=F32) + run_ref[0:1, :]
    p1 = jnp.sum(jnp.where(lane == e1, pos, 0.0), axis=1, keepdims=True)
    p2 = jnp.sum(jnp.where(lane == e2, pos, 0.0), axis=1, keepdims=True)
    total = run_ref[0:1, :] + jnp.sum(hit.astype(F32), axis=0, keepdims=True)
    run_ref[...] = jnp.broadcast_to(total, run_ref.shape)
    cnt_ref[...] = jnp.broadcast_to(total, cnt_ref.shape)

    fields = (e1 - N_GROUPS, e2 - N_GROUPS, w1, w2, p1, p2)
    out = jnp.zeros((tm, LANES), F32)
    for n, f in enumerate(fields):
        out = jnp.where(lane == n, f, out)
    r_ref[...] = out


def _router_call(logits, bias):
    n = logits.shape[0]
    return pl.pallas_call(
        _router_kernel,
        out_shape=[jax.ShapeDtypeStruct((n, LANES), F32), jax.ShapeDtypeStruct((8, LANES), F32)],
        grid=(n // ROW_TILE,),
        in_specs=[pl.BlockSpec((ROW_TILE, LANES), lambda i: (i, 0)),
                  pl.BlockSpec((1, LANES), lambda i: (0, 0))],
        out_specs=[pl.BlockSpec((ROW_TILE, LANES), lambda i: (i, 0)),
                   pl.BlockSpec((8, LANES), lambda i: (0, 0))],
        scratch_shapes=[pltpu.VMEM((8, LANES), F32)],
        compiler_params=_params(("arbitrary",), 16),
        name="router",
    )(logits, bias)


def _moe_kernel(src_ref, te_ref, nt_ref, h_hbm, wg_ref, wu_ref, wd_ref, y_ref,
                xbuf, sem, wg_bf, wu_bf, wd_bf):
    i = pl.program_id(0)
    n_tiles = nt_ref[0]
    tm = xbuf.shape[1]

    def row_copy(tile, slot, r):
        return pltpu.make_async_copy(h_hbm.at[pl.ds(src_ref[tile * tm + r], 1), :],
                                     xbuf.at[slot, pl.ds(r, 1), :], sem.at[slot])

    def start_tile(tile, slot):
        def body(r, carry):
            row_copy(tile, slot, r).start()
            return carry
        lax.fori_loop(0, tm, body, 0, unroll=8)

    def wait_tile(slot):
        pltpu.make_async_copy(h_hbm.at[pl.ds(0, tm), :], xbuf.at[slot], sem.at[slot]).wait()

    @pl.when(i == 0)
    def _():
        start_tile(0, 0)

    @pl.when(i < n_tiles)
    def _():
        slot = i % 2

        @pl.when(i + 1 < n_tiles)
        def _():
            start_tile(i + 1, 1 - slot)

        changed = jnp.logical_or(i == 0, te_ref[i] != te_ref[jnp.maximum(i - 1, 0)])

        @pl.when(changed)
        def _():
            wg_bf[...] = wg_ref[0, 0].astype(BF16)
            wu_bf[...] = wu_ref[0, 0].astype(BF16)
            wd_bf[...] = wd_ref[0, 0].astype(BF16)

        wait_tile(slot)
        xb = xbuf[slot].astype(BF16)
        gt = jnp.dot(xb, wg_bf[...], preferred_element_type=F32)
        up = jnp.dot(xb, wu_bf[...], preferred_element_type=F32)
        act = (gt * (1.0 / (1.0 + jnp.exp(-gt))) * up).astype(BF16)
        y_ref[...] = jnp.dot(act, wd_bf[...], preferred_element_type=F32)

    @pl.when(i >= n_tiles)
    def _():
        y_ref[...] = jnp.zeros_like(y_ref)


def _moe_call(h2, src, tile_expert, n_tiles, wg, wu, wd, layer):
    d = h2.shape[1]
    ff = wg.shape[3]
    max_tiles = tile_expert.shape[0]
    tm = MOE_TILE
    wspec = lambda s: pl.BlockSpec((1, 1) + s, lambda i, src, te, nt: (layer, te[i], 0, 0))
    grid_spec = pltpu.PrefetchScalarGridSpec(
        num_scalar_prefetch=3,
        grid=(max_tiles,),
        in_specs=[pl.BlockSpec(memory_space=pl.ANY), wspec((d, ff)), wspec((d, ff)), wspec((ff, d))],
        out_specs=pl.BlockSpec((tm, d), lambda i, src, te, nt: (i, 0)),
        scratch_shapes=[pltpu.VMEM((2, tm, d), F32), pltpu.SemaphoreType.DMA((2,)),
                        pltpu.VMEM((d, ff), BF16), pltpu.VMEM((d, ff), BF16), pltpu.VMEM((ff, d), BF16)])
    return pl.pallas_call(
        _moe_kernel,
        out_shape=jax.ShapeDtypeStruct((max_tiles * tm, d), F32),
        grid_spec=grid_spec,
        compiler_params=_params(("arbitrary",), 56),
        name="moe_experts",
    )(src, tile_expert, n_tiles, h2, wg, wu, wd)


def _combine_kernel(dst_ref, x_ref, rt_ref, m_ref, g_ref, mn_ref, y_hbm, *rest, last, n_out):
    outs, (ybuf, sem) = rest[:n_out], rest[n_out:]
    i = pl.program_id(0)
    tm = x_ref.shape[0]

    def start_tile(tile, slot):
        def body(r, carry):
            for j in range(2):
                pltpu.make_async_copy(y_hbm.at[pl.ds(dst_ref[(tile * tm + r) * 2 + j], 1), :],
                                      ybuf.at[slot, j, pl.ds(r, 1), :], sem.at[slot]).start()
            return carry
        lax.fori_loop(0, tm, body, 0, unroll=8)

    @pl.when(i == 0)
    def _():
        start_tile(0, 0)

    slot = i % 2

    @pl.when(i + 1 < pl.num_programs(0))
    def _():
        start_tile(i + 1, 1 - slot)

    for j in range(2):
        pltpu.make_async_copy(y_hbm.at[pl.ds(0, tm), :], ybuf.at[slot, j], sem.at[slot]).wait()
    rt = rt_ref[...]
    y = rt[:, 2:3] * ybuf[slot, 0] + rt[:, 3:4] * ybuf[slot, 1]
    xn = x_ref[...] + m_ref[0][5:6] * y
    if last:
        outs[0][...] = xn * lax.rsqrt(jnp.mean(xn * xn, axis=-1, keepdims=True) + EPS) * g_ref[...]
    else:
        outs[0][...] = xn
        mn = mn_ref[0]
        outs[1][...] = _norm_mod(xn, g_ref[...], mn[0:1], mn[1:2]).astype(BF16)


def _combine_call(dst, x, route, mods, g, mods_next, ys, nct, last):
    n, d = x.shape
    tm = ROW_TILE
    mspec = pl.BlockSpec((1, N_MOD, d), lambda i, dst: (jnp.where(i < nct, 1, 0), 0, 0))
    out_shape = [jax.ShapeDtypeStruct((n, d), F32)]
    if not last:
        out_shape.append(jax.ShapeDtypeStruct((n, d), BF16))
    row = lambda w: pl.BlockSpec((tm, w), lambda i, dst: (i, 0))
    grid_spec = pltpu.PrefetchScalarGridSpec(
        num_scalar_prefetch=1,
        grid=(n // tm,),
        in_specs=[row(d), row(LANES), mspec, pl.BlockSpec((1, d), lambda i, dst: (0, 0)), mspec,
                  pl.BlockSpec(memory_space=pl.ANY)],
        out_specs=[row(d)] * len(out_shape),
        scratch_shapes=[pltpu.VMEM((2, 2, tm, d), F32), pltpu.SemaphoreType.DMA((2,))])
    return pl.pallas_call(
        functools.partial(_combine_kernel, last=last, n_out=len(out_shape)),
        out_shape=out_shape,
        grid_spec=grid_spec,
        compiler_params=_params(("arbitrary",), 40),
        name="moe_combine",
    )(dst, x, route, mods, g.reshape(1, d), mods_next, ys)


def _dispatch_plan(route, counts, n_tokens):
    tm = MOE_TILE
    max_tiles = (2 * n_tokens) // tm + N_EXPERTS
    cnt = counts[0, N_GROUPS:N_GROUPS + N_EXPERTS].astype(I32)
    tiles_per = (cnt + tm - 1) // tm
    tile_end = jnp.cumsum(tiles_per)
    offs = (tile_end - tiles_per) * tm
    eid = route[:, 0:2].astype(I32)
    dst = offs[eid] + route[:, 4:6].astype(I32)
    tok = jnp.broadcast_to(jnp.arange(n_tokens, dtype=I32)[:, None], (n_tokens, 2))
    src = jnp.zeros((max_tiles * tm,), I32).at[dst.reshape(-1)].set(tok.reshape(-1))
    n_tiles = tile_end[-1:]
    tile_ids = jnp.minimum(jnp.arange(max_tiles, dtype=I32), n_tiles[0] - 1)
    tile_expert = jnp.sum((tile_end[None, :] <= tile_ids[:, None]).astype(I32), axis=1)
    return dst.reshape(-1), src, tile_expert, n_tiles.astype(I32)


def kernel(x, c, ctx, c_ctx, w_mod, b_mod, norm1_g, norm2_g, w_in, w_out, hg_lb_logits, hg_norm_g,
           q_norm_g, k_norm_g, lam_q1, lam_k1, lam_q2, lam_k2, diff_norm_g, router_group_w,
           router_group_b, router_expert_w, router_expert_b, w_gate, w_up, w_down, final_norm_g):
    depth = w_in.shape[0]
    n_lat, d = x.shape[1], x.shape[2]
    n_ctx = ctx.shape[1]
    nct = n_ctx // ROW_TILE
    assert x.shape[0] == 1 and n_ctx % ROW_TILE == 0 and n_lat % 1024 == 0

    xall = jnp.concatenate([ctx[0], x[0]], axis=0)
    cc = jnp.zeros((8, d), F32).at[0].set(c[0]).at[1].set(c_ctx)
    mods_all = _mod_call(cc, w_mod, b_mod)[:, :2].reshape(depth, 2, N_MOD, d)
    tabs = _rope_tables(n_lat, n_ctx, HEAD_DIM) + _rope_tables(n_lat, n_ctx, HEAD_DIM // 2)

    h1 = _prenorm_call(xall, norm1_g[0], mods_all[0], nct)
    out = None
    for l in range(depth):
        last = l == depth - 1
        mods = mods_all[l]
        tile_off = nct if last else 0
        tm_in = 768 if (n_ctx + n_lat) % 768 == 0 else ROW_TILE
        p = _matmul_call(h1, w_in[l].astype(BF16), tm_in, 1024)
        o = _hgrn_call(p, hg_lb_logits, l, nct)
        qa, ka, va, qd, kd, vd = _attn_prep_call(p, tabs, q_norm_g[l], k_norm_g[l])
        att = _flash_call(qa, ka, va, n_ctx, GQA_HEADS // GQA_KV_HEADS, 1, tile_off)
        lam_init = 0.8 - 0.6 * math.exp(-0.3 * l)
        dif = _flash_call(qd, kd, vd, n_ctx, 2, 2, tile_off,
                          extra=(lam_q1[l], lam_k1[l], lam_q2[l], lam_k2[l], diff_norm_g[l]),
                          lam_init=lam_init)
        rw = jnp.concatenate([router_group_w[l], router_expert_w[l],
                              jnp.zeros((d, LANES - N_GROUPS - N_EXPERTS), F32)], axis=1)
        rb = jnp.concatenate([router_group_b[l], router_expert_b[l],
                              jnp.zeros((LANES - N_GROUPS - N_EXPERTS,), F32)]).reshape(1, LANES)
        xn, h2, logits = _outproj_call(o, p, hg_norm_g[l], att, dif, w_out[l].astype(BF16), xall,
                                       mods, norm2_g[l], rw, nct, tile_off)
        n_tok = xn.shape[0]
        route, counts = _router_call(logits, rb)
        dst, src, tile_expert, n_tiles = _dispatch_plan(route, counts, n_tok)
        ys = _moe_call(h2, src, tile_expert, n_tiles, w_gate, w_up, w_down, l)
        if last:
            (out,) = _combine_call(dst, xn, route, mods, final_norm_g, mods, ys, 0, True)
        else:
            xall, h1 = _combine_call(dst, xn, route, mods, norm1_g[l + 1], mods_all[l + 1], ys, nct, False)
    return out.reshape(1, n_lat, d)
```

```python
import functools
import math

import numpy as np
import jax
import jax.numpy as jnp
from jax import lax
from jax.experimental import pallas as pl
from jax.experimental.pallas import tpu as pltpu

F32 = jnp.float32
BF16 = jnp.bfloat16
I32 = jnp.int32

HEAD_DIM = 128
LANES = 128
GRID_W = 64
ROPE_THETA = 10000.0
EPS = 1e-6
HG_HEADS = 4
HG_CHUNK = 64
GQA_HEADS = 8
GQA_KV_HEADS = 2
DIFF_HEADS = 4
N_GROUPS = 4
EXPERTS_PER_GROUP = 8
N_EXPERTS = N_GROUPS * EXPERTS_PER_GROUP
N_MOD = 6
ROW_TILE = 256
MOE_TILE = 256
_DMA_GROUP = 8
MIB = 1024 * 1024

_LOG2E = math.log2(math.e)
_NT = (((1,), (1,)), ((), ()))
_TN = (((0,), (0,)), ((), ()))


def _params(semantics, vmem_mib):
    return pltpu.CompilerParams(dimension_semantics=semantics, vmem_limit_bytes=vmem_mib * MIB)


def _split_bf16(x, parts):
    out = []
    for _ in range(parts - 1):
        p = x.astype(BF16)
        out.append(p)
        x = x - p.astype(F32)
    out.append(x.astype(BF16))
    return out


def _mod_kernel(a_ref, w_ref, b_ref, o_ref):
    a = a_ref[...]
    a = a * (1.0 / (1.0 + jnp.exp(-a)))
    hi, lo = _split_bf16(a, 2)
    w = w_ref[0].astype(BF16)
    o_ref[0] = (jnp.dot(hi, w, preferred_element_type=F32)
                + jnp.dot(lo, w, preferred_element_type=F32) + b_ref[0])


def _mod_call(cc, w_mod, b_mod):
    depth, d, n = w_mod.shape
    tn = 1024
    return pl.pallas_call(
        _mod_kernel,
        out_shape=jax.ShapeDtypeStruct((depth, 8, n), F32),
        grid=(depth, n // tn),
        in_specs=[pl.BlockSpec((8, d), lambda l, j: (0, 0)),
                  pl.BlockSpec((1, d, tn), lambda l, j: (l, 0, j)),
                  pl.BlockSpec((1, 1, tn), lambda l, j: (l, 0, j))],
        out_specs=pl.BlockSpec((1, 8, tn), lambda l, j: (l, 0, j)),
        compiler_params=_params(("arbitrary", "arbitrary"), 40),
        name="mod_vectors",
    )(cc, w_mod, b_mod.reshape(depth, 1, n))


def _norm_mod(x, g, shift, scale):
    y = x * lax.rsqrt(jnp.mean(x * x, axis=-1, keepdims=True) + EPS) * g
    return y * (1.0 + scale) + shift


def _prenorm_kernel(x_ref, g_ref, m_ref, o_ref):
    m = m_ref[0]
    o_ref[...] = _norm_mod(x_ref[...], g_ref[...], m[0:1], m[1:2]).astype(BF16)


def _mod_spec(d, nct, tile_off=0):
    return pl.BlockSpec((1, N_MOD, d), lambda i: (jnp.where(i + tile_off < nct, 1, 0), 0, 0))


def _prenorm_call(x, g, mods, nct):
    t, d = x.shape
    return pl.pallas_call(
        _prenorm_kernel,
        out_shape=jax.ShapeDtypeStruct((t, d), BF16),
        grid=(t // ROW_TILE,),
        in_specs=[pl.BlockSpec((ROW_TILE, d), lambda i: (i, 0)),
                  pl.BlockSpec((1, d), lambda i: (0, 0)),
                  _mod_spec(d, nct)],
        out_specs=pl.BlockSpec((ROW_TILE, d), lambda i: (i, 0)),
        compiler_params=_params(("arbitrary",), 24),
        name="prenorm",
    )(x, g.reshape(1, d), mods)


def _mm_kernel(a_ref, b_ref, o_ref):
    o_ref[...] = jnp.dot(a_ref[...], b_ref[...], preferred_element_type=F32)


def _matmul_call(a, b, tm, tn):
    m, k = a.shape
    n = b.shape[1]
    return pl.pallas_call(
        _mm_kernel,
        out_shape=jax.ShapeDtypeStruct((m, n), F32),
        grid=(m // tm, n // tn),
        in_specs=[pl.BlockSpec((tm, k), lambda i, j: (i, 0)),
                  pl.BlockSpec((k, tn), lambda i, j: (0, j))],
        out_specs=pl.BlockSpec((tm, tn), lambda i, j: (i, j)),
        compiler_params=_params(("arbitrary", "arbitrary"), 40),
        name="in_proj",
    )(a, b)


_HG_LEVELS = (1, 2, 4, 8, 16, 32)
_HG_TOT_ROW = HG_CHUNK * (len(_HG_LEVELS) + 1)
_HG_W_ROWS = _HG_TOT_ROW + 16
_HG_HEADS_PER_STEP = 4


def _hgrn_consts():
    c = HG_CHUNK
    w = np.zeros((2, _HG_W_ROWS, c), np.float32)
    msk = np.zeros((2, len(_HG_LEVELS) + 1, c, c), np.float32)
    for d in range(2):
        u = np.arange(c) if d == 0 else c - 1 - np.arange(c)
        ut, us = u[:, None], u[None, :]
        w[d, :c] = us <= ut
        for li, lv in enumerate(_HG_LEVELS):
            blk = u // (2 * lv)
            qside = (u % (2 * lv)) >= lv
            bnd = (blk * 2 * lv + lv - 1)[:, None]
            wq = (us > bnd) & (us <= ut)
            wk = (us > ut) & (us <= bnd)
            w[d, c * (li + 1):c * (li + 2)] = np.where(qside[:, None], wq, -1.0 * wk)
            msk[d, li] = (blk[:, None] == blk[None, :]) & qside[:, None] & ~qside[None, :]
        msk[d, len(_HG_LEVELS)] = np.eye(c)
        w[d, _HG_TOT_ROW:] = 1.0
    return jnp.asarray(np.concatenate([w, w, w], axis=2), BF16), jnp.asarray(msk, F32)


def _hgrn_kernel(lbl_ref, q_ref, z_ref, v_ref, w_ref, msk_ref, o_ref, st_ref, *, layer, chunks, hps):
    c = HG_CHUNK
    hd = HEAD_DIM
    d = pl.program_id(0)
    j = pl.program_id(2)

    @pl.when(j == 0)
    def _():
        st_ref[...] = jnp.zeros_like(st_ref)

    lbl = lbl_ref[...]
    rows = [lbl[i:i + 1] for i in range(lbl.shape[0])]
    mx = functools.reduce(jnp.maximum, rows)
    ex = [jnp.exp(r - mx) for r in rows]
    tot = functools.reduce(lambda a, b: a + b, ex)
    lb = jnp.zeros_like(mx)
    for i in range(1, layer + 1):
        lb = lb + ex[i] / tot
    log_lb = jnp.log(lb)
    log_1m_lb = jnp.log1p(-lb)

    wmat = w_ref[0]
    t_idx = lax.broadcasted_iota(I32, (c, hps * hd), 0)
    u_idx = t_idx + d * (c - 1 - 2 * t_idx)
    nlev = len(_HG_LEVELS)

    for ci in range(chunks):
        cc = ci + d * (chunks - 1 - 2 * ci)
        r0 = pl.multiple_of(cc * c, c)
        q = q_ref[pl.ds(r0, c), :]
        z = z_ref[pl.ds(r0, c), :]
        v_bf = v_ref[pl.ds(r0, c), :].astype(BF16)
        l1p = jnp.log1p(jnp.exp(-jnp.abs(z)))
        ls_pos = jnp.minimum(z, 0.0) - l1p
        ls_neg = jnp.minimum(-z, 0.0) - l1p
        a2 = log_1m_lb + ls_pos
        logf = jnp.maximum(log_lb, a2) + jnp.log1p(jnp.exp(-jnp.abs(log_lb - a2)))
        k = (1.0 - lb) * jnp.exp(ls_neg)

        parts = jnp.concatenate(_split_bf16(logf, 3), axis=0)
        sums = jnp.dot(wmat, parts, preferred_element_type=F32)
        b = sums[0:c]
        btot = sums[_HG_TOT_ROW:_HG_TOT_ROW + 1]

        qls, kls = [q.astype(BF16)], [k.astype(BF16)]
        for li, lv in enumerate(_HG_LEVELS):
            fl = jnp.exp(-jnp.abs(sums[c * (li + 1):c * (li + 2)]))
            qside = (u_idx & (2 * lv - 1)) >= lv
            qls.append(jnp.where(qside, q * fl, 0.0).astype(BF16))
            kls.append(jnp.where(qside, 0.0, k * fl).astype(BF16))
        qe = (q * jnp.exp(b)).astype(BF16)
        kd = (k * jnp.exp(btot - b)).astype(BF16)
        decay = jnp.exp(btot)

        for h in range(hps):
            hs = slice(h * hd, (h + 1) * hd)
            amat = lax.dot_general(qls[0][:, hs], kls[0][:, hs], _NT,
                                   preferred_element_type=F32) * msk_ref[0, nlev]
            for li in range(nlev):
                amat = amat + lax.dot_general(qls[li + 1][:, hs], kls[li + 1][:, hs], _NT,
                                              preferred_element_type=F32) * msk_ref[0, li]
            st = st_ref[h]
            o = (jnp.dot(amat.astype(BF16), v_bf[:, hs], preferred_element_type=F32)
                 + lax.dot_general(qe[:, hs], st.astype(BF16), _NT, preferred_element_type=F32))
            o_ref[0, pl.ds(r0, c), hs] = o
            st_ref[h] = st * decay[:, hs] + lax.dot_general(v_bf[:, hs], kd[:, hs], _TN,
                                                            preferred_element_type=F32)


def _hgrn_call(p, lb_logits, layer, nct):
    t = p.shape[0]
    nblk = t // ROW_TILE
    chunks = ROW_TILE // HG_CHUNK
    hps = _HG_HEADS_PER_STEP
    hw = hps * HEAD_DIM
    ng = HG_HEADS // hps
    wmat, msk = _hgrn_consts()

    def blk(d, j):
        back = jnp.where(j < nct, nct - 1 - j, nblk - 1 - (j - nct))
        return jnp.where(d == 0, j, back)

    return pl.pallas_call(
        functools.partial(_hgrn_kernel, layer=layer, chunks=chunks, hps=hps),
        out_shape=jax.ShapeDtypeStruct((2, t, HG_HEADS * HEAD_DIM), F32),
        grid=(2, ng, nblk),
        in_specs=[pl.BlockSpec((lb_logits.shape[0], hw), lambda d, h, j: (0, h)),
                  pl.BlockSpec((ROW_TILE, hw), lambda d, h, j: (blk(d, j), d * ng + h)),
                  pl.BlockSpec((ROW_TILE, hw), lambda d, h, j: (blk(d, j), (2 + d) * ng + h)),
                  pl.BlockSpec((ROW_TILE, hw), lambda d, h, j: (blk(d, j), 4 * ng + h)),
                  pl.BlockSpec((1, _HG_W_ROWS, 3 * HG_CHUNK), lambda d, h, j: (d, 0, 0)),
                  pl.BlockSpec((1, len(_HG_LEVELS) + 1, HG_CHUNK, HG_CHUNK), lambda d, h, j: (d, 0, 0, 0))],
        out_specs=pl.BlockSpec((1, ROW_TILE, hw), lambda d, h, j: (d, blk(d, j), h)),
        scratch_shapes=[pltpu.VMEM((hps, HEAD_DIM, HEAD_DIM), F32)],
        compiler_params=_params(("arbitrary", "arbitrary", "arbitrary"), 24),
        name="hgrn_scan",
    )(lb_logits, p, p, p, wmat, msk)


def _rope_tables(n_lat, n_ctx, dim):
    rows = n_lat // GRID_W
    row = jnp.repeat(jnp.arange(rows, dtype=F32), GRID_W)
    col = jnp.tile(jnp.arange(GRID_W, dtype=F32), rows)
    axis_dim = dim // 2
    inv_freq = ROPE_THETA ** (-jnp.arange(0, axis_dim, 2, dtype=F32) / axis_dim)
    ang = jnp.concatenate([row[:, None] * inv_freq, col[:, None] * inv_freq], axis=-1)
    cos = jnp.repeat(jnp.cos(ang), 2, axis=1)
    sin = jnp.repeat(jnp.sin(ang), 2, axis=1) * jnp.tile(jnp.asarray([-1.0, 1.0], F32), dim // 2)
    reps = LANES // dim
    cos = jnp.concatenate([jnp.ones((n_ctx, dim), F32), cos], axis=0)
    sin = jnp.concatenate([jnp.zeros((n_ctx, dim), F32), sin], axis=0)
    return jnp.tile(cos, (1, reps)), jnp.tile(sin, (1, reps))


def _rope(x, cos, sin):
    lane = lax.broadcasted_iota(I32, x.shape, 1)
    swapped = jnp.where((lane & 1) == 0, pltpu.roll(x, LANES - 1, 1), pltpu.roll(x, 1, 1))
    return x * cos + swapped * sin


def _head_norm(x, g):
    return x * lax.rsqrt(jnp.mean(x * x, axis=-1, keepdims=True) + EPS) * g


def _attn_prep_kernel(pq_ref, pkv_ref, pd_ref, ca_ref, sa_ref, cd_ref, sd_ref, qn_ref, kn_ref,
                      qa_ref, ka_ref, va_ref, qd_ref, kd_ref, vd_ref):
    hd = HEAD_DIM
    ca, sa, cd, sd = ca_ref[...], sa_ref[...], cd_ref[...], sd_ref[...]
    qn, kn = qn_ref[...], kn_ref[...]
    for h in range(GQA_HEADS):
        xq = _rope(_head_norm(pq_ref[:, h * hd:(h + 1) * hd], qn), ca, sa)
        qa_ref[:, h * hd:(h + 1) * hd] = (xq * (hd ** -0.5 * _LOG2E)).astype(BF16)
    for h in range(GQA_KV_HEADS):
        xk = _rope(_head_norm(pkv_ref[:, h * hd:(h + 1) * hd], kn), ca, sa)
        ka_ref[:, h * hd:(h + 1) * hd] = xk.astype(BF16)
    va_ref[...] = pkv_ref[:, GQA_KV_HEADS * hd:].astype(BF16)
    first = lax.broadcasted_iota(I32, (pq_ref.shape[0], hd), 1) < hd // 2
    dw = DIFF_HEADS * hd
    for h in range(DIFF_HEADS):
        xq = _rope(pd_ref[:, h * hd:(h + 1) * hd], cd, sd) * ((hd // 2) ** -0.5 * _LOG2E)
        qd_ref[:, (2 * h) * hd:(2 * h + 1) * hd] = jnp.where(first, xq, 0.0).astype(BF16)
        qd_ref[:, (2 * h + 1) * hd:(2 * h + 2) * hd] = jnp.where(first, 0.0, xq).astype(BF16)
        xk = _rope(pd_ref[:, dw + h * hd:dw + (h + 1) * hd], cd, sd)
        kd_ref[:, h * hd:(h + 1) * hd] = xk.astype(BF16)
    vd_ref[...] = pd_ref[:, 2 * dw:].astype(BF16)


def _attn_prep_call(p, tabs, qn, kn):
    t = p.shape[0]
    hd = HEAD_DIM
    gq, gkv, dw = GQA_HEADS * hd, 2 * GQA_KV_HEADS * hd, DIFF_HEADS * hd
    q_off = 6 * HG_HEADS * hd
    assert q_off % gq == 0 and (q_off + gq) % gkv == 0 and (q_off + gq + gkv) % (3 * dw) == 0
    row = lambda w: pl.BlockSpec((ROW_TILE, w), lambda i: (i, 0))
    vec = pl.BlockSpec((1, hd), lambda i: (0, 0))
    return pl.pallas_call(
        _attn_prep_kernel,
        out_shape=[jax.ShapeDtypeStruct((t, gq), BF16),
                   jax.ShapeDtypeStruct((t, gkv // 2), BF16),
                   jax.ShapeDtypeStruct((t, gkv // 2), BF16),
                   jax.ShapeDtypeStruct((t, 2 * dw), BF16),
                   jax.ShapeDtypeStruct((t, dw), BF16),
                   jax.ShapeDtypeStruct((t, dw), BF16)],
        grid=(t // ROW_TILE,),
        in_specs=[pl.BlockSpec((ROW_TILE, gq), lambda i: (i, q_off // gq)),
                  pl.BlockSpec((ROW_TILE, gkv), lambda i: (i, (q_off + gq) // gkv)),
                  pl.BlockSpec((ROW_TILE, 3 * dw), lambda i: (i, (q_off + gq + gkv) // (3 * dw))),
                  row(hd), row(hd), row(hd), row(hd), vec, vec],
        out_specs=[row(gq), row(gkv // 2), row(gkv // 2), row(2 * dw), row(dw), row(dw)],
        compiler_params=_params(("arbitrary",), 24),
        name="attn_prep",
    )(p, p, p, *tabs, qn.reshape(1, hd), kn.reshape(1, hd))


def _flash_kernel(*refs, g, nh, n_ctx, n_lat, tkl, nct, tile_off, diff, lam_init):
    if diff:
        q_ref, k_ref, v_ref, lq1, lk1, lq2, lk2, dn_ref, o_ref = refs[:9]
    else:
        q_ref, k_ref, v_ref, o_ref = refs[:4]
    nc = nh * g
    m_sc, l_sc, acc_sc, p_sc = (refs[len(refs) - (4 - n) * nc:len(refs) - (3 - n) * nc] for n in range(4))
    hd = HEAD_DIM
    i = pl.program_id(1) + tile_off
    nblk = n_lat // tkl
    chains = [(c, c // g) for c in range(nh * g)]

    def head(ref, rows, h):
        return ref[rows, h * hd:(h + 1) * hd]

    ctx_rows = slice(0, n_ctx)
    for c, kh in chains:
        s = lax.dot_general(head(q_ref, slice(None), c), head(k_ref, ctx_rows, kh), _NT,
                            preferred_element_type=F32)
        m0 = jnp.max(s, axis=1, keepdims=True)
        p = jnp.exp2(s - m0)
        m_sc[c][...] = jnp.broadcast_to(m0, m_sc[c].shape)
        l_sc[c][...] = jnp.broadcast_to(jnp.sum(p, axis=1, keepdims=True), l_sc[c].shape)
        acc_sc[c][...] = jnp.dot(p.astype(BF16), head(v_ref, ctx_rows, kh), preferred_element_type=F32)

    def probs(c, kh, rows):
        s = lax.dot_general(head(q_ref, slice(None), c), head(k_ref, rows, kh), _NT,
                            preferred_element_type=F32)
        m_prev = m_sc[c][...]
        m_new = jnp.maximum(m_prev, jnp.max(s, axis=1, keepdims=True))
        p = jnp.exp2(s - jnp.tile(m_new, (1, tkl // LANES)))
        alpha = jnp.exp2(m_prev - m_new)
        l_sc[c][...] = alpha * l_sc[c][...] + jnp.sum(p, axis=1, keepdims=True)
        m_sc[c][...] = m_new
        return p.astype(BF16), alpha

    @pl.when(i >= nct)
    def _():
        for c, kh in chains:
            p, alpha = probs(c, kh, pl.ds(n_ctx, tkl))
            p_sc[c][0] = p
            acc_sc[c][...] = alpha * acc_sc[c][...]

        for jb in range(1, nblk):
            r0 = n_ctx + jb * tkl
            prev = pl.ds(r0 - tkl, tkl)
            slot = jb % 2
            alphas = []
            for c, kh in chains:
                p, alpha = probs(c, kh, pl.ds(r0, tkl))
                p_sc[c][slot] = p
                alphas.append(alpha)
            for c, kh in chains:
                pv = jnp.dot(p_sc[c][1 - slot], head(v_ref, prev, kh), preferred_element_type=F32)
                acc_sc[c][...] = alphas[c] * (acc_sc[c][...] + pv)

        last = pl.ds(n_ctx + (nblk - 1) * tkl, tkl)
        for c, kh in chains:
            acc_sc[c][...] = acc_sc[c][...] + jnp.dot(p_sc[c][(nblk - 1) % 2], head(v_ref, last, kh),
                                                      preferred_element_type=F32)

    if diff:
        lam = (jnp.exp(jnp.sum(lq1[...] * lk1[...], axis=1, keepdims=True))
               - jnp.exp(jnp.sum(lq2[...] * lk2[...], axis=1, keepdims=True)) + lam_init)
        for kh in range(nh):
            dd = (acc_sc[2 * kh][...] / l_sc[2 * kh][...]
                  - lam * (acc_sc[2 * kh + 1][...] / l_sc[2 * kh + 1][...]))
            o_ref[:, kh * hd:(kh + 1) * hd] = (_head_norm(dd, dn_ref[...]) * (1.0 - lam_init)).astype(BF16)
    else:
        for c, _ in chains:
            o_ref[:, c * hd:(c + 1) * hd] = (acc_sc[c][...] / l_sc[c][...]).astype(BF16)


def _flash_call(q, k, v, n_ctx, g, nh, tile_off, extra=None, lam_init=0.0):
    t = k.shape[0]
    hd = HEAD_DIM
    n_kv = k.shape[1] // (nh * hd)
    n_lat = t - n_ctx
    nct = n_ctx // ROW_TILE
    tkl = 2048
    diff = extra is not None
    ow = nh * hd if diff else nh * g * hd
    in_specs = [pl.BlockSpec((ROW_TILE, nh * g * hd), lambda kv, i: (i + tile_off, kv)),
                pl.BlockSpec((t, nh * hd), lambda kv, i: (0, kv)),
                pl.BlockSpec((t, nh * hd), lambda kv, i: (0, kv))]
    args = [q, k, v]
    if diff:
        for a in extra:
            a = a.reshape(1, -1)
            in_specs.append(pl.BlockSpec(a.shape, lambda kv, i: (0, 0)))
            args.append(a)
    return pl.pallas_call(
        functools.partial(_flash_kernel, g=g, nh=nh, n_ctx=n_ctx, n_lat=n_lat, tkl=tkl, nct=nct,
                          tile_off=tile_off, diff=diff, lam_init=lam_init),
        out_shape=jax.ShapeDtypeStruct((t - tile_off * ROW_TILE, n_kv * ow), BF16),
        grid=(n_kv, t // ROW_TILE - tile_off),
        in_specs=in_specs,
        out_specs=pl.BlockSpec((ROW_TILE, ow), lambda kv, i: (i, kv)),
        scratch_shapes=([pltpu.VMEM((ROW_TILE, LANES), F32)] * (2 * nh * g)
                        + [pltpu.VMEM((ROW_TILE, hd), F32)] * (nh * g)
                        + [pltpu.VMEM((2, ROW_TILE, tkl), BF16)] * (nh * g)),
        compiler_params=_params(("arbitrary", "arbitrary"), 48),
        name="diff_attention" if diff else "gqa_attention",
    )(*args)


def _outproj_kernel(of_ref, ob_ref, gate_ref, ng_ref, att_ref, dif_ref, wo_ref, x_ref, m_ref, n2_ref,
                    rw_ref, xo_ref, h2_ref, lg_ref):
    hd = HEAD_DIM
    hgw = HG_HEADS * hd
    ng = ng_ref[...]
    gate = gate_ref[...]
    silu_gate = gate * (1.0 / (1.0 + jnp.exp(-gate)))
    acc = jnp.dot(att_ref[...], wo_ref[hgw:hgw + att_ref.shape[1], :], preferred_element_type=F32)
    acc = acc + jnp.dot(dif_ref[...], wo_ref[hgw + att_ref.shape[1]:, :], preferred_element_type=F32)
    hg = []
    for h in range(HG_HEADS):
        o = of_ref[0, :, h * hd:(h + 1) * hd] + ob_ref[0, :, h * hd:(h + 1) * hd]
        hg.append((_head_norm(o, ng) * silu_gate[:, h * hd:(h + 1) * hd]).astype(BF16))
    acc = acc + jnp.dot(jnp.concatenate(hg, axis=1), wo_ref[0:hgw, :], preferred_element_type=F32)
    m = m_ref[0]
    xn = x_ref[...] + m[2:3] * acc
    xo_ref[...] = xn
    h2 = _norm_mod(xn, n2_ref[...], m[3:4], m[4:5])
    h2_ref[...] = h2
    hs = _split_bf16(h2, 3)
    ws = _split_bf16(rw_ref[...], 3)
    lg = jnp.zeros(lg_ref.shape, F32)
    for a, b in ((2, 0), (1, 1), (0, 2), (1, 0), (0, 1), (0, 0)):
        lg = lg + jnp.dot(hs[a], ws[b], preferred_element_type=F32)
    lg_ref[...] = lg


def _outproj_call(o, p, ng, att, dif, wo, x, mods, n2, rw, nct, tile_off):
    d = x.shape[1]
    hd = HEAD_DIM
    hgw = HG_HEADS * hd
    n_rows = att.shape[0]
    row = lambda w: pl.BlockSpec((ROW_TILE, w), lambda i: (i, 0))
    full = lambda a: pl.BlockSpec(a.shape, lambda i: (0,) * a.ndim)
    ng, n2 = ng.reshape(1, hd), n2.reshape(1, d)
    x_off = (x.shape[0] - n_rows) // ROW_TILE
    return pl.pallas_call(
        _outproj_kernel,
        out_shape=[jax.ShapeDtypeStruct((n_rows, d), F32),
                   jax.ShapeDtypeStruct((n_rows, d), F32),
                   jax.ShapeDtypeStruct((n_rows, LANES), F32)],
        grid=(n_rows // ROW_TILE,),
        in_specs=[pl.BlockSpec((1, ROW_TILE, hgw), lambda i: (0, i + tile_off, 0)),
                  pl.BlockSpec((1, ROW_TILE, hgw), lambda i: (1, i + tile_off, 0)),
                  pl.BlockSpec((ROW_TILE, hgw), lambda i: (i + tile_off, 5)),
                  full(ng), row(att.shape[1]), row(dif.shape[1]), full(wo),
                  pl.BlockSpec((ROW_TILE, d), lambda i: (i + x_off, 0)),
                  _mod_spec(d, nct, tile_off), full(n2), full(rw)],
        out_specs=[row(d), row(d), row(LANES)],
        compiler_params=_params(("arbitrary",), 56),
        name="out_proj",
    )(o, o, p, ng, att, dif, wo, x, mods, n2, rw)


def _router_kernel(lg_ref, bias_ref, r_ref, cnt_ref, run_ref):
    i = pl.program_id(0)
    tm = lg_ref.shape[0]

    @pl.when(i == 0)
    def _():
        run_ref[...] = jnp.zeros_like(run_ref)

    lane = lax.broadcasted_iota(I32, (tm, LANES), 1).astype(F32)
    lg = lg_ref[...] + bias_ref[...]
    ninf = -jnp.inf

    def first_max(vals):
        mx = jnp.max(vals, axis=1, keepdims=True)
        idx = jnp.min(jnp.where(vals == mx, lane, float(LANES)), axis=1, keepdims=True)
        return mx, idx

    gl = jnp.where(lane < N_GROUPS, lg, ninf)
    gmax, gidx = first_max(gl)
    g_top = 1.0 / jnp.sum(jnp.exp(gl - gmax), axis=1, keepdims=True)
    lo = N_GROUPS + EXPERTS_PER_GROUP * gidx
    el = jnp.where((lane >= lo) & (lane < lo + EXPERTS_PER_GROUP), lg, ninf)
    m1, e1 = first_max(el)
    m2, e2 = first_max(jnp.where(lane == e1, ninf, el))
    r = jnp.exp(m2 - m1)
    w1 = g_top / (1.0 + r)
    w2 = g_top * r / (1.0 + r)

    hit = ((lane == e1) | (lane == e2)).astype(BF16)
    ti = lax.broadcasted_iota(I32, (tm, tm), 0)
    si = lax.broadcasted_iota(I32, (tm, tm), 1)
    before = (si < ti).astype(BF16)
    pos = jnp.dot(before, hit, preferred_element_type=F32) + run_ref[0:1, :]
    p1 = jnp.sum(jnp.where(lane == e1, pos, 0.0), axis=1, keepdims=True)
    p2 = jnp.sum(jnp.where(lane == e2, pos, 0.0), axis=1, keepdims=True)
    total = run_ref[0:1, :] + jnp.sum(hit.astype(F32), axis=0, keepdims=True)
    run_ref[...] = jnp.broadcast_to(total, run_ref.shape)
    cnt_ref[...] = jnp.broadcast_to(total, cnt_ref.shape)

    fields = (e1 - N_GROUPS, e2 - N_GROUPS, w1, w2, p1, p2)
    out = jnp.zeros((tm, LANES), F32)
    for n, f in enumerate(fields):
        out = jnp.where(lane == n, f, out)
    r_ref[...] = out


def _router_call(logits, bias):
    n = logits.shape[0]
    return pl.pallas_call(
        _router_kernel,
        out_shape=[jax.ShapeDtypeStruct((n, LANES), F32), jax.ShapeDtypeStruct((8, LANES), F32)],
        grid=(n // ROW_TILE,),
        in_specs=[pl.BlockSpec((ROW_TILE, LANES), lambda i: (i, 0)),
                  pl.BlockSpec((1, LANES), lambda i: (0, 0))],
        out_specs=[pl.BlockSpec((ROW_TILE, LANES), lambda i: (i, 0)),
                   pl.BlockSpec((8, LANES), lambda i: (0, 0))],
        scratch_shapes=[pltpu.VMEM((8, LANES), F32)],
        compiler_params=_params(("arbitrary",), 16),
        name="router",
    )(logits, bias)


def _moe_kernel(src_ref, te_ref, nt_ref, h_hbm, wg_ref, wu_ref, wd_ref, y_ref,
                xbuf, sem, wg_bf, wu_bf, wd_bf):
    i = pl.program_id(0)
    n_tiles = nt_ref[0]
    tm = xbuf.shape[1]

    def start_tile(tile, slot):
        def body(grp, carry):
            for u in range(_DMA_GROUP):
                r = grp * _DMA_GROUP + u
                pltpu.make_async_copy(h_hbm.at[pl.ds(src_ref[tile * tm + r], 1), :],
                                      xbuf.at[slot, pl.ds(r, 1), :], sem.at[slot]).start(priority=u % 2)
            return carry
        lax.fori_loop(0, tm // _DMA_GROUP, body, 0)

    def wait_tile(slot):
        pltpu.make_async_copy(h_hbm.at[pl.ds(0, tm), :], xbuf.at[slot], sem.at[slot]).wait()

    @pl.when(i == 0)
    def _():
        start_tile(0, 0)

    @pl.when(i < n_tiles)
    def _():
        slot = i % 2

        @pl.when(i + 1 < n_tiles)
        def _():
            start_tile(i + 1, 1 - slot)

        changed = jnp.logical_or(i == 0, te_ref[i] != te_ref[jnp.maximum(i - 1, 0)])

        @pl.when(changed)
        def _():
            wg_bf[...] = wg_ref[0, 0].astype(BF16)
            wu_bf[...] = wu_ref[0, 0].astype(BF16)
            wd_bf[...] = wd_ref[0, 0].astype(BF16)

        wait_tile(slot)
        xb = xbuf[slot].astype(BF16)
        gt = jnp.dot(xb, wg_bf[...], preferred_element_type=F32)
        up = jnp.dot(xb, wu_bf[...], preferred_element_type=F32)
        act = (gt * (1.0 / (1.0 + jnp.exp(-gt))) * up).astype(BF16)
        y_ref[...] = jnp.dot(act, wd_bf[...], preferred_element_type=F32)

    @pl.when(i >= n_tiles)
    def _():
        y_ref[...] = jnp.zeros_like(y_ref)


def _moe_call(h2, src, tile_expert, n_tiles, wg, wu, wd, layer):
    d = h2.shape[1]
    ff = wg.shape[3]
    max_tiles = tile_expert.shape[0]
    tm = MOE_TILE
    wspec = lambda s: pl.BlockSpec((1, 1) + s, lambda i, src, te, nt: (layer, te[i], 0, 0))
    grid_spec = pltpu.PrefetchScalarGridSpec(
        num_scalar_prefetch=3,
        grid=(max_tiles,),
        in_specs=[pl.BlockSpec(memory_space=pl.ANY), wspec((d, ff)), wspec((d, ff)), wspec((ff, d))],
        out_specs=pl.BlockSpec((tm, d), lambda i, src, te, nt: (i, 0)),
        scratch_shapes=[pltpu.VMEM((2, tm, d), F32), pltpu.SemaphoreType.DMA((2,)),
                        pltpu.VMEM((d, ff), BF16), pltpu.VMEM((d, ff), BF16), pltpu.VMEM((ff, d), BF16)])
    return pl.pallas_call(
        _moe_kernel,
        out_shape=jax.ShapeDtypeStruct((max_tiles * tm, d), F32),
        grid_spec=grid_spec,
        compiler_params=_params(("arbitrary",), 56),
        name="moe_experts",
    )(src, tile_expert, n_tiles, h2, wg, wu, wd)


def _combine_kernel(dst_ref, x_ref, rt_ref, m_ref, g_ref, mn_ref, y_hbm, *rest, last, n_out):
    outs, (ybuf, sem) = rest[:n_out], rest[n_out:]
    i = pl.program_id(0)
    tm = x_ref.shape[0]

    def start_tile(tile, slot):
        def body(grp, carry):
            for u in range(_DMA_GROUP):
                r = grp * _DMA_GROUP + u
                for j in range(2):
                    pltpu.make_async_copy(y_hbm.at[pl.ds(dst_ref[(tile * tm + r) * 2 + j], 1), :],
                                          ybuf.at[slot, j, pl.ds(r, 1), :], sem.at[slot]).start(priority=j)
            return carry
        lax.fori_loop(0, tm // _DMA_GROUP, body, 0)

    @pl.when(i == 0)
    def _():
        start_tile(0, 0)

    slot = i % 2

    @pl.when(i + 1 < pl.num_programs(0))
    def _():
        start_tile(i + 1, 1 - slot)

    for j in range(2):
        pltpu.make_async_copy(y_hbm.at[pl.ds(0, tm), :], ybuf.at[slot, j], sem.at[slot]).wait()
    rt = rt_ref[...]
    y = rt[:, 2:3] * ybuf[slot, 0] + rt[:, 3:4] * ybuf[slot, 1]
    xn = x_ref[...] + m_ref[0][5:6] * y
    if last:
        outs[0][...] = xn * lax.rsqrt(jnp.mean(xn * xn, axis=-1, keepdims=True) + EPS) * g_ref[...]
    else:
        outs[0][...] = xn
        mn = mn_ref[0]
        outs[1][...] = _norm_mod(xn, g_ref[...], mn[0:1], mn[1:2]).astype(BF16)


def _combine_call(dst, x, route, mods, g, mods_next, ys, nct, last):
    n, d = x.shape
    tm = ROW_TILE
    mspec = pl.BlockSpec((1, N_MOD, d), lambda i, dst: (jnp.where(i < nct, 1, 0), 0, 0))
    out_shape = [jax.ShapeDtypeStruct((n, d), F32)]
    if not last:
        out_shape.append(jax.ShapeDtypeStruct((n, d), BF16))
    row = lambda w: pl.BlockSpec((tm, w), lambda i, dst: (i, 0))
    grid_spec = pltpu.PrefetchScalarGridSpec(
        num_scalar_prefetch=1,
        grid=(n // tm,),
        in_specs=[row(d), row(LANES), mspec, pl.BlockSpec((1, d), lambda i, dst: (0, 0)), mspec,
                  pl.BlockSpec(memory_space=pl.ANY)],
        out_specs=[row(d)] * len(out_shape),
        scratch_shapes=[pltpu.VMEM((2, 2, tm, d), F32), pltpu.SemaphoreType.DMA((2,))])
    return pl.pallas_call(
        functools.partial(_combine_kernel, last=last, n_out=len(out_shape)),
        out_shape=out_shape,
        grid_spec=grid_spec,
        compiler_params=_params(("arbitrary",), 40),
        name="moe_combine",
    )(dst, x, route, mods, g.reshape(1, d), mods_next, ys)


def _dispatch_plan(route, counts, n_tokens):
    tm = MOE_TILE
    max_tiles = (2 * n_tokens) // tm + N_EXPERTS
    cnt = counts[0, N_GROUPS:N_GROUPS + N_EXPERTS].astype(I32)
    tiles_per = (cnt + tm - 1) // tm
    tile_end = jnp.cumsum(tiles_per)
    offs = (tile_end - tiles_per) * tm
    eid = route[:, 0:2].astype(I32)
    dst = offs[eid] + route[:, 4:6].astype(I32)
    tok = jnp.broadcast_to(jnp.arange(n_tokens, dtype=I32)[:, None], (n_tokens, 2))
    src = jnp.zeros((max_tiles * tm,), I32).at[dst.reshape(-1)].set(tok.reshape(-1))
    n_tiles = tile_end[-1:]
    tile_ids = jnp.minimum(jnp.arange(max_tiles, dtype=I32), n_tiles[0] - 1)
    tile_expert = jnp.sum((tile_end[None, :] <= tile_ids[:, None]).astype(I32), axis=1)
    return dst.reshape(-1), src, tile_expert, n_tiles.astype(I32)


def kernel(x, c, ctx, c_ctx, w_mod, b_mod, norm1_g, norm2_g, w_in, w_out, hg_lb_logits, hg_norm_g,
           q_norm_g, k_norm_g, lam_q1, lam_k1, lam_q2, lam_k2, diff_norm_g, router_group_w,
           router_group_b, router_expert_w, router_expert_b, w_gate, w_up, w_down, final_norm_g):
    depth = w_in.shape[0]
    n_lat, d = x.shape[1], x.shape[2]
    n_ctx = ctx.shape[1]
    nct = n_ctx // ROW_TILE
    assert x.shape[0] == 1 and n_ctx % ROW_TILE == 0 and n_lat % 1024 == 0

    xall = jnp.concatenate([ctx[0], x[0]], axis=0)
    cc = jnp.zeros((8, d), F32).at[0].set(c[0]).at[1].set(c_ctx)
    mods_all = _mod_call(cc, w_mod, b_mod)[:, :2].reshape(depth, 2, N_MOD, d)
    tabs = _rope_tables(n_lat, n_ctx, HEAD_DIM) + _rope_tables(n_lat, n_ctx, HEAD_DIM // 2)

    h1 = _prenorm_call(xall, norm1_g[0], mods_all[0], nct)
    out = None
    for l in range(depth):
        last = l == depth - 1
        mods = mods_all[l]
        tile_off = nct if last else 0
        tm_in = 768 if (n_ctx + n_lat) % 768 == 0 else ROW_TILE
        p = _matmul_call(h1, w_in[l].astype(BF16), tm_in, 1024)
        o = _hgrn_call(p, hg_lb_logits, l, nct)
        qa, ka, va, qd, kd, vd = _attn_prep_call(p, tabs, q_norm_g[l], k_norm_g[l])
        att = _flash_call(qa, ka, va, n_ctx, GQA_HEADS // GQA_KV_HEADS, 1, tile_off)
        lam_init = 0.8 - 0.6 * math.exp(-0.3 * l)
        dif = _flash_call(qd, kd, vd, n_ctx, 2, 2, tile_off,
                          extra=(lam_q1[l], lam_k1[l], lam_q2[l], lam_k2[l], diff_norm_g[l]),
                          lam_init=lam_init)
        rw = jnp.concatenate([router_group_w[l], router_expert_w[l],
                              jnp.zeros((d, LANES - N_GROUPS - N_EXPERTS), F32)], axis=1)
        rb = jnp.concatenate([router_group_b[l], router_expert_b[l],
                              jnp.zeros((LANES - N_GROUPS - N_EXPERTS,), F32)]).reshape(1, LANES)
        xn, h2, logits = _outproj_call(o, p, hg_norm_g[l], att, dif, w_out[l].astype(BF16), xall,
                                       mods, norm2_g[l], rw, nct, tile_off)
        n_tok = xn.shape[0]
        route, counts = _router_call(logits, rb)
        dst, src, tile_expert, n_tiles = _dispatch_plan(route, counts, n_tok)
        ys = _moe_call(h2, src, tile_expert, n_tiles, w_gate, w_up, w_down, l)
        if last:
            (out,) = _combine_call(dst, xn, route, mods, final_norm_g, mods, ys, 0, True)
        else:
            xall, h1 = _combine_call(dst, xn, route, mods, norm1_g[l + 1], mods_all[l + 1], ys, nct, False)
    return out.reshape(1, n_lat, d)
```

```python
import functools
import math

import numpy as np
import jax
import jax.numpy as jnp
from jax import lax
from jax.experimental import pallas as pl
from jax.experimental.pallas import tpu as pltpu

F32 = jnp.float32
BF16 = jnp.bfloat16
I32 = jnp.int32

HEAD_DIM = 128
LANES = 128
GRID_W = 64
ROPE_THETA = 10000.0
EPS = 1e-6
HG_HEADS = 4
HG_CHUNK = 64
GQA_HEADS = 8
GQA_KV_HEADS = 2
DIFF_HEADS = 4
N_GROUPS = 4
EXPERTS_PER_GROUP = 8
N_EXPERTS = N_GROUPS * EXPERTS_PER_GROUP
N_MOD = 6
ROW_TILE = 256
MOE_TILE = 256
_DMA_GROUP = 8
MIB = 1024 * 1024

_LOG2E = math.log2(math.e)
_NT = (((1,), (1,)), ((), ()))
_TN = (((0,), (0,)), ((), ()))


def _params(semantics, vmem_mib):
    return pltpu.CompilerParams(dimension_semantics=semantics, vmem_limit_bytes=vmem_mib * MIB)


def _split_bf16(x, parts):
    out = []
    for _ in range(parts - 1):
        p = x.astype(BF16)
        out.append(p)
        x = x - p.astype(F32)
    out.append(x.astype(BF16))
    return out


def _mod_kernel(a_ref, w_ref, b_ref, o_ref):
    a = a_ref[...]
    a = a * (1.0 / (1.0 + jnp.exp(-a)))
    hi, lo = _split_bf16(a, 2)
    w = w_ref[0].astype(BF16)
    o_ref[0] = (jnp.dot(hi, w, preferred_element_type=F32)
                + jnp.dot(lo, w, preferred_element_type=F32) + b_ref[0])


def _mod_call(cc, w_mod, b_mod):
    depth, d, n = w_mod.shape
    tn = 1024
    return pl.pallas_call(
        _mod_kernel,
        out_shape=jax.ShapeDtypeStruct((depth, 8, n), F32),
        grid=(depth, n // tn),
        in_specs=[pl.BlockSpec((8, d), lambda l, j: (0, 0)),
                  pl.BlockSpec((1, d, tn), lambda l, j: (l, 0, j)),
                  pl.BlockSpec((1, 1, tn), lambda l, j: (l, 0, j))],
        out_specs=pl.BlockSpec((1, 8, tn), lambda l, j: (l, 0, j)),
        compiler_params=_params(("arbitrary", "arbitrary"), 40),
        name="mod_vectors",
    )(cc, w_mod, b_mod.reshape(depth, 1, n))


def _norm_mod(x, g, shift, scale):
    y = x * lax.rsqrt(jnp.mean(x * x, axis=-1, keepdims=True) + EPS) * g
    return y * (1.0 + scale) + shift


def _prenorm_kernel(x_ref, g_ref, m_ref, o_ref):
    m = m_ref[0]
    o_ref[...] = _norm_mod(x_ref[...], g_ref[...], m[0:1], m[1:2]).astype(BF16)


def _mod_spec(d, nct, tile_off=0):
    return pl.BlockSpec((1, N_MOD, d), lambda i: (jnp.where(i + tile_off < nct, 1, 0), 0, 0))


def _prenorm_call(x, g, mods, nct):
    t, d = x.shape
    return pl.pallas_call(
        _prenorm_kernel,
        out_shape=jax.ShapeDtypeStruct((t, d), BF16),
        grid=(t // ROW_TILE,),
        in_specs=[pl.BlockSpec((ROW_TILE, d), lambda i: (i, 0)),
                  pl.BlockSpec((1, d), lambda i: (0, 0)),
                  _mod_spec(d, nct)],
        out_specs=pl.BlockSpec((ROW_TILE, d), lambda i: (i, 0)),
        compiler_params=_params(("arbitrary",), 24),
        name="prenorm",
    )(x, g.reshape(1, d), mods)


def _mm_kernel(a_ref, b_ref, o_ref, b_bf):
    @pl.when(pl.program_id(1) == 0)
    def _():
        b_bf[...] = b_ref[0].astype(BF16)

    o_ref[...] = jnp.dot(a_ref[...], b_bf[...], preferred_element_type=F32)


def _matmul_call(a, b, layer, tm, tn):
    m, k = a.shape
    n = b.shape[2]
    return pl.pallas_call(
        _mm_kernel,
        out_shape=jax.ShapeDtypeStruct((m, n), F32),
        grid=(n // tn, m // tm),
        in_specs=[pl.BlockSpec((tm, k), lambda j, i: (i, 0)),
                  pl.BlockSpec((1, k, tn), lambda j, i: (layer, 0, j))],
        out_specs=pl.BlockSpec((tm, tn), lambda j, i: (i, j)),
        scratch_shapes=[pltpu.VMEM((k, tn), BF16)],
        compiler_params=_params(("arbitrary", "arbitrary"), 48),
        name="in_proj",
    )(a, b)


_HG_LEVELS = (1, 2, 4, 8, 16, 32)
_HG_TOT_ROW = HG_CHUNK * (len(_HG_LEVELS) + 1)
_HG_W_ROWS = _HG_TOT_ROW + 16
_HG_HEADS_PER_STEP = 4


def _hgrn_consts():
    c = HG_CHUNK
    w = np.zeros((2, _HG_W_ROWS, c), np.float32)
    msk = np.zeros((2, len(_HG_LEVELS) + 1, c, c), np.float32)
    for d in range(2):
        u = np.arange(c) if d == 0 else c - 1 - np.arange(c)
        ut, us = u[:, None], u[None, :]
        w[d, :c] = us <= ut
        for li, lv in enumerate(_HG_LEVELS):
            blk = u // (2 * lv)
            qside = (u % (2 * lv)) >= lv
            bnd = (blk * 2 * lv + lv - 1)[:, None]
            wq = (us > bnd) & (us <= ut)
            wk = (us > ut) & (us <= bnd)
            w[d, c * (li + 1):c * (li + 2)] = np.where(qside[:, None], wq, -1.0 * wk)
            msk[d, li] = (blk[:, None] == blk[None, :]) & qside[:, None] & ~qside[None, :]
        msk[d, len(_HG_LEVELS)] = np.eye(c)
        w[d, _HG_TOT_ROW:] = 1.0
    return jnp.asarray(np.concatenate([w, w, w], axis=2), BF16), jnp.asarray(msk, F32)


def _hgrn_kernel(lbl_ref, q_ref, z_ref, v_ref, w_ref, msk_ref, o_ref, st_ref, *, layer, chunks, hps):
    c = HG_CHUNK
    hd = HEAD_DIM
    d = pl.program_id(0)
    j = pl.program_id(2)

    @pl.when(j == 0)
    def _():
        st_ref[...] = jnp.zeros_like(st_ref)

    lbl = lbl_ref[...]
    rows = [lbl[i:i + 1] for i in range(lbl.shape[0])]
    mx = functools.reduce(jnp.maximum, rows)
    ex = [jnp.exp(r - mx) for r in rows]
    tot = functools.reduce(lambda a, b: a + b, ex)
    lb = jnp.zeros_like(mx)
    for i in range(1, layer + 1):
        lb = lb + ex[i] / tot
    log_lb = jnp.log(lb)
    log_1m_lb = jnp.log1p(-lb)

    wmat = w_ref[0]
    t_idx = lax.broadcasted_iota(I32, (c, hps * hd), 0)
    u_idx = t_idx + d * (c - 1 - 2 * t_idx)
    nlev = len(_HG_LEVELS)

    for ci in range(chunks):
        cc = ci + d * (chunks - 1 - 2 * ci)
        r0 = pl.multiple_of(cc * c, c)
        q = q_ref[pl.ds(r0, c), :]
        z = z_ref[pl.ds(r0, c), :]
        v_bf = v_ref[pl.ds(r0, c), :].astype(BF16)
        l1p = jnp.log1p(jnp.exp(-jnp.abs(z)))
        ls_pos = jnp.minimum(z, 0.0) - l1p
        ls_neg = jnp.minimum(-z, 0.0) - l1p
        a2 = log_1m_lb + ls_pos
        logf = jnp.maximum(log_lb, a2) + jnp.log1p(jnp.exp(-jnp.abs(log_lb - a2)))
        k = (1.0 - lb) * jnp.exp(ls_neg)

        parts = jnp.concatenate(_split_bf16(logf, 3), axis=0)
        sums = jnp.dot(wmat, parts, preferred_element_type=F32)
        b = sums[0:c]
        btot = sums[_HG_TOT_ROW:_HG_TOT_ROW + 1]

        qls, kls = [q.astype(BF16)], [k.astype(BF16)]
        for li, lv in enumerate(_HG_LEVELS):
            fl = jnp.exp(-jnp.abs(sums[c * (li + 1):c * (li + 2)]))
            qside = (u_idx & (2 * lv - 1)) >= lv
            qls.append(jnp.where(qside, q * fl, 0.0).astype(BF16))
            kls.append(jnp.where(qside, 0.0, k * fl).astype(BF16))
        qe = (q * jnp.exp(b)).astype(BF16)
        kd = (k * jnp.exp(btot - b)).astype(BF16)
        decay = jnp.exp(btot)

        for h in range(hps):
            hs = slice(h * hd, (h + 1) * hd)
            amat = lax.dot_general(qls[0][:, hs], kls[0][:, hs], _NT,
                                   preferred_element_type=F32) * msk_ref[0, nlev]
            for li in range(nlev):
                amat = amat + lax.dot_general(qls[li + 1][:, hs], kls[li + 1][:, hs], _NT,
                                              preferred_element_type=F32) * msk_ref[0, li]
            st = st_ref[h]
            o = (jnp.dot(amat.astype(BF16), v_bf[:, hs], preferred_element_type=F32)
                 + lax.dot_general(qe[:, hs], st.astype(BF16), _NT, preferred_element_type=F32))
            o_ref[0, pl.ds(r0, c), hs] = o
            st_ref[h] = st * decay[:, hs] + lax.dot_general(v_bf[:, hs], kd[:, hs], _TN,
                                                            preferred_element_type=F32)


def _hgrn_call(p, lb_logits, layer, nct):
    t = p.shape[0]
    nblk = t // ROW_TILE
    chunks = ROW_TILE // HG_CHUNK
    hps = _HG_HEADS_PER_STEP
    hw = hps * HEAD_DIM
    ng = HG_HEADS // hps
    wmat, msk = _hgrn_consts()

    def blk(d, j):
        back = jnp.where(j < nct, nct - 1 - j, nblk - 1 - (j - nct))
        return jnp.where(d == 0, j, back)

    return pl.pallas_call(
        functools.partial(_hgrn_kernel, layer=layer, chunks=chunks, hps=hps),
        out_shape=jax.ShapeDtypeStruct((2, t, HG_HEADS * HEAD_DIM), F32),
        grid=(2, ng, nblk),
        in_specs=[pl.BlockSpec((lb_logits.shape[0], hw), lambda d, h, j: (0, h)),
                  pl.BlockSpec((ROW_TILE, hw), lambda d, h, j: (blk(d, j), d * ng + h)),
                  pl.BlockSpec((ROW_TILE, hw), lambda d, h, j: (blk(d, j), (2 + d) * ng + h)),
                  pl.BlockSpec((ROW_TILE, hw), lambda d, h, j: (blk(d, j), 4 * ng + h)),
                  pl.BlockSpec((1, _HG_W_ROWS, 3 * HG_CHUNK), lambda d, h, j: (d, 0, 0)),
                  pl.BlockSpec((1, len(_HG_LEVELS) + 1, HG_CHUNK, HG_CHUNK), lambda d, h, j: (d, 0, 0, 0))],
        out_specs=pl.BlockSpec((1, ROW_TILE, hw), lambda d, h, j: (d, blk(d, j), h)),
        scratch_shapes=[pltpu.VMEM((hps, HEAD_DIM, HEAD_DIM), F32)],
        compiler_params=_params(("arbitrary", "arbitrary", "arbitrary"), 24),
        name="hgrn_scan",
    )(lb_logits, p, p, p, wmat, msk)


def _rope_tables(n_lat, n_ctx, dim):
    rows = n_lat // GRID_W
    row = jnp.repeat(jnp.arange(rows, dtype=F32), GRID_W)
    col = jnp.tile(jnp.arange(GRID_W, dtype=F32), rows)
    axis_dim = dim // 2
    inv_freq = ROPE_THETA ** (-jnp.arange(0, axis_dim, 2, dtype=F32) / axis_dim)
    ang = jnp.concatenate([row[:, None] * inv_freq, col[:, None] * inv_freq], axis=-1)
    cos = jnp.repeat(jnp.cos(ang), 2, axis=1)
    sin = jnp.repeat(jnp.sin(ang), 2, axis=1) * jnp.tile(jnp.asarray([-1.0, 1.0], F32), dim // 2)
    reps = LANES // dim
    cos = jnp.concatenate([jnp.ones((n_ctx, dim), F32), cos], axis=0)
    sin = jnp.concatenate([jnp.zeros((n_ctx, dim), F32), sin], axis=0)
    return jnp.tile(cos, (1, reps)), jnp.tile(sin, (1, reps))


def _rope(x, cos, sin):
    lane = lax.broadcasted_iota(I32, x.shape, 1)
    swapped = jnp.where((lane & 1) == 0, pltpu.roll(x, LANES - 1, 1), pltpu.roll(x, 1, 1))
    return x * cos + swapped * sin


def _head_norm(x, g):
    return x * lax.rsqrt(jnp.mean(x * x, axis=-1, keepdims=True) + EPS) * g


def _attn_prep_kernel(pq_ref, pkv_ref, pd_ref, ca_ref, sa_ref, cd_ref, sd_ref, qn_ref, kn_ref,
                      qa_ref, ka_ref, va_ref, qd_ref, kd_ref, vd_ref):
    hd = HEAD_DIM
    ca, sa, cd, sd = ca_ref[...], sa_ref[...], cd_ref[...], sd_ref[...]
    qn, kn = qn_ref[...], kn_ref[...]
    for h in range(GQA_HEADS):
        xq = _rope(_head_norm(pq_ref[:, h * hd:(h + 1) * hd], qn), ca, sa)
        qa_ref[:, h * hd:(h + 1) * hd] = (xq * (hd ** -0.5 * _LOG2E)).astype(BF16)
    for h in range(GQA_KV_HEADS):
        xk = _rope(_head_norm(pkv_ref[:, h * hd:(h + 1) * hd], kn), ca, sa)
        ka_ref[:, h * hd:(h + 1) * hd] = xk.astype(BF16)
    ones_col = jnp.where(lax.broadcasted_iota(I32, (pq_ref.shape[0], hd), 1) == 0, 1.0, 0.0).astype(BF16)
    for h in range(GQA_KV_HEADS):
        va_ref[:, (2 * h) * hd:(2 * h + 1) * hd] = pkv_ref[:, (GQA_KV_HEADS + h) * hd:
                                                            (GQA_KV_HEADS + h + 1) * hd].astype(BF16)
        va_ref[:, (2 * h + 1) * hd:(2 * h + 2) * hd] = ones_col
    first = lax.broadcasted_iota(I32, (pq_ref.shape[0], hd), 1) < hd // 2
    dw = DIFF_HEADS * hd
    for h in range(DIFF_HEADS):
        xq = _rope(pd_ref[:, h * hd:(h + 1) * hd], cd, sd) * ((hd // 2) ** -0.5 * _LOG2E)
        qd_ref[:, (2 * h) * hd:(2 * h + 1) * hd] = jnp.where(first, xq, 0.0).astype(BF16)
        qd_ref[:, (2 * h + 1) * hd:(2 * h + 2) * hd] = jnp.where(first, 0.0, xq).astype(BF16)
        xk = _rope(pd_ref[:, dw + h * hd:dw + (h + 1) * hd], cd, sd)
        kd_ref[:, h * hd:(h + 1) * hd] = xk.astype(BF16)
    for h in range(DIFF_HEADS):
        vd_ref[:, (2 * h) * hd:(2 * h + 1) * hd] = pd_ref[:, 2 * dw + h * hd:2 * dw + (h + 1) * hd].astype(BF16)
        vd_ref[:, (2 * h + 1) * hd:(2 * h + 2) * hd] = ones_col


def _attn_prep_call(p, tabs, qn, kn):
    t = p.shape[0]
    hd = HEAD_DIM
    gq, gkv, dw = GQA_HEADS * hd, 2 * GQA_KV_HEADS * hd, DIFF_HEADS * hd
    q_off = 6 * HG_HEADS * hd
    assert q_off % gq == 0 and (q_off + gq) % gkv == 0 and (q_off + gq + gkv) % (3 * dw) == 0
    row = lambda w: pl.BlockSpec((ROW_TILE, w), lambda i: (i, 0))
    vec = pl.BlockSpec((1, hd), lambda i: (0, 0))
    return pl.pallas_call(
        _attn_prep_kernel,
        out_shape=[jax.ShapeDtypeStruct((t, gq), BF16),
                   jax.ShapeDtypeStruct((t, gkv // 2), BF16),
                   jax.ShapeDtypeStruct((t, gkv), BF16),
                   jax.ShapeDtypeStruct((t, 2 * dw), BF16),
                   jax.ShapeDtypeStruct((t, dw), BF16),
                   jax.ShapeDtypeStruct((t, 2 * dw), BF16)],
        grid=(t // ROW_TILE,),
        in_specs=[pl.BlockSpec((ROW_TILE, gq), lambda i: (i, q_off // gq)),
                  pl.BlockSpec((ROW_TILE, gkv), lambda i: (i, (q_off + gq) // gkv)),
                  pl.BlockSpec((ROW_TILE, 3 * dw), lambda i: (i, (q_off + gq + gkv) // (3 * dw))),
                  row(hd), row(hd), row(hd), row(hd), vec, vec],
        out_specs=[row(gq), row(gkv // 2), row(gkv), row(2 * dw), row(dw), row(2 * dw)],
        compiler_params=_params(("arbitrary",), 24),
        name="attn_prep",
    )(p, p, p, *tabs, qn.reshape(1, hd), kn.reshape(1, hd))


def _flash_kernel(*refs, g, nh, n_ctx, n_lat, tkl, nct, tile_off, diff, lam_init):
    if diff:
        q_ref, k_ref, v_ref, lq1, lk1, lq2, lk2, dn_ref, o_ref = refs[:9]
    else:
        q_ref, k_ref, v_ref, o_ref = refs[:4]
    nc = nh * g
    m_sc, acc_sc, p_sc = (refs[len(refs) - (3 - n) * nc:len(refs) - (2 - n) * nc] for n in range(3))
    hd = HEAD_DIM
    i = pl.program_id(1) + tile_off
    nblk = n_lat // tkl
    chains = [(c, c // g) for c in range(nh * g)]

    def head(ref, rows, h, width=1):
        return ref[rows, h * width * hd:(h + 1) * width * hd]

    def vhead(rows, h):
        return head(v_ref, rows, h, 2)

    def normalised(c):
        acc = acc_sc[c][...]
        return acc[:, :hd] / acc[:, hd:hd + 1]

    ctx_rows = slice(0, n_ctx)
    for c, kh in chains:
        s = lax.dot_general(head(q_ref, slice(None), c), head(k_ref, ctx_rows, kh), _NT,
                            preferred_element_type=F32)
        m0 = jnp.max(s, axis=1, keepdims=True)
        p = jnp.exp2(s - m0)
        m_sc[c][...] = jnp.broadcast_to(m0, m_sc[c].shape)
        acc_sc[c][...] = jnp.dot(p.astype(BF16), vhead(ctx_rows, kh), preferred_element_type=F32)

    def probs(c, kh, rows):
        s = lax.dot_general(head(q_ref, slice(None), c), head(k_ref, rows, kh), _NT,
                            preferred_element_type=F32)
        m_prev = m_sc[c][...]
        m_new = jnp.maximum(m_prev, jnp.max(s, axis=1, keepdims=True))
        p = jnp.exp2(s - jnp.tile(m_new, (1, tkl // LANES)))
        alpha = jnp.exp2(m_prev - m_new)
        m_sc[c][...] = m_new
        return p.astype(BF16), jnp.tile(alpha, (1, 2))

    @pl.when(i >= nct)
    def _():
        for c, kh in chains:
            p, alpha = probs(c, kh, pl.ds(n_ctx, tkl))
            p_sc[c][0] = p
            acc_sc[c][...] = alpha * acc_sc[c][...]

        for jb in range(1, nblk):
            r0 = n_ctx + jb * tkl
            prev = pl.ds(r0 - tkl, tkl)
            slot = jb % 2
            alphas = []
            for c, kh in chains:
                p, alpha = probs(c, kh, pl.ds(r0, tkl))
                p_sc[c][slot] = p
                alphas.append(alpha)
            for c, kh in chains:
                pv = jnp.dot(p_sc[c][1 - slot], vhead(prev, kh), preferred_element_type=F32)
                acc_sc[c][...] = alphas[c] * (acc_sc[c][...] + pv)

        last = pl.ds(n_ctx + (nblk - 1) * tkl, tkl)
        for c, kh in chains:
            acc_sc[c][...] = acc_sc[c][...] + jnp.dot(p_sc[c][(nblk - 1) % 2], vhead(last, kh),
                                                      preferred_element_type=F32)

    if diff:
        lam = (jnp.exp(jnp.sum(lq1[...] * lk1[...], axis=1, keepdims=True))
               - jnp.exp(jnp.sum(lq2[...] * lk2[...], axis=1, keepdims=True)) + lam_init)
        for kh in range(nh):
            dd = normalised(2 * kh) - lam * normalised(2 * kh + 1)
            o_ref[:, kh * hd:(kh + 1) * hd] = (_head_norm(dd, dn_ref[...]) * (1.0 - lam_init)).astype(BF16)
    else:
        for c, _ in chains:
            o_ref[:, c * hd:(c + 1) * hd] = normalised(c).astype(BF16)


def _flash_call(q, k, v, n_ctx, g, nh, tile_off, extra=None, lam_init=0.0):
    t = k.shape[0]
    hd = HEAD_DIM
    n_kv = k.shape[1] // (nh * hd)
    n_lat = t - n_ctx
    nct = n_ctx // ROW_TILE
    tkl = 2048
    diff = extra is not None
    ow = nh * hd if diff else nh * g * hd
    in_specs = [pl.BlockSpec((ROW_TILE, nh * g * hd), lambda kv, i: (i + tile_off, kv)),
                pl.BlockSpec((t, nh * hd), lambda kv, i: (0, kv)),
                pl.BlockSpec((t, 2 * nh * hd), lambda kv, i: (0, kv))]
    args = [q, k, v]
    if diff:
        for a in extra:
            a = a.reshape(1, -1)
            in_specs.append(pl.BlockSpec(a.shape, lambda kv, i: (0, 0)))
            args.append(a)
    return pl.pallas_call(
        functools.partial(_flash_kernel, g=g, nh=nh, n_ctx=n_ctx, n_lat=n_lat, tkl=tkl, nct=nct,
                          tile_off=tile_off, diff=diff, lam_init=lam_init),
        out_shape=jax.ShapeDtypeStruct((t - tile_off * ROW_TILE, n_kv * ow), BF16),
        grid=(n_kv, t // ROW_TILE - tile_off),
        in_specs=in_specs,
        out_specs=pl.BlockSpec((ROW_TILE, ow), lambda kv, i: (i, kv)),
        scratch_shapes=([pltpu.VMEM((ROW_TILE, LANES), F32)] * (nh * g)
                        + [pltpu.VMEM((ROW_TILE, 2 * hd), F32)] * (nh * g)
                        + [pltpu.VMEM((2, ROW_TILE, tkl), BF16)] * (nh * g)),
        compiler_params=_params(("arbitrary", "arbitrary"), 48),
        name="diff_attention" if diff else "gqa_attention",
    )(*args)


def _outproj_kernel(of_ref, ob_ref, gate_ref, ng_ref, att_ref, dif_ref, wo_ref, x_ref, m_ref, n2_ref,
                    rw_ref, xo_ref, h2_ref, lg_ref):
    hd = HEAD_DIM
    hgw = HG_HEADS * hd
    ng = ng_ref[...]
    gate = gate_ref[...]
    silu_gate = gate * (1.0 / (1.0 + jnp.exp(-gate)))
    acc = jnp.dot(att_ref[...], wo_ref[hgw:hgw + att_ref.shape[1], :], preferred_element_type=F32)
    acc = acc + jnp.dot(dif_ref[...], wo_ref[hgw + att_ref.shape[1]:, :], preferred_element_type=F32)
    hg = []
    for h in range(HG_HEADS):
        o = of_ref[0, :, h * hd:(h + 1) * hd] + ob_ref[0, :, h * hd:(h + 1) * hd]
        hg.append((_head_norm(o, ng) * silu_gate[:, h * hd:(h + 1) * hd]).astype(BF16))
    acc = acc + jnp.dot(jnp.concatenate(hg, axis=1), wo_ref[0:hgw, :], preferred_element_type=F32)
    m = m_ref[0]
    xn = x_ref[...] + m[2:3] * acc
    xo_ref[...] = xn
    h2 = _norm_mod(xn, n2_ref[...], m[3:4], m[4:5])
    h2_ref[...] = h2
    hs = _split_bf16(h2, 3)
    ws = _split_bf16(rw_ref[...], 3)
    lg = jnp.zeros(lg_ref.shape, F32)
    for a, b in ((2, 0), (1, 1), (0, 2), (1, 0), (0, 1), (0, 0)):
        lg = lg + jnp.dot(hs[a], ws[b], preferred_element_type=F32)
    lg_ref[...] = lg


def _outproj_call(o, p, ng, att, dif, wo, x, mods, n2, rw, nct, tile_off):
    d = x.shape[1]
    hd = HEAD_DIM
    hgw = HG_HEADS * hd
    n_rows = att.shape[0]
    row = lambda w: pl.BlockSpec((ROW_TILE, w), lambda i: (i, 0))
    full = lambda a: pl.BlockSpec(a.shape, lambda i: (0,) * a.ndim)
    ng, n2 = ng.reshape(1, hd), n2.reshape(1, d)
    x_off = (x.shape[0] - n_rows) // ROW_TILE
    return pl.pallas_call(
        _outproj_kernel,
        out_shape=[jax.ShapeDtypeStruct((n_rows, d), F32),
                   jax.ShapeDtypeStruct((n_rows, d), F32),
                   jax.ShapeDtypeStruct((n_rows, LANES), F32)],
        grid=(n_rows // ROW_TILE,),
        in_specs=[pl.BlockSpec((1, ROW_TILE, hgw), lambda i: (0, i + tile_off, 0)),
                  pl.BlockSpec((1, ROW_TILE, hgw), lambda i: (1, i + tile_off, 0)),
                  pl.BlockSpec((ROW_TILE, hgw), lambda i: (i + tile_off, 5)),
                  full(ng), row(att.shape[1]), row(dif.shape[1]), full(wo),
                  pl.BlockSpec((ROW_TILE, d), lambda i: (i + x_off, 0)),
                  _mod_spec(d, nct, tile_off), full(n2), full(rw)],
        out_specs=[row(d), row(d), row(LANES)],
        compiler_params=_params(("arbitrary",), 56),
        name="out_proj",
    )(o, o, p, ng, att, dif, wo, x, mods, n2, rw)


def _router_kernel(lg_ref, bias_ref, r_ref, cnt_ref, run_ref):
    i = pl.program_id(0)
    tm = lg_ref.shape[0]

    @pl.when(i == 0)
    def _():
        run_ref[...] = jnp.zeros_like(run_ref)

    lane = lax.broadcasted_iota(I32, (tm, LANES), 1).astype(F32)
    lg = lg_ref[...] + bias_ref[...]
    ninf = -jnp.inf

    def first_max(vals):
        mx = jnp.max(vals, axis=1, keepdims=True)
        idx = jnp.min(jnp.where(vals == mx, lane, float(LANES)), axis=1, keepdims=True)
        return mx, idx

    gl = jnp.where(lane < N_GROUPS, lg, ninf)
    gmax, gidx = first_max(gl)
    g_top = 1.0 / jnp.sum(jnp.exp(gl - gmax), axis=1, keepdims=True)
    lo = N_GROUPS + EXPERTS_PER_GROUP * gidx
    el = jnp.where((lane >= lo) & (lane < lo + EXPERTS_PER_GROUP), lg, ninf)
    m1, e1 = first_max(el)
    m2, e2 = first_max(jnp.where(lane == e1, ninf, el))
    r = jnp.exp(m2 - m1)
    w1 = g_top / (1.0 + r)
    w2 = g_top * r / (1.0 + r)

    hit = ((lane == e1) | (lane == e2)).astype(BF16)
    ti = lax.broadcasted_iota(I32, (tm, tm), 0)
    si = lax.broadcasted_iota(I32, (tm, tm), 1)
    before = (si < ti).astype(BF16)
    pos = jnp.dot(before, hit, preferred_element_type=F32) + run_ref[0:1, :]
    p1 = jnp.sum(jnp.where(lane == e1, pos, 0.0), axis=1, keepdims=True)
    p2 = jnp.sum(jnp.where(lane == e2, pos, 0.0), axis=1, keepdims=True)
    total = run_ref[0:1, :] + jnp.sum(hit.astype(F32), axis=0, keepdims=True)
    run_ref[...] = jnp.broadcast_to(total, run_ref.shape)
    cnt_ref[...] = jnp.broadcast_to(total, cnt_ref.shape)

    fields = (e1 - N_GROUPS, e2 - N_GROUPS, w1, w2, p1, p2)
    out = jnp.zeros((tm, LANES), F32)
    for n, f in enumerate(fields):
        out = jnp.where(lane == n, f, out)
    r_ref[...] = out


def _router_call(logits, bias):
    n = logits.shape[0]
    return pl.pallas_call(
        _router_kernel,
        out_shape=[jax.ShapeDtypeStruct((n, LANES), F32), jax.ShapeDtypeStruct((8, LANES), F32)],
        grid=(n // ROW_TILE,),
        in_specs=[pl.BlockSpec((ROW_TILE, LANES), lambda i: (i, 0)),
                  pl.BlockSpec((1, LANES), lambda i: (0, 0))],
        out_specs=[pl.BlockSpec((ROW_TILE, LANES), lambda i: (i, 0)),
                   pl.BlockSpec((8, LANES), lambda i: (0, 0))],
        scratch_shapes=[pltpu.VMEM((8, LANES), F32)],
        compiler_params=_params(("arbitrary",), 16),
        name="router",
    )(logits, bias)


def _moe_kernel(src_ref, te_ref, nt_ref, h_hbm, wg_ref, wu_ref, wd_ref, y_ref,
                xbuf, sem, wg_bf, wu_bf, wd_bf):
    i = pl.program_id(0)
    n_tiles = nt_ref[0]
    tm = xbuf.shape[1]

    def start_tile(tile, slot):
        def body(grp, carry):
            for u in range(_DMA_GROUP):
                r = grp * _DMA_GROUP + u
                pltpu.make_async_copy(h_hbm.at[pl.ds(src_ref[tile * tm + r], 1), :],
                                      xbuf.at[slot, pl.ds(r, 1), :], sem.at[slot]).start(priority=u % 2)
            return carry
        lax.fori_loop(0, tm // _DMA_GROUP, body, 0)

    def wait_tile(slot):
        pltpu.make_async_copy(h_hbm.at[pl.ds(0, tm), :], xbuf.at[slot], sem.at[slot]).wait()

    @pl.when(i == 0)
    def _():
        start_tile(0, 0)

    @pl.when(i < n_tiles)
    def _():
        slot = i % 2

        @pl.when(i + 1 < n_tiles)
        def _():
            start_tile(i + 1, 1 - slot)

        changed = jnp.logical_or(i == 0, te_ref[i] != te_ref[jnp.maximum(i - 1, 0)])

        @pl.when(changed)
        def _():
            wg_bf[...] = wg_ref[0, 0].astype(BF16)
            wu_bf[...] = wu_ref[0, 0].astype(BF16)
            wd_bf[...] = wd_ref[0, 0].astype(BF16)

        wait_tile(slot)
        xb = xbuf[slot].astype(BF16)
        gt = jnp.dot(xb, wg_bf[...], preferred_element_type=F32)
        up = jnp.dot(xb, wu_bf[...], preferred_element_type=F32)
        act = (gt * (1.0 / (1.0 + jnp.exp(-gt))) * up).astype(BF16)
        y_ref[...] = jnp.dot(act, wd_bf[...], preferred_element_type=F32)

    @pl.when(i >= n_tiles)
    def _():
        y_ref[...] = jnp.zeros_like(y_ref)


def _moe_call(h2, src, tile_expert, n_tiles, wg, wu, wd, layer):
    d = h2.shape[1]
    ff = wg.shape[3]
    max_tiles = tile_expert.shape[0]
    tm = MOE_TILE
    wspec = lambda s: pl.BlockSpec((1, 1) + s, lambda i, src, te, nt: (layer, te[i], 0, 0))
    grid_spec = pltpu.PrefetchScalarGridSpec(
        num_scalar_prefetch=3,
        grid=(max_tiles,),
        in_specs=[pl.BlockSpec(memory_space=pl.ANY), wspec((d, ff)), wspec((d, ff)), wspec((ff, d))],
        out_specs=pl.BlockSpec((tm, d), lambda i, src, te, nt: (i, 0)),
        scratch_shapes=[pltpu.VMEM((2, tm, d), F32), pltpu.SemaphoreType.DMA((2,)),
                        pltpu.VMEM((d, ff), BF16), pltpu.VMEM((d, ff), BF16), pltpu.VMEM((ff, d), BF16)])
    return pl.pallas_call(
        _moe_kernel,
        out_shape=jax.ShapeDtypeStruct((max_tiles * tm, d), F32),
        grid_spec=grid_spec,
        compiler_params=_params(("arbitrary",), 56),
        name="moe_experts",
    )(src, tile_expert, n_tiles, h2, wg, wu, wd)


def _combine_kernel(dst_ref, x_ref, rt_ref, m_ref, g_ref, mn_ref, y_hbm, *rest, last, n_out):
    outs, (ybuf, sem) = rest[:n_out], rest[n_out:]
    i = pl.program_id(0)
    tm = x_ref.shape[0]

    def start_tile(tile, slot):
        def body(grp, carry):
            for u in range(_DMA_GROUP):
                r = grp * _DMA_GROUP + u
                for j in range(2):
                    pltpu.make_async_copy(y_hbm.at[pl.ds(dst_ref[(tile * tm + r) * 2 + j], 1), :],
                                          ybuf.at[slot, j, pl.ds(r, 1), :], sem.at[slot]).start(priority=j)
            return carry
        lax.fori_loop(0, tm // _DMA_GROUP, body, 0)

    @pl.when(i == 0)
    def _():
        start_tile(0, 0)

    slot = i % 2

    @pl.when(i + 1 < pl.num_programs(0))
    def _():
        start_tile(i + 1, 1 - slot)

    for j in range(2):
        pltpu.make_async_copy(y_hbm.at[pl.ds(0, tm), :], ybuf.at[slot, j], sem.at[slot]).wait()
    rt = rt_ref[...]
    y = rt[:, 2:3] * ybuf[slot, 0] + rt[:, 3:4] * ybuf[slot, 1]
    xn = x_ref[...] + m_ref[0][5:6] * y
    if last:
        outs[0][...] = xn * lax.rsqrt(jnp.mean(xn * xn, axis=-1, keepdims=True) + EPS) * g_ref[...]
    else:
        outs[0][...] = xn
        mn = mn_ref[0]
        outs[1][...] = _norm_mod(xn, g_ref[...], mn[0:1], mn[1:2]).astype(BF16)


def _combine_call(dst, x, route, mods, g, mods_next, ys, nct, last):
    n, d = x.shape
    tm = ROW_TILE
    mspec = pl.BlockSpec((1, N_MOD, d), lambda i, dst: (jnp.where(i < nct, 1, 0), 0, 0))
    out_shape = [jax.ShapeDtypeStruct((n, d), F32)]
    if not last:
        out_shape.append(jax.ShapeDtypeStruct((n, d), BF16))
    row = lambda w: pl.BlockSpec((tm, w), lambda i, dst: (i, 0))
    grid_spec = pltpu.PrefetchScalarGridSpec(
        num_scalar_prefetch=1,
        grid=(n // tm,),
        in_specs=[row(d), row(LANES), mspec, pl.BlockSpec((1, d), lambda i, dst: (0, 0)), mspec,
                  pl.BlockSpec(memory_space=pl.ANY)],
        out_specs=[row(d)] * len(out_shape),
        scratch_shapes=[pltpu.VMEM((2, 2, tm, d), F32), pltpu.SemaphoreType.DMA((2,))])
    return pl.pallas_call(
        functools.partial(_combine_kernel, last=last, n_out=len(out_shape)),
        out_shape=out_shape,
        grid_spec=grid_spec,
        compiler_params=_params(("arbitrary",), 40),
        name="moe_combine",
    )(dst, x, route, mods, g.reshape(1, d), mods_next, ys)


def _dispatch_plan(route, counts, n_tokens):
    tm = MOE_TILE
    max_tiles = (2 * n_tokens) // tm + N_EXPERTS
    cnt = counts[0, N_GROUPS:N_GROUPS + N_EXPERTS].astype(I32)
    tiles_per = (cnt + tm - 1) // tm
    tile_end = jnp.cumsum(tiles_per)
    offs = (tile_end - tiles_per) * tm
    eid = route[:, 0:2].astype(I32)
    dst = offs[eid] + route[:, 4:6].astype(I32)
    tok = jnp.broadcast_to(jnp.arange(n_tokens, dtype=I32)[:, None], (n_tokens, 2))
    src = jnp.zeros((max_tiles * tm,), I32).at[dst.reshape(-1)].set(tok.reshape(-1))
    n_tiles = tile_end[-1:]
    tile_ids = jnp.minimum(jnp.arange(max_tiles, dtype=I32), n_tiles[0] - 1)
    tile_expert = jnp.sum((tile_end[None, :] <= tile_ids[:, None]).astype(I32), axis=1)
    return dst.reshape(-1), src, tile_expert, n_tiles.astype(I32)


def kernel(x, c, ctx, c_ctx, w_mod, b_mod, norm1_g, norm2_g, w_in, w_out, hg_lb_logits, hg_norm_g,
           q_norm_g, k_norm_g, lam_q1, lam_k1, lam_q2, lam_k2, diff_norm_g, router_group_w,
           router_group_b, router_expert_w, router_expert_b, w_gate, w_up, w_down, final_norm_g):
    depth = w_in.shape[0]
    n_lat, d = x.shape[1], x.shape[2]
    n_ctx = ctx.shape[1]
    nct = n_ctx // ROW_TILE
    assert x.shape[0] == 1 and n_ctx % ROW_TILE == 0 and n_lat % 1024 == 0

    xall = jnp.concatenate([ctx[0], x[0]], axis=0)
    cc = jnp.zeros((8, d), F32).at[0].set(c[0]).at[1].set(c_ctx)
    mods_all = _mod_call(cc, w_mod, b_mod)[:, :2].reshape(depth, 2, N_MOD, d)
    tabs = _rope_tables(n_lat, n_ctx, HEAD_DIM) + _rope_tables(n_lat, n_ctx, HEAD_DIM // 2)

    h1 = _prenorm_call(xall, norm1_g[0], mods_all[0], nct)
    out = None
    for l in range(depth):
        last = l == depth - 1
        mods = mods_all[l]
        tile_off = nct if last else 0
        tm_in = 768 if (n_ctx + n_lat) % 768 == 0 else ROW_TILE
        p = _matmul_call(h1, w_in, l, tm_in, 1024)
        o = _hgrn_call(p, hg_lb_logits, l, nct)
        qa, ka, va, qd, kd, vd = _attn_prep_call(p, tabs, q_norm_g[l], k_norm_g[l])
        att = _flash_call(qa, ka, va, n_ctx, GQA_HEADS // GQA_KV_HEADS, 1, tile_off)
        lam_init = 0.8 - 0.6 * math.exp(-0.3 * l)
        dif = _flash_call(qd, kd, vd, n_ctx, 2, 2, tile_off,
                          extra=(lam_q1[l], lam_k1[l], lam_q2[l], lam_k2[l], diff_norm_g[l]),
                          lam_init=lam_init)
        rw = jnp.concatenate([router_group_w[l], router_expert_w[l],
                              jnp.zeros((d, LANES - N_GROUPS - N_EXPERTS), F32)], axis=1)
        rb = jnp.concatenate([router_group_b[l], router_expert_b[l],
                              jnp.zeros((LANES - N_GROUPS - N_EXPERTS,), F32)]).reshape(1, LANES)
        xn, h2, logits = _outproj_call(o, p, hg_norm_g[l], att, dif, w_out[l].astype(BF16), xall,
                                       mods, norm2_g[l], rw, nct, tile_off)
        n_tok = xn.shape[0]
        route, counts = _router_call(logits, rb)
        dst, src, tile_expert, n_tiles = _dispatch_plan(route, counts, n_tok)
        ys = _moe_call(h2, src, tile_expert, n_tiles, w_gate, w_up, w_down, l)
        if last:
            (out,) = _combine_call(dst, xn, route, mods, final_norm_g, mods, ys, 0, True)
        else:
            xall, h1 = _combine_call(dst, xn, route, mods, norm1_g[l + 1], mods_all[l + 1], ys, nct, False)
    return out.reshape(1, n_lat, d)
```

```python
import functools
import math

import numpy as np
import jax
import jax.numpy as jnp
from jax import lax
from jax.experimental import pallas as pl
from jax.experimental.pallas import tpu as pltpu

F32 = jnp.float32
BF16 = jnp.bfloat16
I32 = jnp.int32

HEAD_DIM = 128
LANES = 128
GRID_W = 64
ROPE_THETA = 10000.0
EPS = 1e-6
HG_HEADS = 4
HG_CHUNK = 64
GQA_HEADS = 8
GQA_KV_HEADS = 2
DIFF_HEADS = 4
N_GROUPS = 4
EXPERTS_PER_GROUP = 8
N_EXPERTS = N_GROUPS * EXPERTS_PER_GROUP
N_MOD = 6
ROW_TILE = 256
MOE_TILE = 256
_DMA_GROUP = 8
MIB = 1024 * 1024

_LOG2E = math.log2(math.e)
_NT = (((1,), (1,)), ((), ()))
_TN = (((0,), (0,)), ((), ()))


def _params(semantics, vmem_mib):
    return pltpu.CompilerParams(dimension_semantics=semantics, vmem_limit_bytes=vmem_mib * MIB)


def _split_bf16(x, parts):
    out = []
    for _ in range(parts - 1):
        p = x.astype(BF16)
        out.append(p)
        x = x - p.astype(F32)
    out.append(x.astype(BF16))
    return out


def _mod_kernel(a_ref, w_ref, b_ref, o_ref):
    a = a_ref[...]
    a = a * (1.0 / (1.0 + jnp.exp(-a)))
    hi, lo = _split_bf16(a, 2)
    w = w_ref[0].astype(BF16)
    o_ref[0] = (jnp.dot(hi, w, preferred_element_type=F32)
                + jnp.dot(lo, w, preferred_element_type=F32) + b_ref[0])


def _mod_call(cc, w_mod, b_mod):
    depth, d, n = w_mod.shape
    tn = 1024
    return pl.pallas_call(
        _mod_kernel,
        out_shape=jax.ShapeDtypeStruct((depth, 8, n), F32),
        grid=(depth, n // tn),
        in_specs=[pl.BlockSpec((8, d), lambda l, j: (0, 0)),
                  pl.BlockSpec((1, d, tn), lambda l, j: (l, 0, j)),
                  pl.BlockSpec((1, 1, tn), lambda l, j: (l, 0, j))],
        out_specs=pl.BlockSpec((1, 8, tn), lambda l, j: (l, 0, j)),
        compiler_params=_params(("arbitrary", "arbitrary"), 40),
        name="mod_vectors",
    )(cc, w_mod, b_mod.reshape(depth, 1, n))


def _norm_mod(x, g, shift, scale):
    y = x * lax.rsqrt(jnp.mean(x * x, axis=-1, keepdims=True) + EPS) * g
    return y * (1.0 + scale) + shift


def _stream_specs(d, nct, tile_off=0, lat_off=0):
    return [pl.BlockSpec((ROW_TILE, d), lambda i: (jnp.minimum(i + tile_off, nct - 1), 0)),
            pl.BlockSpec((ROW_TILE, d), lambda i: (jnp.maximum(i + tile_off - nct, 0) + lat_off, 0))]


def _stream_tile(c_ref, x_ref, nct, tile_off=0):
    return jnp.where(pl.program_id(0) + tile_off < nct, c_ref[...], x_ref[...])


def _prenorm_kernel(c_ref, x_ref, g_ref, m_ref, o_ref, *, nct):
    m = m_ref[0]
    o_ref[...] = _norm_mod(_stream_tile(c_ref, x_ref, nct), g_ref[...], m[0:1], m[1:2]).astype(BF16)


def _mod_spec(d, nct, tile_off=0):
    return pl.BlockSpec((1, N_MOD, d), lambda i: (jnp.where(i + tile_off < nct, 1, 0), 0, 0))


def _prenorm_call(xc, xl, g, mods, nct):
    d = xl.shape[1]
    t = xc.shape[0] + xl.shape[0]
    return pl.pallas_call(
        functools.partial(_prenorm_kernel, nct=nct),
        out_shape=jax.ShapeDtypeStruct((t, d), BF16),
        grid=(t // ROW_TILE,),
        in_specs=_stream_specs(d, nct) + [pl.BlockSpec((1, d), lambda i: (0, 0)), _mod_spec(d, nct)],
        out_specs=pl.BlockSpec((ROW_TILE, d), lambda i: (i, 0)),
        compiler_params=_params(("arbitrary",), 24),
        name="prenorm",
    )(xc, xl, g.reshape(1, d), mods)


def _mm_kernel(a_ref, b_ref, o_ref, b_bf):
    @pl.when(pl.program_id(1) == 0)
    def _():
        b_bf[...] = b_ref[0].astype(BF16)

    o_ref[...] = jnp.dot(a_ref[...], b_bf[...], preferred_element_type=F32)


def _matmul_call(a, b, layer, tm, tn):
    m, k = a.shape
    n = b.shape[2]
    return pl.pallas_call(
        _mm_kernel,
        out_shape=jax.ShapeDtypeStruct((m, n), F32),
        grid=(n // tn, m // tm),
        in_specs=[pl.BlockSpec((tm, k), lambda j, i: (i, 0)),
                  pl.BlockSpec((1, k, tn), lambda j, i: (layer, 0, j))],
        out_specs=pl.BlockSpec((tm, tn), lambda j, i: (i, j)),
        scratch_shapes=[pltpu.VMEM((k, tn), BF16)],
        compiler_params=_params(("arbitrary", "arbitrary"), 48),
        name="in_proj",
    )(a, b)


_HG_LEVELS = (1, 2, 4, 8, 16, 32)
_HG_TOT_ROW = HG_CHUNK * (len(_HG_LEVELS) + 1)
_HG_W_ROWS = _HG_TOT_ROW + 16
_HG_HEADS_PER_STEP = 4


def _hgrn_consts():
    c = HG_CHUNK
    w = np.zeros((2, _HG_W_ROWS, c), np.float32)
    msk = np.zeros((2, len(_HG_LEVELS) + 1, c, c), np.float32)
    for d in range(2):
        u = np.arange(c) if d == 0 else c - 1 - np.arange(c)
        ut, us = u[:, None], u[None, :]
        w[d, :c] = us <= ut
        for li, lv in enumerate(_HG_LEVELS):
            blk = u // (2 * lv)
            qside = (u % (2 * lv)) >= lv
            bnd = (blk * 2 * lv + lv - 1)[:, None]
            wq = (us > bnd) & (us <= ut)
            wk = (us > ut) & (us <= bnd)
            w[d, c * (li + 1):c * (li + 2)] = np.where(qside[:, None], wq, -1.0 * wk)
            msk[d, li] = (blk[:, None] == blk[None, :]) & qside[:, None] & ~qside[None, :]
        msk[d, len(_HG_LEVELS)] = np.eye(c)
        w[d, _HG_TOT_ROW:] = 1.0
    return jnp.asarray(np.concatenate([w, w, w], axis=2), BF16), jnp.asarray(msk, F32)


def _hgrn_kernel(lbl_ref, q_ref, z_ref, v_ref, w_ref, msk_ref, o_ref, st_ref, *, layer, chunks, hps):
    c = HG_CHUNK
    hd = HEAD_DIM
    d = pl.program_id(0)
    j = pl.program_id(2)

    @pl.when(j == 0)
    def _():
        st_ref[...] = jnp.zeros_like(st_ref)

    lbl = lbl_ref[...]
    rows = [lbl[i:i + 1] for i in range(lbl.shape[0])]
    mx = functools.reduce(jnp.maximum, rows)
    ex = [jnp.exp(r - mx) for r in rows]
    tot = functools.reduce(lambda a, b: a + b, ex)
    lb = jnp.zeros_like(mx)
    for i in range(1, layer + 1):
        lb = lb + ex[i] / tot
    log_lb = jnp.log(lb)
    log_1m_lb = jnp.log1p(-lb)

    wmat = w_ref[0]
    nlev = len(_HG_LEVELS)

    def finish(h, hs, r0, amat, v_bf, qe, kd, decay):
        st = st_ref[h]
        o = (jnp.dot(amat, v_bf[:, hs], preferred_element_type=F32)
             + lax.dot_general(qe[:, hs], st.astype(BF16), _NT, preferred_element_type=F32))
        o_ref[0, pl.ds(r0, c), hs] = o
        st_ref[h] = st * decay[:, hs] + lax.dot_general(v_bf[:, hs], kd[:, hs], _TN,
                                                        preferred_element_type=F32)

    pending = None
    for ci in range(chunks):
        cc = ci + d * (chunks - 1 - 2 * ci)
        r0 = pl.multiple_of(cc * c, c)
        q = q_ref[pl.ds(r0, c), :]
        z = z_ref[pl.ds(r0, c), :]
        v_bf = v_ref[pl.ds(r0, c), :].astype(BF16)
        l1p = jnp.log(1.0 + jnp.exp(-jnp.abs(z)))
        ls_pos = jnp.minimum(z, 0.0) - l1p
        ls_neg = jnp.minimum(-z, 0.0) - l1p
        a2 = log_1m_lb + ls_pos
        logf = jnp.maximum(log_lb, a2) + jnp.log(1.0 + jnp.exp(-jnp.abs(log_lb - a2)))
        k = (1.0 - lb) * jnp.exp(ls_neg)

        parts = jnp.concatenate(_split_bf16(logf, 3), axis=0)
        sums = jnp.dot(wmat, parts, preferred_element_type=F32)
        b = sums[0:c]
        btot = sums[_HG_TOT_ROW:_HG_TOT_ROW + 1]

        qls, kls = [q.astype(BF16)], [k.astype(BF16)]
        for li in range(nlev):
            fl = jnp.exp(-jnp.abs(sums[c * (li + 1):c * (li + 2)]))
            qls.append((q * fl).astype(BF16))
            kls.append((k * fl).astype(BF16))
        qe = (q * jnp.exp(b)).astype(BF16)
        kd = (k * jnp.exp(btot - b)).astype(BF16)
        decay = jnp.exp(btot)

        for h in range(hps):
            hs = slice(h * hd, (h + 1) * hd)
            amat = lax.dot_general(qls[0][:, hs], kls[0][:, hs], _NT,
                                   preferred_element_type=F32) * msk_ref[0, nlev]
            for li in range(nlev):
                amat = amat + lax.dot_general(qls[li + 1][:, hs], kls[li + 1][:, hs], _NT,
                                              preferred_element_type=F32) * msk_ref[0, li]
            if pending is not None:
                finish(*pending)
            pending = (h, hs, r0, amat.astype(BF16), v_bf, qe, kd, decay)
    finish(*pending)


def _hgrn_call(p, lb_logits, layer, nct):
    t = p.shape[0]
    nblk = t // ROW_TILE
    chunks = ROW_TILE // HG_CHUNK
    hps = _HG_HEADS_PER_STEP
    hw = hps * HEAD_DIM
    ng = HG_HEADS // hps
    wmat, msk = _hgrn_consts()

    def blk(d, j):
        back = jnp.where(j < nct, nct - 1 - j, nblk - 1 - (j - nct))
        return jnp.where(d == 0, j, back)

    return pl.pallas_call(
        functools.partial(_hgrn_kernel, layer=layer, chunks=chunks, hps=hps),
        out_shape=jax.ShapeDtypeStruct((2, t, HG_HEADS * HEAD_DIM), F32),
        grid=(2, ng, nblk),
        in_specs=[pl.BlockSpec((lb_logits.shape[0], hw), lambda d, h, j: (0, h)),
                  pl.BlockSpec((ROW_TILE, hw), lambda d, h, j: (blk(d, j), d * ng + h)),
                  pl.BlockSpec((ROW_TILE, hw), lambda d, h, j: (blk(d, j), (2 + d) * ng + h)),
                  pl.BlockSpec((ROW_TILE, hw), lambda d, h, j: (blk(d, j), 4 * ng + h)),
                  pl.BlockSpec((1, _HG_W_ROWS, 3 * HG_CHUNK), lambda d, h, j: (d, 0, 0)),
                  pl.BlockSpec((1, len(_HG_LEVELS) + 1, HG_CHUNK, HG_CHUNK), lambda d, h, j: (d, 0, 0, 0))],
        out_specs=pl.BlockSpec((1, ROW_TILE, hw), lambda d, h, j: (d, blk(d, j), h)),
        scratch_shapes=[pltpu.VMEM((hps, HEAD_DIM, HEAD_DIM), F32)],
        compiler_params=_params(("arbitrary", "arbitrary", "arbitrary"), 24),
        name="hgrn_scan",
    )(lb_logits, p, p, p, wmat, msk)


def _rope_tables(n_lat, n_ctx, dim):
    rows = n_lat // GRID_W
    row = jnp.repeat(jnp.arange(rows, dtype=F32), GRID_W)
    col = jnp.tile(jnp.arange(GRID_W, dtype=F32), rows)
    axis_dim = dim // 2
    inv_freq = ROPE_THETA ** (-jnp.arange(0, axis_dim, 2, dtype=F32) / axis_dim)
    ang = jnp.concatenate([row[:, None] * inv_freq, col[:, None] * inv_freq], axis=-1)
    cos = jnp.repeat(jnp.cos(ang), 2, axis=1)
    sin = jnp.repeat(jnp.sin(ang), 2, axis=1) * jnp.tile(jnp.asarray([-1.0, 1.0], F32), dim // 2)
    reps = LANES // dim
    cos = jnp.concatenate([jnp.ones((n_ctx, dim), F32), cos], axis=0)
    sin = jnp.concatenate([jnp.zeros((n_ctx, dim), F32), sin], axis=0)
    return jnp.tile(cos, (1, reps)), jnp.tile(sin, (1, reps))


def _rope(x, cos, sin):
    lane = lax.broadcasted_iota(I32, x.shape, 1)
    swapped = jnp.where((lane & 1) == 0, pltpu.roll(x, LANES - 1, 1), pltpu.roll(x, 1, 1))
    return x * cos + swapped * sin


def _head_norm(x, g):
    return x * lax.rsqrt(jnp.mean(x * x, axis=-1, keepdims=True) + EPS) * g


def _attn_prep_kernel(pq_ref, pkv_ref, pd_ref, ca_ref, sa_ref, cd_ref, sd_ref, qn_ref, kn_ref,
                      qa_ref, ka_ref, va_ref, qd_ref, kd_ref, vd_ref):
    hd = HEAD_DIM
    ca, sa, cd, sd = ca_ref[...], sa_ref[...], cd_ref[...], sd_ref[...]
    qn, kn = qn_ref[...], kn_ref[...]
    for h in range(GQA_HEADS):
        xq = _rope(_head_norm(pq_ref[:, h * hd:(h + 1) * hd], qn), ca, sa)
        qa_ref[:, h * hd:(h + 1) * hd] = (xq * (hd ** -0.5 * _LOG2E)).astype(BF16)
    for h in range(GQA_KV_HEADS):
        xk = _rope(_head_norm(pkv_ref[:, h * hd:(h + 1) * hd], kn), ca, sa)
        ka_ref[:, h * hd:(h + 1) * hd] = xk.astype(BF16)
    ones_col = jnp.where(lax.broadcasted_iota(I32, (pq_ref.shape[0], hd), 1) == 0, 1.0, 0.0).astype(BF16)
    for h in range(GQA_KV_HEADS):
        va_ref[:, (2 * h) * hd:(2 * h + 1) * hd] = pkv_ref[:, (GQA_KV_HEADS + h) * hd:
                                                            (GQA_KV_HEADS + h + 1) * hd].astype(BF16)
        va_ref[:, (2 * h + 1) * hd:(2 * h + 2) * hd] = ones_col
    first = lax.broadcasted_iota(I32, (pq_ref.shape[0], hd), 1) < hd // 2
    dw = DIFF_HEADS * hd
    for h in range(DIFF_HEADS):
        xq = _rope(pd_ref[:, h * hd:(h + 1) * hd], cd, sd) * ((hd // 2) ** -0.5 * _LOG2E)
        qd_ref[:, (2 * h) * hd:(2 * h + 1) * hd] = jnp.where(first, xq, 0.0).astype(BF16)
        qd_ref[:, (2 * h + 1) * hd:(2 * h + 2) * hd] = jnp.where(first, 0.0, xq).astype(BF16)
        xk = _rope(pd_ref[:, dw + h * hd:dw + (h + 1) * hd], cd, sd)
        kd_ref[:, h * hd:(h + 1) * hd] = xk.astype(BF16)
    for h in range(DIFF_HEADS):
        vd_ref[:, (2 * h) * hd:(2 * h + 1) * hd] = pd_ref[:, 2 * dw + h * hd:2 * dw + (h + 1) * hd].astype(BF16)
        vd_ref[:, (2 * h + 1) * hd:(2 * h + 2) * hd] = ones_col


def _attn_prep_call(p, tabs, qn, kn):
    t = p.shape[0]
    hd = HEAD_DIM
    gq, gkv, dw = GQA_HEADS * hd, 2 * GQA_KV_HEADS * hd, DIFF_HEADS * hd
    q_off = 6 * HG_HEADS * hd
    assert q_off % gq == 0 and (q_off + gq) % gkv == 0 and (q_off + gq + gkv) % (3 * dw) == 0
    row = lambda w: pl.BlockSpec((ROW_TILE, w), lambda i: (i, 0))
    vec = pl.BlockSpec((1, hd), lambda i: (0, 0))
    return pl.pallas_call(
        _attn_prep_kernel,
        out_shape=[jax.ShapeDtypeStruct((t, gq), BF16),
                   jax.ShapeDtypeStruct((t, gkv // 2), BF16),
                   jax.ShapeDtypeStruct((t, gkv), BF16),
                   jax.ShapeDtypeStruct((t, 2 * dw), BF16),
                   jax.ShapeDtypeStruct((t, dw), BF16),
                   jax.ShapeDtypeStruct((t, 2 * dw), BF16)],
        grid=(t // ROW_TILE,),
        in_specs=[pl.BlockSpec((ROW_TILE, gq), lambda i: (i, q_off // gq)),
                  pl.BlockSpec((ROW_TILE, gkv), lambda i: (i, (q_off + gq) // gkv)),
                  pl.BlockSpec((ROW_TILE, 3 * dw), lambda i: (i, (q_off + gq + gkv) // (3 * dw))),
                  row(hd), row(hd), row(hd), row(hd), vec, vec],
        out_specs=[row(gq), row(gkv // 2), row(gkv), row(2 * dw), row(dw), row(2 * dw)],
        compiler_params=_params(("arbitrary",), 24),
        name="attn_prep",
    )(p, p, p, *tabs, qn.reshape(1, hd), kn.reshape(1, hd))


def _flash_kernel(*refs, g, nh, n_ctx, n_lat, tkl, nct, tile_off, diff, lam_init):
    if diff:
        q_ref, k_ref, v_ref, lq1, lk1, lq2, lk2, dn_ref, o_ref = refs[:9]
    else:
        q_ref, k_ref, v_ref, o_ref = refs[:4]
    nc = nh * g
    m_sc, acc_sc, p_sc = (refs[len(refs) - (3 - n) * nc:len(refs) - (2 - n) * nc] for n in range(3))
    hd = HEAD_DIM
    i = pl.program_id(1) + tile_off
    nblk = n_lat // tkl
    chains = [(c, c // g) for c in range(nh * g)]

    def head(ref, rows, h, width=1):
        return ref[rows, h * width * hd:(h + 1) * width * hd]

    def vhead(rows, h):
        return head(v_ref, rows, h, 2)

    def normalised(c):
        acc = acc_sc[c][...]
        return acc[:, :hd] / acc[:, hd:hd + 1]

    ctx_rows = slice(0, n_ctx)
    for c, kh in chains:
        s = lax.dot_general(head(q_ref, slice(None), c), head(k_ref, ctx_rows, kh), _NT,
                            preferred_element_type=F32)
        m0 = jnp.max(s, axis=1, keepdims=True)
        p = jnp.exp2(s - m0)
        m_sc[c][...] = jnp.broadcast_to(m0, m_sc[c].shape)
        acc_sc[c][...] = jnp.dot(p.astype(BF16), vhead(ctx_rows, kh), preferred_element_type=F32)

    def probs(c, kh, rows):
        s = lax.dot_general(head(q_ref, slice(None), c), head(k_ref, rows, kh), _NT,
                            preferred_element_type=F32)
        m_prev = m_sc[c][...]
        m_new = jnp.maximum(m_prev, jnp.max(s, axis=1, keepdims=True))
        p = jnp.exp2(s - jnp.tile(m_new, (1, tkl // LANES)))
        alpha = jnp.exp2(m_prev - m_new)
        m_sc[c][...] = m_new
        return p.astype(BF16), jnp.tile(alpha, (1, 2))

    @pl.when(i >= nct)
    def _():
        for c, kh in chains:
            p, alpha = probs(c, kh, pl.ds(n_ctx, tkl))
            p_sc[c][0] = p
            acc_sc[c][...] = alpha * acc_sc[c][...]

        for jb in range(1, nblk):
            r0 = n_ctx + jb * tkl
            prev = pl.ds(r0 - tkl, tkl)
            slot = jb % 2
            alphas = []
            for c, kh in chains:
                p, alpha = probs(c, kh, pl.ds(r0, tkl))
                p_sc[c][slot] = p
                alphas.append(alpha)
            for c, kh in chains:
                pv = jnp.dot(p_sc[c][1 - slot], vhead(prev, kh), preferred_element_type=F32)
                acc_sc[c][...] = alphas[c] * (acc_sc[c][...] + pv)

        last = pl.ds(n_ctx + (nblk - 1) * tkl, tkl)
        for c, kh in chains:
            acc_sc[c][...] = acc_sc[c][...] + jnp.dot(p_sc[c][(nblk - 1) % 2], vhead(last, kh),
                                                      preferred_element_type=F32)

    if diff:
        lam = (jnp.exp(jnp.sum(lq1[...] * lk1[...], axis=1, keepdims=True))
               - jnp.exp(jnp.sum(lq2[...] * lk2[...], axis=1, keepdims=True)) + lam_init)
        for kh in range(nh):
            dd = normalised(2 * kh) - lam * normalised(2 * kh + 1)
            o_ref[:, kh * hd:(kh + 1) * hd] = (_head_norm(dd, dn_ref[...]) * (1.0 - lam_init)).astype(BF16)
    else:
        for c, _ in chains:
            o_ref[:, c * hd:(c + 1) * hd] = normalised(c).astype(BF16)


def _flash_call(q, k, v, n_ctx, g, nh, tile_off, extra=None, lam_init=0.0):
    t = k.shape[0]
    hd = HEAD_DIM
    n_kv = k.shape[1] // (nh * hd)
    n_lat = t - n_ctx
    nct = n_ctx // ROW_TILE
    tkl = 2048
    diff = extra is not None
    ow = nh * hd if diff else nh * g * hd
    in_specs = [pl.BlockSpec((ROW_TILE, nh * g * hd), lambda kv, i: (i + tile_off, kv)),
                pl.BlockSpec((t, nh * hd), lambda kv, i: (0, kv)),
                pl.BlockSpec((t, 2 * nh * hd), lambda kv, i: (0, kv))]
    args = [q, k, v]
    if diff:
        for a in extra:
            a = a.reshape(1, -1)
            in_specs.append(pl.BlockSpec(a.shape, lambda kv, i: (0, 0)))
            args.append(a)
    return pl.pallas_call(
        functools.partial(_flash_kernel, g=g, nh=nh, n_ctx=n_ctx, n_lat=n_lat, tkl=tkl, nct=nct,
                          tile_off=tile_off, diff=diff, lam_init=lam_init),
        out_shape=jax.ShapeDtypeStruct((t - tile_off * ROW_TILE, n_kv * ow), BF16),
        grid=(n_kv, t // ROW_TILE - tile_off),
        in_specs=in_specs,
        out_specs=pl.BlockSpec((ROW_TILE, ow), lambda kv, i: (i, kv)),
        scratch_shapes=([pltpu.VMEM((ROW_TILE, LANES), F32)] * (nh * g)
                        + [pltpu.VMEM((ROW_TILE, 2 * hd), F32)] * (nh * g)
                        + [pltpu.VMEM((2, ROW_TILE, tkl), BF16)] * (nh * g)),
        compiler_params=_params(("arbitrary", "arbitrary"), 48),
        name="diff_attention" if diff else "gqa_attention",
    )(*args)


def _outproj_kernel(of_ref, ob_ref, gate_ref, ng_ref, att_ref, dif_ref, wo_ref, xc_ref, xl_ref, m_ref,
                    n2_ref, rw_ref, xo_ref, h2_ref, lg_ref, *, nct, tile_off):
    hd = HEAD_DIM
    hgw = HG_HEADS * hd
    ng = ng_ref[...]
    gate = gate_ref[...]
    silu_gate = gate * (1.0 / (1.0 + jnp.exp(-gate)))
    acc = jnp.dot(att_ref[...], wo_ref[hgw:hgw + att_ref.shape[1], :], preferred_element_type=F32)
    acc = acc + jnp.dot(dif_ref[...], wo_ref[hgw + att_ref.shape[1]:, :], preferred_element_type=F32)
    hg = []
    for h in range(HG_HEADS):
        o = of_ref[0, :, h * hd:(h + 1) * hd] + ob_ref[0, :, h * hd:(h + 1) * hd]
        hg.append((_head_norm(o, ng) * silu_gate[:, h * hd:(h + 1) * hd]).astype(BF16))
    acc = acc + jnp.dot(jnp.concatenate(hg, axis=1), wo_ref[0:hgw, :], preferred_element_type=F32)
    m = m_ref[0]
    xn = _stream_tile(xc_ref, xl_ref, nct, tile_off) + m[2:3] * acc
    xo_ref[...] = xn
    h2 = _norm_mod(xn, n2_ref[...], m[3:4], m[4:5])
    h2_ref[...] = h2
    hs = _split_bf16(h2, 2)
    ws = _split_bf16(rw_ref[...], 2)
    lg = jnp.dot(hs[1], ws[0], preferred_element_type=F32)
    lg = lg + jnp.dot(hs[0], ws[1], preferred_element_type=F32)
    lg_ref[...] = lg + jnp.dot(hs[0], ws[0], preferred_element_type=F32)


def _outproj_call(o, p, ng, att, dif, wo, xc, xl, lat_off, mods, n2, rw, nct, tile_off):
    d = xl.shape[1]
    hd = HEAD_DIM
    hgw = HG_HEADS * hd
    n_rows = att.shape[0]
    row = lambda w: pl.BlockSpec((ROW_TILE, w), lambda i: (i, 0))
    full = lambda a: pl.BlockSpec(a.shape, lambda i: (0,) * a.ndim)
    ng, n2 = ng.reshape(1, hd), n2.reshape(1, d)
    return pl.pallas_call(
        functools.partial(_outproj_kernel, nct=nct, tile_off=tile_off),
        out_shape=[jax.ShapeDtypeStruct((n_rows, d), F32),
                   jax.ShapeDtypeStruct((n_rows, d), F32),
                   jax.ShapeDtypeStruct((n_rows, LANES), F32)],
        grid=(n_rows // ROW_TILE,),
        in_specs=[pl.BlockSpec((1, ROW_TILE, hgw), lambda i: (0, i + tile_off, 0)),
                  pl.BlockSpec((1, ROW_TILE, hgw), lambda i: (1, i + tile_off, 0)),
                  pl.BlockSpec((ROW_TILE, hgw), lambda i: (i + tile_off, 5)),
                  full(ng), row(att.shape[1]), row(dif.shape[1]), full(wo)]
                 + _stream_specs(d, nct, tile_off, lat_off)
                 + [_mod_spec(d, nct, tile_off), full(n2), full(rw)],
        out_specs=[row(d), row(d), row(LANES)],
        compiler_params=_params(("arbitrary",), 56),
        name="out_proj",
    )(o, o, p, ng, att, dif, wo, xc, xl, mods, n2, rw)


def _router_kernel(lg_ref, bias_ref, r_ref, cnt_ref, run_ref):
    i = pl.program_id(0)
    tm = lg_ref.shape[0]

    @pl.when(i == 0)
    def _():
        run_ref[...] = jnp.zeros_like(run_ref)

    lane = lax.broadcasted_iota(I32, (tm, LANES), 1).astype(F32)
    lg = lg_ref[...] + bias_ref[...]
    ninf = -jnp.inf

    def first_max(vals):
        mx = jnp.max(vals, axis=1, keepdims=True)
        idx = jnp.min(jnp.where(vals == mx, lane, float(LANES)), axis=1, keepdims=True)
        return mx, idx

    gl = jnp.where(lane < N_GROUPS, lg, ninf)
    gmax, gidx = first_max(gl)
    g_top = 1.0 / jnp.sum(jnp.exp(gl - gmax), axis=1, keepdims=True)
    lo = N_GROUPS + EXPERTS_PER_GROUP * gidx
    el = jnp.where((lane >= lo) & (lane < lo + EXPERTS_PER_GROUP), lg, ninf)
    m1, e1 = first_max(el)
    m2, e2 = first_max(jnp.where(lane == e1, ninf, el))
    r = jnp.exp(m2 - m1)
    w1 = g_top / (1.0 + r)
    w2 = g_top * r / (1.0 + r)

    hit = ((lane == e1) | (lane == e2)).astype(BF16)
    ti = lax.broadcasted_iota(I32, (tm, tm), 0)
    si = lax.broadcasted_iota(I32, (tm, tm), 1)
    before = (si < ti).astype(BF16)
    pos = jnp.dot(before, hit, preferred_element_type=F32) + run_ref[0:1, :]
    p1 = jnp.sum(jnp.where(lane == e1, pos, 0.0), axis=1, keepdims=True)
    p2 = jnp.sum(jnp.where(lane == e2, pos, 0.0), axis=1, keepdims=True)
    total = run_ref[0:1, :] + jnp.sum(hit.astype(F32), axis=0, keepdims=True)
    run_ref[...] = jnp.broadcast_to(total, run_ref.shape)
    cnt_ref[...] = jnp.broadcast_to(total, cnt_ref.shape)

    fields = (e1 - N_GROUPS, e2 - N_GROUPS, w1, w2, p1, p2)
    out = jnp.zeros((tm, LANES), F32)
    for n, f in enumerate(fields):
        out = jnp.where(lane == n, f, out)
    r_ref[...] = out


def _router_call(logits, bias):
    n = logits.shape[0]
    return pl.pallas_call(
        _router_kernel,
        out_shape=[jax.ShapeDtypeStruct((n, LANES), F32), jax.ShapeDtypeStruct((8, LANES), F32)],
        grid=(n // ROW_TILE,),
        in_specs=[pl.BlockSpec((ROW_TILE, LANES), lambda i: (i, 0)),
                  pl.BlockSpec((1, LANES), lambda i: (0, 0))],
        out_specs=[pl.BlockSpec((ROW_TILE, LANES), lambda i: (i, 0)),
                   pl.BlockSpec((8, LANES), lambda i: (0, 0))],
        scratch_shapes=[pltpu.VMEM((8, LANES), F32)],
        compiler_params=_params(("arbitrary",), 16),
        name="router",
    )(logits, bias)


def _moe_kernel(src_ref, te_ref, nt_ref, h_hbm, wg_ref, wu_ref, wd_ref, y_ref,
                xbuf, sem, wg_bf, wu_bf, wd_bf):
    i = pl.program_id(0)
    n_tiles = nt_ref[0]
    tm = xbuf.shape[1]

    def start_tile(tile, slot):
        def body(grp, carry):
            for u in range(_DMA_GROUP):
                r = grp * _DMA_GROUP + u
                pltpu.make_async_copy(h_hbm.at[pl.ds(src_ref[tile * tm + r], 1), :],
                                      xbuf.at[slot, pl.ds(r, 1), :], sem.at[slot]).start(priority=1)
            return carry
        lax.fori_loop(0, tm // _DMA_GROUP, body, 0)

    def wait_tile(slot):
        pltpu.make_async_copy(h_hbm.at[pl.ds(0, tm), :], xbuf.at[slot], sem.at[slot]).wait()

    @pl.when(i == 0)
    def _():
        start_tile(0, 0)

    @pl.when(i < n_tiles)
    def _():
        slot = i % 2

        @pl.when(i + 1 < n_tiles)
        def _():
            start_tile(i + 1, 1 - slot)

        changed = jnp.logical_or(i == 0, te_ref[i] != te_ref[jnp.maximum(i - 1, 0)])

        @pl.when(changed)
        def _():
            wg_bf[...] = wg_ref[0, 0].astype(BF16)
            wu_bf[...] = wu_ref[0, 0].astype(BF16)
            wd_bf[...] = wd_ref[0, 0].astype(BF16)

        wait_tile(slot)
        xb = xbuf[slot].astype(BF16)
        gt = jnp.dot(xb, wg_bf[...], preferred_element_type=F32)
        up = jnp.dot(xb, wu_bf[...], preferred_element_type=F32)
        act = (gt * (1.0 / (1.0 + jnp.exp(-gt))) * up).astype(BF16)
        y_ref[...] = jnp.dot(act, wd_bf[...], preferred_element_type=F32)

    @pl.when(i >= n_tiles)
    def _():
        y_ref[...] = jnp.zeros_like(y_ref)


def _moe_call(h2, src, tile_expert, n_tiles, wg, wu, wd, layer):
    d = h2.shape[1]
    ff = wg.shape[3]
    max_tiles = tile_expert.shape[0]
    tm = MOE_TILE
    wspec = lambda s: pl.BlockSpec((1, 1) + s, lambda i, src, te, nt: (layer, te[i], 0, 0))
    grid_spec = pltpu.PrefetchScalarGridSpec(
        num_scalar_prefetch=3,
        grid=(max_tiles,),
        in_specs=[pl.BlockSpec(memory_space=pl.ANY), wspec((d, ff)), wspec((d, ff)), wspec((ff, d))],
        out_specs=pl.BlockSpec((tm, d), lambda i, src, te, nt: (i, 0)),
        scratch_shapes=[pltpu.VMEM((2, tm, d), F32), pltpu.SemaphoreType.DMA((2,)),
                        pltpu.VMEM((d, ff), BF16), pltpu.VMEM((d, ff), BF16), pltpu.VMEM((ff, d), BF16)])
    return pl.pallas_call(
        _moe_kernel,
        out_shape=jax.ShapeDtypeStruct((max_tiles * tm, d), F32),
        grid_spec=grid_spec,
        compiler_params=_params(("arbitrary",), 56),
        name="moe_experts",
    )(src, tile_expert, n_tiles, h2, wg, wu, wd)


def _combine_kernel(dst_ref, x_ref, rt_ref, m_ref, g_ref, mn_ref, y_hbm, *rest, last, n_out):
    outs, (ybuf, sem) = rest[:n_out], rest[n_out:]
    i = pl.program_id(0)
    tm = x_ref.shape[0]

    def start_tile(tile, slot):
        def body(grp, carry):
            for u in range(_DMA_GROUP):
                r = grp * _DMA_GROUP + u
                for j in range(2):
                    pltpu.make_async_copy(y_hbm.at[pl.ds(dst_ref[(tile * tm + r) * 2 + j], 1), :],
                                          ybuf.at[slot, j, pl.ds(r, 1), :], sem.at[slot]).start(priority=j)
            return carry
        lax.fori_loop(0, tm // _DMA_GROUP, body, 0)

    @pl.when(i == 0)
    def _():
        start_tile(0, 0)

    slot = i % 2

    @pl.when(i + 1 < pl.num_programs(0))
    def _():
        start_tile(i + 1, 1 - slot)

    for j in range(2):
        pltpu.make_async_copy(y_hbm.at[pl.ds(0, tm), :], ybuf.at[slot, j], sem.at[slot]).wait()
    rt = rt_ref[...]
    y = rt[:, 2:3] * ybuf[slot, 0] + rt[:, 3:4] * ybuf[slot, 1]
    xn = x_ref[...] + m_ref[0][5:6] * y
    if last:
        outs[0][...] = xn * lax.rsqrt(jnp.mean(xn * xn, axis=-1, keepdims=True) + EPS) * g_ref[...]
    else:
        outs[0][...] = xn
        mn = mn_ref[0]
        outs[1][...] = _norm_mod(xn, g_ref[...], mn[0:1], mn[1:2]).astype(BF16)


def _combine_call(dst, x, route, mods, g, mods_next, ys, nct, last):
    n, d = x.shape
    tm = ROW_TILE
    mspec = pl.BlockSpec((1, N_MOD, d), lambda i, dst: (jnp.where(i < nct, 1, 0), 0, 0))
    out_shape = [jax.ShapeDtypeStruct((n, d), F32)]
    if not last:
        out_shape.append(jax.ShapeDtypeStruct((n, d), BF16))
    row = lambda w: pl.BlockSpec((tm, w), lambda i, dst: (i, 0))
    grid_spec = pltpu.PrefetchScalarGridSpec(
        num_scalar_prefetch=1,
        grid=(n // tm,),
        in_specs=[row(d), row(LANES), mspec, pl.BlockSpec((1, d), lambda i, dst: (0, 0)), mspec,
                  pl.BlockSpec(memory_space=pl.ANY)],
        out_specs=[row(d)] * len(out_shape),
        scratch_shapes=[pltpu.VMEM((2, 2, tm, d), F32), pltpu.SemaphoreType.DMA((2,))])
    return pl.pallas_call(
        functools.partial(_combine_kernel, last=last, n_out=len(out_shape)),
        out_shape=out_shape,
        grid_spec=grid_spec,
        compiler_params=_params(("arbitrary",), 40),
        name="moe_combine",
    )(dst, x, route, mods, g.reshape(1, d), mods_next, ys)


def _dispatch_plan(route, counts, n_tokens):
    tm = MOE_TILE
    max_tiles = (2 * n_tokens) // tm + N_EXPERTS
    cnt = counts[0, N_GROUPS:N_GROUPS + N_EXPERTS].astype(I32)
    tiles_per = (cnt + tm - 1) // tm
    tile_end = jnp.cumsum(tiles_per)
    offs = (tile_end - tiles_per) * tm
    eid = route[:, 0:2].astype(I32)
    dst = offs[eid] + route[:, 4:6].astype(I32)
    tok = jnp.broadcast_to(jnp.arange(n_tokens, dtype=I32)[:, None], (n_tokens, 2))
    src = jnp.zeros((max_tiles * tm,), I32).at[dst.reshape(-1)].set(tok.reshape(-1))
    n_tiles = tile_end[-1:]
    tile_ids = jnp.minimum(jnp.arange(max_tiles, dtype=I32), n_tiles[0] - 1)
    tile_expert = jnp.sum((tile_end[None, :] <= tile_ids[:, None]).astype(I32), axis=1)
    return dst.reshape(-1), src, tile_expert, n_tiles.astype(I32)


def kernel(x, c, ctx, c_ctx, w_mod, b_mod, norm1_g, norm2_g, w_in, w_out, hg_lb_logits, hg_norm_g,
           q_norm_g, k_norm_g, lam_q1, lam_k1, lam_q2, lam_k2, diff_norm_g, router_group_w,
           router_group_b, router_expert_w, router_expert_b, w_gate, w_up, w_down, final_norm_g):
    depth = w_in.shape[0]
    n_lat, d = x.shape[1], x.shape[2]
    n_ctx = ctx.shape[1]
    nct = n_ctx // ROW_TILE
    assert x.shape[0] == 1 and n_ctx % ROW_TILE == 0 and n_lat % 1024 == 0

    stream = (ctx[0], x[0], 0)
    cc = jnp.zeros((8, d), F32).at[0].set(c[0]).at[1].set(c_ctx)
    mods_all = _mod_call(cc, w_mod, b_mod)[:, :2].reshape(depth, 2, N_MOD, d)
    tabs = _rope_tables(n_lat, n_ctx, HEAD_DIM) + _rope_tables(n_lat, n_ctx, HEAD_DIM // 2)

    h1 = _prenorm_call(stream[0], stream[1], norm1_g[0], mods_all[0], nct)
    out = None
    for l in range(depth):
        last = l == depth - 1
        mods = mods_all[l]
        tile_off = nct if last else 0
        tm_in = 768 if (n_ctx + n_lat) % 768 == 0 else ROW_TILE
        p = _matmul_call(h1, w_in, l, tm_in, 1024)
        o = _hgrn_call(p, hg_lb_logits, l, nct)
        qa, ka, va, qd, kd, vd = _attn_prep_call(p, tabs, q_norm_g[l], k_norm_g[l])
        att = _flash_call(qa, ka, va, n_ctx, GQA_HEADS // GQA_KV_HEADS, 1, tile_off)
        lam_init = 0.8 - 0.6 * math.exp(-0.3 * l)
        dif = _flash_call(qd, kd, vd, n_ctx, 2, 2, tile_off,
                          extra=(lam_q1[l], lam_k1[l], lam_q2[l], lam_k2[l], diff_norm_g[l]),
                          lam_init=lam_init)
        rw = jnp.concatenate([router_group_w[l], router_expert_w[l],
                              jnp.zeros((d, LANES - N_GROUPS - N_EXPERTS), F32)], axis=1)
        rb = jnp.concatenate([router_group_b[l], router_expert_b[l],
                              jnp.zeros((LANES - N_GROUPS - N_EXPERTS,), F32)]).reshape(1, LANES)
        xn, h2, logits = _outproj_call(o, p, hg_norm_g[l], att, dif, w_out[l].astype(BF16), *stream,
                                       mods, norm2_g[l], rw, nct, tile_off)
        n_tok = xn.shape[0]
        route, counts = _router_call(logits, rb)
        dst, src, tile_expert, n_tiles = _dispatch_plan(route, counts, n_tok)
        ys = _moe_call(h2, src, tile_expert, n_tiles, w_gate, w_up, w_down, l)
        if last:
            (out,) = _combine_call(dst, xn, route, mods, final_norm_g, mods, ys, 0, True)
        else:
            xall, h1 = _combine_call(dst, xn, route, mods, norm1_g[l + 1], mods_all[l + 1], ys, nct, False)
            stream = (xall, xall, nct)
    return out.reshape(1, n_lat, d)
```

```python
import functools
import math

import numpy as np
import jax
import jax.numpy as jnp
from jax import lax
from jax.experimental import pallas as pl
from jax.experimental.pallas import tpu as pltpu

F32 = jnp.float32
BF16 = jnp.bfloat16
I32 = jnp.int32

HEAD_DIM = 128
LANES = 128
GRID_W = 64
ROPE_THETA = 10000.0
EPS = 1e-6
HG_HEADS = 4
HG_CHUNK = 64
GQA_HEADS = 8
GQA_KV_HEADS = 2
DIFF_HEADS = 4
N_GROUPS = 4
EXPERTS_PER_GROUP = 8
N_EXPERTS = N_GROUPS * EXPERTS_PER_GROUP
N_MOD = 6
ROW_TILE = 256
MOE_TILE = 256
_DMA_GROUP = 8
MIB = 1024 * 1024

_LOG2E = math.log2(math.e)
_NT = (((1,), (1,)), ((), ()))
_TN = (((0,), (0,)), ((), ()))


def _params(semantics, vmem_mib):
    return pltpu.CompilerParams(dimension_semantics=semantics, vmem_limit_bytes=vmem_mib * MIB)


def _split_bf16(x, parts):
    out = []
    for _ in range(parts - 1):
        p = x.astype(BF16)
        out.append(p)
        x = x - p.astype(F32)
    out.append(x.astype(BF16))
    return out


def _mod_kernel(a_ref, w_ref, b_ref, o_ref):
    a = a_ref[...]
    a = a * (1.0 / (1.0 + jnp.exp(-a)))
    hi, lo = _split_bf16(a, 2)
    w = w_ref[0].astype(BF16)
    o_ref[0] = (jnp.dot(hi, w, preferred_element_type=F32)
                + jnp.dot(lo, w, preferred_element_type=F32) + b_ref[0])


def _mod_call(cc, w_mod, b_mod):
    depth, d, n = w_mod.shape
    tn = 1024
    return pl.pallas_call(
        _mod_kernel,
        out_shape=jax.ShapeDtypeStruct((depth, 8, n), F32),
        grid=(depth, n // tn),
        in_specs=[pl.BlockSpec((8, d), lambda l, j: (0, 0)),
                  pl.BlockSpec((1, d, tn), lambda l, j: (l, 0, j)),
                  pl.BlockSpec((1, 1, tn), lambda l, j: (l, 0, j))],
        out_specs=pl.BlockSpec((1, 8, tn), lambda l, j: (l, 0, j)),
        compiler_params=_params(("arbitrary", "arbitrary"), 40),
        name="mod_vectors",
    )(cc, w_mod, b_mod.reshape(depth, 1, n))


def _norm_mod(x, g, shift, scale):
    y = x * lax.rsqrt(jnp.mean(x * x, axis=-1, keepdims=True) + EPS) * g
    return y * (1.0 + scale) + shift


def _stream_specs(d, nct, tile_off=0, lat_off=0):
    return [pl.BlockSpec((ROW_TILE, d), lambda i: (jnp.minimum(i + tile_off, nct - 1), 0)),
            pl.BlockSpec((ROW_TILE, d), lambda i: (jnp.maximum(i + tile_off - nct, 0) + lat_off, 0))]


def _stream_tile(c_ref, x_ref, nct, tile_off=0):
    return jnp.where(pl.program_id(0) + tile_off < nct, c_ref[...], x_ref[...])


def _prenorm_kernel(c_ref, x_ref, g_ref, m_ref, o_ref, *, nct):
    m = m_ref[0]
    o_ref[...] = _norm_mod(_stream_tile(c_ref, x_ref, nct), g_ref[...], m[0:1], m[1:2]).astype(BF16)


def _mod_spec(d, nct, tile_off=0):
    return pl.BlockSpec((1, N_MOD, d), lambda i: (jnp.where(i + tile_off < nct, 1, 0), 0, 0))


def _prenorm_call(xc, xl, g, mods, nct):
    d = xl.shape[1]
    t = xc.shape[0] + xl.shape[0]
    return pl.pallas_call(
        functools.partial(_prenorm_kernel, nct=nct),
        out_shape=jax.ShapeDtypeStruct((t, d), BF16),
        grid=(t // ROW_TILE,),
        in_specs=_stream_specs(d, nct) + [pl.BlockSpec((1, d), lambda i: (0, 0)), _mod_spec(d, nct)],
        out_specs=pl.BlockSpec((ROW_TILE, d), lambda i: (i, 0)),
        compiler_params=_params(("arbitrary",), 24),
        name="prenorm",
    )(xc, xl, g.reshape(1, d), mods)


def _mm_kernel(a_ref, b_ref, o_ref, b_bf):
    @pl.when(pl.program_id(1) == 0)
    def _():
        b_bf[...] = b_ref[0].astype(BF16)

    o_ref[...] = jnp.dot(a_ref[...], b_bf[...], preferred_element_type=F32)


def _matmul_call(a, b, layer, tm, tn):
    m, k = a.shape
    n = b.shape[2]
    return pl.pallas_call(
        _mm_kernel,
        out_shape=jax.ShapeDtypeStruct((m, n), F32),
        grid=(n // tn, m // tm),
        in_specs=[pl.BlockSpec((tm, k), lambda j, i: (i, 0)),
                  pl.BlockSpec((1, k, tn), lambda j, i: (layer, 0, j))],
        out_specs=pl.BlockSpec((tm, tn), lambda j, i: (i, j)),
        scratch_shapes=[pltpu.VMEM((k, tn), BF16)],
        compiler_params=_params(("arbitrary", "arbitrary"), 48),
        name="in_proj",
    )(a, b)


_HG_LEVELS = (1, 2, 4, 8, 16, 32)
_HG_TOT_ROW = HG_CHUNK * (len(_HG_LEVELS) + 1)
_HG_W_ROWS = _HG_TOT_ROW + 16
_HG_HEADS_PER_STEP = 4


def _hgrn_consts():
    c = HG_CHUNK
    w = np.zeros((2, _HG_W_ROWS, c), np.float32)
    msk = np.zeros((2, len(_HG_LEVELS) + 1, c, c), np.float32)
    for d in range(2):
        u = np.arange(c) if d == 0 else c - 1 - np.arange(c)
        ut, us = u[:, None], u[None, :]
        w[d, :c] = us <= ut
        for li, lv in enumerate(_HG_LEVELS):
            blk = u // (2 * lv)
            qside = (u % (2 * lv)) >= lv
            bnd = (blk * 2 * lv + lv - 1)[:, None]
            wq = (us > bnd) & (us <= ut)
            wk = (us > ut) & (us <= bnd)
            w[d, c * (li + 1):c * (li + 2)] = np.where(qside[:, None], wq, -1.0 * wk)
            msk[d, li] = (blk[:, None] == blk[None, :]) & qside[:, None] & ~qside[None, :]
        msk[d, len(_HG_LEVELS)] = np.eye(c)
        w[d, _HG_TOT_ROW:] = 1.0
    return jnp.asarray(np.concatenate([w, w, w], axis=2), BF16), jnp.asarray(msk, F32)


def _hgrn_kernel(lbl_ref, q_ref, z_ref, v_ref, w_ref, msk_ref, o_ref, st_ref, *, layer, chunks, hps):
    c = HG_CHUNK
    hd = HEAD_DIM
    d = pl.program_id(0)
    j = pl.program_id(2)

    @pl.when(j == 0)
    def _():
        st_ref[...] = jnp.zeros_like(st_ref)

    lbl = lbl_ref[...]
    rows = [lbl[i:i + 1] for i in range(lbl.shape[0])]
    mx = functools.reduce(jnp.maximum, rows)
    ex = [jnp.exp(r - mx) for r in rows]
    tot = functools.reduce(lambda a, b: a + b, ex)
    lb = jnp.zeros_like(mx)
    for i in range(1, layer + 1):
        lb = lb + ex[i] / tot
    log_lb = jnp.log(lb)
    log_1m_lb = jnp.log1p(-lb)

    wmat = w_ref[0]
    nlev = len(_HG_LEVELS)

    def finish(h, hs, r0, amat, v_bf, qe, kd, decay):
        st = st_ref[h]
        o = (jnp.dot(amat, v_bf[:, hs], preferred_element_type=F32)
             + lax.dot_general(qe[:, hs], st.astype(BF16), _NT, preferred_element_type=F32))
        o_ref[0, pl.ds(r0, c), hs] = o
        st_ref[h] = st * decay[:, hs] + lax.dot_general(v_bf[:, hs], kd[:, hs], _TN,
                                                        preferred_element_type=F32)

    pending = None
    for ci in range(chunks):
        cc = ci + d * (chunks - 1 - 2 * ci)
        r0 = pl.multiple_of(cc * c, c)
        q = q_ref[pl.ds(r0, c), :]
        z = z_ref[pl.ds(r0, c), :]
        v_bf = v_ref[pl.ds(r0, c), :].astype(BF16)
        l1p = jnp.log(1.0 + jnp.exp(-jnp.abs(z)))
        ls_pos = jnp.minimum(z, 0.0) - l1p
        ls_neg = jnp.minimum(-z, 0.0) - l1p
        a2 = log_1m_lb + ls_pos
        logf = jnp.maximum(log_lb, a2) + jnp.log(1.0 + jnp.exp(-jnp.abs(log_lb - a2)))
        k = (1.0 - lb) * jnp.exp(ls_neg)

        parts = jnp.concatenate(_split_bf16(logf, 3), axis=0)
        sums = jnp.dot(wmat, parts, preferred_element_type=F32)
        b = sums[0:c]
        btot = sums[_HG_TOT_ROW:_HG_TOT_ROW + 1]

        qls, kls = [q.astype(BF16)], [k.astype(BF16)]
        for li in range(nlev):
            fl = jnp.exp(-jnp.abs(sums[c * (li + 1):c * (li + 2)]))
            qls.append((q * fl).astype(BF16))
            kls.append((k * fl).astype(BF16))
        qe = (q * jnp.exp(b)).astype(BF16)
        kd = (k * jnp.exp(btot - b)).astype(BF16)
        decay = jnp.exp(btot)

        for h in range(hps):
            hs = slice(h * hd, (h + 1) * hd)
            amat = lax.dot_general(qls[0][:, hs], kls[0][:, hs], _NT,
                                   preferred_element_type=F32) * msk_ref[0, nlev]
            for li in range(nlev):
                amat = amat + lax.dot_general(qls[li + 1][:, hs], kls[li + 1][:, hs], _NT,
                                              preferred_element_type=F32) * msk_ref[0, li]
            if pending is not None:
                finish(*pending)
            pending = (h, hs, r0, amat.astype(BF16), v_bf, qe, kd, decay)
    finish(*pending)


def _hgrn_call(p, lb_logits, layer, nct):
    t = p.shape[0]
    nblk = t // ROW_TILE
    chunks = ROW_TILE // HG_CHUNK
    hps = _HG_HEADS_PER_STEP
    hw = hps * HEAD_DIM
    ng = HG_HEADS // hps
    wmat, msk = _hgrn_consts()

    def blk(d, j):
        back = jnp.where(j < nct, nct - 1 - j, nblk - 1 - (j - nct))
        return jnp.where(d == 0, j, back)

    return pl.pallas_call(
        functools.partial(_hgrn_kernel, layer=layer, chunks=chunks, hps=hps),
        out_shape=jax.ShapeDtypeStruct((2, t, HG_HEADS * HEAD_DIM), F32),
        grid=(2, ng, nblk),
        in_specs=[pl.BlockSpec((lb_logits.shape[0], hw), lambda d, h, j: (0, h)),
                  pl.BlockSpec((ROW_TILE, hw), lambda d, h, j: (blk(d, j), d * ng + h)),
                  pl.BlockSpec((ROW_TILE, hw), lambda d, h, j: (blk(d, j), (2 + d) * ng + h)),
                  pl.BlockSpec((ROW_TILE, hw), lambda d, h, j: (blk(d, j), 4 * ng + h)),
                  pl.BlockSpec((1, _HG_W_ROWS, 3 * HG_CHUNK), lambda d, h, j: (d, 0, 0)),
                  pl.BlockSpec((1, len(_HG_LEVELS) + 1, HG_CHUNK, HG_CHUNK), lambda d, h, j: (d, 0, 0, 0))],
        out_specs=pl.BlockSpec((1, ROW_TILE, hw), lambda d, h, j: (d, blk(d, j), h)),
        scratch_shapes=[pltpu.VMEM((hps, HEAD_DIM, HEAD_DIM), F32)],
        compiler_params=_params(("arbitrary", "arbitrary", "arbitrary"), 24),
        name="hgrn_scan",
    )(lb_logits, p, p, p, wmat, msk)


def _rope_tables(n_lat, n_ctx, dim):
    rows = n_lat // GRID_W
    row = jnp.repeat(jnp.arange(rows, dtype=F32), GRID_W)
    col = jnp.tile(jnp.arange(GRID_W, dtype=F32), rows)
    axis_dim = dim // 2
    inv_freq = ROPE_THETA ** (-jnp.arange(0, axis_dim, 2, dtype=F32) / axis_dim)
    ang = jnp.concatenate([row[:, None] * inv_freq, col[:, None] * inv_freq], axis=-1)
    cos = jnp.repeat(jnp.cos(ang), 2, axis=1)
    sin = jnp.repeat(jnp.sin(ang), 2, axis=1) * jnp.tile(jnp.asarray([-1.0, 1.0], F32), dim // 2)
    reps = LANES // dim
    cos = jnp.concatenate([jnp.ones((n_ctx, dim), F32), cos], axis=0)
    sin = jnp.concatenate([jnp.zeros((n_ctx, dim), F32), sin], axis=0)
    return jnp.tile(cos, (1, reps)), jnp.tile(sin, (1, reps))


def _rope(x, cos, sin):
    lane = lax.broadcasted_iota(I32, x.shape, 1)
    swapped = jnp.where((lane & 1) == 0, pltpu.roll(x, LANES - 1, 1), pltpu.roll(x, 1, 1))
    return x * cos + swapped * sin


def _head_norm(x, g):
    return x * lax.rsqrt(jnp.mean(x * x, axis=-1, keepdims=True) + EPS) * g


def _attn_prep_kernel(pq_ref, pkv_ref, pd_ref, ca_ref, sa_ref, cd_ref, sd_ref, qn_ref, kn_ref,
                      qa_ref, ka_ref, va_ref, qd_ref, kd_ref, vd_ref):
    hd = HEAD_DIM
    ca, sa, cd, sd = ca_ref[...], sa_ref[...], cd_ref[...], sd_ref[...]
    qn, kn = qn_ref[...], kn_ref[...]
    for h in range(GQA_HEADS):
        xq = _rope(_head_norm(pq_ref[:, h * hd:(h + 1) * hd], qn), ca, sa)
        qa_ref[:, h * hd:(h + 1) * hd] = (xq * (hd ** -0.5 * _LOG2E)).astype(BF16)
    for h in range(GQA_KV_HEADS):
        xk = _rope(_head_norm(pkv_ref[:, h * hd:(h + 1) * hd], kn), ca, sa)
        ka_ref[:, h * hd:(h + 1) * hd] = xk.astype(BF16)
    ones_col = jnp.where(lax.broadcasted_iota(I32, (pq_ref.shape[0], hd), 1) == 0, 1.0, 0.0).astype(BF16)
    for h in range(GQA_KV_HEADS):
        va_ref[:, (2 * h) * hd:(2 * h + 1) * hd] = pkv_ref[:, (GQA_KV_HEADS + h) * hd:
                                                            (GQA_KV_HEADS + h + 1) * hd].astype(BF16)
        va_ref[:, (2 * h + 1) * hd:(2 * h + 2) * hd] = ones_col
    first = lax.broadcasted_iota(I32, (pq_ref.shape[0], hd), 1) < hd // 2
    dw = DIFF_HEADS * hd
    for h in range(DIFF_HEADS):
        xq = _rope(pd_ref[:, h * hd:(h + 1) * hd], cd, sd) * ((hd // 2) ** -0.5 * _LOG2E)
        qd_ref[:, (2 * h) * hd:(2 * h + 1) * hd] = jnp.where(first, xq, 0.0).astype(BF16)
        qd_ref[:, (2 * h + 1) * hd:(2 * h + 2) * hd] = jnp.where(first, 0.0, xq).astype(BF16)
        xk = _rope(pd_ref[:, dw + h * hd:dw + (h + 1) * hd], cd, sd)
        kd_ref[:, h * hd:(h + 1) * hd] = xk.astype(BF16)
    for h in range(DIFF_HEADS):
        vd_ref[:, (2 * h) * hd:(2 * h + 1) * hd] = pd_ref[:, 2 * dw + h * hd:2 * dw + (h + 1) * hd].astype(BF16)
        vd_ref[:, (2 * h + 1) * hd:(2 * h + 2) * hd] = ones_col


def _attn_prep_call(p, tabs, qn, kn):
    t = p.shape[0]
    hd = HEAD_DIM
    gq, gkv, dw = GQA_HEADS * hd, 2 * GQA_KV_HEADS * hd, DIFF_HEADS * hd
    q_off = 6 * HG_HEADS * hd
    assert q_off % gq == 0 and (q_off + gq) % gkv == 0 and (q_off + gq + gkv) % (3 * dw) == 0
    row = lambda w: pl.BlockSpec((ROW_TILE, w), lambda i: (i, 0))
    vec = pl.BlockSpec((1, hd), lambda i: (0, 0))
    return pl.pallas_call(
        _attn_prep_kernel,
        out_shape=[jax.ShapeDtypeStruct((t, gq), BF16),
                   jax.ShapeDtypeStruct((t, gkv // 2), BF16),
                   jax.ShapeDtypeStruct((t, gkv), BF16),
                   jax.ShapeDtypeStruct((t, 2 * dw), BF16),
                   jax.ShapeDtypeStruct((t, dw), BF16),
                   jax.ShapeDtypeStruct((t, 2 * dw), BF16)],
        grid=(t // ROW_TILE,),
        in_specs=[pl.BlockSpec((ROW_TILE, gq), lambda i: (i, q_off // gq)),
                  pl.BlockSpec((ROW_TILE, gkv), lambda i: (i, (q_off + gq) // gkv)),
                  pl.BlockSpec((ROW_TILE, 3 * dw), lambda i: (i, (q_off + gq + gkv) // (3 * dw))),
                  row(hd), row(hd), row(hd), row(hd), vec, vec],
        out_specs=[row(gq), row(gkv // 2), row(gkv), row(2 * dw), row(dw), row(2 * dw)],
        compiler_params=_params(("arbitrary",), 24),
        name="attn_prep",
    )(p, p, p, *tabs, qn.reshape(1, hd), kn.reshape(1, hd))


def _flash_kernel(*refs, g, nh, n_ctx, n_lat, tkl, nct, tile_off, diff, lam_init):
    if diff:
        q_ref, k_ref, v_ref, lq1, lk1, lq2, lk2, dn_ref, o_ref = refs[:9]
    else:
        q_ref, k_ref, v_ref, o_ref = refs[:4]
    nc = nh * g
    m_sc, acc_sc, p_sc = (refs[len(refs) - (3 - n) * nc:len(refs) - (2 - n) * nc] for n in range(3))
    hd = HEAD_DIM
    i = pl.program_id(1) + tile_off
    nblk = n_lat // tkl
    chains = [(c, c // g) for c in range(nh * g)]

    def head(ref, rows, h, width=1):
        return ref[rows, h * width * hd:(h + 1) * width * hd]

    def vhead(rows, h):
        return head(v_ref, rows, h, 2)

    def normalised(c):
        acc = acc_sc[c][...]
        return acc[:, :hd] / acc[:, hd:hd + 1]

    ctx_rows = slice(0, n_ctx)
    for c, kh in chains:
        s = lax.dot_general(head(q_ref, slice(None), c), head(k_ref, ctx_rows, kh), _NT,
                            preferred_element_type=F32)
        m0 = jnp.max(s, axis=1, keepdims=True)
        p = jnp.exp2(s - m0)
        m_sc[c][...] = jnp.broadcast_to(m0, m_sc[c].shape)
        acc_sc[c][...] = jnp.dot(p.astype(BF16), vhead(ctx_rows, kh), preferred_element_type=F32)

    def probs(c, kh, rows):
        s = lax.dot_general(head(q_ref, slice(None), c), head(k_ref, rows, kh), _NT,
                            preferred_element_type=F32)
        m_prev = m_sc[c][...]
        m_new = jnp.maximum(m_prev, jnp.max(s, axis=1, keepdims=True))
        p = jnp.exp2(s - jnp.tile(m_new, (1, tkl // LANES)))
        alpha = jnp.exp2(m_prev - m_new)
        m_sc[c][...] = m_new
        return p.astype(BF16), jnp.tile(alpha, (1, 2))

    @pl.when(i >= nct)
    def _():
        for c, kh in chains:
            p, alpha = probs(c, kh, pl.ds(n_ctx, tkl))
            p_sc[c][0] = p
            acc_sc[c][...] = alpha * acc_sc[c][...]

        for jb in range(1, nblk):
            r0 = n_ctx + jb * tkl
            prev = pl.ds(r0 - tkl, tkl)
            slot = jb % 2
            alphas = []
            for c, kh in chains:
                p, alpha = probs(c, kh, pl.ds(r0, tkl))
                p_sc[c][slot] = p
                alphas.append(alpha)
            for c, kh in chains:
                pv = jnp.dot(p_sc[c][1 - slot], vhead(prev, kh), preferred_element_type=F32)
                acc_sc[c][...] = alphas[c] * (acc_sc[c][...] + pv)

        last = pl.ds(n_ctx + (nblk - 1) * tkl, tkl)
        for c, kh in chains:
            acc_sc[c][...] = acc_sc[c][...] + jnp.dot(p_sc[c][(nblk - 1) % 2], vhead(last, kh),
                                                      preferred_element_type=F32)

    if diff:
        lam = (jnp.exp(jnp.sum(lq1[...] * lk1[...], axis=1, keepdims=True))
               - jnp.exp(jnp.sum(lq2[...] * lk2[...], axis=1, keepdims=True)) + lam_init)
        for kh in range(nh):
            dd = normalised(2 * kh) - lam * normalised(2 * kh + 1)
            o_ref[:, kh * hd:(kh + 1) * hd] = (_head_norm(dd, dn_ref[...]) * (1.0 - lam_init)).astype(BF16)
    else:
        for c, _ in chains:
            o_ref[:, c * hd:(c + 1) * hd] = normalised(c).astype(BF16)


def _flash_call(q, k, v, n_ctx, g, nh, tile_off, extra=None, lam_init=0.0):
    t = k.shape[0]
    hd = HEAD_DIM
    n_kv = k.shape[1] // (nh * hd)
    n_lat = t - n_ctx
    nct = n_ctx // ROW_TILE
    tkl = 2048
    diff = extra is not None
    ow = nh * hd if diff else nh * g * hd
    in_specs = [pl.BlockSpec((ROW_TILE, nh * g * hd), lambda kv, i: (i + tile_off, kv)),
                pl.BlockSpec((t, nh * hd), lambda kv, i: (0, kv)),
                pl.BlockSpec((t, 2 * nh * hd), lambda kv, i: (0, kv))]
    args = [q, k, v]
    if diff:
        for a in extra:
            a = a.reshape(1, -1)
            in_specs.append(pl.BlockSpec(a.shape, lambda kv, i: (0, 0)))
            args.append(a)
    return pl.pallas_call(
        functools.partial(_flash_kernel, g=g, nh=nh, n_ctx=n_ctx, n_lat=n_lat, tkl=tkl, nct=nct,
                          tile_off=tile_off, diff=diff, lam_init=lam_init),
        out_shape=jax.ShapeDtypeStruct((t - tile_off * ROW_TILE, n_kv * ow), BF16),
        grid=(n_kv, t // ROW_TILE - tile_off),
        in_specs=in_specs,
        out_specs=pl.BlockSpec((ROW_TILE, ow), lambda kv, i: (i, kv)),
        scratch_shapes=([pltpu.VMEM((ROW_TILE, LANES), F32)] * (nh * g)
                        + [pltpu.VMEM((ROW_TILE, 2 * hd), F32)] * (nh * g)
                        + [pltpu.VMEM((2, ROW_TILE, tkl), BF16)] * (nh * g)),
        compiler_params=_params(("arbitrary", "arbitrary"), 48),
        name="diff_attention" if diff else "gqa_attention",
    )(*args)


def _outproj_kernel(of_ref, ob_ref, gate_ref, ng_ref, att_ref, dif_ref, wo_ref, xc_ref, xl_ref, m_ref,
                    n2_ref, rw_ref, xo_ref, h2_ref, lg_ref, *, nct, tile_off):
    hd = HEAD_DIM
    hgw = HG_HEADS * hd
    ng = ng_ref[...]
    gate = gate_ref[...]
    silu_gate = gate * (1.0 / (1.0 + jnp.exp(-gate)))
    acc = jnp.dot(att_ref[...], wo_ref[hgw:hgw + att_ref.shape[1], :], preferred_element_type=F32)
    acc = acc + jnp.dot(dif_ref[...], wo_ref[hgw + att_ref.shape[1]:, :], preferred_element_type=F32)
    hg = []
    for h in range(HG_HEADS):
        o = of_ref[0, :, h * hd:(h + 1) * hd] + ob_ref[0, :, h * hd:(h + 1) * hd]
        hg.append((_head_norm(o, ng) * silu_gate[:, h * hd:(h + 1) * hd]).astype(BF16))
    acc = acc + jnp.dot(jnp.concatenate(hg, axis=1), wo_ref[0:hgw, :], preferred_element_type=F32)
    m = m_ref[0]
    xn = _stream_tile(xc_ref, xl_ref, nct, tile_off) + m[2:3] * acc
    xo_ref[...] = xn
    h2 = _norm_mod(xn, n2_ref[...], m[3:4], m[4:5])
    h2_ref[...] = h2
    hs = _split_bf16(h2, 2)
    ws = _split_bf16(rw_ref[...], 2)
    lg = jnp.dot(hs[1], ws[0], preferred_element_type=F32)
    lg = lg + jnp.dot(hs[0], ws[1], preferred_element_type=F32)
    lg_ref[...] = lg + jnp.dot(hs[0], ws[0], preferred_element_type=F32)


def _outproj_call(o, p, ng, att, dif, wo, xc, xl, lat_off, mods, n2, rw, nct, tile_off):
    d = xl.shape[1]
    hd = HEAD_DIM
    hgw = HG_HEADS * hd
    n_rows = att.shape[0]
    row = lambda w: pl.BlockSpec((ROW_TILE, w), lambda i: (i, 0))
    full = lambda a: pl.BlockSpec(a.shape, lambda i: (0,) * a.ndim)
    ng, n2 = ng.reshape(1, hd), n2.reshape(1, d)
    return pl.pallas_call(
        functools.partial(_outproj_kernel, nct=nct, tile_off=tile_off),
        out_shape=[jax.ShapeDtypeStruct((n_rows, d), F32),
                   jax.ShapeDtypeStruct((n_rows, d), F32),
                   jax.ShapeDtypeStruct((n_rows, LANES), F32)],
        grid=(n_rows // ROW_TILE,),
        in_specs=[pl.BlockSpec((1, ROW_TILE, hgw), lambda i: (0, i + tile_off, 0)),
                  pl.BlockSpec((1, ROW_TILE, hgw), lambda i: (1, i + tile_off, 0)),
                  pl.BlockSpec((ROW_TILE, hgw), lambda i: (i + tile_off, 5)),
                  full(ng), row(att.shape[1]), row(dif.shape[1]), full(wo)]
                 + _stream_specs(d, nct, tile_off, lat_off)
                 + [_mod_spec(d, nct, tile_off), full(n2), full(rw)],
        out_specs=[row(d), row(d), row(LANES)],
        compiler_params=_params(("arbitrary",), 56),
        name="out_proj",
    )(o, o, p, ng, att, dif, wo, xc, xl, mods, n2, rw)


def _router_kernel(lg_ref, bias_ref, r_ref, cnt_ref, run_ref):
    i = pl.program_id(0)
    tm = lg_ref.shape[0]

    @pl.when(i == 0)
    def _():
        run_ref[...] = jnp.zeros_like(run_ref)

    lane = lax.broadcasted_iota(I32, (tm, LANES), 1).astype(F32)
    lg = lg_ref[...] + bias_ref[...]
    ninf = -jnp.inf

    def first_max(vals):
        mx = jnp.max(vals, axis=1, keepdims=True)
        idx = jnp.min(jnp.where(vals == mx, lane, float(LANES)), axis=1, keepdims=True)
        return mx, idx

    gl = jnp.where(lane < N_GROUPS, lg, ninf)
    gmax, gidx = first_max(gl)
    g_top = 1.0 / jnp.sum(jnp.exp(gl - gmax), axis=1, keepdims=True)
    lo = N_GROUPS + EXPERTS_PER_GROUP * gidx
    el = jnp.where((lane >= lo) & (lane < lo + EXPERTS_PER_GROUP), lg, ninf)
    m1, e1 = first_max(el)
    m2, e2 = first_max(jnp.where(lane == e1, ninf, el))
    r = jnp.exp(m2 - m1)
    w1 = g_top / (1.0 + r)
    w2 = g_top * r / (1.0 + r)

    hit = ((lane == e1) | (lane == e2)).astype(BF16)
    ti = lax.broadcasted_iota(I32, (tm, tm), 0)
    si = lax.broadcasted_iota(I32, (tm, tm), 1)
    before = (si < ti).astype(BF16)
    pos = jnp.dot(before, hit, preferred_element_type=F32) + run_ref[0:1, :]
    p1 = jnp.sum(jnp.where(lane == e1, pos, 0.0), axis=1, keepdims=True)
    p2 = jnp.sum(jnp.where(lane == e2, pos, 0.0), axis=1, keepdims=True)
    total = run_ref[0:1, :] + jnp.sum(hit.astype(F32), axis=0, keepdims=True)
    run_ref[...] = jnp.broadcast_to(total, run_ref.shape)
    cnt_ref[...] = jnp.broadcast_to(total, cnt_ref.shape)

    fields = (e1 - N_GROUPS, e2 - N_GROUPS, w1, w2, p1, p2)
    out = jnp.zeros((tm, LANES), F32)
    for n, f in enumerate(fields):
        out = jnp.where(lane == n, f, out)
    r_ref[...] = out


def _router_call(logits, bias):
    n = logits.shape[0]
    return pl.pallas_call(
        _router_kernel,
        out_shape=[jax.ShapeDtypeStruct((n, LANES), F32), jax.ShapeDtypeStruct((8, LANES), F32)],
        grid=(n // ROW_TILE,),
        in_specs=[pl.BlockSpec((ROW_TILE, LANES), lambda i: (i, 0)),
                  pl.BlockSpec((1, LANES), lambda i: (0, 0))],
        out_specs=[pl.BlockSpec((ROW_TILE, LANES), lambda i: (i, 0)),
                   pl.BlockSpec((8, LANES), lambda i: (0, 0))],
        scratch_shapes=[pltpu.VMEM((8, LANES), F32)],
        compiler_params=_params(("arbitrary",), 16),
        name="router",
    )(logits, bias)


def _moe_kernel(src_ref, te_ref, nx_ref, nt_ref, h_hbm, wg_hbm, wu_hbm, wd_hbm, y_ref,
                xbuf, sem, wg_st, wu_st, wd_st, wsem, wg_bf, wu_bf, wd_bf, *, layer):
    i = pl.program_id(0)
    n_tiles = nt_ref[0]
    tm = xbuf.shape[1]

    def weight_copies(e):
        pairs = ((wg_hbm, wg_st), (wu_hbm, wu_st), (wd_hbm, wd_st))
        return [pltpu.make_async_copy(w.at[layer, e], st, wsem.at[n]) for n, (w, st) in enumerate(pairs)]

    def start_tile(tile, slot):
        def body(grp, carry):
            for u in range(_DMA_GROUP):
                r = grp * _DMA_GROUP + u
                pltpu.make_async_copy(h_hbm.at[pl.ds(src_ref[tile * tm + r], 1), :],
                                      xbuf.at[slot, pl.ds(r, 1), :], sem.at[slot]).start(priority=1)
            return carry
        lax.fori_loop(0, tm // _DMA_GROUP, body, 0)

    def wait_tile(slot):
        pltpu.make_async_copy(h_hbm.at[pl.ds(0, tm), :], xbuf.at[slot], sem.at[slot]).wait()

    @pl.when(i == 0)
    def _():
        start_tile(0, 0)
        for cp in weight_copies(te_ref[0]):
            cp.start()

    @pl.when(i < n_tiles)
    def _():
        slot = i % 2

        @pl.when(i + 1 < n_tiles)
        def _():
            start_tile(i + 1, 1 - slot)

        changed = jnp.logical_or(i == 0, te_ref[i] != te_ref[jnp.maximum(i - 1, 0)])

        @pl.when(changed)
        def _():
            for cp in weight_copies(te_ref[i]):
                cp.wait()
            wg_bf[...] = wg_st[...].astype(BF16)
            wu_bf[...] = wu_st[...].astype(BF16)
            wd_bf[...] = wd_st[...].astype(BF16)

            @pl.when(nx_ref[i] >= 0)
            def _():
                for cp in weight_copies(nx_ref[i]):
                    cp.start()

        wait_tile(slot)
        xb = xbuf[slot].astype(BF16)
        gt = jnp.dot(xb, wg_bf[...], preferred_element_type=F32)
        up = jnp.dot(xb, wu_bf[...], preferred_element_type=F32)
        act = (gt * (1.0 / (1.0 + jnp.exp(-gt))) * up).astype(BF16)
        y_ref[...] = jnp.dot(act, wd_bf[...], preferred_element_type=F32)

    @pl.when(i >= n_tiles)
    def _():
        y_ref[...] = jnp.zeros_like(y_ref)


def _moe_call(h2, src, tile_expert, next_expert, n_tiles, wg, wu, wd, layer):
    d = h2.shape[1]
    ff = wg.shape[3]
    max_tiles = tile_expert.shape[0]
    tm = MOE_TILE
    hbm = pl.BlockSpec(memory_space=pl.ANY)
    grid_spec = pltpu.PrefetchScalarGridSpec(
        num_scalar_prefetch=4,
        grid=(max_tiles,),
        in_specs=[hbm, hbm, hbm, hbm],
        out_specs=pl.BlockSpec((tm, d), lambda i, src, te, nx, nt: (i, 0)),
        scratch_shapes=[pltpu.VMEM((2, tm, d), F32), pltpu.SemaphoreType.DMA((2,)),
                        pltpu.VMEM((d, ff), F32), pltpu.VMEM((d, ff), F32), pltpu.VMEM((ff, d), F32),
                        pltpu.SemaphoreType.DMA((3,)),
                        pltpu.VMEM((d, ff), BF16), pltpu.VMEM((d, ff), BF16), pltpu.VMEM((ff, d), BF16)])
    return pl.pallas_call(
        functools.partial(_moe_kernel, layer=layer),
        out_shape=jax.ShapeDtypeStruct((max_tiles * tm, d), F32),
        grid_spec=grid_spec,
        compiler_params=_params(("arbitrary",), 40),
        name="moe_experts",
    )(src, tile_expert, next_expert, n_tiles, h2, wg, wu, wd)


def _combine_kernel(dst_ref, x_ref, rt_ref, m_ref, g_ref, mn_ref, y_hbm, *rest, last, n_out):
    outs, (ybuf, sem) = rest[:n_out], rest[n_out:]
    i = pl.program_id(0)
    tm = x_ref.shape[0]

    def start_tile(tile, slot):
        def body(grp, carry):
            for u in range(_DMA_GROUP):
                r = grp * _DMA_GROUP + u
                for j in range(2):
                    pltpu.make_async_copy(y_hbm.at[pl.ds(dst_ref[(tile * tm + r) * 2 + j], 1), :],
                                          ybuf.at[slot, j, pl.ds(r, 1), :], sem.at[slot]).start(priority=j)
            return carry
        lax.fori_loop(0, tm // _DMA_GROUP, body, 0)

    @pl.when(i == 0)
    def _():
        start_tile(0, 0)

    slot = i % 2

    @pl.when(i + 1 < pl.num_programs(0))
    def _():
        start_tile(i + 1, 1 - slot)

    for j in range(2):
        pltpu.make_async_copy(y_hbm.at[pl.ds(0, tm), :], ybuf.at[slot, j], sem.at[slot]).wait()
    rt = rt_ref[...]
    y = rt[:, 2:3] * ybuf[slot, 0] + rt[:, 3:4] * ybuf[slot, 1]
    xn = x_ref[...] + m_ref[0][5:6] * y
    if last:
        outs[0][...] = xn * lax.rsqrt(jnp.mean(xn * xn, axis=-1, keepdims=True) + EPS) * g_ref[...]
    else:
        outs[0][...] = xn
        mn = mn_ref[0]
        outs[1][...] = _norm_mod(xn, g_ref[...], mn[0:1], mn[1:2]).astype(BF16)


def _combine_call(dst, x, route, mods, g, mods_next, ys, nct, last):
    n, d = x.shape
    tm = ROW_TILE
    mspec = pl.BlockSpec((1, N_MOD, d), lambda i, dst: (jnp.where(i < nct, 1, 0), 0, 0))
    out_shape = [jax.ShapeDtypeStruct((n, d), F32)]
    if not last:
        out_shape.append(jax.ShapeDtypeStruct((n, d), BF16))
    row = lambda w: pl.BlockSpec((tm, w), lambda i, dst: (i, 0))
    grid_spec = pltpu.PrefetchScalarGridSpec(
        num_scalar_prefetch=1,
        grid=(n // tm,),
        in_specs=[row(d), row(LANES), mspec, pl.BlockSpec((1, d), lambda i, dst: (0, 0)), mspec,
                  pl.BlockSpec(memory_space=pl.ANY)],
        out_specs=[row(d)] * len(out_shape),
        scratch_shapes=[pltpu.VMEM((2, 2, tm, d), F32), pltpu.SemaphoreType.DMA((2,))])
    return pl.pallas_call(
        functools.partial(_combine_kernel, last=last, n_out=len(out_shape)),
        out_shape=out_shape,
        grid_spec=grid_spec,
        compiler_params=_params(("arbitrary",), 40),
        name="moe_combine",
    )(dst, x, route, mods, g.reshape(1, d), mods_next, ys)


def _dispatch_plan(route, counts, n_tokens):
    tm = MOE_TILE
    max_tiles = (2 * n_tokens) // tm + N_EXPERTS
    cnt = counts[0, N_GROUPS:N_GROUPS + N_EXPERTS].astype(I32)
    tiles_per = (cnt + tm - 1) // tm
    tile_end = jnp.cumsum(tiles_per)
    offs = (tile_end - tiles_per) * tm
    eid = route[:, 0:2].astype(I32)
    dst = offs[eid] + route[:, 4:6].astype(I32)
    tok = jnp.broadcast_to(jnp.arange(n_tokens, dtype=I32)[:, None], (n_tokens, 2))
    src = jnp.zeros((max_tiles * tm,), I32).at[dst.reshape(-1)].set(tok.reshape(-1))
    n_tiles = tile_end[-1:]
    tile_ids = jnp.minimum(jnp.arange(max_tiles, dtype=I32), n_tiles[0] - 1)
    tile_expert = jnp.sum((tile_end[None, :] <= tile_ids[:, None]).astype(I32), axis=1)
    e_ids = jnp.arange(N_EXPERTS, dtype=I32)
    later = (e_ids[None, :] > e_ids[:, None]) & (tiles_per[None, :] > 0)
    nxt = jnp.min(jnp.where(later, e_ids[None, :], N_EXPERTS), axis=1)
    next_expert = jnp.where(nxt < N_EXPERTS, nxt, -1)[tile_expert]
    return dst.reshape(-1), src, tile_expert, next_expert, n_tiles.astype(I32)


def kernel(x, c, ctx, c_ctx, w_mod, b_mod, norm1_g, norm2_g, w_in, w_out, hg_lb_logits, hg_norm_g,
           q_norm_g, k_norm_g, lam_q1, lam_k1, lam_q2, lam_k2, diff_norm_g, router_group_w,
           router_group_b, router_expert_w, router_expert_b, w_gate, w_up, w_down, final_norm_g):
    depth = w_in.shape[0]
    n_lat, d = x.shape[1], x.shape[2]
    n_ctx = ctx.shape[1]
    nct = n_ctx // ROW_TILE
    assert x.shape[0] == 1 and n_ctx % ROW_TILE == 0 and n_lat % 1024 == 0

    stream = (ctx[0], x[0], 0)
    cc = jnp.zeros((8, d), F32).at[0].set(c[0]).at[1].set(c_ctx)
    mods_all = _mod_call(cc, w_mod, b_mod)[:, :2].reshape(depth, 2, N_MOD, d)
    tabs = _rope_tables(n_lat, n_ctx, HEAD_DIM) + _rope_tables(n_lat, n_ctx, HEAD_DIM // 2)

    h1 = _prenorm_call(stream[0], stream[1], norm1_g[0], mods_all[0], nct)
    out = None
    for l in range(depth):
        last = l == depth - 1
        mods = mods_all[l]
        tile_off = nct if last else 0
        tm_in = 768 if (n_ctx + n_lat) % 768 == 0 else ROW_TILE
        p = _matmul_call(h1, w_in, l, tm_in, 1024)
        o = _hgrn_call(p, hg_lb_logits, l, nct)
        qa, ka, va, qd, kd, vd = _attn_prep_call(p, tabs, q_norm_g[l], k_norm_g[l])
        att = _flash_call(qa, ka, va, n_ctx, GQA_HEADS // GQA_KV_HEADS, 1, tile_off)
        lam_init = 0.8 - 0.6 * math.exp(-0.3 * l)
        dif = _flash_call(qd, kd, vd, n_ctx, 2, 2, tile_off,
                          extra=(lam_q1[l], lam_k1[l], lam_q2[l], lam_k2[l], diff_norm_g[l]),
                          lam_init=lam_init)
        rw = jnp.concatenate([router_group_w[l], router_expert_w[l],
                              jnp.zeros((d, LANES - N_GROUPS - N_EXPERTS), F32)], axis=1)
        rb = jnp.concatenate([router_group_b[l], router_expert_b[l],
                              jnp.zeros((LANES - N_GROUPS - N_EXPERTS,), F32)]).reshape(1, LANES)
        xn, h2, logits = _outproj_call(o, p, hg_norm_g[l], att, dif, w_out[l].astype(BF16), *stream,
                                       mods, norm2_g[l], rw, nct, tile_off)
        n_tok = xn.shape[0]
        route, counts = _router_call(logits, rb)
        dst, src, tile_expert, next_expert, n_tiles = _dispatch_plan(route, counts, n_tok)
        ys = _moe_call(h2, src, tile_expert, next_expert, n_tiles, w_gate, w_up, w_down, l)
        if last:
            (out,) = _combine_call(dst, xn, route, mods, final_norm_g, mods, ys, 0, True)
        else:
            xall, h1 = _combine_call(dst, xn, route, mods, norm1_g[l + 1], mods_all[l + 1], ys, nct, False)
            stream = (xall, xall, nct)
    return out.reshape(1, n_lat, d)
```

```python
import functools
import math

import numpy as np
import jax
import jax.numpy as jnp
from jax import lax
from jax.experimental import pallas as pl
from jax.experimental.pallas import tpu as pltpu

F32 = jnp.float32
BF16 = jnp.bfloat16
I32 = jnp.int32

HEAD_DIM = 128
LANES = 128
GRID_W = 64
ROPE_THETA = 10000.0
EPS = 1e-6
HG_HEADS = 4
HG_CHUNK = 64
GQA_HEADS = 8
GQA_KV_HEADS = 2
DIFF_HEADS = 4
N_GROUPS = 4
EXPERTS_PER_GROUP = 8
N_EXPERTS = N_GROUPS * EXPERTS_PER_GROUP
N_MOD = 6
ROW_TILE = 256
MOE_TILE = 256
_DMA_GROUP = 8
MIB = 1024 * 1024

_LOG2E = math.log2(math.e)
_NT = (((1,), (1,)), ((), ()))
_TN = (((0,), (0,)), ((), ()))


def _params(semantics, vmem_mib):
    return pltpu.CompilerParams(dimension_semantics=semantics, vmem_limit_bytes=vmem_mib * MIB)


def _to_token_major(ref, x):
    n, d = x.shape
    k = d // LANES
    for s in range(k):
        ref[pl.ds(s, n, stride=k), :] = x[:, s * LANES:(s + 1) * LANES]


def _from_token_major(ref, n):
    k = ref.shape[0] // n
    return jnp.concatenate([ref[pl.ds(s, n, stride=k), :] for s in range(k)], axis=1)


def _split_bf16(x, parts):
    out = []
    for _ in range(parts - 1):
        p = x.astype(BF16)
        out.append(p)
        x = x - p.astype(F32)
    out.append(x.astype(BF16))
    return out


def _mod_kernel(a_ref, w_ref, b_ref, o_ref):
    a = a_ref[...]
    a = a * (1.0 / (1.0 + jnp.exp(-a)))
    hi, lo = _split_bf16(a, 2)
    w = w_ref[0].astype(BF16)
    o_ref[0] = (jnp.dot(hi, w, preferred_element_type=F32)
                + jnp.dot(lo, w, preferred_element_type=F32) + b_ref[0])


def _mod_call(cc, w_mod, b_mod):
    depth, d, n = w_mod.shape
    tn = 1024
    return pl.pallas_call(
        _mod_kernel,
        out_shape=jax.ShapeDtypeStruct((depth, 8, n), F32),
        grid=(depth, n // tn),
        in_specs=[pl.BlockSpec((8, d), lambda l, j: (0, 0)),
                  pl.BlockSpec((1, d, tn), lambda l, j: (l, 0, j)),
                  pl.BlockSpec((1, 1, tn), lambda l, j: (l, 0, j))],
        out_specs=pl.BlockSpec((1, 8, tn), lambda l, j: (l, 0, j)),
        compiler_params=_params(("arbitrary", "arbitrary"), 40),
        name="mod_vectors",
    )(cc, w_mod, b_mod.reshape(depth, 1, n))


def _norm_mod(x, g, shift, scale):
    y = x * lax.rsqrt(jnp.mean(x * x, axis=-1, keepdims=True) + EPS) * g
    return y * (1.0 + scale) + shift


def _stream_specs(d, nct, tile_off=0, lat_off=0):
    return [pl.BlockSpec((ROW_TILE, d), lambda i: (jnp.minimum(i + tile_off, nct - 1), 0)),
            pl.BlockSpec((ROW_TILE, d), lambda i: (jnp.maximum(i + tile_off - nct, 0) + lat_off, 0))]


def _stream_tile(c_ref, x_ref, nct, tile_off=0):
    return jnp.where(pl.program_id(0) + tile_off < nct, c_ref[...], x_ref[...])


def _prenorm_kernel(c_ref, x_ref, g_ref, m_ref, o_ref, *, nct):
    m = m_ref[0]
    o_ref[...] = _norm_mod(_stream_tile(c_ref, x_ref, nct), g_ref[...], m[0:1], m[1:2]).astype(BF16)


def _mod_spec(d, nct, tile_off=0):
    return pl.BlockSpec((1, N_MOD, d), lambda i: (jnp.where(i + tile_off < nct, 1, 0), 0, 0))


def _prenorm_call(xc, xl, g, mods, nct):
    d = xl.shape[1]
    t = xc.shape[0] + xl.shape[0]
    return pl.pallas_call(
        functools.partial(_prenorm_kernel, nct=nct),
        out_shape=jax.ShapeDtypeStruct((t, d), BF16),
        grid=(t // ROW_TILE,),
        in_specs=_stream_specs(d, nct) + [pl.BlockSpec((1, d), lambda i: (0, 0)), _mod_spec(d, nct)],
        out_specs=pl.BlockSpec((ROW_TILE, d), lambda i: (i, 0)),
        compiler_params=_params(("arbitrary",), 24),
        name="prenorm",
    )(xc, xl, g.reshape(1, d), mods)


def _mm_kernel(a_ref, b_ref, o_ref, b_bf):
    @pl.when(pl.program_id(1) == 0)
    def _():
        b_bf[...] = b_ref[0].astype(BF16)

    o_ref[...] = jnp.dot(a_ref[...], b_bf[...], preferred_element_type=F32)


def _matmul_call(a, b, layer, tm, tn):
    m, k = a.shape
    n = b.shape[2]
    return pl.pallas_call(
        _mm_kernel,
        out_shape=jax.ShapeDtypeStruct((m, n), F32),
        grid=(n // tn, m // tm),
        in_specs=[pl.BlockSpec((tm, k), lambda j, i: (i, 0)),
                  pl.BlockSpec((1, k, tn), lambda j, i: (layer, 0, j))],
        out_specs=pl.BlockSpec((tm, tn), lambda j, i: (i, j)),
        scratch_shapes=[pltpu.VMEM((k, tn), BF16)],
        compiler_params=_params(("arbitrary", "arbitrary"), 48),
        name="in_proj",
    )(a, b)


_HG_LEVELS = (1, 2, 4, 8, 16, 32)
_HG_TOT_ROW = HG_CHUNK * (len(_HG_LEVELS) + 1)
_HG_W_ROWS = _HG_TOT_ROW + 16
_HG_HEADS_PER_STEP = 4


def _hgrn_consts():
    c = HG_CHUNK
    w = np.zeros((2, _HG_W_ROWS, c), np.float32)
    msk = np.zeros((2, len(_HG_LEVELS) + 1, c, c), np.float32)
    for d in range(2):
        u = np.arange(c) if d == 0 else c - 1 - np.arange(c)
        ut, us = u[:, None], u[None, :]
        w[d, :c] = us <= ut
        for li, lv in enumerate(_HG_LEVELS):
            blk = u // (2 * lv)
            qside = (u % (2 * lv)) >= lv
            bnd = (blk * 2 * lv + lv - 1)[:, None]
            wq = (us > bnd) & (us <= ut)
            wk = (us > ut) & (us <= bnd)
            w[d, c * (li + 1):c * (li + 2)] = np.where(qside[:, None], wq, -1.0 * wk)
            msk[d, li] = (blk[:, None] == blk[None, :]) & qside[:, None] & ~qside[None, :]
        msk[d, len(_HG_LEVELS)] = np.eye(c)
        w[d, _HG_TOT_ROW:] = 1.0
    return jnp.asarray(np.concatenate([w, w, w], axis=2), BF16), jnp.asarray(msk, F32)


def _hgrn_kernel(lbl_ref, q_ref, z_ref, v_ref, w_ref, msk_ref, o_ref, st_ref, *, layer, chunks, hps):
    c = HG_CHUNK
    hd = HEAD_DIM
    d = pl.program_id(0)
    j = pl.program_id(2)

    @pl.when(j == 0)
    def _():
        st_ref[...] = jnp.zeros_like(st_ref)

    lbl = lbl_ref[...]
    rows = [lbl[i:i + 1] for i in range(lbl.shape[0])]
    mx = functools.reduce(jnp.maximum, rows)
    ex = [jnp.exp(r - mx) for r in rows]
    tot = functools.reduce(lambda a, b: a + b, ex)
    lb = jnp.zeros_like(mx)
    for i in range(1, layer + 1):
        lb = lb + ex[i] / tot
    log_lb = jnp.log(lb)
    log_1m_lb = jnp.log1p(-lb)

    wmat = w_ref[0]
    nlev = len(_HG_LEVELS)

    def finish(h, hs, r0, amat, v_bf, qe, kd, decay):
        st = st_ref[h]
        o = (jnp.dot(amat, v_bf[:, hs], preferred_element_type=F32)
             + lax.dot_general(qe[:, hs], st.astype(BF16), _NT, preferred_element_type=F32))
        o_ref[0, pl.ds(r0, c), hs] = o
        st_ref[h] = st * decay[:, hs] + lax.dot_general(v_bf[:, hs], kd[:, hs], _TN,
                                                        preferred_element_type=F32)

    pending = None
    for ci in range(chunks):
        cc = ci + d * (chunks - 1 - 2 * ci)
        r0 = pl.multiple_of(cc * c, c)
        q = q_ref[pl.ds(r0, c), :]
        z = z_ref[pl.ds(r0, c), :]
        v_bf = v_ref[pl.ds(r0, c), :].astype(BF16)
        l1p = jnp.log(1.0 + jnp.exp(-jnp.abs(z)))
        ls_pos = jnp.minimum(z, 0.0) - l1p
        ls_neg = jnp.minimum(-z, 0.0) - l1p
        a2 = log_1m_lb + ls_pos
        logf = jnp.maximum(log_lb, a2) + jnp.log(1.0 + jnp.exp(-jnp.abs(log_lb - a2)))
        k = (1.0 - lb) * jnp.exp(ls_neg)

        parts = jnp.concatenate(_split_bf16(logf, 3), axis=0)
        sums = jnp.dot(wmat, parts, preferred_element_type=F32)
        b = sums[0:c]
        btot = sums[_HG_TOT_ROW:_HG_TOT_ROW + 1]

        qls, kls = [q.astype(BF16)], [k.astype(BF16)]
        for li in range(nlev):
            fl = jnp.exp(-jnp.abs(sums[c * (li + 1):c * (li + 2)]))
            qls.append((q * fl).astype(BF16))
            kls.append((k * fl).astype(BF16))
        qe = (q * jnp.exp(b)).astype(BF16)
        kd = (k * jnp.exp(btot - b)).astype(BF16)
        decay = jnp.exp(btot)

        for h in range(hps):
            hs = slice(h * hd, (h + 1) * hd)
            amat = lax.dot_general(qls[0][:, hs], kls[0][:, hs], _NT,
                                   preferred_element_type=F32) * msk_ref[0, nlev]
            for li in range(nlev):
                amat = amat + lax.dot_general(qls[li + 1][:, hs], kls[li + 1][:, hs], _NT,
                                              preferred_element_type=F32) * msk_ref[0, li]
            if pending is not None:
                finish(*pending)
            pending = (h, hs, r0, amat.astype(BF16), v_bf, qe, kd, decay)
    finish(*pending)


def _hgrn_call(p, lb_logits, layer, nct):
    t = p.shape[0]
    nblk = t // ROW_TILE
    chunks = ROW_TILE // HG_CHUNK
    hps = _HG_HEADS_PER_STEP
    hw = hps * HEAD_DIM
    ng = HG_HEADS // hps
    wmat, msk = _hgrn_consts()

    def blk(d, j):
        back = jnp.where(j < nct, nct - 1 - j, nblk - 1 - (j - nct))
        return jnp.where(d == 0, j, back)

    return pl.pallas_call(
        functools.partial(_hgrn_kernel, layer=layer, chunks=chunks, hps=hps),
        out_shape=jax.ShapeDtypeStruct((2, t, HG_HEADS * HEAD_DIM), F32),
        grid=(2, ng, nblk),
        in_specs=[pl.BlockSpec((lb_logits.shape[0], hw), lambda d, h, j: (0, h)),
                  pl.BlockSpec((ROW_TILE, hw), lambda d, h, j: (blk(d, j), d * ng + h)),
                  pl.BlockSpec((ROW_TILE, hw), lambda d, h, j: (blk(d, j), (2 + d) * ng + h)),
                  pl.BlockSpec((ROW_TILE, hw), lambda d, h, j: (blk(d, j), 4 * ng + h)),
                  pl.BlockSpec((1, _HG_W_ROWS, 3 * HG_CHUNK), lambda d, h, j: (d, 0, 0)),
                  pl.BlockSpec((1, len(_HG_LEVELS) + 1, HG_CHUNK, HG_CHUNK), lambda d, h, j: (d, 0, 0, 0))],
        out_specs=pl.BlockSpec((1, ROW_TILE, hw), lambda d, h, j: (d, blk(d, j), h)),
        scratch_shapes=[pltpu.VMEM((hps, HEAD_DIM, HEAD_DIM), F32)],
        compiler_params=_params(("arbitrary", "arbitrary", "arbitrary"), 24),
        name="hgrn_scan",
    )(lb_logits, p, p, p, wmat, msk)


def _rope_tables(n_lat, n_ctx, dim):
    rows = n_lat // GRID_W
    row = jnp.repeat(jnp.arange(rows, dtype=F32), GRID_W)
    col = jnp.tile(jnp.arange(GRID_W, dtype=F32), rows)
    axis_dim = dim // 2
    inv_freq = ROPE_THETA ** (-jnp.arange(0, axis_dim, 2, dtype=F32) / axis_dim)
    ang = jnp.concatenate([row[:, None] * inv_freq, col[:, None] * inv_freq], axis=-1)
    cos = jnp.repeat(jnp.cos(ang), 2, axis=1)
    sin = jnp.repeat(jnp.sin(ang), 2, axis=1) * jnp.tile(jnp.asarray([-1.0, 1.0], F32), dim // 2)
    reps = LANES // dim
    cos = jnp.concatenate([jnp.ones((n_ctx, dim), F32), cos], axis=0)
    sin = jnp.concatenate([jnp.zeros((n_ctx, dim), F32), sin], axis=0)
    return jnp.tile(cos, (1, reps)), jnp.tile(sin, (1, reps))


def _rope(x, cos, sin):
    lane = lax.broadcasted_iota(I32, x.shape, 1)
    swapped = jnp.where((lane & 1) == 0, pltpu.roll(x, LANES - 1, 1), pltpu.roll(x, 1, 1))
    return x * cos + swapped * sin


def _head_norm(x, g):
    return x * lax.rsqrt(jnp.mean(x * x, axis=-1, keepdims=True) + EPS) * g


def _attn_prep_kernel(pq_ref, pkv_ref, pd_ref, ca_ref, sa_ref, cd_ref, sd_ref, qn_ref, kn_ref,
                      qa_ref, ka_ref, va_ref, qd_ref, kd_ref, vd_ref):
    hd = HEAD_DIM
    ca, sa, cd, sd = ca_ref[...], sa_ref[...], cd_ref[...], sd_ref[...]
    qn, kn = qn_ref[...], kn_ref[...]
    for h in range(GQA_HEADS):
        xq = _rope(_head_norm(pq_ref[:, h * hd:(h + 1) * hd], qn), ca, sa)
        qa_ref[:, h * hd:(h + 1) * hd] = (xq * (hd ** -0.5 * _LOG2E)).astype(BF16)
    for h in range(GQA_KV_HEADS):
        xk = _rope(_head_norm(pkv_ref[:, h * hd:(h + 1) * hd], kn), ca, sa)
        ka_ref[:, h * hd:(h + 1) * hd] = xk.astype(BF16)
    ones_col = jnp.where(lax.broadcasted_iota(I32, (pq_ref.shape[0], hd), 1) == 0, 1.0, 0.0).astype(BF16)
    for h in range(GQA_KV_HEADS):
        va_ref[:, (2 * h) * hd:(2 * h + 1) * hd] = pkv_ref[:, (GQA_KV_HEADS + h) * hd:
                                                            (GQA_KV_HEADS + h + 1) * hd].astype(BF16)
        va_ref[:, (2 * h + 1) * hd:(2 * h + 2) * hd] = ones_col
    first = lax.broadcasted_iota(I32, (pq_ref.shape[0], hd), 1) < hd // 2
    dw = DIFF_HEADS * hd
    for h in range(DIFF_HEADS):
        xq = _rope(pd_ref[:, h * hd:(h + 1) * hd], cd, sd) * ((hd // 2) ** -0.5 * _LOG2E)
        qd_ref[:, (2 * h) * hd:(2 * h + 1) * hd] = jnp.where(first, xq, 0.0).astype(BF16)
        qd_ref[:, (2 * h + 1) * hd:(2 * h + 2) * hd] = jnp.where(first, 0.0, xq).astype(BF16)
        xk = _rope(pd_ref[:, dw + h * hd:dw + (h + 1) * hd], cd, sd)
        kd_ref[:, h * hd:(h + 1) * hd] = xk.astype(BF16)
    for h in range(DIFF_HEADS):
        vd_ref[:, (2 * h) * hd:(2 * h + 1) * hd] = pd_ref[:, 2 * dw + h * hd:2 * dw + (h + 1) * hd].astype(BF16)
        vd_ref[:, (2 * h + 1) * hd:(2 * h + 2) * hd] = ones_col


def _attn_prep_call(p, tabs, qn, kn):
    t = p.shape[0]
    hd = HEAD_DIM
    gq, gkv, dw = GQA_HEADS * hd, 2 * GQA_KV_HEADS * hd, DIFF_HEADS * hd
    q_off = 6 * HG_HEADS * hd
    assert q_off % gq == 0 and (q_off + gq) % gkv == 0 and (q_off + gq + gkv) % (3 * dw) == 0
    row = lambda w: pl.BlockSpec((ROW_TILE, w), lambda i: (i, 0))
    vec = pl.BlockSpec((1, hd), lambda i: (0, 0))
    return pl.pallas_call(
        _attn_prep_kernel,
        out_shape=[jax.ShapeDtypeStruct((t, gq), BF16),
                   jax.ShapeDtypeStruct((t, gkv // 2), BF16),
                   jax.ShapeDtypeStruct((t, gkv), BF16),
                   jax.ShapeDtypeStruct((t, 2 * dw), BF16),
                   jax.ShapeDtypeStruct((t, dw), BF16),
                   jax.ShapeDtypeStruct((t, 2 * dw), BF16)],
        grid=(t // ROW_TILE,),
        in_specs=[pl.BlockSpec((ROW_TILE, gq), lambda i: (i, q_off // gq)),
                  pl.BlockSpec((ROW_TILE, gkv), lambda i: (i, (q_off + gq) // gkv)),
                  pl.BlockSpec((ROW_TILE, 3 * dw), lambda i: (i, (q_off + gq + gkv) // (3 * dw))),
                  row(hd), row(hd), row(hd), row(hd), vec, vec],
        out_specs=[row(gq), row(gkv // 2), row(gkv), row(2 * dw), row(dw), row(2 * dw)],
        compiler_params=_params(("arbitrary",), 24),
        name="attn_prep",
    )(p, p, p, *tabs, qn.reshape(1, hd), kn.reshape(1, hd))


def _flash_kernel(*refs, g, nh, n_ctx, n_lat, tkl, nct, tile_off, diff, lam_init):
    if diff:
        q_ref, k_ref, v_ref, lq1, lk1, lq2, lk2, dn_ref, o_ref = refs[:9]
    else:
        q_ref, k_ref, v_ref, o_ref = refs[:4]
    nc = nh * g
    m_sc, acc_sc, p_sc = (refs[len(refs) - (3 - n) * nc:len(refs) - (2 - n) * nc] for n in range(3))
    hd = HEAD_DIM
    i = pl.program_id(1) + tile_off
    nblk = n_lat // tkl
    chains = [(c, c // g) for c in range(nh * g)]

    def head(ref, rows, h, width=1):
        return ref[rows, h * width * hd:(h + 1) * width * hd]

    def vhead(rows, h):
        return head(v_ref, rows, h, 2)

    def normalised(c):
        acc = acc_sc[c][...]
        return acc[:, :hd] / acc[:, hd:hd + 1]

    ctx_rows = slice(0, n_ctx)
    for c, kh in chains:
        s = lax.dot_general(head(q_ref, slice(None), c), head(k_ref, ctx_rows, kh), _NT,
                            preferred_element_type=F32)
        m0 = jnp.max(s, axis=1, keepdims=True)
        p = jnp.exp2(s - m0)
        m_sc[c][...] = jnp.broadcast_to(m0, m_sc[c].shape)
        acc_sc[c][...] = jnp.dot(p.astype(BF16), vhead(ctx_rows, kh), preferred_element_type=F32)

    def probs(c, kh, rows):
        s = lax.dot_general(head(q_ref, slice(None), c), head(k_ref, rows, kh), _NT,
                            preferred_element_type=F32)
        m_prev = m_sc[c][...]
        m_new = jnp.maximum(m_prev, jnp.max(s, axis=1, keepdims=True))
        p = jnp.exp2(s - jnp.tile(m_new, (1, tkl // LANES)))
        alpha = jnp.exp2(m_prev - m_new)
        m_sc[c][...] = m_new
        return p.astype(BF16), jnp.tile(alpha, (1, 2))

    @pl.when(i >= nct)
    def _():
        for c, kh in chains:
            p, alpha = probs(c, kh, pl.ds(n_ctx, tkl))
            p_sc[c][0] = p
            acc_sc[c][...] = alpha * acc_sc[c][...]

        for jb in range(1, nblk):
            r0 = n_ctx + jb * tkl
            prev = pl.ds(r0 - tkl, tkl)
            slot = jb % 2
            alphas = []
            for c, kh in chains:
                p, alpha = probs(c, kh, pl.ds(r0, tkl))
                p_sc[c][slot] = p
                alphas.append(alpha)
            for c, kh in chains:
                pv = jnp.dot(p_sc[c][1 - slot], vhead(prev, kh), preferred_element_type=F32)
                acc_sc[c][...] = alphas[c] * (acc_sc[c][...] + pv)

        last = pl.ds(n_ctx + (nblk - 1) * tkl, tkl)
        for c, kh in chains:
            acc_sc[c][...] = acc_sc[c][...] + jnp.dot(p_sc[c][(nblk - 1) % 2], vhead(last, kh),
                                                      preferred_element_type=F32)

    if diff:
        lam = (jnp.exp(jnp.sum(lq1[...] * lk1[...], axis=1, keepdims=True))
               - jnp.exp(jnp.sum(lq2[...] * lk2[...], axis=1, keepdims=True)) + lam_init)
        for kh in range(nh):
            dd = normalised(2 * kh) - lam * normalised(2 * kh + 1)
            o_ref[:, kh * hd:(kh + 1) * hd] = (_head_norm(dd, dn_ref[...]) * (1.0 - lam_init)).astype(BF16)
    else:
        for c, _ in chains:
            o_ref[:, c * hd:(c + 1) * hd] = normalised(c).astype(BF16)


def _flash_call(q, k, v, n_ctx, g, nh, tile_off, extra=None, lam_init=0.0):
    t = k.shape[0]
    hd = HEAD_DIM
    n_kv = k.shape[1] // (nh * hd)
    n_lat = t - n_ctx
    nct = n_ctx // ROW_TILE
    tkl = 2048
    diff = extra is not None
    ow = nh * hd if diff else nh * g * hd
    in_specs = [pl.BlockSpec((ROW_TILE, nh * g * hd), lambda kv, i: (i + tile_off, kv)),
                pl.BlockSpec((t, nh * hd), lambda kv, i: (0, kv)),
                pl.BlockSpec((t, 2 * nh * hd), lambda kv, i: (0, kv))]
    args = [q, k, v]
    if diff:
        for a in extra:
            a = a.reshape(1, -1)
            in_specs.append(pl.BlockSpec(a.shape, lambda kv, i: (0, 0)))
            args.append(a)
    return pl.pallas_call(
        functools.partial(_flash_kernel, g=g, nh=nh, n_ctx=n_ctx, n_lat=n_lat, tkl=tkl, nct=nct,
                          tile_off=tile_off, diff=diff, lam_init=lam_init),
        out_shape=jax.ShapeDtypeStruct((t - tile_off * ROW_TILE, n_kv * ow), BF16),
        grid=(n_kv, t // ROW_TILE - tile_off),
        in_specs=in_specs,
        out_specs=pl.BlockSpec((ROW_TILE, ow), lambda kv, i: (i, kv)),
        scratch_shapes=([pltpu.VMEM((ROW_TILE, LANES), F32)] * (nh * g)
                        + [pltpu.VMEM((ROW_TILE, 2 * hd), F32)] * (nh * g)
                        + [pltpu.VMEM((2, ROW_TILE, tkl), BF16)] * (nh * g)),
        compiler_params=_params(("arbitrary", "arbitrary"), 48),
        name="diff_attention" if diff else "gqa_attention",
    )(*args)


def _outproj_kernel(of_ref, ob_ref, gate_ref, ng_ref, att_ref, dif_ref, wo_ref, xc_ref, xl_ref, m_ref,
                    n2_ref, rw_ref, xo_ref, h2_ref, lg_ref, *, nct, tile_off):
    hd = HEAD_DIM
    hgw = HG_HEADS * hd
    ng = ng_ref[...]
    gate = gate_ref[...]
    silu_gate = gate * (1.0 / (1.0 + jnp.exp(-gate)))
    acc = jnp.dot(att_ref[...], wo_ref[hgw:hgw + att_ref.shape[1], :], preferred_element_type=F32)
    acc = acc + jnp.dot(dif_ref[...], wo_ref[hgw + att_ref.shape[1]:, :], preferred_element_type=F32)
    hg = []
    for h in range(HG_HEADS):
        o = of_ref[0, :, h * hd:(h + 1) * hd] + ob_ref[0, :, h * hd:(h + 1) * hd]
        hg.append((_head_norm(o, ng) * silu_gate[:, h * hd:(h + 1) * hd]).astype(BF16))
    acc = acc + jnp.dot(jnp.concatenate(hg, axis=1), wo_ref[0:hgw, :], preferred_element_type=F32)
    m = m_ref[0]
    xn = _stream_tile(xc_ref, xl_ref, nct, tile_off) + m[2:3] * acc
    xo_ref[...] = xn
    h2 = _norm_mod(xn, n2_ref[...], m[3:4], m[4:5])
    _to_token_major(h2_ref, h2)
    hs = _split_bf16(h2, 2)
    ws = _split_bf16(rw_ref[...], 2)
    lg = jnp.dot(hs[1], ws[0], preferred_element_type=F32)
    lg = lg + jnp.dot(hs[0], ws[1], preferred_element_type=F32)
    lg_ref[...] = lg + jnp.dot(hs[0], ws[0], preferred_element_type=F32)


def _outproj_call(o, p, ng, att, dif, wo, xc, xl, lat_off, mods, n2, rw, nct, tile_off):
    d = xl.shape[1]
    hd = HEAD_DIM
    hgw = HG_HEADS * hd
    n_rows = att.shape[0]
    row = lambda w: pl.BlockSpec((ROW_TILE, w), lambda i: (i, 0))
    full = lambda a: pl.BlockSpec(a.shape, lambda i: (0,) * a.ndim)
    ng, n2 = ng.reshape(1, hd), n2.reshape(1, d)
    return pl.pallas_call(
        functools.partial(_outproj_kernel, nct=nct, tile_off=tile_off),
        out_shape=[jax.ShapeDtypeStruct((n_rows, d), F32),
                   jax.ShapeDtypeStruct((n_rows * (d // LANES), LANES), F32),
                   jax.ShapeDtypeStruct((n_rows, LANES), F32)],
        grid=(n_rows // ROW_TILE,),
        in_specs=[pl.BlockSpec((1, ROW_TILE, hgw), lambda i: (0, i + tile_off, 0)),
                  pl.BlockSpec((1, ROW_TILE, hgw), lambda i: (1, i + tile_off, 0)),
                  pl.BlockSpec((ROW_TILE, hgw), lambda i: (i + tile_off, 5)),
                  full(ng), row(att.shape[1]), row(dif.shape[1]), full(wo)]
                 + _stream_specs(d, nct, tile_off, lat_off)
                 + [_mod_spec(d, nct, tile_off), full(n2), full(rw)],
        out_specs=[row(d), pl.BlockSpec((ROW_TILE * (d // LANES), LANES), lambda i: (i, 0)), row(LANES)],
        compiler_params=_params(("arbitrary",), 56),
        name="out_proj",
    )(o, o, p, ng, att, dif, wo, xc, xl, mods, n2, rw)


def _router_kernel(lg_ref, bias_ref, r_ref, cnt_ref, run_ref):
    i = pl.program_id(0)
    tm = lg_ref.shape[0]

    @pl.when(i == 0)
    def _():
        run_ref[...] = jnp.zeros_like(run_ref)

    lane = lax.broadcasted_iota(I32, (tm, LANES), 1).astype(F32)
    lg = lg_ref[...] + bias_ref[...]
    ninf = -jnp.inf

    def first_max(vals):
        mx = jnp.max(vals, axis=1, keepdims=True)
        idx = jnp.min(jnp.where(vals == mx, lane, float(LANES)), axis=1, keepdims=True)
        return mx, idx

    gl = jnp.where(lane < N_GROUPS, lg, ninf)
    gmax, gidx = first_max(gl)
    g_top = 1.0 / jnp.sum(jnp.exp(gl - gmax), axis=1, keepdims=True)
    lo = N_GROUPS + EXPERTS_PER_GROUP * gidx
    el = jnp.where((lane >= lo) & (lane < lo + EXPERTS_PER_GROUP), lg, ninf)
    m1, e1 = first_max(el)
    m2, e2 = first_max(jnp.where(lane == e1, ninf, el))
    r = jnp.exp(m2 - m1)
    w1 = g_top / (1.0 + r)
    w2 = g_top * r / (1.0 + r)

    hit = ((lane == e1) | (lane == e2)).astype(BF16)
    ti = lax.broadcasted_iota(I32, (tm, tm), 0)
    si = lax.broadcasted_iota(I32, (tm, tm), 1)
    before = (si < ti).astype(BF16)
    pos = jnp.dot(before, hit, preferred_element_type=F32) + run_ref[0:1, :]
    p1 = jnp.sum(jnp.where(lane == e1, pos, 0.0), axis=1, keepdims=True)
    p2 = jnp.sum(jnp.where(lane == e2, pos, 0.0), axis=1, keepdims=True)
    total = run_ref[0:1, :] + jnp.sum(hit.astype(F32), axis=0, keepdims=True)
    run_ref[...] = jnp.broadcast_to(total, run_ref.shape)
    cnt_ref[...] = jnp.broadcast_to(total, cnt_ref.shape)

    fields = (e1 - N_GROUPS, e2 - N_GROUPS, w1, w2, p1, p2)
    out = jnp.zeros((tm, LANES), F32)
    for n, f in enumerate(fields):
        out = jnp.where(lane == n, f, out)
    r_ref[...] = out


def _router_call(logits, bias):
    n = logits.shape[0]
    return pl.pallas_call(
        _router_kernel,
        out_shape=[jax.ShapeDtypeStruct((n, LANES), F32), jax.ShapeDtypeStruct((8, LANES), F32)],
        grid=(n // ROW_TILE,),
        in_specs=[pl.BlockSpec((ROW_TILE, LANES), lambda i: (i, 0)),
                  pl.BlockSpec((1, LANES), lambda i: (0, 0))],
        out_specs=[pl.BlockSpec((ROW_TILE, LANES), lambda i: (i, 0)),
                   pl.BlockSpec((8, LANES), lambda i: (0, 0))],
        scratch_shapes=[pltpu.VMEM((8, LANES), F32)],
        compiler_params=_params(("arbitrary",), 16),
        name="router",
    )(logits, bias)


def _moe_kernel(src_ref, te_ref, nx_ref, nt_ref, h_hbm, wg_hbm, wu_hbm, wd_hbm, y_ref,
                xbuf, sem, wg_st, wu_st, wd_st, wsem, wg_bf, wu_bf, wd_bf, *, layer):
    i = pl.program_id(0)
    n_tiles = nt_ref[0]
    tm = MOE_TILE
    k = xbuf.shape[1] // tm

    def weight_copies(e):
        pairs = ((wg_hbm, wg_st), (wu_hbm, wu_st), (wd_hbm, wd_st))
        return [pltpu.make_async_copy(w.at[layer, e], st, wsem.at[n]) for n, (w, st) in enumerate(pairs)]

    def start_tile(tile, slot):
        def body(grp, carry):
            for u in range(_DMA_GROUP):
                r = grp * _DMA_GROUP + u
                src_row = pl.multiple_of(src_ref[tile * tm + r] * k, k)
                pltpu.make_async_copy(h_hbm.at[pl.ds(src_row, k), :],
                                      xbuf.at[slot, pl.ds(pl.multiple_of(r * k, k), k), :],
                                      sem.at[slot]).start(priority=1)
            return carry
        lax.fori_loop(0, tm // _DMA_GROUP, body, 0)

    def wait_tile(slot):
        pltpu.make_async_copy(h_hbm.at[pl.ds(0, tm * k), :], xbuf.at[slot], sem.at[slot]).wait()

    @pl.when(i == 0)
    def _():
        start_tile(0, 0)
        for cp in weight_copies(te_ref[0]):
            cp.start()

    @pl.when(i < n_tiles)
    def _():
        slot = i % 2

        @pl.when(i + 1 < n_tiles)
        def _():
            start_tile(i + 1, 1 - slot)

        changed = jnp.logical_or(i == 0, te_ref[i] != te_ref[jnp.maximum(i - 1, 0)])

        @pl.when(changed)
        def _():
            for cp in weight_copies(te_ref[i]):
                cp.wait()
            wg_bf[...] = wg_st[...].astype(BF16)
            wu_bf[...] = wu_st[...].astype(BF16)
            wd_bf[...] = wd_st[...].astype(BF16)

            @pl.when(nx_ref[i] >= 0)
            def _():
                for cp in weight_copies(nx_ref[i]):
                    cp.start()

        wait_tile(slot)
        xb = _from_token_major(xbuf.at[slot], tm).astype(BF16)
        gt = jnp.dot(xb, wg_bf[...], preferred_element_type=F32)
        up = jnp.dot(xb, wu_bf[...], preferred_element_type=F32)
        act = (gt * (1.0 / (1.0 + jnp.exp(-gt))) * up).astype(BF16)
        _to_token_major(y_ref, jnp.dot(act, wd_bf[...], preferred_element_type=F32))

    @pl.when(i >= n_tiles)
    def _():
        y_ref[...] = jnp.zeros_like(y_ref)


def _moe_call(h2, src, tile_expert, next_expert, n_tiles, wg, wu, wd, layer):
    d, ff = wg.shape[2], wg.shape[3]
    k = d // LANES
    max_tiles = tile_expert.shape[0]
    tm = MOE_TILE
    hbm = pl.BlockSpec(memory_space=pl.ANY)
    grid_spec = pltpu.PrefetchScalarGridSpec(
        num_scalar_prefetch=4,
        grid=(max_tiles,),
        in_specs=[hbm, hbm, hbm, hbm],
        out_specs=pl.BlockSpec((tm * k, LANES), lambda i, src, te, nx, nt: (i, 0)),
        scratch_shapes=[pltpu.VMEM((2, tm * k, LANES), F32), pltpu.SemaphoreType.DMA((2,)),
                        pltpu.VMEM((d, ff), F32), pltpu.VMEM((d, ff), F32), pltpu.VMEM((ff, d), F32),
                        pltpu.SemaphoreType.DMA((3,)),
                        pltpu.VMEM((d, ff), BF16), pltpu.VMEM((d, ff), BF16), pltpu.VMEM((ff, d), BF16)])
    return pl.pallas_call(
        functools.partial(_moe_kernel, layer=layer),
        out_shape=jax.ShapeDtypeStruct((max_tiles * tm * k, LANES), F32),
        grid_spec=grid_spec,
        compiler_params=_params(("arbitrary",), 40),
        name="moe_experts",
    )(src, tile_expert, next_expert, n_tiles, h2, wg, wu, wd)


def _combine_kernel(dst_ref, x_ref, rt_ref, m_ref, g_ref, mn_ref, y_hbm, *rest, last, n_out):
    outs, (ybuf, sem) = rest[:n_out], rest[n_out:]
    i = pl.program_id(0)
    tm = x_ref.shape[0]
    k = ybuf.shape[2] // tm

    def start_tile(tile, slot):
        def body(grp, carry):
            for u in range(_DMA_GROUP):
                r = grp * _DMA_GROUP + u
                for j in range(2):
                    src_row = pl.multiple_of(dst_ref[(tile * tm + r) * 2 + j] * k, k)
                    pltpu.make_async_copy(y_hbm.at[pl.ds(src_row, k), :],
                                          ybuf.at[slot, j, pl.ds(pl.multiple_of(r * k, k), k), :],
                                          sem.at[slot]).start(priority=j)
            return carry
        lax.fori_loop(0, tm // _DMA_GROUP, body, 0)

    @pl.when(i == 0)
    def _():
        start_tile(0, 0)

    slot = i % 2

    @pl.when(i + 1 < pl.num_programs(0))
    def _():
        start_tile(i + 1, 1 - slot)

    for j in range(2):
        pltpu.make_async_copy(y_hbm.at[pl.ds(0, tm * k), :], ybuf.at[slot, j], sem.at[slot]).wait()
    rt = rt_ref[...]
    y = (rt[:, 2:3] * _from_token_major(ybuf.at[slot, 0], tm)
         + rt[:, 3:4] * _from_token_major(ybuf.at[slot, 1], tm))
    xn = x_ref[...] + m_ref[0][5:6] * y
    if last:
        outs[0][...] = xn * lax.rsqrt(jnp.mean(xn * xn, axis=-1, keepdims=True) + EPS) * g_ref[...]
    else:
        outs[0][...] = xn
        mn = mn_ref[0]
        outs[1][...] = _norm_mod(xn, g_ref[...], mn[0:1], mn[1:2]).astype(BF16)


def _combine_call(dst, x, route, mods, g, mods_next, ys, nct, last):
    n, d = x.shape
    tm = ROW_TILE
    mspec = pl.BlockSpec((1, N_MOD, d), lambda i, dst: (jnp.where(i < nct, 1, 0), 0, 0))
    out_shape = [jax.ShapeDtypeStruct((n, d), F32)]
    if not last:
        out_shape.append(jax.ShapeDtypeStruct((n, d), BF16))
    row = lambda w: pl.BlockSpec((tm, w), lambda i, dst: (i, 0))
    grid_spec = pltpu.PrefetchScalarGridSpec(
        num_scalar_prefetch=1,
        grid=(n // tm,),
        in_specs=[row(d), row(LANES), mspec, pl.BlockSpec((1, d), lambda i, dst: (0, 0)), mspec,
                  pl.BlockSpec(memory_space=pl.ANY)],
        out_specs=[row(d)] * len(out_shape),
        scratch_shapes=[pltpu.VMEM((2, 2, tm * (d // LANES), LANES), F32), pltpu.SemaphoreType.DMA((2,))])
    return pl.pallas_call(
        functools.partial(_combine_kernel, last=last, n_out=len(out_shape)),
        out_shape=out_shape,
        grid_spec=grid_spec,
        compiler_params=_params(("arbitrary",), 40),
        name="moe_combine",
    )(dst, x, route, mods, g.reshape(1, d), mods_next, ys)


def _dispatch_plan(route, counts, n_tokens):
    tm = MOE_TILE
    max_tiles = (2 * n_tokens) // tm + N_EXPERTS
    cnt = counts[0, N_GROUPS:N_GROUPS + N_EXPERTS].astype(I32)
    tiles_per = (cnt + tm - 1) // tm
    tile_end = jnp.cumsum(tiles_per)
    offs = (tile_end - tiles_per) * tm
    eid = route[:, 0:2].astype(I32)
    dst = offs[eid] + route[:, 4:6].astype(I32)
    tok = jnp.broadcast_to(jnp.arange(n_tokens, dtype=I32)[:, None], (n_tokens, 2))
    src = jnp.zeros((max_tiles * tm,), I32).at[dst.reshape(-1)].set(tok.reshape(-1))
    n_tiles = tile_end[-1:]
    tile_ids = jnp.minimum(jnp.arange(max_tiles, dtype=I32), n_tiles[0] - 1)
    tile_expert = jnp.sum((tile_end[None, :] <= tile_ids[:, None]).astype(I32), axis=1)
    e_ids = jnp.arange(N_EXPERTS, dtype=I32)
    later = (e_ids[None, :] > e_ids[:, None]) & (tiles_per[None, :] > 0)
    nxt = jnp.min(jnp.where(later, e_ids[None, :], N_EXPERTS), axis=1)
    next_expert = jnp.where(nxt < N_EXPERTS, nxt, -1)[tile_expert]
    return dst.reshape(-1), src, tile_expert, next_expert, n_tiles.astype(I32)


def kernel(x, c, ctx, c_ctx, w_mod, b_mod, norm1_g, norm2_g, w_in, w_out, hg_lb_logits, hg_norm_g,
           q_norm_g, k_norm_g, lam_q1, lam_k1, lam_q2, lam_k2, diff_norm_g, router_group_w,
           router_group_b, router_expert_w, router_expert_b, w_gate, w_up, w_down, final_norm_g):
    depth = w_in.shape[0]
    n_lat, d = x.shape[1], x.shape[2]
    n_ctx = ctx.shape[1]
    nct = n_ctx // ROW_TILE
    assert x.shape[0] == 1 and n_ctx % ROW_TILE == 0 and n_lat % 1024 == 0

    stream = (ctx[0], x[0], 0)
    cc = jnp.zeros((8, d), F32).at[0].set(c[0]).at[1].set(c_ctx)
    mods_all = _mod_call(cc, w_mod, b_mod)[:, :2].reshape(depth, 2, N_MOD, d)
    tabs = _rope_tables(n_lat, n_ctx, HEAD_DIM) + _rope_tables(n_lat, n_ctx, HEAD_DIM // 2)

    h1 = _prenorm_call(stream[0], stream[1], norm1_g[0], mods_all[0], nct)
    out = None
    for l in range(depth):
        last = l == depth - 1
        mods = mods_all[l]
        tile_off = nct if last else 0
        tm_in = 768 if (n_ctx + n_lat) % 768 == 0 else ROW_TILE
        p = _matmul_call(h1, w_in, l, tm_in, 1024)
        o = _hgrn_call(p, hg_lb_logits, l, nct)
        qa, ka, va, qd, kd, vd = _attn_prep_call(p, tabs, q_norm_g[l], k_norm_g[l])
        att = _flash_call(qa, ka, va, n_ctx, GQA_HEADS // GQA_KV_HEADS, 1, tile_off)
        lam_init = 0.8 - 0.6 * math.exp(-0.3 * l)
        dif = _flash_call(qd, kd, vd, n_ctx, 2, 2, tile_off,
                          extra=(lam_q1[l], lam_k1[l], lam_q2[l], lam_k2[l], diff_norm_g[l]),
                          lam_init=lam_init)
        rw = jnp.concatenate([router_group_w[l], router_expert_w[l],
                              jnp.zeros((d, LANES - N_GROUPS - N_EXPERTS), F32)], axis=1)
        rb = jnp.concatenate([router_group_b[l], router_expert_b[l],
                              jnp.zeros((LANES - N_GROUPS - N_EXPERTS,), F32)]).reshape(1, LANES)
        xn, h2, logits = _outproj_call(o, p, hg_norm_g[l], att, dif, w_out[l].astype(BF16), *stream,
                                       mods, norm2_g[l], rw, nct, tile_off)
        n_tok = xn.shape[0]
        route, counts = _router_call(logits, rb)
        dst, src, tile_expert, next_expert, n_tiles = _dispatch_plan(route, counts, n_tok)
        ys = _moe_call(h2, src, tile_expert, next_expert, n_tiles, w_gate, w_up, w_down, l)
        if last:
            (out,) = _combine_call(dst, xn, route, mods, final_norm_g, mods, ys, 0, True)
        else:
            xall, h1 = _combine_call(dst, xn, route, mods, norm1_g[l + 1], mods_all[l + 1], ys, nct, False)
            stream = (xall, xall, nct)
    return out.reshape(1, n_lat, d)
```

```python
import functools
import math

import numpy as np
import jax
import jax.numpy as jnp
from jax import lax
from jax.experimental import pallas as pl
from jax.experimental.pallas import tpu as pltpu

F32 = jnp.float32
BF16 = jnp.bfloat16
I32 = jnp.int32

HEAD_DIM = 128
LANES = 128
GRID_W = 64
ROPE_THETA = 10000.0
EPS = 1e-6
HG_HEADS = 4
HG_CHUNK = 64
GQA_HEADS = 8
GQA_KV_HEADS = 2
DIFF_HEADS = 4
N_GROUPS = 4
EXPERTS_PER_GROUP = 8
N_EXPERTS = N_GROUPS * EXPERTS_PER_GROUP
N_MOD = 6
ROW_TILE = 256
MOE_TILE = 256
_DMA_GROUP = 8
MIB = 1024 * 1024

_LOG2E = math.log2(math.e)
_NT = (((1,), (1,)), ((), ()))
_TN = (((0,), (0,)), ((), ()))


def _params(semantics, vmem_mib):
    return pltpu.CompilerParams(dimension_semantics=semantics, vmem_limit_bytes=vmem_mib * MIB)


def _to_token_major(ref, x):
    n, d = x.shape
    k = d // LANES
    for s in range(k):
        ref[pl.ds(s, n, stride=k), :] = x[:, s * LANES:(s + 1) * LANES]


def _from_token_major(ref, n):
    k = ref.shape[0] // n
    return jnp.concatenate([ref[pl.ds(s, n, stride=k), :] for s in range(k)], axis=1)


def _split_bf16(x, parts):
    out = []
    for _ in range(parts - 1):
        p = x.astype(BF16)
        out.append(p)
        x = x - p.astype(F32)
    out.append(x.astype(BF16))
    return out


def _mod_kernel(a_ref, w_ref, b_ref, o_ref):
    a = a_ref[...]
    a = a * (1.0 / (1.0 + jnp.exp(-a)))
    hi, lo = _split_bf16(a, 2)
    w = w_ref[0].astype(BF16)
    o_ref[0] = (jnp.dot(hi, w, preferred_element_type=F32)
                + jnp.dot(lo, w, preferred_element_type=F32) + b_ref[0])


def _mod_call(cc, w_mod, b_mod):
    depth, d, n = w_mod.shape
    tn = 1024
    return pl.pallas_call(
        _mod_kernel,
        out_shape=jax.ShapeDtypeStruct((depth, 8, n), F32),
        grid=(depth, n // tn),
        in_specs=[pl.BlockSpec((8, d), lambda l, j: (0, 0)),
                  pl.BlockSpec((1, d, tn), lambda l, j: (l, 0, j)),
                  pl.BlockSpec((1, 1, tn), lambda l, j: (l, 0, j))],
        out_specs=pl.BlockSpec((1, 8, tn), lambda l, j: (l, 0, j)),
        compiler_params=_params(("arbitrary", "arbitrary"), 40),
        name="mod_vectors",
    )(cc, w_mod, b_mod.reshape(depth, 1, n))


def _norm_mod(x, g, shift, scale):
    y = x * lax.rsqrt(jnp.mean(x * x, axis=-1, keepdims=True) + EPS) * g
    return y * (1.0 + scale) + shift


def _stream_specs(d, nct, tile_off=0, lat_off=0):
    return [pl.BlockSpec((ROW_TILE, d), lambda i: (jnp.minimum(i + tile_off, nct - 1), 0)),
            pl.BlockSpec((ROW_TILE, d), lambda i: (jnp.maximum(i + tile_off - nct, 0) + lat_off, 0))]


def _stream_tile(c_ref, x_ref, nct, tile_off=0):
    return jnp.where(pl.program_id(0) + tile_off < nct, c_ref[...], x_ref[...])


def _prenorm_kernel(c_ref, x_ref, g_ref, m_ref, o_ref, *, nct):
    m = m_ref[0]
    o_ref[...] = _norm_mod(_stream_tile(c_ref, x_ref, nct), g_ref[...], m[0:1], m[1:2]).astype(BF16)


def _mod_spec(d, nct, tile_off=0):
    return pl.BlockSpec((1, N_MOD, d), lambda i: (jnp.where(i + tile_off < nct, 1, 0), 0, 0))


def _prenorm_call(xc, xl, g, mods, nct):
    d = xl.shape[1]
    t = xc.shape[0] + xl.shape[0]
    return pl.pallas_call(
        functools.partial(_prenorm_kernel, nct=nct),
        out_shape=jax.ShapeDtypeStruct((t, d), BF16),
        grid=(t // ROW_TILE,),
        in_specs=_stream_specs(d, nct) + [pl.BlockSpec((1, d), lambda i: (0, 0)), _mod_spec(d, nct)],
        out_specs=pl.BlockSpec((ROW_TILE, d), lambda i: (i, 0)),
        compiler_params=_params(("arbitrary",), 24),
        name="prenorm",
    )(xc, xl, g.reshape(1, d), mods)


def _mm_kernel(a_ref, b_ref, o_ref, b_bf):
    @pl.when(pl.program_id(1) == 0)
    def _():
        b_bf[...] = b_ref[0].astype(BF16)

    o_ref[...] = jnp.dot(a_ref[...], b_bf[...], preferred_element_type=F32)


def _matmul_call(a, b, layer, tm, tn):
    m, k = a.shape
    n = b.shape[2]
    return pl.pallas_call(
        _mm_kernel,
        out_shape=jax.ShapeDtypeStruct((m, n), F32),
        grid=(n // tn, m // tm),
        in_specs=[pl.BlockSpec((tm, k), lambda j, i: (i, 0)),
                  pl.BlockSpec((1, k, tn), lambda j, i: (layer, 0, j))],
        out_specs=pl.BlockSpec((tm, tn), lambda j, i: (i, j)),
        scratch_shapes=[pltpu.VMEM((k, tn), BF16)],
        compiler_params=_params(("arbitrary", "arbitrary"), 48),
        name="in_proj",
    )(a, b)


_HG_LEVELS = (1, 2, 4, 8, 16, 32)
_HG_TOT_ROW = HG_CHUNK * (len(_HG_LEVELS) + 1)
_HG_W_ROWS = _HG_TOT_ROW + 16
_HG_HEADS_PER_STEP = 4


def _hgrn_consts():
    c = HG_CHUNK
    w = np.zeros((2, _HG_W_ROWS, c), np.float32)
    msk = np.zeros((2, len(_HG_LEVELS) + 1, c, c), np.float32)
    for d in range(2):
        u = np.arange(c) if d == 0 else c - 1 - np.arange(c)
        ut, us = u[:, None], u[None, :]
        w[d, :c] = us <= ut
        for li, lv in enumerate(_HG_LEVELS):
            blk = u // (2 * lv)
            qside = (u % (2 * lv)) >= lv
            bnd = (blk * 2 * lv + lv - 1)[:, None]
            wq = (us > bnd) & (us <= ut)
            wk = (us > ut) & (us <= bnd)
            w[d, c * (li + 1):c * (li + 2)] = np.where(qside[:, None], wq, -1.0 * wk)
            msk[d, li] = (blk[:, None] == blk[None, :]) & qside[:, None] & ~qside[None, :]
        msk[d, len(_HG_LEVELS)] = np.eye(c)
        w[d, _HG_TOT_ROW:] = 1.0
    return jnp.asarray(np.concatenate([w, w, w], axis=2), BF16), jnp.asarray(msk, F32)


def _hgrn_kernel(lbl_ref, q_ref, z_ref, v_ref, w_ref, msk_ref, o_ref, st_ref, *, layer, chunks, hps):
    c = HG_CHUNK
    hd = HEAD_DIM
    d = pl.program_id(0)
    j = pl.program_id(2)

    @pl.when(j == 0)
    def _():
        st_ref[...] = jnp.zeros_like(st_ref)

    lbl = lbl_ref[...]
    rows = [lbl[i:i + 1] for i in range(lbl.shape[0])]
    mx = functools.reduce(jnp.maximum, rows)
    ex = [jnp.exp(r - mx) for r in rows]
    tot = functools.reduce(lambda a, b: a + b, ex)
    lb = jnp.zeros_like(mx)
    for i in range(1, layer + 1):
        lb = lb + ex[i] / tot
    log_lb = jnp.log(lb)
    log_1m_lb = jnp.log1p(-lb)

    wmat = w_ref[0]
    nlev = len(_HG_LEVELS)

    def finish(h, hs, r0, amat, v_bf, qe, kd, decay):
        st = st_ref[h]
        o = (jnp.dot(amat, v_bf[:, hs], preferred_element_type=F32)
             + lax.dot_general(qe[:, hs], st.astype(BF16), _NT, preferred_element_type=F32))
        o_ref[0, pl.ds(r0, c), hs] = o
        st_ref[h] = st * decay[:, hs] + lax.dot_general(v_bf[:, hs], kd[:, hs], _TN,
                                                        preferred_element_type=F32)

    pending = None
    for ci in range(chunks):
        cc = ci + d * (chunks - 1 - 2 * ci)
        r0 = pl.multiple_of(cc * c, c)
        q = q_ref[pl.ds(r0, c), :]
        z = z_ref[pl.ds(r0, c), :]
        v_bf = v_ref[pl.ds(r0, c), :].astype(BF16)
        l1p = jnp.log(1.0 + jnp.exp(-jnp.abs(z)))
        ls_pos = jnp.minimum(z, 0.0) - l1p
        ls_neg = jnp.minimum(-z, 0.0) - l1p
        a2 = log_1m_lb + ls_pos
        logf = jnp.maximum(log_lb, a2) + jnp.log(1.0 + jnp.exp(-jnp.abs(log_lb - a2)))
        k = (1.0 - lb) * jnp.exp(ls_neg)

        parts = jnp.concatenate(_split_bf16(logf, 3), axis=0)
        sums = jnp.dot(wmat, parts, preferred_element_type=F32)
        b = sums[0:c]
        btot = sums[_HG_TOT_ROW:_HG_TOT_ROW + 1]

        qls, kls = [q.astype(BF16)], [k.astype(BF16)]
        for li in range(nlev):
            fl = jnp.exp(-jnp.abs(sums[c * (li + 1):c * (li + 2)]))
            qls.append((q * fl).astype(BF16))
            kls.append((k * fl).astype(BF16))
        qe = (q * jnp.exp(b)).astype(BF16)
        kd = (k * jnp.exp(btot - b)).astype(BF16)
        decay = jnp.exp(btot)

        for h in range(hps):
            hs = slice(h * hd, (h + 1) * hd)
            amat = lax.dot_general(qls[0][:, hs], kls[0][:, hs], _NT,
                                   preferred_element_type=F32) * msk_ref[0, nlev]
            for li in range(nlev):
                amat = amat + lax.dot_general(qls[li + 1][:, hs], kls[li + 1][:, hs], _NT,
                                              preferred_element_type=F32) * msk_ref[0, li]
            if pending is not None:
                finish(*pending)
            pending = (h, hs, r0, amat.astype(BF16), v_bf, qe, kd, decay)
    finish(*pending)


def _hgrn_call(p, lb_logits, layer, nct):
    t = p.shape[0]
    nblk = t // ROW_TILE
    chunks = ROW_TILE // HG_CHUNK
    hps = _HG_HEADS_PER_STEP
    hw = hps * HEAD_DIM
    ng = HG_HEADS // hps
    wmat, msk = _hgrn_consts()

    def blk(d, j):
        back = jnp.where(j < nct, nct - 1 - j, nblk - 1 - (j - nct))
        return jnp.where(d == 0, j, back)

    return pl.pallas_call(
        functools.partial(_hgrn_kernel, layer=layer, chunks=chunks, hps=hps),
        out_shape=jax.ShapeDtypeStruct((2, t, HG_HEADS * HEAD_DIM), F32),
        grid=(2, ng, nblk),
        in_specs=[pl.BlockSpec((lb_logits.shape[0], hw), lambda d, h, j: (0, h)),
                  pl.BlockSpec((ROW_TILE, hw), lambda d, h, j: (blk(d, j), d * ng + h)),
                  pl.BlockSpec((ROW_TILE, hw), lambda d, h, j: (blk(d, j), (2 + d) * ng + h)),
                  pl.BlockSpec((ROW_TILE, hw), lambda d, h, j: (blk(d, j), 4 * ng + h)),
                  pl.BlockSpec((1, _HG_W_ROWS, 3 * HG_CHUNK), lambda d, h, j: (d, 0, 0)),
                  pl.BlockSpec((1, len(_HG_LEVELS) + 1, HG_CHUNK, HG_CHUNK), lambda d, h, j: (d, 0, 0, 0))],
        out_specs=pl.BlockSpec((1, ROW_TILE, hw), lambda d, h, j: (d, blk(d, j), h)),
        scratch_shapes=[pltpu.VMEM((hps, HEAD_DIM, HEAD_DIM), F32)],
        compiler_params=_params(("arbitrary", "arbitrary", "arbitrary"), 24),
        name="hgrn_scan",
    )(lb_logits, p, p, p, wmat, msk)


def _rope_tables(n_lat, n_ctx, dim):
    rows = n_lat // GRID_W
    row = jnp.repeat(jnp.arange(rows, dtype=F32), GRID_W)
    col = jnp.tile(jnp.arange(GRID_W, dtype=F32), rows)
    axis_dim = dim // 2
    inv_freq = ROPE_THETA ** (-jnp.arange(0, axis_dim, 2, dtype=F32) / axis_dim)
    ang = jnp.concatenate([row[:, None] * inv_freq, col[:, None] * inv_freq], axis=-1)
    cos = jnp.repeat(jnp.cos(ang), 2, axis=1)
    sin = jnp.repeat(jnp.sin(ang), 2, axis=1) * jnp.tile(jnp.asarray([-1.0, 1.0], F32), dim // 2)
    reps = LANES // dim
    cos = jnp.concatenate([jnp.ones((n_ctx, dim), F32), cos], axis=0)
    sin = jnp.concatenate([jnp.zeros((n_ctx, dim), F32), sin], axis=0)
    return jnp.tile(cos, (1, reps)), jnp.tile(sin, (1, reps))


def _rope(x, cos, sin):
    lane = lax.broadcasted_iota(I32, x.shape, 1)
    swapped = jnp.where((lane & 1) == 0, pltpu.roll(x, LANES - 1, 1), pltpu.roll(x, 1, 1))
    return x * cos + swapped * sin


def _head_norm(x, g):
    return x * lax.rsqrt(jnp.mean(x * x, axis=-1, keepdims=True) + EPS) * g


def _attn_prep_kernel(pq_ref, pkv_ref, pd_ref, ca_ref, sa_ref, cd_ref, sd_ref, qn_ref, kn_ref,
                      qa_ref, ka_ref, va_ref, qd_ref, kd_ref, vd_ref):
    hd = HEAD_DIM
    ca, sa, cd, sd = ca_ref[...], sa_ref[...], cd_ref[...], sd_ref[...]
    qn, kn = qn_ref[...], kn_ref[...]
    for h in range(GQA_HEADS):
        xq = _rope(_head_norm(pq_ref[:, h * hd:(h + 1) * hd], qn), ca, sa)
        qa_ref[:, h * hd:(h + 1) * hd] = (xq * (hd ** -0.5 * _LOG2E)).astype(BF16)
    for h in range(GQA_KV_HEADS):
        xk = _rope(_head_norm(pkv_ref[:, h * hd:(h + 1) * hd], kn), ca, sa)
        ka_ref[:, h * hd:(h + 1) * hd] = xk.astype(BF16)
    ones_col = jnp.where(lax.broadcasted_iota(I32, (pq_ref.shape[0], hd), 1) == 0, 1.0, 0.0).astype(BF16)
    for h in range(GQA_KV_HEADS):
        va_ref[:, (2 * h) * hd:(2 * h + 1) * hd] = pkv_ref[:, (GQA_KV_HEADS + h) * hd:
                                                            (GQA_KV_HEADS + h + 1) * hd].astype(BF16)
        va_ref[:, (2 * h + 1) * hd:(2 * h + 2) * hd] = ones_col
    first = lax.broadcasted_iota(I32, (pq_ref.shape[0], hd), 1) < hd // 2
    dw = DIFF_HEADS * hd
    for h in range(DIFF_HEADS):
        xq = _rope(pd_ref[:, h * hd:(h + 1) * hd], cd, sd) * ((hd // 2) ** -0.5 * _LOG2E)
        qd_ref[:, (2 * h) * hd:(2 * h + 1) * hd] = jnp.where(first, xq, 0.0).astype(BF16)
        qd_ref[:, (2 * h + 1) * hd:(2 * h + 2) * hd] = jnp.where(first, 0.0, xq).astype(BF16)
        xk = _rope(pd_ref[:, dw + h * hd:dw + (h + 1) * hd], cd, sd)
        kd_ref[:, h * hd:(h + 1) * hd] = xk.astype(BF16)
    for h in range(DIFF_HEADS):
        vd_ref[:, (2 * h) * hd:(2 * h + 1) * hd] = pd_ref[:, 2 * dw + h * hd:2 * dw + (h + 1) * hd].astype(BF16)
        vd_ref[:, (2 * h + 1) * hd:(2 * h + 2) * hd] = ones_col


def _attn_prep_call(p, tabs, qn, kn):
    t = p.shape[0]
    hd = HEAD_DIM
    gq, gkv, dw = GQA_HEADS * hd, 2 * GQA_KV_HEADS * hd, DIFF_HEADS * hd
    q_off = 6 * HG_HEADS * hd
    assert q_off % gq == 0 and (q_off + gq) % gkv == 0 and (q_off + gq + gkv) % (3 * dw) == 0
    row = lambda w: pl.BlockSpec((ROW_TILE, w), lambda i: (i, 0))
    vec = pl.BlockSpec((1, hd), lambda i: (0, 0))
    return pl.pallas_call(
        _attn_prep_kernel,
        out_shape=[jax.ShapeDtypeStruct((t, gq), BF16),
                   jax.ShapeDtypeStruct((t, gkv // 2), BF16),
                   jax.ShapeDtypeStruct((t, gkv), BF16),
                   jax.ShapeDtypeStruct((t, 2 * dw), BF16),
                   jax.ShapeDtypeStruct((t, dw), BF16),
                   jax.ShapeDtypeStruct((t, 2 * dw), BF16)],
        grid=(t // ROW_TILE,),
        in_specs=[pl.BlockSpec((ROW_TILE, gq), lambda i: (i, q_off // gq)),
                  pl.BlockSpec((ROW_TILE, gkv), lambda i: (i, (q_off + gq) // gkv)),
                  pl.BlockSpec((ROW_TILE, 3 * dw), lambda i: (i, (q_off + gq + gkv) // (3 * dw))),
                  row(hd), row(hd), row(hd), row(hd), vec, vec],
        out_specs=[row(gq), row(gkv // 2), row(gkv), row(2 * dw), row(dw), row(2 * dw)],
        compiler_params=_params(("arbitrary",), 24),
        name="attn_prep",
    )(p, p, p, *tabs, qn.reshape(1, hd), kn.reshape(1, hd))


def _flash_kernel(*refs, g, nh, n_ctx, n_lat, tkl, nct, tile_off, diff, lam_init):
    if diff:
        q_ref, k_ref, v_ref, lq1, lk1, lq2, lk2, dn_ref, o_ref = refs[:9]
    else:
        q_ref, k_ref, v_ref, o_ref = refs[:4]
    nc = nh * g
    m_sc, acc_sc, p_sc = (refs[len(refs) - (3 - n) * nc:len(refs) - (2 - n) * nc] for n in range(3))
    hd = HEAD_DIM
    i = pl.program_id(1) + tile_off
    nblk = n_lat // tkl
    chains = [(c, c // g) for c in range(nh * g)]

    def head(ref, rows, h, width=1):
        return ref[rows, h * width * hd:(h + 1) * width * hd]

    def vhead(rows, h):
        return head(v_ref, rows, h, 2)

    def normalised(c):
        acc = acc_sc[c][...]
        return acc[:, :hd] / acc[:, hd:hd + 1]

    ctx_rows = slice(0, n_ctx)
    for c, kh in chains:
        s = lax.dot_general(head(q_ref, slice(None), c), head(k_ref, ctx_rows, kh), _NT,
                            preferred_element_type=F32)
        m0 = jnp.max(s, axis=1, keepdims=True)
        p = jnp.exp2(s - m0)
        m_sc[c][...] = jnp.broadcast_to(m0, m_sc[c].shape)
        acc_sc[c][...] = jnp.dot(p.astype(BF16), vhead(ctx_rows, kh), preferred_element_type=F32)

    def probs(c, kh, rows):
        s = lax.dot_general(head(q_ref, slice(None), c), head(k_ref, rows, kh), _NT,
                            preferred_element_type=F32)
        m_prev = m_sc[c][...]
        m_new = jnp.maximum(m_prev, jnp.max(s, axis=1, keepdims=True))
        p = jnp.exp2(s - jnp.tile(m_new, (1, tkl // LANES)))
        alpha = jnp.exp2(m_prev - m_new)
        m_sc[c][...] = m_new
        return p.astype(BF16), jnp.tile(alpha, (1, 2))

    @pl.when(i >= nct)
    def _():
        for c, kh in chains:
            p, alpha = probs(c, kh, pl.ds(n_ctx, tkl))
            p_sc[c][0] = p
            acc_sc[c][...] = alpha * acc_sc[c][...]

        for jb in range(1, nblk):
            r0 = n_ctx + jb * tkl
            prev = pl.ds(r0 - tkl, tkl)
            slot = jb % 2
            alphas = []
            for c, kh in chains:
                p, alpha = probs(c, kh, pl.ds(r0, tkl))
                p_sc[c][slot] = p
                alphas.append(alpha)
            for c, kh in chains:
                pv = jnp.dot(p_sc[c][1 - slot], vhead(prev, kh), preferred_element_type=F32)
                acc_sc[c][...] = alphas[c] * (acc_sc[c][...] + pv)

        last = pl.ds(n_ctx + (nblk - 1) * tkl, tkl)
        for c, kh in chains:
            acc_sc[c][...] = acc_sc[c][...] + jnp.dot(p_sc[c][(nblk - 1) % 2], vhead(last, kh),
                                                      preferred_element_type=F32)

    if diff:
        lam = (jnp.exp(jnp.sum(lq1[...] * lk1[...], axis=1, keepdims=True))
               - jnp.exp(jnp.sum(lq2[...] * lk2[...], axis=1, keepdims=True)) + lam_init)
        for kh in range(nh):
            dd = normalised(2 * kh) - lam * normalised(2 * kh + 1)
            o_ref[:, kh * hd:(kh + 1) * hd] = (_head_norm(dd, dn_ref[...]) * (1.0 - lam_init)).astype(BF16)
    else:
        for c, _ in chains:
            o_ref[:, c * hd:(c + 1) * hd] = normalised(c).astype(BF16)


def _flash_call(q, k, v, n_ctx, g, nh, tile_off, extra=None, lam_init=0.0):
    t = k.shape[0]
    hd = HEAD_DIM
    n_kv = k.shape[1] // (nh * hd)
    n_lat = t - n_ctx
    nct = n_ctx // ROW_TILE
    tkl = 2048
    diff = extra is not None
    ow = nh * hd if diff else nh * g * hd
    in_specs = [pl.BlockSpec((ROW_TILE, nh * g * hd), lambda kv, i: (i + tile_off, kv)),
                pl.BlockSpec((t, nh * hd), lambda kv, i: (0, kv)),
                pl.BlockSpec((t, 2 * nh * hd), lambda kv, i: (0, kv))]
    args = [q, k, v]
    if diff:
        for a in extra:
            a = a.reshape(1, -1)
            in_specs.append(pl.BlockSpec(a.shape, lambda kv, i: (0, 0)))
            args.append(a)
    return pl.pallas_call(
        functools.partial(_flash_kernel, g=g, nh=nh, n_ctx=n_ctx, n_lat=n_lat, tkl=tkl, nct=nct,
                          tile_off=tile_off, diff=diff, lam_init=lam_init),
        out_shape=jax.ShapeDtypeStruct((t - tile_off * ROW_TILE, n_kv * ow), BF16),
        grid=(n_kv, t // ROW_TILE - tile_off),
        in_specs=in_specs,
        out_specs=pl.BlockSpec((ROW_TILE, ow), lambda kv, i: (i, kv)),
        scratch_shapes=([pltpu.VMEM((ROW_TILE, LANES), F32)] * (nh * g)
                        + [pltpu.VMEM((ROW_TILE, 2 * hd), F32)] * (nh * g)
                        + [pltpu.VMEM((2, ROW_TILE, tkl), BF16)] * (nh * g)),
        compiler_params=_params(("arbitrary", "arbitrary"), 48),
        name="diff_attention" if diff else "gqa_attention",
    )(*args)


def _outproj_kernel(of_ref, ob_ref, gate_ref, ng_ref, att_ref, dif_ref, wo_ref, xc_ref, xl_ref, m_ref,
                    n2_ref, rw_ref, xo_ref, h2_ref, lg_ref, *, nct, tile_off):
    hd = HEAD_DIM
    hgw = HG_HEADS * hd
    ng = ng_ref[...]
    gate = gate_ref[...]
    silu_gate = gate * (1.0 / (1.0 + jnp.exp(-gate)))
    acc = jnp.dot(att_ref[...], wo_ref[hgw:hgw + att_ref.shape[1], :], preferred_element_type=F32)
    acc = acc + jnp.dot(dif_ref[...], wo_ref[hgw + att_ref.shape[1]:, :], preferred_element_type=F32)
    hg = []
    for h in range(HG_HEADS):
        o = of_ref[0, :, h * hd:(h + 1) * hd] + ob_ref[0, :, h * hd:(h + 1) * hd]
        hg.append((_head_norm(o, ng) * silu_gate[:, h * hd:(h + 1) * hd]).astype(BF16))
    acc = acc + jnp.dot(jnp.concatenate(hg, axis=1), wo_ref[0:hgw, :], preferred_element_type=F32)
    m = m_ref[0]
    xn = _stream_tile(xc_ref, xl_ref, nct, tile_off) + m[2:3] * acc
    xo_ref[...] = xn
    h2 = _norm_mod(xn, n2_ref[...], m[3:4], m[4:5])
    _to_token_major(h2_ref, h2)
    hs = _split_bf16(h2, 2)
    ws = _split_bf16(rw_ref[...], 2)
    lg = jnp.dot(hs[1], ws[0], preferred_element_type=F32)
    lg = lg + jnp.dot(hs[0], ws[1], preferred_element_type=F32)
    lg_ref[...] = lg + jnp.dot(hs[0], ws[0], preferred_element_type=F32)


def _outproj_call(o, p, ng, att, dif, wo, xc, xl, lat_off, mods, n2, rw, nct, tile_off):
    d = xl.shape[1]
    hd = HEAD_DIM
    hgw = HG_HEADS * hd
    n_rows = att.shape[0]
    row = lambda w: pl.BlockSpec((ROW_TILE, w), lambda i: (i, 0))
    full = lambda a: pl.BlockSpec(a.shape, lambda i: (0,) * a.ndim)
    ng, n2 = ng.reshape(1, hd), n2.reshape(1, d)
    return pl.pallas_call(
        functools.partial(_outproj_kernel, nct=nct, tile_off=tile_off),
        out_shape=[jax.ShapeDtypeStruct((n_rows, d), F32),
                   jax.ShapeDtypeStruct((n_rows * (d // LANES), LANES), F32),
                   jax.ShapeDtypeStruct((n_rows, LANES), F32)],
        grid=(n_rows // ROW_TILE,),
        in_specs=[pl.BlockSpec((1, ROW_TILE, hgw), lambda i: (0, i + tile_off, 0)),
                  pl.BlockSpec((1, ROW_TILE, hgw), lambda i: (1, i + tile_off, 0)),
                  pl.BlockSpec((ROW_TILE, hgw), lambda i: (i + tile_off, 5)),
                  full(ng), row(att.shape[1]), row(dif.shape[1]), full(wo)]
                 + _stream_specs(d, nct, tile_off, lat_off)
                 + [_mod_spec(d, nct, tile_off), full(n2), full(rw)],
        out_specs=[row(d), pl.BlockSpec((ROW_TILE * (d // LANES), LANES), lambda i: (i, 0)), row(LANES)],
        compiler_params=_params(("arbitrary",), 56),
        name="out_proj",
    )(o, o, p, ng, att, dif, wo, xc, xl, mods, n2, rw)


def _router_kernel(lg_ref, bias_ref, r_ref, cnt_ref, run_ref):
    i = pl.program_id(0)
    tm = lg_ref.shape[0]

    @pl.when(i == 0)
    def _():
        run_ref[...] = jnp.zeros_like(run_ref)

    lane = lax.broadcasted_iota(I32, (tm, LANES), 1).astype(F32)
    lg = lg_ref[...] + bias_ref[...]
    ninf = -jnp.inf

    def first_max(vals):
        mx = jnp.max(vals, axis=1, keepdims=True)
        idx = jnp.min(jnp.where(vals == mx, lane, float(LANES)), axis=1, keepdims=True)
        return mx, idx

    gl = jnp.where(lane < N_GROUPS, lg, ninf)
    gmax, gidx = first_max(gl)
    g_top = 1.0 / jnp.sum(jnp.exp(gl - gmax), axis=1, keepdims=True)
    lo = N_GROUPS + EXPERTS_PER_GROUP * gidx
    el = jnp.where((lane >= lo) & (lane < lo + EXPERTS_PER_GROUP), lg, ninf)
    m1, e1 = first_max(el)
    m2, e2 = first_max(jnp.where(lane == e1, ninf, el))
    r = jnp.exp(m2 - m1)
    w1 = g_top / (1.0 + r)
    w2 = g_top * r / (1.0 + r)

    hit = ((lane == e1) | (lane == e2)).astype(BF16)
    ti = lax.broadcasted_iota(I32, (tm, tm), 0)
    si = lax.broadcasted_iota(I32, (tm, tm), 1)
    before = (si < ti).astype(BF16)
    pos = jnp.dot(before, hit, preferred_element_type=F32) + run_ref[0:1, :]
    p1 = jnp.sum(jnp.where(lane == e1, pos, 0.0), axis=1, keepdims=True)
    p2 = jnp.sum(jnp.where(lane == e2, pos, 0.0), axis=1, keepdims=True)
    total = run_ref[0:1, :] + jnp.sum(hit.astype(F32), axis=0, keepdims=True)
    run_ref[...] = jnp.broadcast_to(total, run_ref.shape)
    cnt_ref[...] = jnp.broadcast_to(total, cnt_ref.shape)

    fields = (e1 - N_GROUPS, e2 - N_GROUPS, w1, w2, p1, p2)
    out = jnp.zeros((tm, LANES), F32)
    for n, f in enumerate(fields):
        out = jnp.where(lane == n, f, out)
    r_ref[...] = out


def _router_call(logits, bias):
    n = logits.shape[0]
    return pl.pallas_call(
        _router_kernel,
        out_shape=[jax.ShapeDtypeStruct((n, LANES), F32), jax.ShapeDtypeStruct((8, LANES), F32)],
        grid=(n // ROW_TILE,),
        in_specs=[pl.BlockSpec((ROW_TILE, LANES), lambda i: (i, 0)),
                  pl.BlockSpec((1, LANES), lambda i: (0, 0))],
        out_specs=[pl.BlockSpec((ROW_TILE, LANES), lambda i: (i, 0)),
                   pl.BlockSpec((8, LANES), lambda i: (0, 0))],
        scratch_shapes=[pltpu.VMEM((8, LANES), F32)],
        compiler_params=_params(("arbitrary",), 16),
        name="router",
    )(logits, bias)


def _moe_kernel(src_ref, te_ref, nx_ref, nt_ref, h_hbm, wg_hbm, wu_hbm, wd_hbm, y_ref,
                xbuf, sem, wg_st, wu_st, wd_st, wsem, wg_bf, wu_bf, wd_bf, xb_sc, act_sc, *, layer):
    i = pl.program_id(0)
    n_tiles = nt_ref[0]
    tm = MOE_TILE
    k = xbuf.shape[1] // tm

    def weight_copies(e):
        pairs = ((wg_hbm, wg_st), (wu_hbm, wu_st), (wd_hbm, wd_st))
        return [pltpu.make_async_copy(w.at[layer, e], st, wsem.at[n]) for n, (w, st) in enumerate(pairs)]

    def start_row(tile, slot, r):
        src_row = pl.multiple_of(src_ref[tile * tm + r] * k, k)
        pltpu.make_async_copy(h_hbm.at[pl.ds(src_row, k), :],
                              xbuf.at[slot, pl.ds(pl.multiple_of(r * k, k), k), :],
                              sem.at[slot]).start(priority=1)

    def start_tile(tile, slot):
        def body(grp, carry):
            for u in range(_DMA_GROUP):
                start_row(tile, slot, grp * _DMA_GROUP + u)
            return carry
        lax.fori_loop(0, tm // _DMA_GROUP, body, 0)

    def wait_tile(slot):
        pltpu.make_async_copy(h_hbm.at[pl.ds(0, tm * k), :], xbuf.at[slot], sem.at[slot]).wait()

    @pl.when(i == 0)
    def _():
        start_tile(0, 0)
        for cp in weight_copies(te_ref[0]):
            cp.start()

    @pl.when(i < n_tiles)
    def _():
        slot = i % 2
        n_batches = 4
        per_batch = tm // n_batches

        def prefetch_rows(b):
            @pl.when(i + 1 < n_tiles)
            def _():
                for r in range(b * per_batch, (b + 1) * per_batch):
                    start_row(i + 1, 1 - slot, r)

        changed = jnp.logical_or(i == 0, te_ref[i] != te_ref[jnp.maximum(i - 1, 0)])

        @pl.when(changed)
        def _():
            for cp in weight_copies(te_ref[i]):
                cp.wait()
            wg_bf[...] = wg_st[...].astype(BF16)
            wu_bf[...] = wu_st[...].astype(BF16)
            wd_bf[...] = wd_st[...].astype(BF16)

            @pl.when(nx_ref[i] >= 0)
            def _():
                for cp in weight_copies(nx_ref[i]):
                    cp.start()

        wait_tile(slot)
        xb_sc[...] = _from_token_major(xbuf.at[slot], tm).astype(BF16)
        prefetch_rows(0)
        gt = jnp.dot(xb_sc[...], wg_bf[...], preferred_element_type=F32)
        act_sc[...] = gt * (1.0 / (1.0 + jnp.exp(-gt)))
        prefetch_rows(1)
        up = jnp.dot(xb_sc[...], wu_bf[...], preferred_element_type=F32)
        act_sc[...] = act_sc[...] * up
        prefetch_rows(2)
        _to_token_major(y_ref, jnp.dot(act_sc[...].astype(BF16), wd_bf[...], preferred_element_type=F32))
        prefetch_rows(3)

    @pl.when(i >= n_tiles)
    def _():
        y_ref[...] = jnp.zeros_like(y_ref)


def _moe_call(h2, src, tile_expert, next_expert, n_tiles, wg, wu, wd, layer):
    d, ff = wg.shape[2], wg.shape[3]
    k = d // LANES
    max_tiles = tile_expert.shape[0]
    tm = MOE_TILE
    hbm = pl.BlockSpec(memory_space=pl.ANY)
    grid_spec = pltpu.PrefetchScalarGridSpec(
        num_scalar_prefetch=4,
        grid=(max_tiles,),
        in_specs=[hbm, hbm, hbm, hbm],
        out_specs=pl.BlockSpec((tm * k, LANES), lambda i, src, te, nx, nt: (i, 0)),
        scratch_shapes=[pltpu.VMEM((2, tm * k, LANES), F32), pltpu.SemaphoreType.DMA((2,)),
                        pltpu.VMEM((d, ff), F32), pltpu.VMEM((d, ff), F32), pltpu.VMEM((ff, d), F32),
                        pltpu.SemaphoreType.DMA((3,)),
                        pltpu.VMEM((d, ff), BF16), pltpu.VMEM((d, ff), BF16), pltpu.VMEM((ff, d), BF16),
                        pltpu.VMEM((tm, d), BF16), pltpu.VMEM((tm, ff), F32)])
    return pl.pallas_call(
        functools.partial(_moe_kernel, layer=layer),
        out_shape=jax.ShapeDtypeStruct((max_tiles * tm * k, LANES), F32),
        grid_spec=grid_spec,
        compiler_params=_params(("arbitrary",), 40),
        name="moe_experts",
    )(src, tile_expert, next_expert, n_tiles, h2, wg, wu, wd)


def _combine_kernel(dst_ref, x_ref, rt_ref, m_ref, g_ref, mn_ref, y_hbm, *rest, last, n_out):
    outs, (ybuf, sem) = rest[:n_out], rest[n_out:]
    i = pl.program_id(0)
    tm = x_ref.shape[0]
    k = ybuf.shape[2] // tm

    def start_tile(tile, slot):
        def body(grp, carry):
            for u in range(_DMA_GROUP):
                r = grp * _DMA_GROUP + u
                for j in range(2):
                    src_row = pl.multiple_of(dst_ref[(tile * tm + r) * 2 + j] * k, k)
                    pltpu.make_async_copy(y_hbm.at[pl.ds(src_row, k), :],
                                          ybuf.at[slot, j, pl.ds(pl.multiple_of(r * k, k), k), :],
                                          sem.at[slot]).start(priority=j)
            return carry
        lax.fori_loop(0, tm // _DMA_GROUP, body, 0)

    @pl.when(i == 0)
    def _():
        start_tile(0, 0)

    slot = i % 2

    @pl.when(i + 1 < pl.num_programs(0))
    def _():
        start_tile(i + 1, 1 - slot)

    for j in range(2):
        pltpu.make_async_copy(y_hbm.at[pl.ds(0, tm * k), :], ybuf.at[slot, j], sem.at[slot]).wait()
    rt = rt_ref[...]
    y = (rt[:, 2:3] * _from_token_major(ybuf.at[slot, 0], tm)
         + rt[:, 3:4] * _from_token_major(ybuf.at[slot, 1], tm))
    xn = x_ref[...] + m_ref[0][5:6] * y
    if last:
        outs[0][...] = xn * lax.rsqrt(jnp.mean(xn * xn, axis=-1, keepdims=True) + EPS) * g_ref[...]
    else:
        outs[0][...] = xn
        mn = mn_ref[0]
        outs[1][...] = _norm_mod(xn, g_ref[...], mn[0:1], mn[1:2]).astype(BF16)


def _combine_call(dst, x, route, mods, g, mods_next, ys, nct, last):
    n, d = x.shape
    tm = ROW_TILE
    mspec = pl.BlockSpec((1, N_MOD, d), lambda i, dst: (jnp.where(i < nct, 1, 0), 0, 0))
    out_shape = [jax.ShapeDtypeStruct((n, d), F32)]
    if not last:
        out_shape.append(jax.ShapeDtypeStruct((n, d), BF16))
    row = lambda w: pl.BlockSpec((tm, w), lambda i, dst: (i, 0))
    grid_spec = pltpu.PrefetchScalarGridSpec(
        num_scalar_prefetch=1,
        grid=(n // tm,),
        in_specs=[row(d), row(LANES), mspec, pl.BlockSpec((1, d), lambda i, dst: (0, 0)), mspec,
                  pl.BlockSpec(memory_space=pl.ANY)],
        out_specs=[row(d)] * len(out_shape),
        scratch_shapes=[pltpu.VMEM((2, 2, tm * (d // LANES), LANES), F32), pltpu.SemaphoreType.DMA((2,))])
    return pl.pallas_call(
        functools.partial(_combine_kernel, last=last, n_out=len(out_shape)),
        out_shape=out_shape,
        grid_spec=grid_spec,
        compiler_params=_params(("arbitrary",), 40),
        name="moe_combine",
    )(dst, x, route, mods, g.reshape(1, d), mods_next, ys)


def _dispatch_plan(route, counts, n_tokens):
    tm = MOE_TILE
    max_tiles = (2 * n_tokens) // tm + N_EXPERTS
    cnt = counts[0, N_GROUPS:N_GROUPS + N_EXPERTS].astype(I32)
    tiles_per = (cnt + tm - 1) // tm
    tile_end = jnp.cumsum(tiles_per)
    offs = (tile_end - tiles_per) * tm
    eid = route[:, 0:2].astype(I32)
    dst = offs[eid] + route[:, 4:6].astype(I32)
    tok = jnp.broadcast_to(jnp.arange(n_tokens, dtype=I32)[:, None], (n_tokens, 2))
    src = jnp.zeros((max_tiles * tm,), I32).at[dst.reshape(-1)].set(tok.reshape(-1))
    n_tiles = tile_end[-1:]
    tile_ids = jnp.minimum(jnp.arange(max_tiles, dtype=I32), n_tiles[0] - 1)
    tile_expert = jnp.sum((tile_end[None, :] <= tile_ids[:, None]).astype(I32), axis=1)
    e_ids = jnp.arange(N_EXPERTS, dtype=I32)
    later = (e_ids[None, :] > e_ids[:, None]) & (tiles_per[None, :] > 0)
    nxt = jnp.min(jnp.where(later, e_ids[None, :], N_EXPERTS), axis=1)
    next_expert = jnp.where(nxt < N_EXPERTS, nxt, -1)[tile_expert]
    return dst.reshape(-1), src, tile_expert, next_expert, n_tiles.astype(I32)


def kernel(x, c, ctx, c_ctx, w_mod, b_mod, norm1_g, norm2_g, w_in, w_out, hg_lb_logits, hg_norm_g,
           q_norm_g, k_norm_g, lam_q1, lam_k1, lam_q2, lam_k2, diff_norm_g, router_group_w,
           router_group_b, router_expert_w, router_expert_b, w_gate, w_up, w_down, final_norm_g):
    depth = w_in.shape[0]
    n_lat, d = x.shape[1], x.shape[2]
    n_ctx = ctx.shape[1]
    nct = n_ctx // ROW_TILE
    assert x.shape[0] == 1 and n_ctx % ROW_TILE == 0 and n_lat % 1024 == 0

    stream = (ctx[0], x[0], 0)
    cc = jnp.zeros((8, d), F32).at[0].set(c[0]).at[1].set(c_ctx)
    mods_all = _mod_call(cc, w_mod, b_mod)[:, :2].reshape(depth, 2, N_MOD, d)
    tabs = _rope_tables(n_lat, n_ctx, HEAD_DIM) + _rope_tables(n_lat, n_ctx, HEAD_DIM // 2)

    h1 = _prenorm_call(stream[0], stream[1], norm1_g[0], mods_all[0], nct)
    out = None
    for l in range(depth):
        last = l == depth - 1
        mods = mods_all[l]
        tile_off = nct if last else 0
        tm_in = 768 if (n_ctx + n_lat) % 768 == 0 else ROW_TILE
        p = _matmul_call(h1, w_in, l, tm_in, 1024)
        o = _hgrn_call(p, hg_lb_logits, l, nct)
        qa, ka, va, qd, kd, vd = _attn_prep_call(p, tabs, q_norm_g[l], k_norm_g[l])
        att = _flash_call(qa, ka, va, n_ctx, GQA_HEADS // GQA_KV_HEADS, 1, tile_off)
        lam_init = 0.8 - 0.6 * math.exp(-0.3 * l)
        dif = _flash_call(qd, kd, vd, n_ctx, 2, 2, tile_off,
                          extra=(lam_q1[l], lam_k1[l], lam_q2[l], lam_k2[l], diff_norm_g[l]),
                          lam_init=lam_init)
        rw = jnp.concatenate([router_group_w[l], router_expert_w[l],
                              jnp.zeros((d, LANES - N_GROUPS - N_EXPERTS), F32)], axis=1)
        rb = jnp.concatenate([router_group_b[l], router_expert_b[l],
                              jnp.zeros((LANES - N_GROUPS - N_EXPERTS,), F32)]).reshape(1, LANES)
        xn, h2, logits = _outproj_call(o, p, hg_norm_g[l], att, dif, w_out[l].astype(BF16), *stream,
                                       mods, norm2_g[l], rw, nct, tile_off)
        n_tok = xn.shape[0]
        route, counts = _router_call(logits, rb)
        dst, src, tile_expert, next_expert, n_tiles = _dispatch_plan(route, counts, n_tok)
        ys = _moe_call(h2, src, tile_expert, next_expert, n_tiles, w_gate, w_up, w_down, l)
        if last:
            (out,) = _combine_call(dst, xn, route, mods, final_norm_g, mods, ys, 0, True)
        else:
            xall, h1 = _combine_call(dst, xn, route, mods, norm1_g[l + 1], mods_all[l + 1], ys, nct, False)
            stream = (xall, xall, nct)
    return out.reshape(1, n_lat, d)
```

```python
import functools
import math

import numpy as np
import jax
import jax.numpy as jnp
from jax import lax
from jax.experimental import pallas as pl
from jax.experimental.pallas import tpu as pltpu

F32 = jnp.float32
BF16 = jnp.bfloat16
I32 = jnp.int32

HEAD_DIM = 128
LANES = 128
GRID_W = 64
ROPE_THETA = 10000.0
EPS = 1e-6
HG_HEADS = 4
HG_CHUNK = 64
GQA_HEADS = 8
GQA_KV_HEADS = 2
DIFF_HEADS = 4
N_GROUPS = 4
EXPERTS_PER_GROUP = 8
N_EXPERTS = N_GROUPS * EXPERTS_PER_GROUP
N_MOD = 6
ROW_TILE = 256
MOE_TILE = 256
_DMA_GROUP = 8
MIB = 1024 * 1024

_LOG2E = math.log2(math.e)
_NT = (((1,), (1,)), ((), ()))
_TN = (((0,), (0,)), ((), ()))


def _params(semantics, vmem_mib):
    return pltpu.CompilerParams(dimension_semantics=semantics, vmem_limit_bytes=vmem_mib * MIB)


def _to_token_major(ref, x):
    n, d = x.shape
    k = d // LANES
    for s in range(k):
        ref[pl.ds(s, n, stride=k), :] = x[:, s * LANES:(s + 1) * LANES]


def _from_token_major(ref, n):
    k = ref.shape[0] // n
    return jnp.concatenate([ref[pl.ds(s, n, stride=k), :] for s in range(k)], axis=1)


def _split_bf16(x, parts):
    out = []
    for _ in range(parts - 1):
        p = x.astype(BF16)
        out.append(p)
        x = x - p.astype(F32)
    out.append(x.astype(BF16))
    return out


def _mod_kernel(a_ref, w_ref, b_ref, o_ref):
    a = a_ref[...]
    a = a * (1.0 / (1.0 + jnp.exp(-a)))
    hi, lo = _split_bf16(a, 2)
    w = w_ref[0].astype(BF16)
    o_ref[0] = (jnp.dot(hi, w, preferred_element_type=F32)
                + jnp.dot(lo, w, preferred_element_type=F32) + b_ref[0])


def _mod_call(cc, w_mod, b_mod):
    depth, d, n = w_mod.shape
    tn = 1024
    return pl.pallas_call(
        _mod_kernel,
        out_shape=jax.ShapeDtypeStruct((depth, 8, n), F32),
        grid=(depth, n // tn),
        in_specs=[pl.BlockSpec((8, d), lambda l, j: (0, 0)),
                  pl.BlockSpec((1, d, tn), lambda l, j: (l, 0, j)),
                  pl.BlockSpec((1, 1, tn), lambda l, j: (l, 0, j))],
        out_specs=pl.BlockSpec((1, 8, tn), lambda l, j: (l, 0, j)),
        compiler_params=_params(("arbitrary", "arbitrary"), 40),
        name="mod_vectors",
    )(cc, w_mod, b_mod.reshape(depth, 1, n))


def _norm_mod(x, g, shift, scale):
    y = x * lax.rsqrt(jnp.mean(x * x, axis=-1, keepdims=True) + EPS) * g
    return y * (1.0 + scale) + shift


def _stream_specs(d, nct, tile_off=0, lat_off=0):
    return [pl.BlockSpec((ROW_TILE, d), lambda i: (jnp.minimum(i + tile_off, nct - 1), 0)),
            pl.BlockSpec((ROW_TILE, d), lambda i: (jnp.maximum(i + tile_off - nct, 0) + lat_off, 0))]


def _stream_tile(c_ref, x_ref, nct, tile_off=0):
    return jnp.where(pl.program_id(0) + tile_off < nct, c_ref[...], x_ref[...])


def _prenorm_kernel(c_ref, x_ref, g_ref, m_ref, o_ref, *, nct):
    m = m_ref[0]
    o_ref[...] = _norm_mod(_stream_tile(c_ref, x_ref, nct), g_ref[...], m[0:1], m[1:2]).astype(BF16)


def _mod_spec(d, nct, tile_off=0):
    return pl.BlockSpec((1, N_MOD, d), lambda i: (jnp.where(i + tile_off < nct, 1, 0), 0, 0))


def _prenorm_call(xc, xl, g, mods, nct):
    d = xl.shape[1]
    t = xc.shape[0] + xl.shape[0]
    return pl.pallas_call(
        functools.partial(_prenorm_kernel, nct=nct),
        out_shape=jax.ShapeDtypeStruct((t, d), BF16),
        grid=(t // ROW_TILE,),
        in_specs=_stream_specs(d, nct) + [pl.BlockSpec((1, d), lambda i: (0, 0)), _mod_spec(d, nct)],
        out_specs=pl.BlockSpec((ROW_TILE, d), lambda i: (i, 0)),
        compiler_params=_params(("arbitrary",), 24),
        name="prenorm",
    )(xc, xl, g.reshape(1, d), mods)


def _mm_kernel(a_ref, b_ref, o_ref, b_bf):
    @pl.when(pl.program_id(1) == 0)
    def _():
        b_bf[...] = b_ref[0].astype(BF16)

    o_ref[...] = jnp.dot(a_ref[...], b_bf[...], preferred_element_type=F32)


def _matmul_call(a, b, layer, tm, tn):
    m, k = a.shape
    n = b.shape[2]
    return pl.pallas_call(
        _mm_kernel,
        out_shape=jax.ShapeDtypeStruct((m, n), F32),
        grid=(n // tn, m // tm),
        in_specs=[pl.BlockSpec((tm, k), lambda j, i: (i, 0)),
                  pl.BlockSpec((1, k, tn), lambda j, i: (layer, 0, j))],
        out_specs=pl.BlockSpec((tm, tn), lambda j, i: (i, j)),
        scratch_shapes=[pltpu.VMEM((k, tn), BF16)],
        compiler_params=_params(("arbitrary", "arbitrary"), 48),
        name="in_proj",
    )(a, b)


_HG_LEVELS = (1, 2, 4, 8, 16, 32)
_HG_TOT_ROW = HG_CHUNK * (len(_HG_LEVELS) + 1)
_HG_W_ROWS = _HG_TOT_ROW + 16
_HG_HEADS_PER_STEP = 4


def _hgrn_consts():
    c = HG_CHUNK
    w = np.zeros((2, _HG_W_ROWS, c), np.float32)
    msk = np.zeros((2, len(_HG_LEVELS) + 1, c, c), np.float32)
    for d in range(2):
        u = np.arange(c) if d == 0 else c - 1 - np.arange(c)
        ut, us = u[:, None], u[None, :]
        w[d, :c] = us <= ut
        for li, lv in enumerate(_HG_LEVELS):
            blk = u // (2 * lv)
            qside = (u % (2 * lv)) >= lv
            bnd = (blk * 2 * lv + lv - 1)[:, None]
            wq = (us > bnd) & (us <= ut)
            wk = (us > ut) & (us <= bnd)
            w[d, c * (li + 1):c * (li + 2)] = np.where(qside[:, None], wq, -1.0 * wk)
            msk[d, li] = (blk[:, None] == blk[None, :]) & qside[:, None] & ~qside[None, :]
        msk[d, len(_HG_LEVELS)] = np.eye(c)
        w[d, _HG_TOT_ROW:] = 1.0
    return jnp.asarray(np.concatenate([w, w, w], axis=2), BF16), jnp.asarray(msk, F32)


def _hgrn_kernel(lbl_ref, q_ref, z_ref, v_ref, w_ref, msk_ref, o_ref, st_ref, *, layer, chunks, hps):
    c = HG_CHUNK
    hd = HEAD_DIM
    d = pl.program_id(0)
    j = pl.program_id(2)

    @pl.when(j == 0)
    def _():
        st_ref[...] = jnp.zeros_like(st_ref)

    lbl = lbl_ref[...]
    rows = [lbl[i:i + 1] for i in range(lbl.shape[0])]
    mx = functools.reduce(jnp.maximum, rows)
    ex = [jnp.exp(r - mx) for r in rows]
    tot = functools.reduce(lambda a, b: a + b, ex)
    lb = jnp.zeros_like(mx)
    for i in range(1, layer + 1):
        lb = lb + ex[i] / tot
    log_lb = jnp.log(lb)
    log_1m_lb = jnp.log1p(-lb)

    wmat = w_ref[0]
    nlev = len(_HG_LEVELS)

    def finish(h, hs, r0, amat, v_bf, qe, kd, decay):
        st = st_ref[h]
        o = (jnp.dot(amat, v_bf[:, hs], preferred_element_type=F32)
             + lax.dot_general(qe[:, hs], st.astype(BF16), _NT, preferred_element_type=F32))
        o_ref[0, pl.ds(r0, c), hs] = o
        st_ref[h] = st * decay[:, hs] + lax.dot_general(v_bf[:, hs], kd[:, hs], _TN,
                                                        preferred_element_type=F32)

    pending = None
    for ci in range(chunks):
        cc = ci + d * (chunks - 1 - 2 * ci)
        r0 = pl.multiple_of(cc * c, c)
        q = q_ref[pl.ds(r0, c), :]
        z = z_ref[pl.ds(r0, c), :]
        v_bf = v_ref[pl.ds(r0, c), :].astype(BF16)
        l1p = jnp.log(1.0 + jnp.exp(-jnp.abs(z)))
        ls_pos = jnp.minimum(z, 0.0) - l1p
        ls_neg = jnp.minimum(-z, 0.0) - l1p
        a2 = log_1m_lb + ls_pos
        logf = jnp.maximum(log_lb, a2) + jnp.log(1.0 + jnp.exp(-jnp.abs(log_lb - a2)))
        k = (1.0 - lb) * jnp.exp(ls_neg)

        parts = jnp.concatenate(_split_bf16(logf, 3), axis=0)
        sums = jnp.dot(wmat, parts, preferred_element_type=F32)
        b = sums[0:c]
        btot = sums[_HG_TOT_ROW:_HG_TOT_ROW + 1]

        qls, kls = [q.astype(BF16)], [k.astype(BF16)]
        for li in range(nlev):
            fl = jnp.exp(-jnp.abs(sums[c * (li + 1):c * (li + 2)]))
            qls.append((q * fl).astype(BF16))
            kls.append((k * fl).astype(BF16))
        qe = (q * jnp.exp(b)).astype(BF16)
        kd = (k * jnp.exp(btot - b)).astype(BF16)
        decay = jnp.exp(btot)

        for h in range(hps):
            hs = slice(h * hd, (h + 1) * hd)
            amat = lax.dot_general(qls[0][:, hs], kls[0][:, hs], _NT,
                                   preferred_element_type=F32) * msk_ref[0, nlev]
            for li in range(nlev):
                amat = amat + lax.dot_general(qls[li + 1][:, hs], kls[li + 1][:, hs], _NT,
                                              preferred_element_type=F32) * msk_ref[0, li]
            if pending is not None:
                finish(*pending)
            pending = (h, hs, r0, amat.astype(BF16), v_bf, qe, kd, decay)
    finish(*pending)


def _hgrn_call(p, lb_logits, layer, nct):
    t = p.shape[0]
    nblk = t // ROW_TILE
    chunks = ROW_TILE // HG_CHUNK
    hps = _HG_HEADS_PER_STEP
    hw = hps * HEAD_DIM
    ng = HG_HEADS // hps
    wmat, msk = _hgrn_consts()

    def blk(d, j):
        back = jnp.where(j < nct, nct - 1 - j, nblk - 1 - (j - nct))
        return jnp.where(d == 0, j, back)

    return pl.pallas_call(
        functools.partial(_hgrn_kernel, layer=layer, chunks=chunks, hps=hps),
        out_shape=jax.ShapeDtypeStruct((2, t, HG_HEADS * HEAD_DIM), F32),
        grid=(2, ng, nblk),
        in_specs=[pl.BlockSpec((lb_logits.shape[0], hw), lambda d, h, j: (0, h)),
                  pl.BlockSpec((ROW_TILE, hw), lambda d, h, j: (blk(d, j), d * ng + h)),
                  pl.BlockSpec((ROW_TILE, hw), lambda d, h, j: (blk(d, j), (2 + d) * ng + h)),
                  pl.BlockSpec((ROW_TILE, hw), lambda d, h, j: (blk(d, j), 4 * ng + h)),
                  pl.BlockSpec((1, _HG_W_ROWS, 3 * HG_CHUNK), lambda d, h, j: (d, 0, 0)),
                  pl.BlockSpec((1, len(_HG_LEVELS) + 1, HG_CHUNK, HG_CHUNK), lambda d, h, j: (d, 0, 0, 0))],
        out_specs=pl.BlockSpec((1, ROW_TILE, hw), lambda d, h, j: (d, blk(d, j), h)),
        scratch_shapes=[pltpu.VMEM((hps, HEAD_DIM, HEAD_DIM), F32)],
        compiler_params=_params(("arbitrary", "arbitrary", "arbitrary"), 24),
        name="hgrn_scan",
    )(lb_logits, p, p, p, wmat, msk)


def _rope_tables(n_lat, n_ctx, dim):
    rows = n_lat // GRID_W
    row = jnp.repeat(jnp.arange(rows, dtype=F32), GRID_W)
    col = jnp.tile(jnp.arange(GRID_W, dtype=F32), rows)
    axis_dim = dim // 2
    inv_freq = ROPE_THETA ** (-jnp.arange(0, axis_dim, 2, dtype=F32) / axis_dim)
    ang = jnp.concatenate([row[:, None] * inv_freq, col[:, None] * inv_freq], axis=-1)
    cos = jnp.repeat(jnp.cos(ang), 2, axis=1)
    sin = jnp.repeat(jnp.sin(ang), 2, axis=1) * jnp.tile(jnp.asarray([-1.0, 1.0], F32), dim // 2)
    reps = LANES // dim
    cos = jnp.concatenate([jnp.ones((n_ctx, dim), F32), cos], axis=0)
    sin = jnp.concatenate([jnp.zeros((n_ctx, dim), F32), sin], axis=0)
    return jnp.tile(cos, (1, reps)), jnp.tile(sin, (1, reps))


def _rope(x, cos, sin):
    lane = lax.broadcasted_iota(I32, x.shape, 1)
    swapped = jnp.where((lane & 1) == 0, pltpu.roll(x, LANES - 1, 1), pltpu.roll(x, 1, 1))
    return x * cos + swapped * sin


def _head_norm(x, g):
    return x * lax.rsqrt(jnp.mean(x * x, axis=-1, keepdims=True) + EPS) * g


def _attn_prep_kernel(pq_ref, pkv_ref, pd_ref, ca_ref, sa_ref, cd_ref, sd_ref, qn_ref, kn_ref,
                      qa_ref, ka_ref, va_ref, qd_ref, kd_ref, vd_ref):
    hd = HEAD_DIM
    ca, sa, cd, sd = ca_ref[...], sa_ref[...], cd_ref[...], sd_ref[...]
    qn, kn = qn_ref[...], kn_ref[...]
    for h in range(GQA_HEADS):
        xq = _rope(_head_norm(pq_ref[:, h * hd:(h + 1) * hd], qn), ca, sa)
        qa_ref[:, h * hd:(h + 1) * hd] = (xq * (hd ** -0.5 * _LOG2E)).astype(BF16)
    for h in range(GQA_KV_HEADS):
        xk = _rope(_head_norm(pkv_ref[:, h * hd:(h + 1) * hd], kn), ca, sa)
        ka_ref[:, h * hd:(h + 1) * hd] = xk.astype(BF16)
    ones_col = jnp.where(lax.broadcasted_iota(I32, (pq_ref.shape[0], hd), 1) == 0, 1.0, 0.0).astype(BF16)
    for h in range(GQA_KV_HEADS):
        va_ref[:, (2 * h) * hd:(2 * h + 1) * hd] = pkv_ref[:, (GQA_KV_HEADS + h) * hd:
                                                            (GQA_KV_HEADS + h + 1) * hd].astype(BF16)
        va_ref[:, (2 * h + 1) * hd:(2 * h + 2) * hd] = ones_col
    first = lax.broadcasted_iota(I32, (pq_ref.shape[0], hd), 1) < hd // 2
    dw = DIFF_HEADS * hd
    for h in range(DIFF_HEADS):
        xq = _rope(pd_ref[:, h * hd:(h + 1) * hd], cd, sd) * ((hd // 2) ** -0.5 * _LOG2E)
        qd_ref[:, (2 * h) * hd:(2 * h + 1) * hd] = jnp.where(first, xq, 0.0).astype(BF16)
        qd_ref[:, (2 * h + 1) * hd:(2 * h + 2) * hd] = jnp.where(first, 0.0, xq).astype(BF16)
        xk = _rope(pd_ref[:, dw + h * hd:dw + (h + 1) * hd], cd, sd)
        kd_ref[:, h * hd:(h + 1) * hd] = xk.astype(BF16)
    for h in range(DIFF_HEADS):
        vd_ref[:, (2 * h) * hd:(2 * h + 1) * hd] = pd_ref[:, 2 * dw + h * hd:2 * dw + (h + 1) * hd].astype(BF16)
        vd_ref[:, (2 * h + 1) * hd:(2 * h + 2) * hd] = ones_col


def _attn_prep_call(p, tabs, qn, kn):
    t = p.shape[0]
    hd = HEAD_DIM
    gq, gkv, dw = GQA_HEADS * hd, 2 * GQA_KV_HEADS * hd, DIFF_HEADS * hd
    q_off = 6 * HG_HEADS * hd
    assert q_off % gq == 0 and (q_off + gq) % gkv == 0 and (q_off + gq + gkv) % (3 * dw) == 0
    row = lambda w: pl.BlockSpec((ROW_TILE, w), lambda i: (i, 0))
    vec = pl.BlockSpec((1, hd), lambda i: (0, 0))
    return pl.pallas_call(
        _attn_prep_kernel,
        out_shape=[jax.ShapeDtypeStruct((t, gq), BF16),
                   jax.ShapeDtypeStruct((t, gkv // 2), BF16),
                   jax.ShapeDtypeStruct((t, gkv), BF16),
                   jax.ShapeDtypeStruct((t, 2 * dw), BF16),
                   jax.ShapeDtypeStruct((t, dw), BF16),
                   jax.ShapeDtypeStruct((t, 2 * dw), BF16)],
        grid=(t // ROW_TILE,),
        in_specs=[pl.BlockSpec((ROW_TILE, gq), lambda i: (i, q_off // gq)),
                  pl.BlockSpec((ROW_TILE, gkv), lambda i: (i, (q_off + gq) // gkv)),
                  pl.BlockSpec((ROW_TILE, 3 * dw), lambda i: (i, (q_off + gq + gkv) // (3 * dw))),
                  row(hd), row(hd), row(hd), row(hd), vec, vec],
        out_specs=[row(gq), row(gkv // 2), row(gkv), row(2 * dw), row(dw), row(2 * dw)],
        compiler_params=_params(("arbitrary",), 24),
        name="attn_prep",
    )(p, p, p, *tabs, qn.reshape(1, hd), kn.reshape(1, hd))


def _flash_kernel(*refs, g, nh, n_ctx, n_lat, tkl, nct, tile_off, diff, lam_init):
    if diff:
        q_ref, k_ref, v_ref, lq1, lk1, lq2, lk2, dn_ref, o_ref = refs[:9]
    else:
        q_ref, k_ref, v_ref, o_ref = refs[:4]
    nc = nh * g
    m_sc, acc_sc, p_sc = (refs[len(refs) - (3 - n) * nc:len(refs) - (2 - n) * nc] for n in range(3))
    hd = HEAD_DIM
    i = pl.program_id(1) + tile_off
    nblk = n_lat // tkl
    chains = [(c, c // g) for c in range(nh * g)]

    def head(ref, rows, h, width=1):
        return ref[rows, h * width * hd:(h + 1) * width * hd]

    def vhead(rows, h):
        return head(v_ref, rows, h, 2)

    def normalised(c):
        acc = acc_sc[c][...]
        return acc[:, :hd] / acc[:, hd:hd + 1]

    ctx_rows = slice(0, n_ctx)
    for c, kh in chains:
        s = lax.dot_general(head(q_ref, slice(None), c), head(k_ref, ctx_rows, kh), _NT,
                            preferred_element_type=F32)
        m0 = jnp.max(s, axis=1, keepdims=True)
        p = jnp.exp2(s - m0)
        m_sc[c][...] = jnp.broadcast_to(m0, m_sc[c].shape)
        acc_sc[c][...] = jnp.dot(p.astype(BF16), vhead(ctx_rows, kh), preferred_element_type=F32)

    def probs(c, kh, rows):
        s = lax.dot_general(head(q_ref, slice(None), c), head(k_ref, rows, kh), _NT,
                            preferred_element_type=F32)
        m_prev = m_sc[c][...]
        m_new = jnp.maximum(m_prev, jnp.max(s, axis=1, keepdims=True))
        p = jnp.exp2(s - jnp.tile(m_new, (1, tkl // LANES)))
        alpha = jnp.exp2(m_prev - m_new)
        m_sc[c][...] = m_new
        return p.astype(BF16), jnp.tile(alpha, (1, 2))

    @pl.when(i >= nct)
    def _():
        for c, kh in chains:
            p, alpha = probs(c, kh, pl.ds(n_ctx, tkl))
            p_sc[c][0] = p
            acc_sc[c][...] = alpha * acc_sc[c][...]

        for jb in range(1, nblk):
            r0 = n_ctx + jb * tkl
            prev = pl.ds(r0 - tkl, tkl)
            slot = jb % 2
            alphas = []
            for c, kh in chains:
                p, alpha = probs(c, kh, pl.ds(r0, tkl))
                p_sc[c][slot] = p
                alphas.append(alpha)
            for c, kh in chains:
                pv = jnp.dot(p_sc[c][1 - slot], vhead(prev, kh), preferred_element_type=F32)
                acc_sc[c][...] = alphas[c] * (acc_sc[c][...] + pv)

        last = pl.ds(n_ctx + (nblk - 1) * tkl, tkl)
        for c, kh in chains:
            acc_sc[c][...] = acc_sc[c][...] + jnp.dot(p_sc[c][(nblk - 1) % 2], vhead(last, kh),
                                                      preferred_element_type=F32)

    if diff:
        lam = (jnp.exp(jnp.sum(lq1[...] * lk1[...], axis=1, keepdims=True))
               - jnp.exp(jnp.sum(lq2[...] * lk2[...], axis=1, keepdims=True)) + lam_init)
        for kh in range(nh):
            dd = normalised(2 * kh) - lam * normalised(2 * kh + 1)
            o_ref[:, kh * hd:(kh + 1) * hd] = (_head_norm(dd, dn_ref[...]) * (1.0 - lam_init)).astype(BF16)
    else:
        for c, _ in chains:
            o_ref[:, c * hd:(c + 1) * hd] = normalised(c).astype(BF16)


def _flash_call(q, k, v, n_ctx, g, nh, tile_off, extra=None, lam_init=0.0):
    t = k.shape[0]
    hd = HEAD_DIM
    n_kv = k.shape[1] // (nh * hd)
    n_lat = t - n_ctx
    nct = n_ctx // ROW_TILE
    tkl = 2048
    diff = extra is not None
    ow = nh * hd if diff else nh * g * hd
    in_specs = [pl.BlockSpec((ROW_TILE, nh * g * hd), lambda kv, i: (i + tile_off, kv)),
                pl.BlockSpec((t, nh * hd), lambda kv, i: (0, kv)),
                pl.BlockSpec((t, 2 * nh * hd), lambda kv, i: (0, kv))]
    args = [q, k, v]
    if diff:
        for a in extra:
            a = a.reshape(1, -1)
            in_specs.append(pl.BlockSpec(a.shape, lambda kv, i: (0, 0)))
            args.append(a)
    return pl.pallas_call(
        functools.partial(_flash_kernel, g=g, nh=nh, n_ctx=n_ctx, n_lat=n_lat, tkl=tkl, nct=nct,
                          tile_off=tile_off, diff=diff, lam_init=lam_init),
        out_shape=jax.ShapeDtypeStruct((t - tile_off * ROW_TILE, n_kv * ow), BF16),
        grid=(n_kv, t // ROW_TILE - tile_off),
        in_specs=in_specs,
        out_specs=pl.BlockSpec((ROW_TILE, ow), lambda kv, i: (i, kv)),
        scratch_shapes=([pltpu.VMEM((ROW_TILE, LANES), F32)] * (nh * g)
                        + [pltpu.VMEM((ROW_TILE, 2 * hd), F32)] * (nh * g)
                        + [pltpu.VMEM((2, ROW_TILE, tkl), BF16)] * (nh * g)),
        compiler_params=_params(("arbitrary", "arbitrary"), 48),
        name="diff_attention" if diff else "gqa_attention",
    )(*args)


def _outproj_kernel(of_ref, ob_ref, gate_ref, ng_ref, att_ref, dif_ref, wo_ref, xc_ref, xl_ref, m_ref,
                    n2_ref, rw_ref, xo_ref, h2_ref, lg_ref, *, nct, tile_off):
    hd = HEAD_DIM
    hgw = HG_HEADS * hd
    ng = ng_ref[...]
    gate = gate_ref[...]
    silu_gate = gate * (1.0 / (1.0 + jnp.exp(-gate)))
    acc = jnp.dot(att_ref[...], wo_ref[hgw:hgw + att_ref.shape[1], :], preferred_element_type=F32)
    acc = acc + jnp.dot(dif_ref[...], wo_ref[hgw + att_ref.shape[1]:, :], preferred_element_type=F32)
    hg = []
    for h in range(HG_HEADS):
        o = of_ref[0, :, h * hd:(h + 1) * hd] + ob_ref[0, :, h * hd:(h + 1) * hd]
        hg.append((_head_norm(o, ng) * silu_gate[:, h * hd:(h + 1) * hd]).astype(BF16))
    acc = acc + jnp.dot(jnp.concatenate(hg, axis=1), wo_ref[0:hgw, :], preferred_element_type=F32)
    m = m_ref[0]
    xn = _stream_tile(xc_ref, xl_ref, nct, tile_off) + m[2:3] * acc
    xo_ref[...] = xn
    h2 = _norm_mod(xn, n2_ref[...], m[3:4], m[4:5])
    _to_token_major(h2_ref, h2)
    hs = _split_bf16(h2, 2)
    ws = _split_bf16(rw_ref[...], 2)
    lg = jnp.dot(hs[1], ws[0], preferred_element_type=F32)
    lg = lg + jnp.dot(hs[0], ws[1], preferred_element_type=F32)
    lg_ref[...] = lg + jnp.dot(hs[0], ws[0], preferred_element_type=F32)


def _outproj_call(o, p, ng, att, dif, wo, xc, xl, lat_off, mods, n2, rw, nct, tile_off):
    d = xl.shape[1]
    hd = HEAD_DIM
    hgw = HG_HEADS * hd
    n_rows = att.shape[0]
    row = lambda w: pl.BlockSpec((ROW_TILE, w), lambda i: (i, 0))
    full = lambda a: pl.BlockSpec(a.shape, lambda i: (0,) * a.ndim)
    ng, n2 = ng.reshape(1, hd), n2.reshape(1, d)
    return pl.pallas_call(
        functools.partial(_outproj_kernel, nct=nct, tile_off=tile_off),
        out_shape=[jax.ShapeDtypeStruct((n_rows, d), F32),
                   jax.ShapeDtypeStruct((n_rows * (d // LANES), LANES), F32),
                   jax.ShapeDtypeStruct((n_rows, LANES), F32)],
        grid=(n_rows // ROW_TILE,),
        in_specs=[pl.BlockSpec((1, ROW_TILE, hgw), lambda i: (0, i + tile_off, 0)),
                  pl.BlockSpec((1, ROW_TILE, hgw), lambda i: (1, i + tile_off, 0)),
                  pl.BlockSpec((ROW_TILE, hgw), lambda i: (i + tile_off, 5)),
                  full(ng), row(att.shape[1]), row(dif.shape[1]), full(wo)]
                 + _stream_specs(d, nct, tile_off, lat_off)
                 + [_mod_spec(d, nct, tile_off), full(n2), full(rw)],
        out_specs=[row(d), pl.BlockSpec((ROW_TILE * (d // LANES), LANES), lambda i: (i, 0)), row(LANES)],
        compiler_params=_params(("arbitrary",), 56),
        name="out_proj",
    )(o, o, p, ng, att, dif, wo, xc, xl, mods, n2, rw)


def _router_kernel(lg_ref, bias_ref, r_ref, cnt_ref, run_ref):
    i = pl.program_id(0)
    tm = lg_ref.shape[0]

    @pl.when(i == 0)
    def _():
        run_ref[...] = jnp.zeros_like(run_ref)

    lane = lax.broadcasted_iota(I32, (tm, LANES), 1).astype(F32)
    lg = lg_ref[...] + bias_ref[...]
    ninf = -jnp.inf

    def first_max(vals):
        mx = jnp.max(vals, axis=1, keepdims=True)
        idx = jnp.min(jnp.where(vals == mx, lane, float(LANES)), axis=1, keepdims=True)
        return mx, idx

    gl = jnp.where(lane < N_GROUPS, lg, ninf)
    gmax, gidx = first_max(gl)
    g_top = 1.0 / jnp.sum(jnp.exp(gl - gmax), axis=1, keepdims=True)
    lo = N_GROUPS + EXPERTS_PER_GROUP * gidx
    el = jnp.where((lane >= lo) & (lane < lo + EXPERTS_PER_GROUP), lg, ninf)
    m1, e1 = first_max(el)
    m2, e2 = first_max(jnp.where(lane == e1, ninf, el))
    r = jnp.exp(m2 - m1)
    w1 = g_top / (1.0 + r)
    w2 = g_top * r / (1.0 + r)

    hit = ((lane == e1) | (lane == e2)).astype(BF16)
    ti = lax.broadcasted_iota(I32, (tm, tm), 0)
    si = lax.broadcasted_iota(I32, (tm, tm), 1)
    before = (si < ti).astype(BF16)
    pos = jnp.dot(before, hit, preferred_element_type=F32) + run_ref[0:1, :]
    p1 = jnp.sum(jnp.where(lane == e1, pos, 0.0), axis=1, keepdims=True)
    p2 = jnp.sum(jnp.where(lane == e2, pos, 0.0), axis=1, keepdims=True)
    total = run_ref[0:1, :] + jnp.sum(hit.astype(F32), axis=0, keepdims=True)
    run_ref[...] = jnp.broadcast_to(total, run_ref.shape)
    cnt_ref[...] = jnp.broadcast_to(total, cnt_ref.shape)

    fields = (e1 - N_GROUPS, e2 - N_GROUPS, w1, w2, p1, p2)
    out = jnp.zeros((tm, LANES), F32)
    for n, f in enumerate(fields):
        out = jnp.where(lane == n, f, out)
    r_ref[...] = out


def _router_call(logits, bias):
    n = logits.shape[0]
    return pl.pallas_call(
        _router_kernel,
        out_shape=[jax.ShapeDtypeStruct((n, LANES), F32), jax.ShapeDtypeStruct((8, LANES), F32)],
        grid=(n // ROW_TILE,),
        in_specs=[pl.BlockSpec((ROW_TILE, LANES), lambda i: (i, 0)),
                  pl.BlockSpec((1, LANES), lambda i: (0, 0))],
        out_specs=[pl.BlockSpec((ROW_TILE, LANES), lambda i: (i, 0)),
                   pl.BlockSpec((8, LANES), lambda i: (0, 0))],
        scratch_shapes=[pltpu.VMEM((8, LANES), F32)],
        compiler_params=_params(("arbitrary",), 16),
        name="router",
    )(logits, bias)


def _moe_kernel(src_ref, te_ref, nx_ref, ng_ref, nt_ref, h_hbm, wg_hbm, wu_hbm, wd_hbm, y_ref,
                xbuf, sem, wg_st, wu_st, wd_st, wsem, wg_bf, wu_bf, wd_bf, *, layer):
    i = pl.program_id(0)
    n_tiles = nt_ref[0]
    tm = MOE_TILE
    k = xbuf.shape[1] // tm
    grp_rows = _DMA_GROUP * k

    def weight_copies(e):
        pairs = ((wg_hbm, wg_st), (wu_hbm, wu_st), (wd_hbm, wd_st))
        return [pltpu.make_async_copy(w.at[layer, e], st, wsem.at[n]) for n, (w, st) in enumerate(pairs)]

    def start_tile(tile, slot):
        def body(grp, carry):
            for u in range(_DMA_GROUP):
                r = grp * _DMA_GROUP + u
                src_row = pl.multiple_of(src_ref[tile * tm + r] * k, k)
                pltpu.make_async_copy(h_hbm.at[pl.ds(src_row, k), :],
                                      xbuf.at[slot, pl.ds(pl.multiple_of(r * k, k), k), :],
                                      sem.at[slot]).start(priority=1)
            return carry
        lax.fori_loop(0, ng_ref[tile], body, 0)

    def wait_tile(tile, slot):
        def body(grp, carry):
            pltpu.make_async_copy(h_hbm.at[pl.ds(0, grp_rows), :], xbuf.at[slot, pl.ds(0, grp_rows), :],
                                  sem.at[slot]).wait()
            return carry
        lax.fori_loop(0, ng_ref[tile], body, 0)

    @pl.when(i == 0)
    def _():
        xbuf[...] = jnp.zeros_like(xbuf)
        start_tile(0, 0)
        for cp in weight_copies(te_ref[0]):
            cp.start()

    @pl.when(i < n_tiles)
    def _():
        slot = i % 2

        @pl.when(i + 1 < n_tiles)
        def _():
            start_tile(i + 1, 1 - slot)

        changed = jnp.logical_or(i == 0, te_ref[i] != te_ref[jnp.maximum(i - 1, 0)])

        @pl.when(changed)
        def _():
            for cp in weight_copies(te_ref[i]):
                cp.wait()
            wg_bf[...] = wg_st[...].astype(BF16)
            wu_bf[...] = wu_st[...].astype(BF16)
            wd_bf[...] = wd_st[...].astype(BF16)

            @pl.when(nx_ref[i] >= 0)
            def _():
                for cp in weight_copies(nx_ref[i]):
                    cp.start()

        wait_tile(i, slot)
        xb = _from_token_major(xbuf.at[slot], tm).astype(BF16)
        gt = jnp.dot(xb, wg_bf[...], preferred_element_type=F32)
        up = jnp.dot(xb, wu_bf[...], preferred_element_type=F32)
        act = (gt * (1.0 / (1.0 + jnp.exp(-gt))) * up).astype(BF16)
        _to_token_major(y_ref, jnp.dot(act, wd_bf[...], preferred_element_type=F32))

    @pl.when(i >= n_tiles)
    def _():
        y_ref[...] = jnp.zeros_like(y_ref)


def _moe_call(h2, src, tile_expert, next_expert, tile_groups, n_tiles, wg, wu, wd, layer):
    d, ff = wg.shape[2], wg.shape[3]
    k = d // LANES
    max_tiles = tile_expert.shape[0]
    tm = MOE_TILE
    hbm = pl.BlockSpec(memory_space=pl.ANY)
    grid_spec = pltpu.PrefetchScalarGridSpec(
        num_scalar_prefetch=5,
        grid=(max_tiles,),
        in_specs=[hbm, hbm, hbm, hbm],
        out_specs=pl.BlockSpec((tm * k, LANES), lambda i, src, te, nx, ng, nt: (i, 0)),
        scratch_shapes=[pltpu.VMEM((2, tm * k, LANES), F32), pltpu.SemaphoreType.DMA((2,)),
                        pltpu.VMEM((d, ff), F32), pltpu.VMEM((d, ff), F32), pltpu.VMEM((ff, d), F32),
                        pltpu.SemaphoreType.DMA((3,)),
                        pltpu.VMEM((d, ff), BF16), pltpu.VMEM((d, ff), BF16), pltpu.VMEM((ff, d), BF16)])
    return pl.pallas_call(
        functools.partial(_moe_kernel, layer=layer),
        out_shape=jax.ShapeDtypeStruct((max_tiles * tm * k, LANES), F32),
        grid_spec=grid_spec,
        compiler_params=_params(("arbitrary",), 40),
        name="moe_experts",
    )(src, tile_expert, next_expert, tile_groups, n_tiles, h2, wg, wu, wd)


def _combine_kernel(dst_ref, x_ref, rt_ref, m_ref, g_ref, mn_ref, y_hbm, *rest, last, n_out):
    outs, (ybuf, sem) = rest[:n_out], rest[n_out:]
    i = pl.program_id(0)
    tm = x_ref.shape[0]
    k = ybuf.shape[2] // tm

    def start_tile(tile, slot):
        def body(grp, carry):
            for u in range(_DMA_GROUP):
                r = grp * _DMA_GROUP + u
                for j in range(2):
                    src_row = pl.multiple_of(dst_ref[(tile * tm + r) * 2 + j] * k, k)
                    pltpu.make_async_copy(y_hbm.at[pl.ds(src_row, k), :],
                                          ybuf.at[slot, j, pl.ds(pl.multiple_of(r * k, k), k), :],
                                          sem.at[slot]).start(priority=j)
            return carry
        lax.fori_loop(0, tm // _DMA_GROUP, body, 0)

    @pl.when(i == 0)
    def _():
        start_tile(0, 0)

    slot = i % 2

    @pl.when(i + 1 < pl.num_programs(0))
    def _():
        start_tile(i + 1, 1 - slot)

    for j in range(2):
        pltpu.make_async_copy(y_hbm.at[pl.ds(0, tm * k), :], ybuf.at[slot, j], sem.at[slot]).wait()
    rt = rt_ref[...]
    y = (rt[:, 2:3] * _from_token_major(ybuf.at[slot, 0], tm)
         + rt[:, 3:4] * _from_token_major(ybuf.at[slot, 1], tm))
    xn = x_ref[...] + m_ref[0][5:6] * y
    if last:
        outs[0][...] = xn * lax.rsqrt(jnp.mean(xn * xn, axis=-1, keepdims=True) + EPS) * g_ref[...]
    else:
        outs[0][...] = xn
        mn = mn_ref[0]
        outs[1][...] = _norm_mod(xn, g_ref[...], mn[0:1], mn[1:2]).astype(BF16)


def _combine_call(dst, x, route, mods, g, mods_next, ys, nct, last):
    n, d = x.shape
    tm = ROW_TILE
    mspec = pl.BlockSpec((1, N_MOD, d), lambda i, dst: (jnp.where(i < nct, 1, 0), 0, 0))
    out_shape = [jax.ShapeDtypeStruct((n, d), F32)]
    if not last:
        out_shape.append(jax.ShapeDtypeStruct((n, d), BF16))
    row = lambda w: pl.BlockSpec((tm, w), lambda i, dst: (i, 0))
    grid_spec = pltpu.PrefetchScalarGridSpec(
        num_scalar_prefetch=1,
        grid=(n // tm,),
        in_specs=[row(d), row(LANES), mspec, pl.BlockSpec((1, d), lambda i, dst: (0, 0)), mspec,
                  pl.BlockSpec(memory_space=pl.ANY)],
        out_specs=[row(d)] * len(out_shape),
        scratch_shapes=[pltpu.VMEM((2, 2, tm * (d // LANES), LANES), F32), pltpu.SemaphoreType.DMA((2,))])
    return pl.pallas_call(
        functools.partial(_combine_kernel, last=last, n_out=len(out_shape)),
        out_shape=out_shape,
        grid_spec=grid_spec,
        compiler_params=_params(("arbitrary",), 40),
        name="moe_combine",
    )(dst, x, route, mods, g.reshape(1, d), mods_next, ys)


def _dispatch_plan(route, counts, n_tokens):
    tm = MOE_TILE
    max_tiles = (2 * n_tokens) // tm + N_EXPERTS
    cnt = counts[0, N_GROUPS:N_GROUPS + N_EXPERTS].astype(I32)
    tiles_per = (cnt + tm - 1) // tm
    tile_end = jnp.cumsum(tiles_per)
    offs = (tile_end - tiles_per) * tm
    eid = route[:, 0:2].astype(I32)
    e_ids = jnp.arange(N_EXPERTS, dtype=I32)
    offs_of = jnp.sum(jnp.where(eid[..., None] == e_ids, offs, 0), axis=-1)
    dst = offs_of + route[:, 4:6].astype(I32)
    tok = jnp.broadcast_to(jnp.arange(n_tokens, dtype=I32)[:, None], (n_tokens, 2))
    src = jnp.zeros((max_tiles * tm,), I32).at[dst.reshape(-1)].set(tok.reshape(-1))
    n_tiles = tile_end[-1:]
    tile_ids = jnp.minimum(jnp.arange(max_tiles, dtype=I32), n_tiles[0] - 1)
    tile_expert = jnp.sum((tile_end[None, :] <= tile_ids[:, None]).astype(I32), axis=1)
    later = (e_ids[None, :] > e_ids[:, None]) & (tiles_per[None, :] > 0)
    nxt = jnp.min(jnp.where(later, e_ids[None, :], N_EXPERTS), axis=1)
    next_expert = jnp.where(nxt < N_EXPERTS, nxt, -1)[tile_expert]
    within = tile_ids - (tile_end - tiles_per)[tile_expert]
    rows = jnp.clip(cnt[tile_expert] - within * tm, 0, tm)
    tile_groups = (rows + _DMA_GROUP - 1) // _DMA_GROUP
    return dst.reshape(-1), src, tile_expert, next_expert, tile_groups.astype(I32), n_tiles.astype(I32)


def kernel(x, c, ctx, c_ctx, w_mod, b_mod, norm1_g, norm2_g, w_in, w_out, hg_lb_logits, hg_norm_g,
           q_norm_g, k_norm_g, lam_q1, lam_k1, lam_q2, lam_k2, diff_norm_g, router_group_w,
           router_group_b, router_expert_w, router_expert_b, w_gate, w_up, w_down, final_norm_g):
    depth = w_in.shape[0]
    n_lat, d = x.shape[1], x.shape[2]
    n_ctx = ctx.shape[1]
    nct = n_ctx // ROW_TILE
    assert x.shape[0] == 1 and n_ctx % ROW_TILE == 0 and n_lat % 1024 == 0

    stream = (ctx[0], x[0], 0)
    cc = jnp.zeros((8, d), F32).at[0].set(c[0]).at[1].set(c_ctx)
    mods_all = _mod_call(cc, w_mod, b_mod)[:, :2].reshape(depth, 2, N_MOD, d)
    tabs = _rope_tables(n_lat, n_ctx, HEAD_DIM) + _rope_tables(n_lat, n_ctx, HEAD_DIM // 2)

    h1 = _prenorm_call(stream[0], stream[1], norm1_g[0], mods_all[0], nct)
    out = None
    for l in range(depth):
        last = l == depth - 1
        mods = mods_all[l]
        tile_off = nct if last else 0
        tm_in = 768 if (n_ctx + n_lat) % 768 == 0 else ROW_TILE
        p = _matmul_call(h1, w_in, l, tm_in, 1024)
        o = _hgrn_call(p, hg_lb_logits, l, nct)
        qa, ka, va, qd, kd, vd = _attn_prep_call(p, tabs, q_norm_g[l], k_norm_g[l])
        att = _flash_call(qa, ka, va, n_ctx, GQA_HEADS // GQA_KV_HEADS, 1, tile_off)
        lam_init = 0.8 - 0.6 * math.exp(-0.3 * l)
        dif = _flash_call(qd, kd, vd, n_ctx, 2, 2, tile_off,
                          extra=(lam_q1[l], lam_k1[l], lam_q2[l], lam_k2[l], diff_norm_g[l]),
                          lam_init=lam_init)
        rw = jnp.concatenate([router_group_w[l], router_expert_w[l],
                              jnp.zeros((d, LANES - N_GROUPS - N_EXPERTS), F32)], axis=1)
        rb = jnp.concatenate([router_group_b[l], router_expert_b[l],
                              jnp.zeros((LANES - N_GROUPS - N_EXPERTS,), F32)]).reshape(1, LANES)
        xn, h2, logits = _outproj_call(o, p, hg_norm_g[l], att, dif, w_out[l].astype(BF16), *stream,
                                       mods, norm2_g[l], rw, nct, tile_off)
        n_tok = xn.shape[0]
        route, counts = _router_call(logits, rb)
        dst, src, tile_expert, next_expert, tile_groups, n_tiles = _dispatch_plan(route, counts, n_tok)
        ys = _moe_call(h2, src, tile_expert, next_expert, tile_groups, n_tiles, w_gate, w_up, w_down, l)
        if last:
            (out,) = _combine_call(dst, xn, route, mods, final_norm_g, mods, ys, 0, True)
        else:
            xall, h1 = _combine_call(dst, xn, route, mods, norm1_g[l + 1], mods_all[l + 1], ys, nct, False)
            stream = (xall, xall, nct)
    return out.reshape(1, n_lat, d)
```

```python
import functools
import math

import numpy as np
import jax
import jax.numpy as jnp
from jax import lax
from jax.experimental import pallas as pl
from jax.experimental.pallas import tpu as pltpu

F32 = jnp.float32
BF16 = jnp.bfloat16
I32 = jnp.int32

HEAD_DIM = 128
LANES = 128
GRID_W = 64
ROPE_THETA = 10000.0
EPS = 1e-6
HG_HEADS = 4
HG_CHUNK = 64
GQA_HEADS = 8
GQA_KV_HEADS = 2
DIFF_HEADS = 4
N_GROUPS = 4
EXPERTS_PER_GROUP = 8
N_EXPERTS = N_GROUPS * EXPERTS_PER_GROUP
N_MOD = 6
ROW_TILE = 256
MOE_TILE = 256
_DMA_GROUP = 8
MIB = 1024 * 1024

_LOG2E = math.log2(math.e)
_ATT_KEY_BLOCK = 2816
_NT = (((1,), (1,)), ((), ()))
_TN = (((0,), (0,)), ((), ()))


def _params(semantics, vmem_mib):
    return pltpu.CompilerParams(dimension_semantics=semantics, vmem_limit_bytes=vmem_mib * MIB)


def _to_token_major(ref, x):
    n, d = x.shape
    k = d // LANES
    for s in range(k):
        ref[pl.ds(s, n, stride=k), :] = x[:, s * LANES:(s + 1) * LANES]


def _from_token_major(ref, n):
    k = ref.shape[0] // n
    return jnp.concatenate([ref[pl.ds(s, n, stride=k), :] for s in range(k)], axis=1)


def _split_bf16(x, parts):
    out = []
    for _ in range(parts - 1):
        p = x.astype(BF16)
        out.append(p)
        x = x - p.astype(F32)
    out.append(x.astype(BF16))
    return out


def _mod_kernel(a_ref, w_ref, b_ref, o_ref):
    a = a_ref[...]
    a = a * (1.0 / (1.0 + jnp.exp(-a)))
    hi, lo = _split_bf16(a, 2)
    w = w_ref[0].astype(BF16)
    o_ref[0] = (jnp.dot(hi, w, preferred_element_type=F32)
                + jnp.dot(lo, w, preferred_element_type=F32) + b_ref[0])


def _mod_call(cc, w_mod, b_mod):
    depth, d, n = w_mod.shape
    tn = 1024
    return pl.pallas_call(
        _mod_kernel,
        out_shape=jax.ShapeDtypeStruct((depth, 8, n), F32),
        grid=(depth, n // tn),
        in_specs=[pl.BlockSpec((8, d), lambda l, j: (0, 0)),
                  pl.BlockSpec((1, d, tn), lambda l, j: (l, 0, j)),
                  pl.BlockSpec((1, 1, tn), lambda l, j: (l, 0, j))],
        out_specs=pl.BlockSpec((1, 8, tn), lambda l, j: (l, 0, j)),
        compiler_params=_params(("arbitrary", "arbitrary"), 40),
        name="mod_vectors",
    )(cc, w_mod, b_mod.reshape(depth, 1, n))


def _norm_mod(x, g, shift, scale):
    y = x * lax.rsqrt(jnp.mean(x * x, axis=-1, keepdims=True) + EPS) * g
    return y * (1.0 + scale) + shift


def _stream_specs(d, nct, tile_off=0, lat_off=0):
    return [pl.BlockSpec((ROW_TILE, d), lambda i: (jnp.minimum(i + tile_off, nct - 1), 0)),
            pl.BlockSpec((ROW_TILE, d), lambda i: (jnp.maximum(i + tile_off - nct, 0) + lat_off, 0))]


def _stream_tile(c_ref, x_ref, nct, tile_off=0):
    return jnp.where(pl.program_id(0) + tile_off < nct, c_ref[...], x_ref[...])


def _prenorm_kernel(c_ref, x_ref, g_ref, m_ref, o_ref, *, nct):
    m = m_ref[0]
    o_ref[...] = _norm_mod(_stream_tile(c_ref, x_ref, nct), g_ref[...], m[0:1], m[1:2]).astype(BF16)


def _mod_spec(d, nct, tile_off=0):
    return pl.BlockSpec((1, N_MOD, d), lambda i: (jnp.where(i + tile_off < nct, 1, 0), 0, 0))


def _prenorm_call(xc, xl, g, mods, nct):
    d = xl.shape[1]
    t = xc.shape[0] + xl.shape[0]
    return pl.pallas_call(
        functools.partial(_prenorm_kernel, nct=nct),
        out_shape=jax.ShapeDtypeStruct((t, d), BF16),
        grid=(t // ROW_TILE,),
        in_specs=_stream_specs(d, nct) + [pl.BlockSpec((1, d), lambda i: (0, 0)), _mod_spec(d, nct)],
        out_specs=pl.BlockSpec((ROW_TILE, d), lambda i: (i, 0)),
        compiler_params=_params(("arbitrary",), 24),
        name="prenorm",
    )(xc, xl, g.reshape(1, d), mods)


def _mm_kernel(a_ref, b_ref, o_ref, b_bf):
    @pl.when(pl.program_id(1) == 0)
    def _():
        b_bf[...] = b_ref[0].astype(BF16)

    o_ref[...] = jnp.dot(a_ref[...], b_bf[...], preferred_element_type=F32)


def _matmul_call(a, b, layer, tm, tn):
    m, k = a.shape
    n = b.shape[2]
    return pl.pallas_call(
        _mm_kernel,
        out_shape=jax.ShapeDtypeStruct((m, n), F32),
        grid=(n // tn, m // tm),
        in_specs=[pl.BlockSpec((tm, k), lambda j, i: (i, 0)),
                  pl.BlockSpec((1, k, tn), lambda j, i: (layer, 0, j))],
        out_specs=pl.BlockSpec((tm, tn), lambda j, i: (i, j)),
        scratch_shapes=[pltpu.VMEM((k, tn), BF16)],
        compiler_params=_params(("arbitrary", "arbitrary"), 48),
        name="in_proj",
    )(a, b)


_HG_LEVELS = (1, 2, 4, 8, 16, 32)
_HG_TOT_ROW = HG_CHUNK * (len(_HG_LEVELS) + 1)
_HG_W_ROWS = _HG_TOT_ROW + 16
_HG_HEADS_PER_STEP = 4


def _hgrn_consts():
    c = HG_CHUNK
    w = np.zeros((2, _HG_W_ROWS, c), np.float32)
    msk = np.zeros((2, len(_HG_LEVELS) + 1, c, c), np.float32)
    for d in range(2):
        u = np.arange(c) if d == 0 else c - 1 - np.arange(c)
        ut, us = u[:, None], u[None, :]
        w[d, :c] = us <= ut
        for li, lv in enumerate(_HG_LEVELS):
            blk = u // (2 * lv)
            qside = (u % (2 * lv)) >= lv
            bnd = (blk * 2 * lv + lv - 1)[:, None]
            wq = (us > bnd) & (us <= ut)
            wk = (us > ut) & (us <= bnd)
            w[d, c * (li + 1):c * (li + 2)] = np.where(qside[:, None], wq, -1.0 * wk)
            msk[d, li] = (blk[:, None] == blk[None, :]) & qside[:, None] & ~qside[None, :]
        msk[d, len(_HG_LEVELS)] = np.eye(c)
        w[d, _HG_TOT_ROW:] = 1.0
    return jnp.asarray(np.concatenate([w, w, w], axis=2), BF16), jnp.asarray(msk, F32)


def _hgrn_kernel(lbl_ref, q_ref, z_ref, v_ref, w_ref, msk_ref, o_ref, st_ref, *, layer, chunks, hps):
    c = HG_CHUNK
    hd = HEAD_DIM
    d = pl.program_id(0)
    j = pl.program_id(2)

    @pl.when(j == 0)
    def _():
        st_ref[...] = jnp.zeros_like(st_ref)

    lbl = lbl_ref[...]
    rows = [lbl[i:i + 1] for i in range(lbl.shape[0])]
    mx = functools.reduce(jnp.maximum, rows)
    ex = [jnp.exp(r - mx) for r in rows]
    tot = functools.reduce(lambda a, b: a + b, ex)
    lb = jnp.zeros_like(mx)
    for i in range(1, layer + 1):
        lb = lb + ex[i] / tot
    log_lb = jnp.log(lb)
    log_1m_lb = jnp.log1p(-lb)

    wmat = w_ref[0]
    nlev = len(_HG_LEVELS)

    def finish(h, hs, r0, amat, v_bf, qe, kd, decay):
        st = st_ref[h]
        o = (jnp.dot(amat, v_bf[:, hs], preferred_element_type=F32)
             + lax.dot_general(qe[:, hs], st.astype(BF16), _NT, preferred_element_type=F32))
        o_ref[0, pl.ds(r0, c), hs] = o
        st_ref[h] = st * decay[:, hs] + lax.dot_general(v_bf[:, hs], kd[:, hs], _TN,
                                                        preferred_element_type=F32)

    pending = None
    for ci in range(chunks):
        cc = ci + d * (chunks - 1 - 2 * ci)
        r0 = pl.multiple_of(cc * c, c)
        q = q_ref[pl.ds(r0, c), :]
        z = z_ref[pl.ds(r0, c), :]
        v_bf = v_ref[pl.ds(r0, c), :].astype(BF16)
        l1p = jnp.log(1.0 + jnp.exp(-jnp.abs(z)))
        ls_pos = jnp.minimum(z, 0.0) - l1p
        ls_neg = jnp.minimum(-z, 0.0) - l1p
        a2 = log_1m_lb + ls_pos
        logf = jnp.maximum(log_lb, a2) + jnp.log(1.0 + jnp.exp(-jnp.abs(log_lb - a2)))
        k = (1.0 - lb) * jnp.exp(ls_neg)

        parts = jnp.concatenate(_split_bf16(logf, 3), axis=0)
        sums = jnp.dot(wmat, parts, preferred_element_type=F32)
        b = sums[0:c]
        btot = sums[_HG_TOT_ROW:_HG_TOT_ROW + 1]

        qls, kls = [q.astype(BF16)], [k.astype(BF16)]
        for li in range(nlev):
            fl = jnp.exp(-jnp.abs(sums[c * (li + 1):c * (li + 2)]))
            qls.append((q * fl).astype(BF16))
            kls.append((k * fl).astype(BF16))
        qe = (q * jnp.exp(b)).astype(BF16)
        kd = (k * jnp.exp(btot - b)).astype(BF16)
        decay = jnp.exp(btot)

        for h in range(hps):
            hs = slice(h * hd, (h + 1) * hd)
            amat = lax.dot_general(qls[0][:, hs], kls[0][:, hs], _NT,
                                   preferred_element_type=F32) * msk_ref[0, nlev]
            for li in range(nlev):
                amat = amat + lax.dot_general(qls[li + 1][:, hs], kls[li + 1][:, hs], _NT,
                                              preferred_element_type=F32) * msk_ref[0, li]
            if pending is not None:
                finish(*pending)
            pending = (h, hs, r0, amat.astype(BF16), v_bf, qe, kd, decay)
    finish(*pending)


def _hgrn_call(p, lb_logits, layer, nct):
    t = p.shape[0]
    nblk = t // ROW_TILE
    chunks = ROW_TILE // HG_CHUNK
    hps = _HG_HEADS_PER_STEP
    hw = hps * HEAD_DIM
    ng = HG_HEADS // hps
    wmat, msk = _hgrn_consts()

    def blk(d, j):
        back = jnp.where(j < nct, nct - 1 - j, nblk - 1 - (j - nct))
        return jnp.where(d == 0, j, back)

    return pl.pallas_call(
        functools.partial(_hgrn_kernel, layer=layer, chunks=chunks, hps=hps),
        out_shape=jax.ShapeDtypeStruct((2, t, HG_HEADS * HEAD_DIM), F32),
        grid=(2, ng, nblk),
        in_specs=[pl.BlockSpec((lb_logits.shape[0], hw), lambda d, h, j: (0, h)),
                  pl.BlockSpec((ROW_TILE, hw), lambda d, h, j: (blk(d, j), d * ng + h)),
                  pl.BlockSpec((ROW_TILE, hw), lambda d, h, j: (blk(d, j), (2 + d) * ng + h)),
                  pl.BlockSpec((ROW_TILE, hw), lambda d, h, j: (blk(d, j), 4 * ng + h)),
                  pl.BlockSpec((1, _HG_W_ROWS, 3 * HG_CHUNK), lambda d, h, j: (d, 0, 0)),
                  pl.BlockSpec((1, len(_HG_LEVELS) + 1, HG_CHUNK, HG_CHUNK), lambda d, h, j: (d, 0, 0, 0))],
        out_specs=pl.BlockSpec((1, ROW_TILE, hw), lambda d, h, j: (d, blk(d, j), h)),
        scratch_shapes=[pltpu.VMEM((hps, HEAD_DIM, HEAD_DIM), F32)],
        compiler_params=_params(("arbitrary", "arbitrary", "arbitrary"), 24),
        name="hgrn_scan",
    )(lb_logits, p, p, p, wmat, msk)


def _rope_tables(n_lat, n_ctx, dim):
    rows = n_lat // GRID_W
    row = jnp.repeat(jnp.arange(rows, dtype=F32), GRID_W)
    col = jnp.tile(jnp.arange(GRID_W, dtype=F32), rows)
    axis_dim = dim // 2
    inv_freq = ROPE_THETA ** (-jnp.arange(0, axis_dim, 2, dtype=F32) / axis_dim)
    ang = jnp.concatenate([row[:, None] * inv_freq, col[:, None] * inv_freq], axis=-1)
    cos = jnp.repeat(jnp.cos(ang), 2, axis=1)
    sin = jnp.repeat(jnp.sin(ang), 2, axis=1) * jnp.tile(jnp.asarray([-1.0, 1.0], F32), dim // 2)
    reps = LANES // dim
    cos = jnp.concatenate([jnp.ones((n_ctx, dim), F32), cos], axis=0)
    sin = jnp.concatenate([jnp.zeros((n_ctx, dim), F32), sin], axis=0)
    return jnp.tile(cos, (1, reps)), jnp.tile(sin, (1, reps))


def _rope(x, cos, sin):
    lane = lax.broadcasted_iota(I32, x.shape, 1)
    swapped = jnp.where((lane & 1) == 0, pltpu.roll(x, LANES - 1, 1), pltpu.roll(x, 1, 1))
    return x * cos + swapped * sin


def _head_norm(x, g):
    return x * lax.rsqrt(jnp.mean(x * x, axis=-1, keepdims=True) + EPS) * g


def _attn_prep_kernel(pq_ref, pkv_ref, pd_ref, ca_ref, sa_ref, cd_ref, sd_ref, qn_ref, kn_ref,
                      qa_ref, ka_ref, va_ref, qd_ref, kd_ref, vd_ref):
    hd = HEAD_DIM
    ca, sa, cd, sd = ca_ref[...], sa_ref[...], cd_ref[...], sd_ref[...]
    qn, kn = qn_ref[...], kn_ref[...]
    for h in range(GQA_HEADS):
        xq = _rope(_head_norm(pq_ref[:, h * hd:(h + 1) * hd], qn), ca, sa)
        qa_ref[:, h * hd:(h + 1) * hd] = (xq * (hd ** -0.5 * _LOG2E)).astype(BF16)
    for h in range(GQA_KV_HEADS):
        xk = _rope(_head_norm(pkv_ref[:, h * hd:(h + 1) * hd], kn), ca, sa)
        ka_ref[:, h * hd:(h + 1) * hd] = xk.astype(BF16)
    ones_col = jnp.where(lax.broadcasted_iota(I32, (pq_ref.shape[0], hd), 1) == 0, 1.0, 0.0).astype(BF16)
    for h in range(GQA_KV_HEADS):
        va_ref[:, (2 * h) * hd:(2 * h + 1) * hd] = pkv_ref[:, (GQA_KV_HEADS + h) * hd:
                                                            (GQA_KV_HEADS + h + 1) * hd].astype(BF16)
        va_ref[:, (2 * h + 1) * hd:(2 * h + 2) * hd] = ones_col
    first = lax.broadcasted_iota(I32, (pq_ref.shape[0], hd), 1) < hd // 2
    dw = DIFF_HEADS * hd
    for h in range(DIFF_HEADS):
        xq = _rope(pd_ref[:, h * hd:(h + 1) * hd], cd, sd) * ((hd // 2) ** -0.5 * _LOG2E)
        qd_ref[:, (2 * h) * hd:(2 * h + 1) * hd] = jnp.where(first, xq, 0.0).astype(BF16)
        qd_ref[:, (2 * h + 1) * hd:(2 * h + 2) * hd] = jnp.where(first, 0.0, xq).astype(BF16)
        xk = _rope(pd_ref[:, dw + h * hd:dw + (h + 1) * hd], cd, sd)
        kd_ref[:, h * hd:(h + 1) * hd] = xk.astype(BF16)
    for h in range(DIFF_HEADS):
        vd_ref[:, (2 * h) * hd:(2 * h + 1) * hd] = pd_ref[:, 2 * dw + h * hd:2 * dw + (h + 1) * hd].astype(BF16)
        vd_ref[:, (2 * h + 1) * hd:(2 * h + 2) * hd] = ones_col


def _attn_prep_call(p, tabs, qn, kn):
    t = p.shape[0]
    hd = HEAD_DIM
    gq, gkv, dw = GQA_HEADS * hd, 2 * GQA_KV_HEADS * hd, DIFF_HEADS * hd
    q_off = 6 * HG_HEADS * hd
    assert q_off % gq == 0 and (q_off + gq) % gkv == 0 and (q_off + gq + gkv) % (3 * dw) == 0
    row = lambda w: pl.BlockSpec((ROW_TILE, w), lambda i: (i, 0))
    vec = pl.BlockSpec((1, hd), lambda i: (0, 0))
    return pl.pallas_call(
        _attn_prep_kernel,
        out_shape=[jax.ShapeDtypeStruct((t, gq), BF16),
                   jax.ShapeDtypeStruct((t, gkv // 2), BF16),
                   jax.ShapeDtypeStruct((t, gkv), BF16),
                   jax.ShapeDtypeStruct((t, 2 * dw), BF16),
                   jax.ShapeDtypeStruct((t, dw), BF16),
                   jax.ShapeDtypeStruct((t, 2 * dw), BF16)],
        grid=(t // ROW_TILE,),
        in_specs=[pl.BlockSpec((ROW_TILE, gq), lambda i: (i, q_off // gq)),
                  pl.BlockSpec((ROW_TILE, gkv), lambda i: (i, (q_off + gq) // gkv)),
                  pl.BlockSpec((ROW_TILE, 3 * dw), lambda i: (i, (q_off + gq + gkv) // (3 * dw))),
                  row(hd), row(hd), row(hd), row(hd), vec, vec],
        out_specs=[row(gq), row(gkv // 2), row(gkv), row(2 * dw), row(dw), row(2 * dw)],
        compiler_params=_params(("arbitrary",), 24),
        name="attn_prep",
    )(p, p, p, *tabs, qn.reshape(1, hd), kn.reshape(1, hd))


def _flash_kernel(*refs, g, nh, n_ctx, n_lat, tkl, nct, tile_off, diff, lam_init):
    if diff:
        q_ref, k_ref, v_ref, lq1, lk1, lq2, lk2, dn_ref, o_ref = refs[:9]
    else:
        q_ref, k_ref, v_ref, o_ref = refs[:4]
    nc = nh * g
    m_sc, acc_sc, p_sc = (refs[len(refs) - (3 - n) * nc:len(refs) - (2 - n) * nc] for n in range(3))
    hd = HEAD_DIM
    i = pl.program_id(1) + tile_off
    nblk = (n_ctx + n_lat) // tkl
    chains = [(c, c // g) for c in range(nh * g)]

    def head(ref, rows, h, width=1):
        return ref[rows, h * width * hd:(h + 1) * width * hd]

    def vhead(rows, h):
        return head(v_ref, rows, h, 2)

    def scores(c, kh, rows):
        return lax.dot_general(head(q_ref, slice(None), c), head(k_ref, rows, kh), _NT,
                               preferred_element_type=F32)

    def emit(outputs):
        norm = lambda a: a[:, :hd] / a[:, hd:hd + 1]
        if diff:
            lam = (jnp.exp(jnp.sum(lq1[...] * lk1[...], axis=1, keepdims=True))
                   - jnp.exp(jnp.sum(lq2[...] * lk2[...], axis=1, keepdims=True)) + lam_init)
            for kh in range(nh):
                dd = norm(outputs[2 * kh]) - lam * norm(outputs[2 * kh + 1])
                o_ref[:, kh * hd:(kh + 1) * hd] = (_head_norm(dd, dn_ref[...])
                                                   * (1.0 - lam_init)).astype(BF16)
        else:
            for c, _ in chains:
                o_ref[:, c * hd:(c + 1) * hd] = norm(outputs[c]).astype(BF16)

    @pl.when(i < nct)
    def _():
        ctx_rows = slice(0, n_ctx)
        outs = []
        for c, kh in chains:
            s = scores(c, kh, ctx_rows)
            p = jnp.exp2(s - jnp.max(s, axis=1, keepdims=True))
            outs.append(jnp.dot(p.astype(BF16), vhead(ctx_rows, kh), preferred_element_type=F32))
        emit(outs)

    @pl.when(i >= nct)
    def _():
        for jb in range(nblk):
            rows = pl.ds(jb * tkl, tkl)
            slot = jb % 2
            alphas = []
            for c, kh in chains:
                s = scores(c, kh, rows)
                m_cur = jnp.max(s, axis=1, keepdims=True)
                if jb == 0:
                    m_new = jnp.broadcast_to(m_cur, m_sc[c].shape)
                else:
                    m_prev = m_sc[c][...]
                    m_new = jnp.maximum(m_prev, m_cur)
                    alphas.append(jnp.tile(jnp.exp2(m_prev - m_new), (1, 2)))
                p_sc[c][slot] = jnp.exp2(s - jnp.tile(m_new, (1, tkl // LANES))).astype(BF16)
                m_sc[c][...] = m_new
            if jb == 0:
                continue
            prev = pl.ds((jb - 1) * tkl, tkl)
            for c, kh in chains:
                pv = jnp.dot(p_sc[c][1 - slot], vhead(prev, kh), preferred_element_type=F32)
                acc_sc[c][...] = alphas[c] * (pv if jb == 1 else acc_sc[c][...] + pv)

        last = pl.ds((nblk - 1) * tkl, tkl)
        outs = []
        for c, kh in chains:
            pv = jnp.dot(p_sc[c][(nblk - 1) % 2], vhead(last, kh), preferred_element_type=F32)
            outs.append(pv if nblk == 1 else acc_sc[c][...] + pv)
        emit(outs)


def _flash_call(q, k, v, n_ctx, g, nh, tile_off, extra=None, lam_init=0.0):
    t = k.shape[0]
    hd = HEAD_DIM
    n_kv = k.shape[1] // (nh * hd)
    n_lat = t - n_ctx
    nct = n_ctx // ROW_TILE
    nblk = next(n for n in range(1, t // LANES + 1)
                if t % (n * LANES) == 0 and t // n <= _ATT_KEY_BLOCK)
    tkl = t // nblk
    diff = extra is not None
    ow = nh * hd if diff else nh * g * hd
    resident = pl.Buffered(1)
    in_specs = [pl.BlockSpec((ROW_TILE, nh * g * hd), lambda kv, i: (i + tile_off, kv)),
                pl.BlockSpec((t, nh * hd), lambda kv, i: (0, kv), pipeline_mode=resident),
                pl.BlockSpec((t, 2 * nh * hd), lambda kv, i: (0, kv), pipeline_mode=resident)]
    args = [q, k, v]
    if diff:
        for a in extra:
            a = a.reshape(1, -1)
            in_specs.append(pl.BlockSpec(a.shape, lambda kv, i: (0, 0)))
            args.append(a)
    return pl.pallas_call(
        functools.partial(_flash_kernel, g=g, nh=nh, n_ctx=n_ctx, n_lat=n_lat, tkl=tkl, nct=nct,
                          tile_off=tile_off, diff=diff, lam_init=lam_init),
        out_shape=jax.ShapeDtypeStruct((t - tile_off * ROW_TILE, n_kv * ow), BF16),
        grid=(n_kv, t // ROW_TILE - tile_off),
        in_specs=in_specs,
        out_specs=pl.BlockSpec((ROW_TILE, ow), lambda kv, i: (i, kv)),
        scratch_shapes=([pltpu.VMEM((ROW_TILE, LANES), F32)] * (nh * g)
                        + [pltpu.VMEM((ROW_TILE, 2 * hd), F32)] * (nh * g)
                        + [pltpu.VMEM((2, ROW_TILE, tkl), BF16)] * (nh * g)),
        compiler_params=_params(("arbitrary", "arbitrary"), 48),
        name="diff_attention" if diff else "gqa_attention",
    )(*args)


def _outproj_kernel(of_ref, ob_ref, gate_ref, ng_ref, att_ref, dif_ref, wo_ref, xc_ref, xl_ref, m_ref,
                    n2_ref, rw_ref, xo_ref, h2_ref, lg_ref, *, nct, tile_off):
    hd = HEAD_DIM
    hgw = HG_HEADS * hd
    ng = ng_ref[...]
    gate = gate_ref[...]
    silu_gate = gate * (1.0 / (1.0 + jnp.exp(-gate)))
    acc = jnp.dot(att_ref[...], wo_ref[hgw:hgw + att_ref.shape[1], :], preferred_element_type=F32)
    acc = acc + jnp.dot(dif_ref[...], wo_ref[hgw + att_ref.shape[1]:, :], preferred_element_type=F32)
    hg = []
    for h in range(HG_HEADS):
        o = of_ref[0, :, h * hd:(h + 1) * hd] + ob_ref[0, :, h * hd:(h + 1) * hd]
        hg.append((_head_norm(o, ng) * silu_gate[:, h * hd:(h + 1) * hd]).astype(BF16))
    acc = acc + jnp.dot(jnp.concatenate(hg, axis=1), wo_ref[0:hgw, :], preferred_element_type=F32)
    m = m_ref[0]
    xn = _stream_tile(xc_ref, xl_ref, nct, tile_off) + m[2:3] * acc
    xo_ref[...] = xn
    h2 = _norm_mod(xn, n2_ref[...], m[3:4], m[4:5])
    _to_token_major(h2_ref, h2)
    hs = _split_bf16(h2, 2)
    ws = _split_bf16(rw_ref[...], 2)
    lg = jnp.dot(hs[1], ws[0], preferred_element_type=F32)
    lg = lg + jnp.dot(hs[0], ws[1], preferred_element_type=F32)
    lg_ref[...] = lg + jnp.dot(hs[0], ws[0], preferred_element_type=F32)


def _outproj_call(o, p, ng, att, dif, wo, xc, xl, lat_off, mods, n2, rw, nct, tile_off):
    d = xl.shape[1]
    hd = HEAD_DIM
    hgw = HG_HEADS * hd
    n_rows = att.shape[0]
    row = lambda w: pl.BlockSpec((ROW_TILE, w), lambda i: (i, 0))
    full = lambda a: pl.BlockSpec(a.shape, lambda i: (0,) * a.ndim)
    ng, n2 = ng.reshape(1, hd), n2.reshape(1, d)
    return pl.pallas_call(
        functools.partial(_outproj_kernel, nct=nct, tile_off=tile_off),
        out_shape=[jax.ShapeDtypeStruct((n_rows, d), F32),
                   jax.ShapeDtypeStruct((n_rows * (d // LANES), LANES), F32),
                   jax.ShapeDtypeStruct((n_rows, LANES), F32)],
        grid=(n_rows // ROW_TILE,),
        in_specs=[pl.BlockSpec((1, ROW_TILE, hgw), lambda i: (0, i + tile_off, 0)),
                  pl.BlockSpec((1, ROW_TILE, hgw), lambda i: (1, i + tile_off, 0)),
                  pl.BlockSpec((ROW_TILE, hgw), lambda i: (i + tile_off, 5)),
                  full(ng), row(att.shape[1]), row(dif.shape[1]), full(wo)]
                 + _stream_specs(d, nct, tile_off, lat_off)
                 + [_mod_spec(d, nct, tile_off), full(n2), full(rw)],
        out_specs=[row(d), pl.BlockSpec((ROW_TILE * (d // LANES), LANES), lambda i: (i, 0)), row(LANES)],
        compiler_params=_params(("arbitrary",), 56),
        name="out_proj",
    )(o, o, p, ng, att, dif, wo, xc, xl, mods, n2, rw)


def _router_kernel(lg_ref, bias_ref, r_ref, cnt_ref, run_ref):
    i = pl.program_id(0)
    tm = lg_ref.shape[0]

    @pl.when(i == 0)
    def _():
        run_ref[...] = jnp.zeros_like(run_ref)

    lane = lax.broadcasted_iota(I32, (tm, LANES), 1).astype(F32)
    lg = lg_ref[...] + bias_ref[...]
    ninf = -jnp.inf

    def first_max(vals):
        mx = jnp.max(vals, axis=1, keepdims=True)
        idx = jnp.min(jnp.where(vals == mx, lane, float(LANES)), axis=1, keepdims=True)
        return mx, idx

    gl = jnp.where(lane < N_GROUPS, lg, ninf)
    gmax, gidx = first_max(gl)
    g_top = 1.0 / jnp.sum(jnp.exp(gl - gmax), axis=1, keepdims=True)
    lo = N_GROUPS + EXPERTS_PER_GROUP * gidx
    el = jnp.where((lane >= lo) & (lane < lo + EXPERTS_PER_GROUP), lg, ninf)
    m1, e1 = first_max(el)
    m2, e2 = first_max(jnp.where(lane == e1, ninf, el))
    r = jnp.exp(m2 - m1)
    w1 = g_top / (1.0 + r)
    w2 = g_top * r / (1.0 + r)

    hit = ((lane == e1) | (lane == e2)).astype(BF16)
    ti = lax.broadcasted_iota(I32, (tm, tm), 0)
    si = lax.broadcasted_iota(I32, (tm, tm), 1)
    before = (si < ti).astype(BF16)
    pos = jnp.dot(before, hit, preferred_element_type=F32) + run_ref[0:1, :]
    p1 = jnp.sum(jnp.where(lane == e1, pos, 0.0), axis=1, keepdims=True)
    p2 = jnp.sum(jnp.where(lane == e2, pos, 0.0), axis=1, keepdims=True)
    total = run_ref[0:1, :] + jnp.sum(hit.astype(F32), axis=0, keepdims=True)
    run_ref[...] = jnp.broadcast_to(total, run_ref.shape)
    cnt_ref[...] = jnp.broadcast_to(total, cnt_ref.shape)

    fields = (e1 - N_GROUPS, e2 - N_GROUPS, w1, w2, p1, p2)
    out = jnp.zeros((tm, LANES), F32)
    for n, f in enumerate(fields):
        out = jnp.where(lane == n, f, out)
    r_ref[...] = out


def _router_call(logits, bias):
    n = logits.shape[0]
    return pl.pallas_call(
        _router_kernel,
        out_shape=[jax.ShapeDtypeStruct((n, LANES), F32), jax.ShapeDtypeStruct((8, LANES), F32)],
        grid=(n // ROW_TILE,),
        in_specs=[pl.BlockSpec((ROW_TILE, LANES), lambda i: (i, 0)),
                  pl.BlockSpec((1, LANES), lambda i: (0, 0))],
        out_specs=[pl.BlockSpec((ROW_TILE, LANES), lambda i: (i, 0)),
                   pl.BlockSpec((8, LANES), lambda i: (0, 0))],
        scratch_shapes=[pltpu.VMEM((8, LANES), F32)],
        compiler_params=_params(("arbitrary",), 16),
        name="router",
    )(logits, bias)


def _moe_kernel(src_ref, te_ref, nx_ref, ng_ref, nt_ref, h_hbm, wg_hbm, wu_hbm, wd_hbm, y_ref,
                xbuf, sem, wg_st, wu_st, wd_st, wsem, wg_bf, wu_bf, wd_bf, *, layer):
    i = pl.program_id(0)
    n_tiles = nt_ref[0]
    tm = MOE_TILE
    k = xbuf.shape[1] // tm
    grp_rows = _DMA_GROUP * k

    def weight_copies(e):
        pairs = ((wg_hbm, wg_st), (wu_hbm, wu_st), (wd_hbm, wd_st))
        return [pltpu.make_async_copy(w.at[layer, e], st, wsem.at[n]) for n, (w, st) in enumerate(pairs)]

    def start_tile(tile, slot):
        def body(grp, carry):
            for u in range(_DMA_GROUP):
                r = grp * _DMA_GROUP + u
                src_row = pl.multiple_of(src_ref[tile * tm + r] * k, k)
                pltpu.make_async_copy(h_hbm.at[pl.ds(src_row, k), :],
                                      xbuf.at[slot, pl.ds(pl.multiple_of(r * k, k), k), :],
                                      sem.at[slot]).start(priority=1)
            return carry
        lax.fori_loop(0, ng_ref[tile], body, 0)

    def wait_tile(tile, slot):
        def body(grp, carry):
            pltpu.make_async_copy(h_hbm.at[pl.ds(0, grp_rows), :], xbuf.at[slot, pl.ds(0, grp_rows), :],
                                  sem.at[slot]).wait()
            return carry
        lax.fori_loop(0, ng_ref[tile], body, 0)

    @pl.when(i == 0)
    def _():
        xbuf[...] = jnp.zeros_like(xbuf)
        start_tile(0, 0)
        for cp in weight_copies(te_ref[0]):
            cp.start()

    @pl.when(i < n_tiles)
    def _():
        slot = i % 2

        @pl.when(i + 1 < n_tiles)
        def _():
            start_tile(i + 1, 1 - slot)

        changed = jnp.logical_or(i == 0, te_ref[i] != te_ref[jnp.maximum(i - 1, 0)])

        @pl.when(changed)
        def _():
            for cp in weight_copies(te_ref[i]):
                cp.wait()
            wg_bf[...] = wg_st[...].astype(BF16)
            wu_bf[...] = wu_st[...].astype(BF16)
            wd_bf[...] = wd_st[...].astype(BF16)

            @pl.when(nx_ref[i] >= 0)
            def _():
                for cp in weight_copies(nx_ref[i]):
                    cp.start()

        wait_tile(i, slot)
        xb = _from_token_major(xbuf.at[slot], tm).astype(BF16)
        gt = jnp.dot(xb, wg_bf[...], preferred_element_type=F32)
        up = jnp.dot(xb, wu_bf[...], preferred_element_type=F32)
        act = (gt * (1.0 / (1.0 + jnp.exp(-gt))) * up).astype(BF16)
        _to_token_major(y_ref, jnp.dot(act, wd_bf[...], preferred_element_type=F32))

    @pl.when(i >= n_tiles)
    def _():
        y_ref[...] = jnp.zeros_like(y_ref)


def _moe_call(h2, src, tile_expert, next_expert, tile_groups, n_tiles, wg, wu, wd, layer):
    d, ff = wg.shape[2], wg.shape[3]
    k = d // LANES
    max_tiles = tile_expert.shape[0]
    tm = MOE_TILE
    hbm = pl.BlockSpec(memory_space=pl.ANY)
    grid_spec = pltpu.PrefetchScalarGridSpec(
        num_scalar_prefetch=5,
        grid=(max_tiles,),
        in_specs=[hbm, hbm, hbm, hbm],
        out_specs=pl.BlockSpec((tm * k, LANES), lambda i, src, te, nx, ng, nt: (i, 0)),
        scratch_shapes=[pltpu.VMEM((2, tm * k, LANES), F32), pltpu.SemaphoreType.DMA((2,)),
                        pltpu.VMEM((d, ff), F32), pltpu.VMEM((d, ff), F32), pltpu.VMEM((ff, d), F32),
                        pltpu.SemaphoreType.DMA((3,)),
                        pltpu.VMEM((d, ff), BF16), pltpu.VMEM((d, ff), BF16), pltpu.VMEM((ff, d), BF16)])
    return pl.pallas_call(
        functools.partial(_moe_kernel, layer=layer),
        out_shape=jax.ShapeDtypeStruct((max_tiles * tm * k, LANES), F32),
        grid_spec=grid_spec,
        compiler_params=_params(("arbitrary",), 40),
        name="moe_experts",
    )(src, tile_expert, next_expert, tile_groups, n_tiles, h2, wg, wu, wd)


def _combine_kernel(dst_ref, x_ref, rt_ref, m_ref, g_ref, mn_ref, y_hbm, *rest, last, n_out):
    outs, (ybuf, sem) = rest[:n_out], rest[n_out:]
    i = pl.program_id(0)
    tm = x_ref.shape[0]
    k = ybuf.shape[2] // tm

    def start_tile(tile, slot):
        def body(grp, carry):
            for u in range(_DMA_GROUP):
                r = grp * _DMA_GROUP + u
                for j in range(2):
                    src_row = pl.multiple_of(dst_ref[(tile * tm + r) * 2 + j] * k, k)
                    pltpu.make_async_copy(y_hbm.at[pl.ds(src_row, k), :],
                                          ybuf.at[slot, j, pl.ds(pl.multiple_of(r * k, k), k), :],
                                          sem.at[slot]).start(priority=j)
            return carry
        lax.fori_loop(0, tm // _DMA_GROUP, body, 0)

    @pl.when(i == 0)
    def _():
        start_tile(0, 0)

    slot = i % 2

    @pl.when(i + 1 < pl.num_programs(0))
    def _():
        start_tile(i + 1, 1 - slot)

    for j in range(2):
        pltpu.make_async_copy(y_hbm.at[pl.ds(0, tm * k), :], ybuf.at[slot, j], sem.at[slot]).wait()
    rt = rt_ref[...]
    y = (rt[:, 2:3] * _from_token_major(ybuf.at[slot, 0], tm)
         + rt[:, 3:4] * _from_token_major(ybuf.at[slot, 1], tm))
    xn = x_ref[...] + m_ref[0][5:6] * y
    if last:
        outs[0][...] = xn * lax.rsqrt(jnp.mean(xn * xn, axis=-1, keepdims=True) + EPS) * g_ref[...]
    else:
        outs[0][...] = xn
        mn = mn_ref[0]
        outs[1][...] = _norm_mod(xn, g_ref[...], mn[0:1], mn[1:2]).astype(BF16)


def _combine_call(dst, x, route, mods, g, mods_next, ys, nct, last):
    n, d = x.shape
    tm = ROW_TILE
    mspec = pl.BlockSpec((1, N_MOD, d), lambda i, dst: (jnp.where(i < nct, 1, 0), 0, 0))
    out_shape = [jax.ShapeDtypeStruct((n, d), F32)]
    if not last:
        out_shape.append(jax.ShapeDtypeStruct((n, d), BF16))
    row = lambda w: pl.BlockSpec((tm, w), lambda i, dst: (i, 0))
    grid_spec = pltpu.PrefetchScalarGridSpec(
        num_scalar_prefetch=1,
        grid=(n // tm,),
        in_specs=[row(d), row(LANES), mspec, pl.BlockSpec((1, d), lambda i, dst: (0, 0)), mspec,
                  pl.BlockSpec(memory_space=pl.ANY)],
        out_specs=[row(d)] * len(out_shape),
        scratch_shapes=[pltpu.VMEM((2, 2, tm * (d // LANES), LANES), F32), pltpu.SemaphoreType.DMA((2,))])
    return pl.pallas_call(
        functools.partial(_combine_kernel, last=last, n_out=len(out_shape)),
        out_shape=out_shape,
        grid_spec=grid_spec,
        compiler_params=_params(("arbitrary",), 40),
        name="moe_combine",
    )(dst, x, route, mods, g.reshape(1, d), mods_next, ys)


def _dispatch_plan(route, counts, n_tokens):
    tm = MOE_TILE
    max_tiles = (2 * n_tokens) // tm + N_EXPERTS
    cnt = counts[0, N_GROUPS:N_GROUPS + N_EXPERTS].astype(I32)
    tiles_per = (cnt + tm - 1) // tm
    tile_end = jnp.cumsum(tiles_per)
    offs = (tile_end - tiles_per) * tm
    eid = route[:, 0:2].astype(I32)
    e_ids = jnp.arange(N_EXPERTS, dtype=I32)
    offs_of = jnp.sum(jnp.where(eid[..., None] == e_ids, offs, 0), axis=-1)
    dst = offs_of + route[:, 4:6].astype(I32)
    tok = jnp.broadcast_to(jnp.arange(n_tokens, dtype=I32)[:, None], (n_tokens, 2))
    src = jnp.zeros((max_tiles * tm,), I32).at[dst.reshape(-1)].set(tok.reshape(-1))
    n_tiles = tile_end[-1:]
    tile_ids = jnp.minimum(jnp.arange(max_tiles, dtype=I32), n_tiles[0] - 1)
    tile_expert = jnp.sum((tile_end[None, :] <= tile_ids[:, None]).astype(I32), axis=1)
    later = (e_ids[None, :] > e_ids[:, None]) & (tiles_per[None, :] > 0)
    nxt = jnp.min(jnp.where(later, e_ids[None, :], N_EXPERTS), axis=1)
    next_expert = jnp.where(nxt < N_EXPERTS, nxt, -1)[tile_expert]
    within = tile_ids - (tile_end - tiles_per)[tile_expert]
    rows = jnp.clip(cnt[tile_expert] - within * tm, 0, tm)
    tile_groups = (rows + _DMA_GROUP - 1) // _DMA_GROUP
    return dst.reshape(-1), src, tile_expert, next_expert, tile_groups.astype(I32), n_tiles.astype(I32)


def kernel(x, c, ctx, c_ctx, w_mod, b_mod, norm1_g, norm2_g, w_in, w_out, hg_lb_logits, hg_norm_g,
           q_norm_g, k_norm_g, lam_q1, lam_k1, lam_q2, lam_k2, diff_norm_g, router_group_w,
           router_group_b, router_expert_w, router_expert_b, w_gate, w_up, w_down, final_norm_g):
    depth = w_in.shape[0]
    n_lat, d = x.shape[1], x.shape[2]
    n_ctx = ctx.shape[1]
    nct = n_ctx // ROW_TILE
    assert x.shape[0] == 1 and n_ctx % ROW_TILE == 0 and n_lat % 1024 == 0

    stream = (ctx[0], x[0], 0)
    cc = jnp.zeros((8, d), F32).at[0].set(c[0]).at[1].set(c_ctx)
    mods_all = _mod_call(cc, w_mod, b_mod)[:, :2].reshape(depth, 2, N_MOD, d)
    tabs = _rope_tables(n_lat, n_ctx, HEAD_DIM) + _rope_tables(n_lat, n_ctx, HEAD_DIM // 2)

    h1 = _prenorm_call(stream[0], stream[1], norm1_g[0], mods_all[0], nct)
    out = None
    for l in range(depth):
        last = l == depth - 1
        mods = mods_all[l]
        tile_off = nct if last else 0
        tm_in = 768 if (n_ctx + n_lat) % 768 == 0 else ROW_TILE
        p = _matmul_call(h1, w_in, l, tm_in, 1024)
        o = _hgrn_call(p, hg_lb_logits, l, nct)
        qa, ka, va, qd, kd, vd = _attn_prep_call(p, tabs, q_norm_g[l], k_norm_g[l])
        att = _flash_call(qa, ka, va, n_ctx, GQA_HEADS // GQA_KV_HEADS, 1, tile_off)
        lam_init = 0.8 - 0.6 * math.exp(-0.3 * l)
        dif = _flash_call(qd, kd, vd, n_ctx, 2, 2, tile_off,
                          extra=(lam_q1[l], lam_k1[l], lam_q2[l], lam_k2[l], diff_norm_g[l]),
                          lam_init=lam_init)
        rw = jnp.concatenate([router_group_w[l], router_expert_w[l],
                              jnp.zeros((d, LANES - N_GROUPS - N_EXPERTS), F32)], axis=1)
        rb = jnp.concatenate([router_group_b[l], router_expert_b[l],
                              jnp.zeros((LANES - N_GROUPS - N_EXPERTS,), F32)]).reshape(1, LANES)
        xn, h2, logits = _outproj_call(o, p, hg_norm_g[l], att, dif, w_out[l].astype(BF16), *stream,
                                       mods, norm2_g[l], rw, nct, tile_off)
        n_tok = xn.shape[0]
        route, counts = _router_call(logits, rb)
        dst, src, tile_expert, next_expert, tile_groups, n_tiles = _dispatch_plan(route, counts, n_tok)
        ys = _moe_call(h2, src, tile_expert, next_expert, tile_groups, n_tiles, w_gate, w_up, w_down, l)
        if last:
            (out,) = _combine_call(dst, xn, route, mods, final_norm_g, mods, ys, 0, True)
        else:
            xall, h1 = _combine_call(dst, xn, route, mods, norm1_g[l + 1], mods_all[l + 1], ys, nct, False)
            stream = (xall, xall, nct)
    return out.reshape(1, n_lat, d)
```

```python
import functools
import math

import numpy as np
import jax
import jax.numpy as jnp
from jax import lax
from jax.experimental import pallas as pl
from jax.experimental.pallas import tpu as pltpu

F32 = jnp.float32
BF16 = jnp.bfloat16
I32 = jnp.int32

HEAD_DIM = 128
LANES = 128
GRID_W = 64
ROPE_THETA = 10000.0
EPS = 1e-6
HG_HEADS = 4
HG_CHUNK = 64
GQA_HEADS = 8
GQA_KV_HEADS = 2
DIFF_HEADS = 4
N_GROUPS = 4
EXPERTS_PER_GROUP = 8
N_EXPERTS = N_GROUPS * EXPERTS_PER_GROUP
N_MOD = 6
ROW_TILE = 256
MOE_TILE = 256
_DMA_GROUP = 8
MIB = 1024 * 1024

_LOG2E = math.log2(math.e)
_ATT_KEY_BLOCK = 2816
_NT = (((1,), (1,)), ((), ()))
_TN = (((0,), (0,)), ((), ()))


def _params(semantics, vmem_mib):
    return pltpu.CompilerParams(dimension_semantics=semantics, vmem_limit_bytes=vmem_mib * MIB)


def _to_token_major(ref, x):
    n, d = x.shape
    k = d // LANES
    for s in range(k):
        ref[pl.ds(s, n, stride=k), :] = x[:, s * LANES:(s + 1) * LANES]


def _from_token_major(ref, n):
    k = ref.shape[0] // n
    return jnp.concatenate([ref[pl.ds(s, n, stride=k), :] for s in range(k)], axis=1)


def _split_bf16(x, parts):
    out = []
    for _ in range(parts - 1):
        p = x.astype(BF16)
        out.append(p)
        x = x - p.astype(F32)
    out.append(x.astype(BF16))
    return out


def _mod_kernel(a_ref, w_ref, b_ref, o_ref):
    a = a_ref[...]
    a = a * (1.0 / (1.0 + jnp.exp(-a)))
    hi, lo = _split_bf16(a, 2)
    w = w_ref[0].astype(BF16)
    o_ref[0] = (jnp.dot(hi, w, preferred_element_type=F32)
                + jnp.dot(lo, w, preferred_element_type=F32) + b_ref[0])


def _mod_call(cc, w_mod, b_mod):
    depth, d, n = w_mod.shape
    tn = 1024
    return pl.pallas_call(
        _mod_kernel,
        out_shape=jax.ShapeDtypeStruct((depth, 8, n), F32),
        grid=(depth, n // tn),
        in_specs=[pl.BlockSpec((8, d), lambda l, j: (0, 0)),
                  pl.BlockSpec((1, d, tn), lambda l, j: (l, 0, j)),
                  pl.BlockSpec((1, 1, tn), lambda l, j: (l, 0, j))],
        out_specs=pl.BlockSpec((1, 8, tn), lambda l, j: (l, 0, j)),
        compiler_params=_params(("arbitrary", "arbitrary"), 40),
        name="mod_vectors",
    )(cc, w_mod, b_mod.reshape(depth, 1, n))


def _norm_mod(x, g, shift, scale):
    y = x * lax.rsqrt(jnp.mean(x * x, axis=-1, keepdims=True) + EPS) * g
    return y * (1.0 + scale) + shift


def _stream_specs(d, nct, tile_off=0, lat_off=0):
    return [pl.BlockSpec((ROW_TILE, d), lambda i: (jnp.minimum(i + tile_off, nct - 1), 0)),
            pl.BlockSpec((ROW_TILE, d), lambda i: (jnp.maximum(i + tile_off - nct, 0) + lat_off, 0))]


def _stream_tile(c_ref, x_ref, nct, tile_off=0):
    return jnp.where(pl.program_id(0) + tile_off < nct, c_ref[...], x_ref[...])


def _prenorm_kernel(c_ref, x_ref, g_ref, m_ref, o_ref, *, nct):
    m = m_ref[0]
    o_ref[...] = _norm_mod(_stream_tile(c_ref, x_ref, nct), g_ref[...], m[0:1], m[1:2]).astype(BF16)


def _mod_spec(d, nct, tile_off=0):
    return pl.BlockSpec((1, N_MOD, d), lambda i: (jnp.where(i + tile_off < nct, 1, 0), 0, 0))


def _prenorm_call(xc, xl, g, mods, nct):
    d = xl.shape[1]
    t = xc.shape[0] + xl.shape[0]
    return pl.pallas_call(
        functools.partial(_prenorm_kernel, nct=nct),
        out_shape=jax.ShapeDtypeStruct((t, d), BF16),
        grid=(t // ROW_TILE,),
        in_specs=_stream_specs(d, nct) + [pl.BlockSpec((1, d), lambda i: (0, 0)), _mod_spec(d, nct)],
        out_specs=pl.BlockSpec((ROW_TILE, d), lambda i: (i, 0)),
        compiler_params=_params(("arbitrary",), 24),
        name="prenorm",
    )(xc, xl, g.reshape(1, d), mods)


def _mm_kernel(a_ref, b_ref, o_ref, b_bf):
    @pl.when(pl.program_id(1) == 0)
    def _():
        b_bf[...] = b_ref[0].astype(BF16)

    o_ref[...] = jnp.dot(a_ref[...], b_bf[...], preferred_element_type=F32)


def _matmul_call(a, b, layer, tm, tn):
    m, k = a.shape
    n = b.shape[2]
    return pl.pallas_call(
        _mm_kernel,
        out_shape=jax.ShapeDtypeStruct((m, n), F32),
        grid=(n // tn, m // tm),
        in_specs=[pl.BlockSpec((tm, k), lambda j, i: (i, 0)),
                  pl.BlockSpec((1, k, tn), lambda j, i: (layer, 0, j))],
        out_specs=pl.BlockSpec((tm, tn), lambda j, i: (i, j)),
        scratch_shapes=[pltpu.VMEM((k, tn), BF16)],
        compiler_params=_params(("arbitrary", "arbitrary"), 56),
        name="in_proj",
    )(a, b)


_HG_LEVELS = (1, 2, 4, 8, 16, 32)
_HG_TOT_ROW = HG_CHUNK * (len(_HG_LEVELS) + 1)
_HG_W_ROWS = _HG_TOT_ROW + 16
_HG_HEADS_PER_STEP = 4


def _hgrn_consts():
    c = HG_CHUNK
    w = np.zeros((2, _HG_W_ROWS, c), np.float32)
    msk = np.zeros((2, len(_HG_LEVELS) + 1, c, c), np.float32)
    for d in range(2):
        u = np.arange(c) if d == 0 else c - 1 - np.arange(c)
        ut, us = u[:, None], u[None, :]
        w[d, :c] = us <= ut
        for li, lv in enumerate(_HG_LEVELS):
            blk = u // (2 * lv)
            qside = (u % (2 * lv)) >= lv
            bnd = (blk * 2 * lv + lv - 1)[:, None]
            wq = (us > bnd) & (us <= ut)
            wk = (us > ut) & (us <= bnd)
            w[d, c * (li + 1):c * (li + 2)] = np.where(qside[:, None], wq, -1.0 * wk)
            msk[d, li] = (blk[:, None] == blk[None, :]) & qside[:, None] & ~qside[None, :]
        msk[d, len(_HG_LEVELS)] = np.eye(c)
        w[d, _HG_TOT_ROW:] = 1.0
    return jnp.asarray(np.concatenate([w, w, w], axis=2), BF16), jnp.asarray(msk, F32)


def _hgrn_kernel(lbl_ref, q_ref, z_ref, v_ref, w_ref, msk_ref, o_ref, st_ref, *, layer, chunks, hps):
    c = HG_CHUNK
    hd = HEAD_DIM
    d = pl.program_id(0)
    j = pl.program_id(2)

    @pl.when(j == 0)
    def _():
        st_ref[...] = jnp.zeros_like(st_ref)

    lbl = lbl_ref[...]
    rows = [lbl[i:i + 1] for i in range(lbl.shape[0])]
    mx = functools.reduce(jnp.maximum, rows)
    ex = [jnp.exp(r - mx) for r in rows]
    tot = functools.reduce(lambda a, b: a + b, ex)
    lb = jnp.zeros_like(mx)
    for i in range(1, layer + 1):
        lb = lb + ex[i] / tot
    log_lb = jnp.log(lb)
    log_1m_lb = jnp.log1p(-lb)

    wmat = w_ref[0]
    nlev = len(_HG_LEVELS)

    def finish(h, hs, r0, amat, v_bf, qe, kd, decay):
        st = st_ref[h]
        o = (jnp.dot(amat, v_bf[:, hs], preferred_element_type=F32)
             + lax.dot_general(qe[:, hs], st.astype(BF16), _NT, preferred_element_type=F32))
        o_ref[0, pl.ds(r0, c), hs] = o
        st_ref[h] = st * decay[:, hs] + lax.dot_general(v_bf[:, hs], kd[:, hs], _TN,
                                                        preferred_element_type=F32)

    pending = None
    for ci in range(chunks):
        cc = ci + d * (chunks - 1 - 2 * ci)
        r0 = pl.multiple_of(cc * c, c)
        q = q_ref[pl.ds(r0, c), :]
        z = z_ref[pl.ds(r0, c), :]
        v_bf = v_ref[pl.ds(r0, c), :].astype(BF16)
        l1p = jnp.log(1.0 + jnp.exp(-jnp.abs(z)))
        ls_pos = jnp.minimum(z, 0.0) - l1p
        ls_neg = jnp.minimum(-z, 0.0) - l1p
        a2 = log_1m_lb + ls_pos
        logf = jnp.maximum(log_lb, a2) + jnp.log(1.0 + jnp.exp(-jnp.abs(log_lb - a2)))
        k = (1.0 - lb) * jnp.exp(ls_neg)

        parts = jnp.concatenate(_split_bf16(logf, 3), axis=0)
        sums = jnp.dot(wmat, parts, preferred_element_type=F32)
        b = sums[0:c]
        btot = sums[_HG_TOT_ROW:_HG_TOT_ROW + 1]

        qls, kls = [q.astype(BF16)], [k.astype(BF16)]
        for li in range(nlev):
            fl = jnp.exp(-jnp.abs(sums[c * (li + 1):c * (li + 2)]))
            qls.append((q * fl).astype(BF16))
            kls.append((k * fl).astype(BF16))
        qe = (q * jnp.exp(b)).astype(BF16)
        kd = (k * jnp.exp(btot - b)).astype(BF16)
        decay = jnp.exp(btot)

        for h in range(hps):
            hs = slice(h * hd, (h + 1) * hd)
            amat = lax.dot_general(qls[0][:, hs], kls[0][:, hs], _NT,
                                   preferred_element_type=F32) * msk_ref[0, nlev]
            for li in range(nlev):
                amat = amat + lax.dot_general(qls[li + 1][:, hs], kls[li + 1][:, hs], _NT,
                                              preferred_element_type=F32) * msk_ref[0, li]
            if pending is not None:
                finish(*pending)
            pending = (h, hs, r0, amat.astype(BF16), v_bf, qe, kd, decay)
    finish(*pending)


def _hgrn_call(p, lb_logits, layer, nct):
    t = p.shape[0]
    nblk = t // ROW_TILE
    chunks = ROW_TILE // HG_CHUNK
    hps = _HG_HEADS_PER_STEP
    hw = hps * HEAD_DIM
    ng = HG_HEADS // hps
    wmat, msk = _hgrn_consts()

    def blk(d, j):
        back = jnp.where(j < nct, nct - 1 - j, nblk - 1 - (j - nct))
        return jnp.where(d == 0, j, back)

    return pl.pallas_call(
        functools.partial(_hgrn_kernel, layer=layer, chunks=chunks, hps=hps),
        out_shape=jax.ShapeDtypeStruct((2, t, HG_HEADS * HEAD_DIM), F32),
        grid=(2, ng, nblk),
        in_specs=[pl.BlockSpec((lb_logits.shape[0], hw), lambda d, h, j: (0, h)),
                  pl.BlockSpec((ROW_TILE, hw), lambda d, h, j: (blk(d, j), d * ng + h)),
                  pl.BlockSpec((ROW_TILE, hw), lambda d, h, j: (blk(d, j), (2 + d) * ng + h)),
                  pl.BlockSpec((ROW_TILE, hw), lambda d, h, j: (blk(d, j), 4 * ng + h)),
                  pl.BlockSpec((1, _HG_W_ROWS, 3 * HG_CHUNK), lambda d, h, j: (d, 0, 0)),
                  pl.BlockSpec((1, len(_HG_LEVELS) + 1, HG_CHUNK, HG_CHUNK), lambda d, h, j: (d, 0, 0, 0))],
        out_specs=pl.BlockSpec((1, ROW_TILE, hw), lambda d, h, j: (d, blk(d, j), h)),
        scratch_shapes=[pltpu.VMEM((hps, HEAD_DIM, HEAD_DIM), F32)],
        compiler_params=_params(("arbitrary", "arbitrary", "arbitrary"), 24),
        name="hgrn_scan",
    )(lb_logits, p, p, p, wmat, msk)


def _rope_tables(n_lat, n_ctx, dim):
    rows = n_lat // GRID_W
    row = jnp.repeat(jnp.arange(rows, dtype=F32), GRID_W)
    col = jnp.tile(jnp.arange(GRID_W, dtype=F32), rows)
    axis_dim = dim // 2
    inv_freq = ROPE_THETA ** (-jnp.arange(0, axis_dim, 2, dtype=F32) / axis_dim)
    ang = jnp.concatenate([row[:, None] * inv_freq, col[:, None] * inv_freq], axis=-1)
    cos = jnp.repeat(jnp.cos(ang), 2, axis=1)
    sin = jnp.repeat(jnp.sin(ang), 2, axis=1) * jnp.tile(jnp.asarray([-1.0, 1.0], F32), dim // 2)
    reps = LANES // dim
    cos = jnp.concatenate([jnp.ones((n_ctx, dim), F32), cos], axis=0)
    sin = jnp.concatenate([jnp.zeros((n_ctx, dim), F32), sin], axis=0)
    return jnp.tile(cos, (1, reps)), jnp.tile(sin, (1, reps))


def _rope(x, cos, sin):
    lane = lax.broadcasted_iota(I32, x.shape, 1)
    swapped = jnp.where((lane & 1) == 0, pltpu.roll(x, LANES - 1, 1), pltpu.roll(x, 1, 1))
    return x * cos + swapped * sin


def _head_norm(x, g):
    return x * lax.rsqrt(jnp.mean(x * x, axis=-1, keepdims=True) + EPS) * g


def _attn_prep_kernel(pq_ref, pkv_ref, pd_ref, ca_ref, sa_ref, cd_ref, sd_ref, qn_ref, kn_ref,
                      qa_ref, ka_ref, va_ref, qd_ref, kd_ref, vd_ref):
    hd = HEAD_DIM
    ca, sa, cd, sd = ca_ref[...], sa_ref[...], cd_ref[...], sd_ref[...]
    qn, kn = qn_ref[...], kn_ref[...]
    for h in range(GQA_HEADS):
        xq = _rope(_head_norm(pq_ref[:, h * hd:(h + 1) * hd], qn), ca, sa)
        qa_ref[:, h * hd:(h + 1) * hd] = (xq * (hd ** -0.5 * _LOG2E)).astype(BF16)
    for h in range(GQA_KV_HEADS):
        xk = _rope(_head_norm(pkv_ref[:, h * hd:(h + 1) * hd], kn), ca, sa)
        ka_ref[:, h * hd:(h + 1) * hd] = xk.astype(BF16)
    ones_col = jnp.where(lax.broadcasted_iota(I32, (pq_ref.shape[0], hd), 1) == 0, 1.0, 0.0).astype(BF16)
    for h in range(GQA_KV_HEADS):
        va_ref[:, (2 * h) * hd:(2 * h + 1) * hd] = pkv_ref[:, (GQA_KV_HEADS + h) * hd:
                                                            (GQA_KV_HEADS + h + 1) * hd].astype(BF16)
        va_ref[:, (2 * h + 1) * hd:(2 * h + 2) * hd] = ones_col
    first = lax.broadcasted_iota(I32, (pq_ref.shape[0], hd), 1) < hd // 2
    dw = DIFF_HEADS * hd
    for h in range(DIFF_HEADS):
        xq = _rope(pd_ref[:, h * hd:(h + 1) * hd], cd, sd) * ((hd // 2) ** -0.5 * _LOG2E)
        qd_ref[:, (2 * h) * hd:(2 * h + 1) * hd] = jnp.where(first, xq, 0.0).astype(BF16)
        qd_ref[:, (2 * h + 1) * hd:(2 * h + 2) * hd] = jnp.where(first, 0.0, xq).astype(BF16)
        xk = _rope(pd_ref[:, dw + h * hd:dw + (h + 1) * hd], cd, sd)
        kd_ref[:, h * hd:(h + 1) * hd] = xk.astype(BF16)
    for h in range(DIFF_HEADS):
        vd_ref[:, (2 * h) * hd:(2 * h + 1) * hd] = pd_ref[:, 2 * dw + h * hd:2 * dw + (h + 1) * hd].astype(BF16)
        vd_ref[:, (2 * h + 1) * hd:(2 * h + 2) * hd] = ones_col


def _attn_prep_call(p, tabs, qn, kn):
    t = p.shape[0]
    hd = HEAD_DIM
    gq, gkv, dw = GQA_HEADS * hd, 2 * GQA_KV_HEADS * hd, DIFF_HEADS * hd
    q_off = 6 * HG_HEADS * hd
    assert q_off % gq == 0 and (q_off + gq) % gkv == 0 and (q_off + gq + gkv) % (3 * dw) == 0
    tp = next(n for n in (768, ROW_TILE) if t % n == 0)
    row = lambda w: pl.BlockSpec((tp, w), lambda i: (i, 0))
    vec = pl.BlockSpec((1, hd), lambda i: (0, 0))
    return pl.pallas_call(
        _attn_prep_kernel,
        out_shape=[jax.ShapeDtypeStruct((t, gq), BF16),
                   jax.ShapeDtypeStruct((t, gkv // 2), BF16),
                   jax.ShapeDtypeStruct((t, gkv), BF16),
                   jax.ShapeDtypeStruct((t, 2 * dw), BF16),
                   jax.ShapeDtypeStruct((t, dw), BF16),
                   jax.ShapeDtypeStruct((t, 2 * dw), BF16)],
        grid=(t // tp,),
        in_specs=[pl.BlockSpec((tp, gq), lambda i: (i, q_off // gq)),
                  pl.BlockSpec((tp, gkv), lambda i: (i, (q_off + gq) // gkv)),
                  pl.BlockSpec((tp, 3 * dw), lambda i: (i, (q_off + gq + gkv) // (3 * dw))),
                  row(hd), row(hd), row(hd), row(hd), vec, vec],
        out_specs=[row(gq), row(gkv // 2), row(gkv), row(2 * dw), row(dw), row(2 * dw)],
        compiler_params=_params(("arbitrary",), 48),
        name="attn_prep",
    )(p, p, p, *tabs, qn.reshape(1, hd), kn.reshape(1, hd))


def _flash_kernel(*refs, g, nh, n_ctx, n_lat, tkl, nct, tile_off, diff, lam_init):
    if diff:
        q_ref, k_ref, v_ref, lq1, lk1, lq2, lk2, dn_ref, o_ref = refs[:9]
    else:
        q_ref, k_ref, v_ref, o_ref = refs[:4]
    nc = nh * g
    m_sc, acc_sc, p_sc = (refs[len(refs) - (3 - n) * nc:len(refs) - (2 - n) * nc] for n in range(3))
    hd = HEAD_DIM
    i = pl.program_id(1) + tile_off
    nblk = (n_ctx + n_lat) // tkl
    chains = [(c, c // g) for c in range(nh * g)]

    def head(ref, rows, h, width=1):
        return ref[rows, h * width * hd:(h + 1) * width * hd]

    def vhead(rows, h):
        return head(v_ref, rows, h, 2)

    def scores(c, kh, rows):
        return lax.dot_general(head(q_ref, slice(None), c), head(k_ref, rows, kh), _NT,
                               preferred_element_type=F32)

    def emit(outputs):
        norm = lambda a: a[:, :hd] / a[:, hd:hd + 1]
        if diff:
            lam = (jnp.exp(jnp.sum(lq1[...] * lk1[...], axis=1, keepdims=True))
                   - jnp.exp(jnp.sum(lq2[...] * lk2[...], axis=1, keepdims=True)) + lam_init)
            for kh in range(nh):
                dd = norm(outputs[2 * kh]) - lam * norm(outputs[2 * kh + 1])
                o_ref[:, kh * hd:(kh + 1) * hd] = (_head_norm(dd, dn_ref[...])
                                                   * (1.0 - lam_init)).astype(BF16)
        else:
            for c, _ in chains:
                o_ref[:, c * hd:(c + 1) * hd] = norm(outputs[c]).astype(BF16)

    @pl.when(i < nct)
    def _():
        ctx_rows = slice(0, n_ctx)
        outs = []
        for c, kh in chains:
            s = scores(c, kh, ctx_rows)
            p = jnp.exp2(s - jnp.max(s, axis=1, keepdims=True))
            outs.append(jnp.dot(p.astype(BF16), vhead(ctx_rows, kh), preferred_element_type=F32))
        emit(outs)

    @pl.when(i >= nct)
    def _():
        for jb in range(nblk):
            rows = pl.ds(jb * tkl, tkl)
            slot = jb % 2
            alphas = []
            for c, kh in chains:
                s = scores(c, kh, rows)
                m_cur = jnp.max(s, axis=1, keepdims=True)
                if jb == 0:
                    m_new = jnp.broadcast_to(m_cur, m_sc[c].shape)
                else:
                    m_prev = m_sc[c][...]
                    m_new = jnp.maximum(m_prev, m_cur)
                    alphas.append(jnp.tile(jnp.exp2(m_prev - m_new), (1, 2)))
                p_sc[c][slot] = jnp.exp2(s - jnp.tile(m_new, (1, tkl // LANES))).astype(BF16)
                m_sc[c][...] = m_new
            if jb == 0:
                continue
            prev = pl.ds((jb - 1) * tkl, tkl)
            for c, kh in chains:
                pv = jnp.dot(p_sc[c][1 - slot], vhead(prev, kh), preferred_element_type=F32)
                acc_sc[c][...] = alphas[c] * (pv if jb == 1 else acc_sc[c][...] + pv)

        last = pl.ds((nblk - 1) * tkl, tkl)
        outs = []
        for c, kh in chains:
            pv = jnp.dot(p_sc[c][(nblk - 1) % 2], vhead(last, kh), preferred_element_type=F32)
            outs.append(pv if nblk == 1 else acc_sc[c][...] + pv)
        emit(outs)


def _flash_call(q, k, v, n_ctx, g, nh, tile_off, extra=None, lam_init=0.0):
    t = k.shape[0]
    hd = HEAD_DIM
    n_kv = k.shape[1] // (nh * hd)
    n_lat = t - n_ctx
    nct = n_ctx // ROW_TILE
    nblk = next(n for n in range(1, t // LANES + 1)
                if t % (n * LANES) == 0 and t // n <= _ATT_KEY_BLOCK)
    tkl = t // nblk
    diff = extra is not None
    ow = nh * hd if diff else nh * g * hd
    resident = pl.Buffered(1)
    in_specs = [pl.BlockSpec((ROW_TILE, nh * g * hd), lambda kv, i: (i + tile_off, kv)),
                pl.BlockSpec((t, nh * hd), lambda kv, i: (0, kv), pipeline_mode=resident),
                pl.BlockSpec((t, 2 * nh * hd), lambda kv, i: (0, kv), pipeline_mode=resident)]
    args = [q, k, v]
    if diff:
        for a in extra:
            a = a.reshape(1, -1)
            in_specs.append(pl.BlockSpec(a.shape, lambda kv, i: (0, 0)))
            args.append(a)
    return pl.pallas_call(
        functools.partial(_flash_kernel, g=g, nh=nh, n_ctx=n_ctx, n_lat=n_lat, tkl=tkl, nct=nct,
                          tile_off=tile_off, diff=diff, lam_init=lam_init),
        out_shape=jax.ShapeDtypeStruct((t - tile_off * ROW_TILE, n_kv * ow), BF16),
        grid=(n_kv, t // ROW_TILE - tile_off),
        in_specs=in_specs,
        out_specs=pl.BlockSpec((ROW_TILE, ow), lambda kv, i: (i, kv)),
        scratch_shapes=([pltpu.VMEM((ROW_TILE, LANES), F32)] * (nh * g)
                        + [pltpu.VMEM((ROW_TILE, 2 * hd), F32)] * (nh * g)
                        + [pltpu.VMEM((2, ROW_TILE, tkl), BF16)] * (nh * g)),
        compiler_params=_params(("arbitrary", "arbitrary"), 48),
        name="diff_attention" if diff else "gqa_attention",
    )(*args)


def _outproj_kernel(of_ref, ob_ref, gate_ref, ng_ref, att_ref, dif_ref, wo_ref, xc_ref, xl_ref, m_ref,
                    n2_ref, rw_ref, xo_ref, h2_ref, lg_ref, *, nct, tile_off):
    hd = HEAD_DIM
    hgw = HG_HEADS * hd
    ng = ng_ref[...]
    gate = gate_ref[...]
    silu_gate = gate * (1.0 / (1.0 + jnp.exp(-gate)))
    acc = jnp.dot(att_ref[...], wo_ref[hgw:hgw + att_ref.shape[1], :], preferred_element_type=F32)
    acc = acc + jnp.dot(dif_ref[...], wo_ref[hgw + att_ref.shape[1]:, :], preferred_element_type=F32)
    hg = []
    for h in range(HG_HEADS):
        o = of_ref[0, :, h * hd:(h + 1) * hd] + ob_ref[0, :, h * hd:(h + 1) * hd]
        hg.append((_head_norm(o, ng) * silu_gate[:, h * hd:(h + 1) * hd]).astype(BF16))
    acc = acc + jnp.dot(jnp.concatenate(hg, axis=1), wo_ref[0:hgw, :], preferred_element_type=F32)
    m = m_ref[0]
    xn = _stream_tile(xc_ref, xl_ref, nct, tile_off) + m[2:3] * acc
    xo_ref[...] = xn
    h2 = _norm_mod(xn, n2_ref[...], m[3:4], m[4:5])
    _to_token_major(h2_ref, h2)
    hs = _split_bf16(h2, 2)
    ws = _split_bf16(rw_ref[...], 2)
    lg = jnp.dot(hs[1], ws[0], preferred_element_type=F32)
    lg = lg + jnp.dot(hs[0], ws[1], preferred_element_type=F32)
    lg_ref[...] = lg + jnp.dot(hs[0], ws[0], preferred_element_type=F32)


def _outproj_call(o, p, ng, att, dif, wo, xc, xl, lat_off, mods, n2, rw, nct, tile_off):
    d = xl.shape[1]
    hd = HEAD_DIM
    hgw = HG_HEADS * hd
    n_rows = att.shape[0]
    row = lambda w: pl.BlockSpec((ROW_TILE, w), lambda i: (i, 0))
    full = lambda a: pl.BlockSpec(a.shape, lambda i: (0,) * a.ndim)
    ng, n2 = ng.reshape(1, hd), n2.reshape(1, d)
    return pl.pallas_call(
        functools.partial(_outproj_kernel, nct=nct, tile_off=tile_off),
        out_shape=[jax.ShapeDtypeStruct((n_rows, d), F32),
                   jax.ShapeDtypeStruct((n_rows * (d // LANES), LANES), F32),
                   jax.ShapeDtypeStruct((n_rows, LANES), F32)],
        grid=(n_rows // ROW_TILE,),
        in_specs=[pl.BlockSpec((1, ROW_TILE, hgw), lambda i: (0, i + tile_off, 0)),
                  pl.BlockSpec((1, ROW_TILE, hgw), lambda i: (1, i + tile_off, 0)),
                  pl.BlockSpec((ROW_TILE, hgw), lambda i: (i + tile_off, 5)),
                  full(ng), row(att.shape[1]), row(dif.shape[1]), full(wo)]
                 + _stream_specs(d, nct, tile_off, lat_off)
                 + [_mod_spec(d, nct, tile_off), full(n2), full(rw)],
        out_specs=[row(d), pl.BlockSpec((ROW_TILE * (d // LANES), LANES), lambda i: (i, 0)), row(LANES)],
        compiler_params=_params(("arbitrary",), 56),
        name="out_proj",
    )(o, o, p, ng, att, dif, wo, xc, xl, mods, n2, rw)


def _router_kernel(lg_ref, bias_ref, r_ref, cnt_ref, run_ref):
    i = pl.program_id(0)
    tm = lg_ref.shape[0]

    @pl.when(i == 0)
    def _():
        run_ref[...] = jnp.zeros_like(run_ref)

    lane = lax.broadcasted_iota(I32, (tm, LANES), 1).astype(F32)
    lg = lg_ref[...] + bias_ref[...]
    ninf = -jnp.inf

    def first_max(vals):
        mx = jnp.max(vals, axis=1, keepdims=True)
        idx = jnp.min(jnp.where(vals == mx, lane, float(LANES)), axis=1, keepdims=True)
        return mx, idx

    gl = jnp.where(lane < N_GROUPS, lg, ninf)
    gmax, gidx = first_max(gl)
    g_top = 1.0 / jnp.sum(jnp.exp(gl - gmax), axis=1, keepdims=True)
    lo = N_GROUPS + EXPERTS_PER_GROUP * gidx
    el = jnp.where((lane >= lo) & (lane < lo + EXPERTS_PER_GROUP), lg, ninf)
    m1, e1 = first_max(el)
    m2, e2 = first_max(jnp.where(lane == e1, ninf, el))
    r = jnp.exp(m2 - m1)
    w1 = g_top / (1.0 + r)
    w2 = g_top * r / (1.0 + r)

    hit = ((lane == e1) | (lane == e2)).astype(BF16)
    ti = lax.broadcasted_iota(I32, (tm, tm), 0)
    si = lax.broadcasted_iota(I32, (tm, tm), 1)
    before = (si < ti).astype(BF16)
    pos = jnp.dot(before, hit, preferred_element_type=F32) + run_ref[0:1, :]
    p1 = jnp.sum(jnp.where(lane == e1, pos, 0.0), axis=1, keepdims=True)
    p2 = jnp.sum(jnp.where(lane == e2, pos, 0.0), axis=1, keepdims=True)
    total = run_ref[0:1, :] + jnp.sum(hit.astype(F32), axis=0, keepdims=True)
    run_ref[...] = jnp.broadcast_to(total, run_ref.shape)
    cnt_ref[...] = jnp.broadcast_to(total, cnt_ref.shape)

    fields = (e1 - N_GROUPS, e2 - N_GROUPS, w1, w2, p1, p2)
    out = jnp.zeros((tm, LANES), F32)
    for n, f in enumerate(fields):
        out = jnp.where(lane == n, f, out)
    r_ref[...] = out


def _router_call(logits, bias):
    n = logits.shape[0]
    return pl.pallas_call(
        _router_kernel,
        out_shape=[jax.ShapeDtypeStruct((n, LANES), F32), jax.ShapeDtypeStruct((8, LANES), F32)],
        grid=(n // ROW_TILE,),
        in_specs=[pl.BlockSpec((ROW_TILE, LANES), lambda i: (i, 0)),
                  pl.BlockSpec((1, LANES), lambda i: (0, 0))],
        out_specs=[pl.BlockSpec((ROW_TILE, LANES), lambda i: (i, 0)),
                   pl.BlockSpec((8, LANES), lambda i: (0, 0))],
        scratch_shapes=[pltpu.VMEM((8, LANES), F32)],
        compiler_params=_params(("arbitrary",), 16),
        name="router",
    )(logits, bias)


def _moe_kernel(src_ref, te_ref, nx_ref, ng_ref, nt_ref, h_hbm, wg_hbm, wu_hbm, wd_hbm, y_ref,
                xbuf, sem, wg_st, wu_st, wd_st, wsem, wg_bf, wu_bf, wd_bf, *, layer):
    i = pl.program_id(0)
    n_tiles = nt_ref[0]
    tm = MOE_TILE
    k = xbuf.shape[1] // tm
    grp_rows = _DMA_GROUP * k

    def weight_copies(e):
        pairs = ((wg_hbm, wg_st), (wu_hbm, wu_st), (wd_hbm, wd_st))
        return [pltpu.make_async_copy(w.at[layer, e], st, wsem.at[n]) for n, (w, st) in enumerate(pairs)]

    def start_tile(tile, slot):
        def body(grp, carry):
            for u in range(_DMA_GROUP):
                r = grp * _DMA_GROUP + u
                src_row = pl.multiple_of(src_ref[tile * tm + r] * k, k)
                pltpu.make_async_copy(h_hbm.at[pl.ds(src_row, k), :],
                                      xbuf.at[slot, pl.ds(pl.multiple_of(r * k, k), k), :],
                                      sem.at[slot]).start(priority=1)
            return carry
        lax.fori_loop(0, ng_ref[tile], body, 0)

    def wait_tile(tile, slot):
        def body(grp, carry):
            pltpu.make_async_copy(h_hbm.at[pl.ds(0, grp_rows), :], xbuf.at[slot, pl.ds(0, grp_rows), :],
                                  sem.at[slot]).wait()
            return carry
        lax.fori_loop(0, ng_ref[tile], body, 0)

    @pl.when(i == 0)
    def _():
        xbuf[...] = jnp.zeros_like(xbuf)
        start_tile(0, 0)
        for cp in weight_copies(te_ref[0]):
            cp.start()

    @pl.when(i < n_tiles)
    def _():
        slot = i % 2

        @pl.when(i + 1 < n_tiles)
        def _():
            start_tile(i + 1, 1 - slot)

        changed = jnp.logical_or(i == 0, te_ref[i] != te_ref[jnp.maximum(i - 1, 0)])

        @pl.when(changed)
        def _():
            for cp in weight_copies(te_ref[i]):
                cp.wait()
            wg_bf[...] = wg_st[...].astype(BF16)
            wu_bf[...] = wu_st[...].astype(BF16)
            wd_bf[...] = wd_st[...].astype(BF16)

            @pl.when(nx_ref[i] >= 0)
            def _():
                for cp in weight_copies(nx_ref[i]):
                    cp.start()

        wait_tile(i, slot)
        xb = _from_token_major(xbuf.at[slot], tm).astype(BF16)
        gt = jnp.dot(xb, wg_bf[...], preferred_element_type=F32)
        up = jnp.dot(xb, wu_bf[...], preferred_element_type=F32)
        act = (gt * (1.0 / (1.0 + jnp.exp(-gt))) * up).astype(BF16)
        _to_token_major(y_ref, jnp.dot(act, wd_bf[...], preferred_element_type=F32))

    @pl.when(i >= n_tiles)
    def _():
        y_ref[...] = jnp.zeros_like(y_ref)


def _moe_call(h2, src, tile_expert, next_expert, tile_groups, n_tiles, wg, wu, wd, layer):
    d, ff = wg.shape[2], wg.shape[3]
    k = d // LANES
    max_tiles = tile_expert.shape[0]
    tm = MOE_TILE
    hbm = pl.BlockSpec(memory_space=pl.ANY)
    grid_spec = pltpu.PrefetchScalarGridSpec(
        num_scalar_prefetch=5,
        grid=(max_tiles,),
        in_specs=[hbm, hbm, hbm, hbm],
        out_specs=pl.BlockSpec((tm * k, LANES), lambda i, src, te, nx, ng, nt: (i, 0)),
        scratch_shapes=[pltpu.VMEM((2, tm * k, LANES), F32), pltpu.SemaphoreType.DMA((2,)),
                        pltpu.VMEM((d, ff), F32), pltpu.VMEM((d, ff), F32), pltpu.VMEM((ff, d), F32),
                        pltpu.SemaphoreType.DMA((3,)),
                        pltpu.VMEM((d, ff), BF16), pltpu.VMEM((d, ff), BF16), pltpu.VMEM((ff, d), BF16)])
    return pl.pallas_call(
        functools.partial(_moe_kernel, layer=layer),
        out_shape=jax.ShapeDtypeStruct((max_tiles * tm * k, LANES), F32),
        grid_spec=grid_spec,
        compiler_params=_params(("arbitrary",), 40),
        name="moe_experts",
    )(src, tile_expert, next_expert, tile_groups, n_tiles, h2, wg, wu, wd)


def _combine_kernel(dst_ref, x_ref, rt_ref, m_ref, g_ref, mn_ref, y_hbm, *rest, last, n_out):
    outs, (ybuf, sem) = rest[:n_out], rest[n_out:]
    i = pl.program_id(0)
    tm = x_ref.shape[0]
    k = ybuf.shape[2] // tm

    def start_tile(tile, slot):
        def body(grp, carry):
            for u in range(_DMA_GROUP):
                r = grp * _DMA_GROUP + u
                for j in range(2):
                    src_row = pl.multiple_of(dst_ref[(tile * tm + r) * 2 + j] * k, k)
                    pltpu.make_async_copy(y_hbm.at[pl.ds(src_row, k), :],
                                          ybuf.at[slot, j, pl.ds(pl.multiple_of(r * k, k), k), :],
                                          sem.at[slot]).start(priority=j)
            return carry
        lax.fori_loop(0, tm // _DMA_GROUP, body, 0)

    @pl.when(i == 0)
    def _():
        start_tile(0, 0)

    slot = i % 2

    @pl.when(i + 1 < pl.num_programs(0))
    def _():
        start_tile(i + 1, 1 - slot)

    for j in range(2):
        pltpu.make_async_copy(y_hbm.at[pl.ds(0, tm * k), :], ybuf.at[slot, j], sem.at[slot]).wait()
    rt = rt_ref[...]
    y = (rt[:, 2:3] * _from_token_major(ybuf.at[slot, 0], tm)
         + rt[:, 3:4] * _from_token_major(ybuf.at[slot, 1], tm))
    xn = x_ref[...] + m_ref[0][5:6] * y
    if last:
        outs[0][...] = xn * lax.rsqrt(jnp.mean(xn * xn, axis=-1, keepdims=True) + EPS) * g_ref[...]
    else:
        outs[0][...] = xn
        mn = mn_ref[0]
        outs[1][...] = _norm_mod(xn, g_ref[...], mn[0:1], mn[1:2]).astype(BF16)


def _combine_call(dst, x, route, mods, g, mods_next, ys, nct, last):
    n, d = x.shape
    tm = ROW_TILE
    mspec = pl.BlockSpec((1, N_MOD, d), lambda i, dst: (jnp.where(i < nct, 1, 0), 0, 0))
    out_shape = [jax.ShapeDtypeStruct((n, d), F32)]
    if not last:
        out_shape.append(jax.ShapeDtypeStruct((n, d), BF16))
    row = lambda w: pl.BlockSpec((tm, w), lambda i, dst: (i, 0))
    grid_spec = pltpu.PrefetchScalarGridSpec(
        num_scalar_prefetch=1,
        grid=(n // tm,),
        in_specs=[row(d), row(LANES), mspec, pl.BlockSpec((1, d), lambda i, dst: (0, 0)), mspec,
                  pl.BlockSpec(memory_space=pl.ANY)],
        out_specs=[row(d)] * len(out_shape),
        scratch_shapes=[pltpu.VMEM((2, 2, tm * (d // LANES), LANES), F32), pltpu.SemaphoreType.DMA((2,))])
    return pl.pallas_call(
        functools.partial(_combine_kernel, last=last, n_out=len(out_shape)),
        out_shape=out_shape,
        grid_spec=grid_spec,
        compiler_params=_params(("arbitrary",), 40),
        name="moe_combine",
    )(dst, x, route, mods, g.reshape(1, d), mods_next, ys)


def _dispatch_plan(route, counts, n_tokens):
    tm = MOE_TILE
    max_tiles = (2 * n_tokens) // tm + N_EXPERTS
    cnt = counts[0, N_GROUPS:N_GROUPS + N_EXPERTS].astype(I32)
    tiles_per = (cnt + tm - 1) // tm
    tile_end = jnp.cumsum(tiles_per)
    offs = (tile_end - tiles_per) * tm
    eid = route[:, 0:2].astype(I32)
    e_ids = jnp.arange(N_EXPERTS, dtype=I32)
    offs_of = jnp.sum(jnp.where(eid[..., None] == e_ids, offs, 0), axis=-1)
    dst = offs_of + route[:, 4:6].astype(I32)
    tok = jnp.broadcast_to(jnp.arange(n_tokens, dtype=I32)[:, None], (n_tokens, 2))
    src = jnp.zeros((max_tiles * tm,), I32).at[dst.reshape(-1)].set(tok.reshape(-1))
    n_tiles = tile_end[-1:]
    tile_ids = jnp.minimum(jnp.arange(max_tiles, dtype=I32), n_tiles[0] - 1)
    tile_expert = jnp.sum((tile_end[None, :] <= tile_ids[:, None]).astype(I32), axis=1)
    later = (e_ids[None, :] > e_ids[:, None]) & (tiles_per[None, :] > 0)
    nxt = jnp.min(jnp.where(later, e_ids[None, :], N_EXPERTS), axis=1)
    next_expert = jnp.where(nxt < N_EXPERTS, nxt, -1)[tile_expert]
    within = tile_ids - (tile_end - tiles_per)[tile_expert]
    rows = jnp.clip(cnt[tile_expert] - within * tm, 0, tm)
    tile_groups = (rows + _DMA_GROUP - 1) // _DMA_GROUP
    return dst.reshape(-1), src, tile_expert, next_expert, tile_groups.astype(I32), n_tiles.astype(I32)


def kernel(x, c, ctx, c_ctx, w_mod, b_mod, norm1_g, norm2_g, w_in, w_out, hg_lb_logits, hg_norm_g,
           q_norm_g, k_norm_g, lam_q1, lam_k1, lam_q2, lam_k2, diff_norm_g, router_group_w,
           router_group_b, router_expert_w, router_expert_b, w_gate, w_up, w_down, final_norm_g):
    depth = w_in.shape[0]
    n_lat, d = x.shape[1], x.shape[2]
    n_ctx = ctx.shape[1]
    nct = n_ctx // ROW_TILE
    assert x.shape[0] == 1 and n_ctx % ROW_TILE == 0 and n_lat % 1024 == 0

    stream = (ctx[0], x[0], 0)
    cc = jnp.zeros((8, d), F32).at[0].set(c[0]).at[1].set(c_ctx)
    mods_all = _mod_call(cc, w_mod, b_mod)[:, :2].reshape(depth, 2, N_MOD, d)
    tabs = _rope_tables(n_lat, n_ctx, HEAD_DIM) + _rope_tables(n_lat, n_ctx, HEAD_DIM // 2)

    h1 = _prenorm_call(stream[0], stream[1], norm1_g[0], mods_all[0], nct)
    out = None
    for l in range(depth):
        last = l == depth - 1
        mods = mods_all[l]
        tile_off = nct if last else 0
        tm_in = next(tm for tm in (1408, 768, ROW_TILE) if (n_ctx + n_lat) % tm == 0)
        p = _matmul_call(h1, w_in, l, tm_in, 1024)
        o = _hgrn_call(p, hg_lb_logits, l, nct)
        qa, ka, va, qd, kd, vd = _attn_prep_call(p, tabs, q_norm_g[l], k_norm_g[l])
        att = _flash_call(qa, ka, va, n_ctx, GQA_HEADS // GQA_KV_HEADS, 1, tile_off)
        lam_init = 0.8 - 0.6 * math.exp(-0.3 * l)
        dif = _flash_call(qd, kd, vd, n_ctx, 2, 2, tile_off,
                          extra=(lam_q1[l], lam_k1[l], lam_q2[l], lam_k2[l], diff_norm_g[l]),
                          lam_init=lam_init)
        rw = jnp.concatenate([router_group_w[l], router_expert_w[l],
                              jnp.zeros((d, LANES - N_GROUPS - N_EXPERTS), F32)], axis=1)
        rb = jnp.concatenate([router_group_b[l], router_expert_b[l],
                              jnp.zeros((LANES - N_GROUPS - N_EXPERTS,), F32)]).reshape(1, LANES)
        xn, h2, logits = _outproj_call(o, p, hg_norm_g[l], att, dif, w_out[l].astype(BF16), *stream,
                                       mods, norm2_g[l], rw, nct, tile_off)
        n_tok = xn.shape[0]
        route, counts = _router_call(logits, rb)
        dst, src, tile_expert, next_expert, tile_groups, n_tiles = _dispatch_plan(route, counts, n_tok)
        ys = _moe_call(h2, src, tile_expert, next_expert, tile_groups, n_tiles, w_gate, w_up, w_down, l)
        if last:
            (out,) = _combine_call(dst, xn, route, mods, final_norm_g, mods, ys, 0, True)
        else:
            xall, h1 = _combine_call(dst, xn, route, mods, norm1_g[l + 1], mods_all[l + 1], ys, nct, False)
            stream = (xall, xall, nct)
    return out.reshape(1, n_lat, d)
```

```python
import functools
import math

import numpy as np
import jax
import jax.numpy as jnp
from jax import lax
from jax.experimental import pallas as pl
from jax.experimental.pallas import tpu as pltpu

F32 = jnp.float32
BF16 = jnp.bfloat16
I32 = jnp.int32

HEAD_DIM = 128
LANES = 128
GRID_W = 64
ROPE_THETA = 10000.0
EPS = 1e-6
HG_HEADS = 4
HG_CHUNK = 64
GQA_HEADS = 8
GQA_KV_HEADS = 2
DIFF_HEADS = 4
N_GROUPS = 4
EXPERTS_PER_GROUP = 8
N_EXPERTS = N_GROUPS * EXPERTS_PER_GROUP
N_MOD = 6
ROW_TILE = 256
MOE_TILE = 256
_IN_PROJ_ROW_TILES = (1408, 768, ROW_TILE)
_WEIGHT_COL_TILE = 1024
_DMA_GROUP = 8
MIB = 1024 * 1024

_LOG2E = math.log2(math.e)
_ATT_KEY_BLOCK = 2816
_NT = (((1,), (1,)), ((), ()))
_TN = (((0,), (0,)), ((), ()))


def _params(semantics, vmem_mib):
    return pltpu.CompilerParams(dimension_semantics=semantics, vmem_limit_bytes=vmem_mib * MIB)


def _to_token_major(ref, x):
    n, d = x.shape
    k = d // LANES
    for s in range(k):
        ref[pl.ds(s, n, stride=k), :] = x[:, s * LANES:(s + 1) * LANES]


def _from_token_major(ref, n):
    k = ref.shape[0] // n
    return jnp.concatenate([ref[pl.ds(s, n, stride=k), :] for s in range(k)], axis=1)


def _split_bf16(x, parts):
    out = []
    for _ in range(parts - 1):
        p = x.astype(BF16)
        out.append(p)
        x = x - p.astype(F32)
    out.append(x.astype(BF16))
    return out


def _mod_kernel(a_ref, w_ref, b_ref, o_ref):
    a = a_ref[...]
    a = a * (1.0 / (1.0 + jnp.exp(-a)))
    hi, lo = _split_bf16(a, 2)
    w = w_ref[0].astype(BF16)
    o_ref[0] = (jnp.dot(hi, w, preferred_element_type=F32)
                + jnp.dot(lo, w, preferred_element_type=F32) + b_ref[0])


def _mod_call(cc, w_mod, b_mod):
    depth, d, n = w_mod.shape
    tn = _WEIGHT_COL_TILE
    return pl.pallas_call(
        _mod_kernel,
        out_shape=jax.ShapeDtypeStruct((depth, 8, n), F32),
        grid=(depth, n // tn),
        in_specs=[pl.BlockSpec((8, d), lambda l, j: (0, 0)),
                  pl.BlockSpec((1, d, tn), lambda l, j: (l, 0, j)),
                  pl.BlockSpec((1, 1, tn), lambda l, j: (l, 0, j))],
        out_specs=pl.BlockSpec((1, 8, tn), lambda l, j: (l, 0, j)),
        compiler_params=_params(("arbitrary", "arbitrary"), 40),
        name="mod_vectors",
    )(cc, w_mod, b_mod.reshape(depth, 1, n))


def _norm_mod(x, g, shift, scale):
    y = x * lax.rsqrt(jnp.mean(x * x, axis=-1, keepdims=True) + EPS) * g
    return y * (1.0 + scale) + shift


def _stream_specs(d, nct, tile_off=0, lat_off=0):
    return [pl.BlockSpec((ROW_TILE, d), lambda i: (jnp.minimum(i + tile_off, nct - 1), 0)),
            pl.BlockSpec((ROW_TILE, d), lambda i: (jnp.maximum(i + tile_off - nct, 0) + lat_off, 0))]


def _stream_tile(c_ref, x_ref, nct, tile_off=0):
    return jnp.where(pl.program_id(0) + tile_off < nct, c_ref[...], x_ref[...])


def _prenorm_kernel(c_ref, x_ref, g_ref, m_ref, o_ref, *, nct):
    m = m_ref[0]
    o_ref[...] = _norm_mod(_stream_tile(c_ref, x_ref, nct), g_ref[...], m[0:1], m[1:2]).astype(BF16)


def _mod_spec(d, nct, tile_off=0):
    return pl.BlockSpec((1, N_MOD, d), lambda i: (jnp.where(i + tile_off < nct, 1, 0), 0, 0))


def _prenorm_call(xc, xl, g, mods, nct):
    d = xl.shape[1]
    t = xc.shape[0] + xl.shape[0]
    return pl.pallas_call(
        functools.partial(_prenorm_kernel, nct=nct),
        out_shape=jax.ShapeDtypeStruct((t, d), BF16),
        grid=(t // ROW_TILE,),
        in_specs=_stream_specs(d, nct) + [pl.BlockSpec((1, d), lambda i: (0, 0)), _mod_spec(d, nct)],
        out_specs=pl.BlockSpec((ROW_TILE, d), lambda i: (i, 0)),
        compiler_params=_params(("arbitrary",), 24),
        name="prenorm",
    )(xc, xl, g.reshape(1, d), mods)


def _mm_kernel(a_ref, b_ref, o_ref, b_bf):
    @pl.when(pl.program_id(1) == 0)
    def _():
        b_bf[...] = b_ref[0].astype(BF16)

    o_ref[...] = jnp.dot(a_ref[...], b_bf[...], preferred_element_type=F32)


def _matmul_call(a, b, layer, tm, tn):
    m, k = a.shape
    n = b.shape[2]
    return pl.pallas_call(
        _mm_kernel,
        out_shape=jax.ShapeDtypeStruct((m, n), F32),
        grid=(n // tn, m // tm),
        in_specs=[pl.BlockSpec((tm, k), lambda j, i: (i, 0)),
                  pl.BlockSpec((1, k, tn), lambda j, i: (layer, 0, j))],
        out_specs=pl.BlockSpec((tm, tn), lambda j, i: (i, j)),
        scratch_shapes=[pltpu.VMEM((k, tn), BF16)],
        compiler_params=_params(("arbitrary", "arbitrary"), 56),
        name="in_proj",
    )(a, b)


_HG_LEVELS = (1, 2, 4, 8, 16, 32)
_HG_TOT_ROW = HG_CHUNK * (len(_HG_LEVELS) + 1)
_HG_W_ROWS = _HG_TOT_ROW + 16
_HG_HEADS_PER_STEP = 4


def _hgrn_consts():
    c = HG_CHUNK
    w = np.zeros((2, _HG_W_ROWS, c), np.float32)
    msk = np.zeros((2, len(_HG_LEVELS) + 1, c, c), np.float32)
    for d in range(2):
        u = np.arange(c) if d == 0 else c - 1 - np.arange(c)
        ut, us = u[:, None], u[None, :]
        w[d, :c] = us <= ut
        for li, lv in enumerate(_HG_LEVELS):
            blk = u // (2 * lv)
            qside = (u % (2 * lv)) >= lv
            bnd = (blk * 2 * lv + lv - 1)[:, None]
            wq = (us > bnd) & (us <= ut)
            wk = (us > ut) & (us <= bnd)
            w[d, c * (li + 1):c * (li + 2)] = np.where(qside[:, None], wq, -1.0 * wk)
            msk[d, li] = (blk[:, None] == blk[None, :]) & qside[:, None] & ~qside[None, :]
        msk[d, len(_HG_LEVELS)] = np.eye(c)
        w[d, _HG_TOT_ROW:] = 1.0
    return jnp.asarray(np.concatenate([w, w, w], axis=2), BF16), jnp.asarray(msk, F32)


def _hgrn_kernel(lbl_ref, q_ref, z_ref, v_ref, w_ref, msk_ref, o_ref, st_ref, *, layer, chunks, hps):
    c = HG_CHUNK
    hd = HEAD_DIM
    d = pl.program_id(0)
    j = pl.program_id(2)

    @pl.when(j == 0)
    def _():
        st_ref[...] = jnp.zeros_like(st_ref)

    lbl = lbl_ref[...]
    rows = [lbl[i:i + 1] for i in range(lbl.shape[0])]
    mx = functools.reduce(jnp.maximum, rows)
    ex = [jnp.exp(r - mx) for r in rows]
    tot = functools.reduce(lambda a, b: a + b, ex)
    lb = jnp.zeros_like(mx)
    for i in range(1, layer + 1):
        lb = lb + ex[i] / tot
    log_lb = jnp.log(lb)
    log_1m_lb = jnp.log1p(-lb)

    wmat = w_ref[0]
    nlev = len(_HG_LEVELS)

    def finish(h, hs, r0, amat, v_bf, qe, kd, decay):
        st = st_ref[h]
        o = (jnp.dot(amat, v_bf[:, hs], preferred_element_type=F32)
             + lax.dot_general(qe[:, hs], st.astype(BF16), _NT, preferred_element_type=F32))
        o_ref[0, pl.ds(r0, c), hs] = o
        st_ref[h] = st * decay[:, hs] + lax.dot_general(v_bf[:, hs], kd[:, hs], _TN,
                                                        preferred_element_type=F32)

    pending = None
    for ci in range(chunks):
        cc = ci + d * (chunks - 1 - 2 * ci)
        r0 = pl.multiple_of(cc * c, c)
        q = q_ref[pl.ds(r0, c), :]
        z = z_ref[pl.ds(r0, c), :]
        v_bf = v_ref[pl.ds(r0, c), :].astype(BF16)
        l1p = jnp.log(1.0 + jnp.exp(-jnp.abs(z)))
        ls_pos = jnp.minimum(z, 0.0) - l1p
        ls_neg = jnp.minimum(-z, 0.0) - l1p
        a2 = log_1m_lb + ls_pos
        logf = jnp.maximum(log_lb, a2) + jnp.log(1.0 + jnp.exp(-jnp.abs(log_lb - a2)))
        k = (1.0 - lb) * jnp.exp(ls_neg)

        parts = jnp.concatenate(_split_bf16(logf, 3), axis=0)
        sums = jnp.dot(wmat, parts, preferred_element_type=F32)
        b = sums[0:c]
        btot = sums[_HG_TOT_ROW:_HG_TOT_ROW + 1]

        qls, kls = [q.astype(BF16)], [k.astype(BF16)]
        for li in range(nlev):
            fl = jnp.exp(-jnp.abs(sums[c * (li + 1):c * (li + 2)]))
            qls.append((q * fl).astype(BF16))
            kls.append((k * fl).astype(BF16))
        qe = (q * jnp.exp(b)).astype(BF16)
        kd = (k * jnp.exp(btot - b)).astype(BF16)
        decay = jnp.exp(btot)

        for h in range(hps):
            hs = slice(h * hd, (h + 1) * hd)
            amat = lax.dot_general(qls[0][:, hs], kls[0][:, hs], _NT,
                                   preferred_element_type=F32) * msk_ref[0, nlev]
            for li in range(nlev):
                amat = amat + lax.dot_general(qls[li + 1][:, hs], kls[li + 1][:, hs], _NT,
                                              preferred_element_type=F32) * msk_ref[0, li]
            if pending is not None:
                finish(*pending)
            pending = (h, hs, r0, amat.astype(BF16), v_bf, qe, kd, decay)
    finish(*pending)


def _hgrn_call(p, lb_logits, layer, nct):
    t = p.shape[0]
    nblk = t // ROW_TILE
    chunks = ROW_TILE // HG_CHUNK
    hps = _HG_HEADS_PER_STEP
    hw = hps * HEAD_DIM
    ng = HG_HEADS // hps
    wmat, msk = _hgrn_consts()

    def blk(d, j):
        back = jnp.where(j < nct, nct - 1 - j, nblk - 1 - (j - nct))
        return jnp.where(d == 0, j, back)

    return pl.pallas_call(
        functools.partial(_hgrn_kernel, layer=layer, chunks=chunks, hps=hps),
        out_shape=jax.ShapeDtypeStruct((2, t, HG_HEADS * HEAD_DIM), F32),
        grid=(2, ng, nblk),
        in_specs=[pl.BlockSpec((lb_logits.shape[0], hw), lambda d, h, j: (0, h)),
                  pl.BlockSpec((ROW_TILE, hw), lambda d, h, j: (blk(d, j), d * ng + h)),
                  pl.BlockSpec((ROW_TILE, hw), lambda d, h, j: (blk(d, j), (2 + d) * ng + h)),
                  pl.BlockSpec((ROW_TILE, hw), lambda d, h, j: (blk(d, j), 4 * ng + h)),
                  pl.BlockSpec((1, _HG_W_ROWS, 3 * HG_CHUNK), lambda d, h, j: (d, 0, 0)),
                  pl.BlockSpec((1, len(_HG_LEVELS) + 1, HG_CHUNK, HG_CHUNK), lambda d, h, j: (d, 0, 0, 0))],
        out_specs=pl.BlockSpec((1, ROW_TILE, hw), lambda d, h, j: (d, blk(d, j), h)),
        scratch_shapes=[pltpu.VMEM((hps, HEAD_DIM, HEAD_DIM), F32)],
        compiler_params=_params(("arbitrary", "arbitrary", "arbitrary"), 24),
        name="hgrn_scan",
    )(lb_logits, p, p, p, wmat, msk)


def _rope_tables(n_lat, n_ctx, dim):
    rows = n_lat // GRID_W
    row = jnp.repeat(jnp.arange(rows, dtype=F32), GRID_W)
    col = jnp.tile(jnp.arange(GRID_W, dtype=F32), rows)
    axis_dim = dim // 2
    inv_freq = ROPE_THETA ** (-jnp.arange(0, axis_dim, 2, dtype=F32) / axis_dim)
    ang = jnp.concatenate([row[:, None] * inv_freq, col[:, None] * inv_freq], axis=-1)
    cos = jnp.repeat(jnp.cos(ang), 2, axis=1)
    sin = jnp.repeat(jnp.sin(ang), 2, axis=1) * jnp.tile(jnp.asarray([-1.0, 1.0], F32), dim // 2)
    reps = LANES // dim
    cos = jnp.concatenate([jnp.ones((n_ctx, dim), F32), cos], axis=0)
    sin = jnp.concatenate([jnp.zeros((n_ctx, dim), F32), sin], axis=0)
    return jnp.tile(cos, (1, reps)), jnp.tile(sin, (1, reps))


def _rope(x, cos, sin):
    lane = lax.broadcasted_iota(I32, x.shape, 1)
    swapped = jnp.where((lane & 1) == 0, pltpu.roll(x, LANES - 1, 1), pltpu.roll(x, 1, 1))
    return x * cos + swapped * sin


def _head_norm(x, g):
    return x * lax.rsqrt(jnp.mean(x * x, axis=-1, keepdims=True) + EPS) * g


def _attn_prep_kernel(pq_ref, pkv_ref, pd_ref, ca_ref, sa_ref, cd_ref, sd_ref, qn_ref, kn_ref,
                      qa_ref, ka_ref, va_ref, qd_ref, kd_ref, vd_ref):
    hd = HEAD_DIM
    ca, sa, cd, sd = ca_ref[...], sa_ref[...], cd_ref[...], sd_ref[...]
    qn, kn = qn_ref[...], kn_ref[...]
    for h in range(GQA_HEADS):
        xq = _rope(_head_norm(pq_ref[:, h * hd:(h + 1) * hd], qn), ca, sa)
        qa_ref[:, h * hd:(h + 1) * hd] = (xq * (hd ** -0.5 * _LOG2E)).astype(BF16)
    for h in range(GQA_KV_HEADS):
        xk = _rope(_head_norm(pkv_ref[:, h * hd:(h + 1) * hd], kn), ca, sa)
        ka_ref[:, h * hd:(h + 1) * hd] = xk.astype(BF16)
    ones_col = jnp.where(lax.broadcasted_iota(I32, (pq_ref.shape[0], hd), 1) == 0, 1.0, 0.0).astype(BF16)
    for h in range(GQA_KV_HEADS):
        va_ref[:, (2 * h) * hd:(2 * h + 1) * hd] = pkv_ref[:, (GQA_KV_HEADS + h) * hd:
                                                            (GQA_KV_HEADS + h + 1) * hd].astype(BF16)
        va_ref[:, (2 * h + 1) * hd:(2 * h + 2) * hd] = ones_col
    first = lax.broadcasted_iota(I32, (pq_ref.shape[0], hd), 1) < hd // 2
    dw = DIFF_HEADS * hd
    for h in range(DIFF_HEADS):
        xq = _rope(pd_ref[:, h * hd:(h + 1) * hd], cd, sd) * ((hd // 2) ** -0.5 * _LOG2E)
        qd_ref[:, (2 * h) * hd:(2 * h + 1) * hd] = jnp.where(first, xq, 0.0).astype(BF16)
        qd_ref[:, (2 * h + 1) * hd:(2 * h + 2) * hd] = jnp.where(first, 0.0, xq).astype(BF16)
        xk = _rope(pd_ref[:, dw + h * hd:dw + (h + 1) * hd], cd, sd)
        kd_ref[:, h * hd:(h + 1) * hd] = xk.astype(BF16)
    for h in range(DIFF_HEADS):
        vd_ref[:, (2 * h) * hd:(2 * h + 1) * hd] = pd_ref[:, 2 * dw + h * hd:2 * dw + (h + 1) * hd].astype(BF16)
        vd_ref[:, (2 * h + 1) * hd:(2 * h + 2) * hd] = ones_col


def _attn_prep_call(p, tabs, qn, kn):
    t = p.shape[0]
    hd = HEAD_DIM
    gq, gkv, dw = GQA_HEADS * hd, 2 * GQA_KV_HEADS * hd, DIFF_HEADS * hd
    q_off = 6 * HG_HEADS * hd
    assert q_off % gq == 0 and (q_off + gq) % gkv == 0 and (q_off + gq + gkv) % (3 * dw) == 0
    tp = next(n for n in (768, ROW_TILE) if t % n == 0)
    row = lambda w: pl.BlockSpec((tp, w), lambda i: (i, 0))
    vec = pl.BlockSpec((1, hd), lambda i: (0, 0))
    return pl.pallas_call(
        _attn_prep_kernel,
        out_shape=[jax.ShapeDtypeStruct((t, gq), BF16),
                   jax.ShapeDtypeStruct((t, gkv // 2), BF16),
                   jax.ShapeDtypeStruct((t, gkv), BF16),
                   jax.ShapeDtypeStruct((t, 2 * dw), BF16),
                   jax.ShapeDtypeStruct((t, dw), BF16),
                   jax.ShapeDtypeStruct((t, 2 * dw), BF16)],
        grid=(t // tp,),
        in_specs=[pl.BlockSpec((tp, gq), lambda i: (i, q_off // gq)),
                  pl.BlockSpec((tp, gkv), lambda i: (i, (q_off + gq) // gkv)),
                  pl.BlockSpec((tp, 3 * dw), lambda i: (i, (q_off + gq + gkv) // (3 * dw))),
                  row(hd), row(hd), row(hd), row(hd), vec, vec],
        out_specs=[row(gq), row(gkv // 2), row(gkv), row(2 * dw), row(dw), row(2 * dw)],
        compiler_params=_params(("arbitrary",), 48),
        name="attn_prep",
    )(p, p, p, *tabs, qn.reshape(1, hd), kn.reshape(1, hd))


def _flash_kernel(*refs, g, nh, n_ctx, n_lat, tkl, nct, tile_off, diff, lam_init):
    if diff:
        q_ref, k_ref, v_ref, lq1, lk1, lq2, lk2, dn_ref, o_ref = refs[:9]
    else:
        q_ref, k_ref, v_ref, o_ref = refs[:4]
    nc = nh * g
    m_sc, acc_sc, p_sc = (refs[len(refs) - (3 - n) * nc:len(refs) - (2 - n) * nc] for n in range(3))
    hd = HEAD_DIM
    i = pl.program_id(1) + tile_off
    nblk = (n_ctx + n_lat) // tkl
    chains = [(c, c // g) for c in range(nh * g)]

    def head(ref, rows, h, width=1):
        return ref[rows, h * width * hd:(h + 1) * width * hd]

    def vhead(rows, h):
        return head(v_ref, rows, h, 2)

    def scores(c, kh, rows):
        return lax.dot_general(head(q_ref, slice(None), c), head(k_ref, rows, kh), _NT,
                               preferred_element_type=F32)

    def emit(outputs):
        norm = lambda a: a[:, :hd] / a[:, hd:hd + 1]
        if diff:
            lam = (jnp.exp(jnp.sum(lq1[...] * lk1[...], axis=1, keepdims=True))
                   - jnp.exp(jnp.sum(lq2[...] * lk2[...], axis=1, keepdims=True)) + lam_init)
            for kh in range(nh):
                dd = norm(outputs[2 * kh]) - lam * norm(outputs[2 * kh + 1])
                o_ref[:, kh * hd:(kh + 1) * hd] = (_head_norm(dd, dn_ref[...])
                                                   * (1.0 - lam_init)).astype(BF16)
        else:
            for c, _ in chains:
                o_ref[:, c * hd:(c + 1) * hd] = norm(outputs[c]).astype(BF16)

    @pl.when(i < nct)
    def _():
        ctx_rows = slice(0, n_ctx)
        outs = []
        for c, kh in chains:
            s = scores(c, kh, ctx_rows)
            p = jnp.exp2(s - jnp.max(s, axis=1, keepdims=True))
            outs.append(jnp.dot(p.astype(BF16), vhead(ctx_rows, kh), preferred_element_type=F32))
        emit(outs)

    @pl.when(i >= nct)
    def _():
        for jb in range(nblk):
            rows = pl.ds(jb * tkl, tkl)
            slot = jb % 2
            alphas = []
            for c, kh in chains:
                s = scores(c, kh, rows)
                m_cur = jnp.max(s, axis=1, keepdims=True)
                if jb == 0:
                    m_new = jnp.broadcast_to(m_cur, m_sc[c].shape)
                else:
                    m_prev = m_sc[c][...]
                    m_new = jnp.maximum(m_prev, m_cur)
                    alphas.append(jnp.tile(jnp.exp2(m_prev - m_new), (1, 2)))
                p_sc[c][slot] = jnp.exp2(s - jnp.tile(m_new, (1, tkl // LANES))).astype(BF16)
                m_sc[c][...] = m_new
            if jb == 0:
                continue
            prev = pl.ds((jb - 1) * tkl, tkl)
            for c, kh in chains:
                pv = jnp.dot(p_sc[c][1 - slot], vhead(prev, kh), preferred_element_type=F32)
                acc_sc[c][...] = alphas[c] * (pv if jb == 1 else acc_sc[c][...] + pv)

        last = pl.ds((nblk - 1) * tkl, tkl)
        outs = []
        for c, kh in chains:
            pv = jnp.dot(p_sc[c][(nblk - 1) % 2], vhead(last, kh), preferred_element_type=F32)
            outs.append(pv if nblk == 1 else acc_sc[c][...] + pv)
        emit(outs)


def _flash_call(q, k, v, n_ctx, g, nh, tile_off, extra=None, lam_init=0.0):
    t = k.shape[0]
    hd = HEAD_DIM
    n_kv = k.shape[1] // (nh * hd)
    n_lat = t - n_ctx
    nct = n_ctx // ROW_TILE
    nblk = next(n for n in range(1, t // LANES + 1)
                if t % (n * LANES) == 0 and t // n <= _ATT_KEY_BLOCK)
    tkl = t // nblk
    diff = extra is not None
    ow = nh * hd if diff else nh * g * hd
    resident = pl.Buffered(1)
    in_specs = [pl.BlockSpec((ROW_TILE, nh * g * hd), lambda kv, i: (i + tile_off, kv)),
                pl.BlockSpec((t, nh * hd), lambda kv, i: (0, kv), pipeline_mode=resident),
                pl.BlockSpec((t, 2 * nh * hd), lambda kv, i: (0, kv), pipeline_mode=resident)]
    args = [q, k, v]
    if diff:
        for a in extra:
            a = a.reshape(1, -1)
            in_specs.append(pl.BlockSpec(a.shape, lambda kv, i: (0, 0)))
            args.append(a)
    return pl.pallas_call(
        functools.partial(_flash_kernel, g=g, nh=nh, n_ctx=n_ctx, n_lat=n_lat, tkl=tkl, nct=nct,
                          tile_off=tile_off, diff=diff, lam_init=lam_init),
        out_shape=jax.ShapeDtypeStruct((t - tile_off * ROW_TILE, n_kv * ow), BF16),
        grid=(n_kv, t // ROW_TILE - tile_off),
        in_specs=in_specs,
        out_specs=pl.BlockSpec((ROW_TILE, ow), lambda kv, i: (i, kv)),
        scratch_shapes=([pltpu.VMEM((ROW_TILE, LANES), F32)] * (nh * g)
                        + [pltpu.VMEM((ROW_TILE, 2 * hd), F32)] * (nh * g)
                        + [pltpu.VMEM((2, ROW_TILE, tkl), BF16)] * (nh * g)),
        compiler_params=_params(("arbitrary", "arbitrary"), 48),
        name="diff_attention" if diff else "gqa_attention",
    )(*args)


def _outproj_kernel(of_ref, ob_ref, gate_ref, ng_ref, att_ref, dif_ref, wo_ref, xc_ref, xl_ref, m_ref,
                    n2_ref, rw_ref, xo_ref, h2_ref, lg_ref, *, nct, tile_off):
    hd = HEAD_DIM
    hgw = HG_HEADS * hd
    ng = ng_ref[...]
    gate = gate_ref[...]
    silu_gate = gate * (1.0 / (1.0 + jnp.exp(-gate)))
    acc = jnp.dot(att_ref[...], wo_ref[hgw:hgw + att_ref.shape[1], :], preferred_element_type=F32)
    acc = acc + jnp.dot(dif_ref[...], wo_ref[hgw + att_ref.shape[1]:, :], preferred_element_type=F32)
    hg = []
    for h in range(HG_HEADS):
        o = of_ref[0, :, h * hd:(h + 1) * hd] + ob_ref[0, :, h * hd:(h + 1) * hd]
        hg.append((_head_norm(o, ng) * silu_gate[:, h * hd:(h + 1) * hd]).astype(BF16))
    acc = acc + jnp.dot(jnp.concatenate(hg, axis=1), wo_ref[0:hgw, :], preferred_element_type=F32)
    m = m_ref[0]
    xn = _stream_tile(xc_ref, xl_ref, nct, tile_off) + m[2:3] * acc
    xo_ref[...] = xn
    h2 = _norm_mod(xn, n2_ref[...], m[3:4], m[4:5])
    _to_token_major(h2_ref, h2)
    hs = _split_bf16(h2, 2)
    ws = _split_bf16(rw_ref[...], 2)
    lg = jnp.dot(hs[1], ws[0], preferred_element_type=F32)
    lg = lg + jnp.dot(hs[0], ws[1], preferred_element_type=F32)
    lg_ref[...] = lg + jnp.dot(hs[0], ws[0], preferred_element_type=F32)


def _outproj_call(o, p, ng, att, dif, wo, xc, xl, lat_off, mods, n2, rw, nct, tile_off):
    d = xl.shape[1]
    hd = HEAD_DIM
    hgw = HG_HEADS * hd
    n_rows = att.shape[0]
    row = lambda w: pl.BlockSpec((ROW_TILE, w), lambda i: (i, 0))
    full = lambda a: pl.BlockSpec(a.shape, lambda i: (0,) * a.ndim)
    ng, n2 = ng.reshape(1, hd), n2.reshape(1, d)
    return pl.pallas_call(
        functools.partial(_outproj_kernel, nct=nct, tile_off=tile_off),
        out_shape=[jax.ShapeDtypeStruct((n_rows, d), F32),
                   jax.ShapeDtypeStruct((n_rows * (d // LANES), LANES), F32),
                   jax.ShapeDtypeStruct((n_rows, LANES), F32)],
        grid=(n_rows // ROW_TILE,),
        in_specs=[pl.BlockSpec((1, ROW_TILE, hgw), lambda i: (0, i + tile_off, 0)),
                  pl.BlockSpec((1, ROW_TILE, hgw), lambda i: (1, i + tile_off, 0)),
                  pl.BlockSpec((ROW_TILE, hgw), lambda i: (i + tile_off, 5)),
                  full(ng), row(att.shape[1]), row(dif.shape[1]), full(wo)]
                 + _stream_specs(d, nct, tile_off, lat_off)
                 + [_mod_spec(d, nct, tile_off), full(n2), full(rw)],
        out_specs=[row(d), pl.BlockSpec((ROW_TILE * (d // LANES), LANES), lambda i: (i, 0)), row(LANES)],
        compiler_params=_params(("arbitrary",), 56),
        name="out_proj",
    )(o, o, p, ng, att, dif, wo, xc, xl, mods, n2, rw)


def _router_kernel(lg_ref, bias_ref, r_ref, cnt_ref, run_ref):
    i = pl.program_id(0)
    tm = lg_ref.shape[0]

    @pl.when(i == 0)
    def _():
        run_ref[...] = jnp.zeros_like(run_ref)

    lane = lax.broadcasted_iota(I32, (tm, LANES), 1).astype(F32)
    lg = lg_ref[...] + bias_ref[...]
    ninf = -jnp.inf

    def first_max(vals):
        mx = jnp.max(vals, axis=1, keepdims=True)
        idx = jnp.min(jnp.where(vals == mx, lane, float(LANES)), axis=1, keepdims=True)
        return mx, idx

    gl = jnp.where(lane < N_GROUPS, lg, ninf)
    gmax, gidx = first_max(gl)
    g_top = 1.0 / jnp.sum(jnp.exp(gl - gmax), axis=1, keepdims=True)
    lo = N_GROUPS + EXPERTS_PER_GROUP * gidx
    el = jnp.where((lane >= lo) & (lane < lo + EXPERTS_PER_GROUP), lg, ninf)
    m1, e1 = first_max(el)
    m2, e2 = first_max(jnp.where(lane == e1, ninf, el))
    r = jnp.exp(m2 - m1)
    w1 = g_top / (1.0 + r)
    w2 = g_top * r / (1.0 + r)

    hit = ((lane == e1) | (lane == e2)).astype(BF16)
    ti = lax.broadcasted_iota(I32, (tm, tm), 0)
    si = lax.broadcasted_iota(I32, (tm, tm), 1)
    before = (si < ti).astype(BF16)
    pos = jnp.dot(before, hit, preferred_element_type=F32) + run_ref[0:1, :]
    p1 = jnp.sum(jnp.where(lane == e1, pos, 0.0), axis=1, keepdims=True)
    p2 = jnp.sum(jnp.where(lane == e2, pos, 0.0), axis=1, keepdims=True)
    total = run_ref[0:1, :] + jnp.sum(hit.astype(F32), axis=0, keepdims=True)
    run_ref[...] = jnp.broadcast_to(total, run_ref.shape)
    cnt_ref[...] = jnp.broadcast_to(total, cnt_ref.shape)

    fields = (e1 - N_GROUPS, e2 - N_GROUPS, w1, w2, p1, p2)
    out = jnp.zeros((tm, LANES), F32)
    for n, f in enumerate(fields):
        out = jnp.where(lane == n, f, out)
    r_ref[...] = out


def _router_call(logits, bias):
    n = logits.shape[0]
    return pl.pallas_call(
        _router_kernel,
        out_shape=[jax.ShapeDtypeStruct((n, LANES), F32), jax.ShapeDtypeStruct((8, LANES), F32)],
        grid=(n // ROW_TILE,),
        in_specs=[pl.BlockSpec((ROW_TILE, LANES), lambda i: (i, 0)),
                  pl.BlockSpec((1, LANES), lambda i: (0, 0))],
        out_specs=[pl.BlockSpec((ROW_TILE, LANES), lambda i: (i, 0)),
                   pl.BlockSpec((8, LANES), lambda i: (0, 0))],
        scratch_shapes=[pltpu.VMEM((8, LANES), F32)],
        compiler_params=_params(("arbitrary",), 16),
        name="router",
    )(logits, bias)


def _moe_kernel(src_ref, te_ref, nx_ref, ng_ref, nt_ref, h_hbm, wg_hbm, wu_hbm, wd_hbm, y_ref,
                xbuf, sem, wg_st, wu_st, wd_st, wsem, wg_bf, wu_bf, wd_bf, *, layer):
    i = pl.program_id(0)
    n_tiles = nt_ref[0]
    tm = MOE_TILE
    k = xbuf.shape[1] // tm
    grp_rows = _DMA_GROUP * k

    def weight_copies(e):
        pairs = ((wg_hbm, wg_st), (wu_hbm, wu_st), (wd_hbm, wd_st))
        return [pltpu.make_async_copy(w.at[layer, e], st, wsem.at[n]) for n, (w, st) in enumerate(pairs)]

    def start_tile(tile, slot):
        def body(grp, carry):
            for u in range(_DMA_GROUP):
                r = grp * _DMA_GROUP + u
                src_row = pl.multiple_of(src_ref[tile * tm + r] * k, k)
                pltpu.make_async_copy(h_hbm.at[pl.ds(src_row, k), :],
                                      xbuf.at[slot, pl.ds(pl.multiple_of(r * k, k), k), :],
                                      sem.at[slot]).start(priority=1)
            return carry
        lax.fori_loop(0, ng_ref[tile], body, 0)

    def wait_tile(tile, slot):
        def body(grp, carry):
            pltpu.make_async_copy(h_hbm.at[pl.ds(0, grp_rows), :], xbuf.at[slot, pl.ds(0, grp_rows), :],
                                  sem.at[slot]).wait()
            return carry
        lax.fori_loop(0, ng_ref[tile], body, 0)

    @pl.when(i == 0)
    def _():
        xbuf[...] = jnp.zeros_like(xbuf)
        start_tile(0, 0)
        for cp in weight_copies(te_ref[0]):
            cp.start()

    @pl.when(i < n_tiles)
    def _():
        slot = i % 2

        @pl.when(i + 1 < n_tiles)
        def _():
            start_tile(i + 1, 1 - slot)

        changed = jnp.logical_or(i == 0, te_ref[i] != te_ref[jnp.maximum(i - 1, 0)])

        @pl.when(changed)
        def _():
            for cp in weight_copies(te_ref[i]):
                cp.wait()
            wg_bf[...] = wg_st[...].astype(BF16)
            wu_bf[...] = wu_st[...].astype(BF16)
            wd_bf[...] = wd_st[...].astype(BF16)

            @pl.when(nx_ref[i] >= 0)
            def _():
                for cp in weight_copies(nx_ref[i]):
                    cp.start()

        wait_tile(i, slot)
        xb = _from_token_major(xbuf.at[slot], tm).astype(BF16)
        gt = jnp.dot(xb, wg_bf[...], preferred_element_type=F32)
        up = jnp.dot(xb, wu_bf[...], preferred_element_type=F32)
        act = (gt * (1.0 / (1.0 + jnp.exp(-gt))) * up).astype(BF16)
        _to_token_major(y_ref, jnp.dot(act, wd_bf[...], preferred_element_type=F32))

    @pl.when(i >= n_tiles)
    def _():
        y_ref[...] = jnp.zeros_like(y_ref)


def _moe_call(h2, src, tile_expert, next_expert, tile_groups, n_tiles, wg, wu, wd, layer):
    d, ff = wg.shape[2], wg.shape[3]
    k = d // LANES
    max_tiles = tile_expert.shape[0]
    tm = MOE_TILE
    hbm = pl.BlockSpec(memory_space=pl.ANY)
    grid_spec = pltpu.PrefetchScalarGridSpec(
        num_scalar_prefetch=5,
        grid=(max_tiles,),
        in_specs=[hbm, hbm, hbm, hbm],
        out_specs=pl.BlockSpec((tm * k, LANES), lambda i, src, te, nx, ng, nt: (i, 0)),
        scratch_shapes=[pltpu.VMEM((2, tm * k, LANES), F32), pltpu.SemaphoreType.DMA((2,)),
                        pltpu.VMEM((d, ff), F32), pltpu.VMEM((d, ff), F32), pltpu.VMEM((ff, d), F32),
                        pltpu.SemaphoreType.DMA((3,)),
                        pltpu.VMEM((d, ff), BF16), pltpu.VMEM((d, ff), BF16), pltpu.VMEM((ff, d), BF16)])
    return pl.pallas_call(
        functools.partial(_moe_kernel, layer=layer),
        out_shape=jax.ShapeDtypeStruct((max_tiles * tm * k, LANES), F32),
        grid_spec=grid_spec,
        compiler_params=_params(("arbitrary",), 40),
        name="moe_experts",
    )(src, tile_expert, next_expert, tile_groups, n_tiles, h2, wg, wu, wd)


def _combine_kernel(dst_ref, x_ref, rt_ref, m_ref, g_ref, mn_ref, y_hbm, *rest, last, n_out):
    outs, (ybuf, sem) = rest[:n_out], rest[n_out:]
    i = pl.program_id(0)
    tm = x_ref.shape[0]
    k = ybuf.shape[2] // tm

    def start_tile(tile, slot):
        def body(grp, carry):
            for u in range(_DMA_GROUP):
                r = grp * _DMA_GROUP + u
                for j in range(2):
                    src_row = pl.multiple_of(dst_ref[(tile * tm + r) * 2 + j] * k, k)
                    pltpu.make_async_copy(y_hbm.at[pl.ds(src_row, k), :],
                                          ybuf.at[slot, j, pl.ds(pl.multiple_of(r * k, k), k), :],
                                          sem.at[slot]).start(priority=j)
            return carry
        lax.fori_loop(0, tm // _DMA_GROUP, body, 0)

    @pl.when(i == 0)
    def _():
        start_tile(0, 0)

    slot = i % 2

    @pl.when(i + 1 < pl.num_programs(0))
    def _():
        start_tile(i + 1, 1 - slot)

    for j in range(2):
        pltpu.make_async_copy(y_hbm.at[pl.ds(0, tm * k), :], ybuf.at[slot, j], sem.at[slot]).wait()
    rt = rt_ref[...]
    y = (rt[:, 2:3] * _from_token_major(ybuf.at[slot, 0], tm)
         + rt[:, 3:4] * _from_token_major(ybuf.at[slot, 1], tm))
    xn = x_ref[...] + m_ref[0][5:6] * y
    if last:
        outs[0][...] = xn * lax.rsqrt(jnp.mean(xn * xn, axis=-1, keepdims=True) + EPS) * g_ref[...]
    else:
        outs[0][...] = xn
        mn = mn_ref[0]
        outs[1][...] = _norm_mod(xn, g_ref[...], mn[0:1], mn[1:2]).astype(BF16)


def _combine_call(dst, x, route, mods, g, mods_next, ys, nct, last):
    n, d = x.shape
    tm = ROW_TILE
    mspec = pl.BlockSpec((1, N_MOD, d), lambda i, dst: (jnp.where(i < nct, 1, 0), 0, 0))
    out_shape = [jax.ShapeDtypeStruct((n, d), F32)]
    if not last:
        out_shape.append(jax.ShapeDtypeStruct((n, d), BF16))
    row = lambda w: pl.BlockSpec((tm, w), lambda i, dst: (i, 0))
    grid_spec = pltpu.PrefetchScalarGridSpec(
        num_scalar_prefetch=1,
        grid=(n // tm,),
        in_specs=[row(d), row(LANES), mspec, pl.BlockSpec((1, d), lambda i, dst: (0, 0)), mspec,
                  pl.BlockSpec(memory_space=pl.ANY)],
        out_specs=[row(d)] * len(out_shape),
        scratch_shapes=[pltpu.VMEM((2, 2, tm * (d // LANES), LANES), F32), pltpu.SemaphoreType.DMA((2,))])
    return pl.pallas_call(
        functools.partial(_combine_kernel, last=last, n_out=len(out_shape)),
        out_shape=out_shape,
        grid_spec=grid_spec,
        compiler_params=_params(("arbitrary",), 40),
        name="moe_combine",
    )(dst, x, route, mods, g.reshape(1, d), mods_next, ys)


def _dispatch_plan(route, counts, n_tokens):
    tm = MOE_TILE
    max_tiles = (2 * n_tokens) // tm + N_EXPERTS
    cnt = counts[0, N_GROUPS:N_GROUPS + N_EXPERTS].astype(I32)
    tiles_per = (cnt + tm - 1) // tm
    tile_end = jnp.cumsum(tiles_per)
    offs = (tile_end - tiles_per) * tm
    eid = route[:, 0:2].astype(I32)
    e_ids = jnp.arange(N_EXPERTS, dtype=I32)
    offs_of = jnp.sum(jnp.where(eid[..., None] == e_ids, offs, 0), axis=-1)
    dst = offs_of + route[:, 4:6].astype(I32)
    tok = jnp.broadcast_to(jnp.arange(n_tokens, dtype=I32)[:, None], (n_tokens, 2))
    src = jnp.zeros((max_tiles * tm,), I32).at[dst.reshape(-1)].set(tok.reshape(-1))
    n_tiles = tile_end[-1:]
    tile_ids = jnp.minimum(jnp.arange(max_tiles, dtype=I32), n_tiles[0] - 1)
    tile_expert = jnp.sum((tile_end[None, :] <= tile_ids[:, None]).astype(I32), axis=1)
    later = (e_ids[None, :] > e_ids[:, None]) & (tiles_per[None, :] > 0)
    nxt = jnp.min(jnp.where(later, e_ids[None, :], N_EXPERTS), axis=1)
    next_expert = jnp.where(nxt < N_EXPERTS, nxt, -1)[tile_expert]
    within = tile_ids - (tile_end - tiles_per)[tile_expert]
    rows = jnp.clip(cnt[tile_expert] - within * tm, 0, tm)
    tile_groups = (rows + _DMA_GROUP - 1) // _DMA_GROUP
    return dst.reshape(-1), src, tile_expert, next_expert, tile_groups.astype(I32), n_tiles.astype(I32)


def kernel(x, c, ctx, c_ctx, w_mod, b_mod, norm1_g, norm2_g, w_in, w_out, hg_lb_logits, hg_norm_g,
           q_norm_g, k_norm_g, lam_q1, lam_k1, lam_q2, lam_k2, diff_norm_g, router_group_w,
           router_group_b, router_expert_w, router_expert_b, w_gate, w_up, w_down, final_norm_g):
    depth = w_in.shape[0]
    n_lat, d = x.shape[1], x.shape[2]
    n_ctx = ctx.shape[1]
    nct = n_ctx // ROW_TILE
    assert x.shape[0] == 1 and n_ctx % ROW_TILE == 0 and n_lat % ROW_TILE == 0

    stream = (ctx[0], x[0], 0)
    cc = jnp.zeros((8, d), F32).at[0].set(c[0]).at[1].set(c_ctx)
    mods_all = _mod_call(cc, w_mod, b_mod)[:, :2].reshape(depth, 2, N_MOD, d)
    tabs = _rope_tables(n_lat, n_ctx, HEAD_DIM) + _rope_tables(n_lat, n_ctx, HEAD_DIM // 2)

    h1 = _prenorm_call(stream[0], stream[1], norm1_g[0], mods_all[0], nct)
    out = None
    for l in range(depth):
        last = l == depth - 1
        mods = mods_all[l]
        tile_off = nct if last else 0
        tm_in = next(tm for tm in _IN_PROJ_ROW_TILES if (n_ctx + n_lat) % tm == 0)
        p = _matmul_call(h1, w_in, l, tm_in, _WEIGHT_COL_TILE)
        o = _hgrn_call(p, hg_lb_logits, l, nct)
        qa, ka, va, qd, kd, vd = _attn_prep_call(p, tabs, q_norm_g[l], k_norm_g[l])
        att = _flash_call(qa, ka, va, n_ctx, GQA_HEADS // GQA_KV_HEADS, 1, tile_off)
        lam_init = 0.8 - 0.6 * math.exp(-0.3 * l)
        dif = _flash_call(qd, kd, vd, n_ctx, 2, 2, tile_off,
                          extra=(lam_q1[l], lam_k1[l], lam_q2[l], lam_k2[l], diff_norm_g[l]),
                          lam_init=lam_init)
        rw = jnp.concatenate([router_group_w[l], router_expert_w[l],
                              jnp.zeros((d, LANES - N_GROUPS - N_EXPERTS), F32)], axis=1)
        rb = jnp.concatenate([router_group_b[l], router_expert_b[l],
                              jnp.zeros((LANES - N_GROUPS - N_EXPERTS,), F32)]).reshape(1, LANES)
        xn, h2, logits = _outproj_call(o, p, hg_norm_g[l], att, dif, w_out[l].astype(BF16), *stream,
                                       mods, norm2_g[l], rw, nct, tile_off)
        n_tok = xn.shape[0]
        route, counts = _router_call(logits, rb)
        dst, src, tile_expert, next_expert, tile_groups, n_tiles = _dispatch_plan(route, counts, n_tok)
        ys = _moe_call(h2, src, tile_expert, next_expert, tile_groups, n_tiles, w_gate, w_up, w_down, l)
        if last:
            (out,) = _combine_call(dst, xn, route, mods, final_norm_g, mods, ys, 0, True)
        else:
            xall, h1 = _combine_call(dst, xn, route, mods, norm1_g[l + 1], mods_all[l + 1], ys, nct, False)
            stream = (xall, xall, nct)
    return out.reshape(1, n_lat, d)
```

```python
import functools
import math

import numpy as np
import jax
import jax.numpy as jnp
from jax import lax
from jax.experimental import pallas as pl
from jax.experimental.pallas import tpu as pltpu

F32 = jnp.float32
BF16 = jnp.bfloat16
I32 = jnp.int32

HEAD_DIM = 128
LANES = 128
GRID_W = 64
ROPE_THETA = 10000.0
EPS = 1e-6
HG_HEADS = 4
HG_CHUNK = 64
GQA_HEADS = 8
GQA_KV_HEADS = 2
DIFF_HEADS = 4
N_GROUPS = 4
EXPERTS_PER_GROUP = 8
N_EXPERTS = N_GROUPS * EXPERTS_PER_GROUP
N_MOD = 6
ROW_TILE = 256
MOE_TILE = 256
_IN_PROJ_ROW_TILES = (1408, 768, ROW_TILE)
_WEIGHT_COL_TILE = 1024
_DMA_GROUP = 8
MIB = 1024 * 1024

_LOG2E = math.log2(math.e)
_ATT_KEY_BLOCK = 2816
_NT = (((1,), (1,)), ((), ()))
_TN = (((0,), (0,)), ((), ()))


def _params(semantics, vmem_mib):
    return pltpu.CompilerParams(dimension_semantics=semantics, vmem_limit_bytes=vmem_mib * MIB)


def _to_token_major(ref, x):
    n, d = x.shape
    k = d // LANES
    for s in range(k):
        ref[pl.ds(s, n, stride=k), :] = x[:, s * LANES:(s + 1) * LANES]


def _from_token_major(ref, n):
    k = ref.shape[0] // n
    return jnp.concatenate([ref[pl.ds(s, n, stride=k), :] for s in range(k)], axis=1)


def _split_bf16(x, parts):
    out = []
    for _ in range(parts - 1):
        p = x.astype(BF16)
        out.append(p)
        x = x - p.astype(F32)
    out.append(x.astype(BF16))
    return out


def _mod_kernel(a_ref, w_ref, b_ref, o_ref):
    a = a_ref[...]
    a = a * (1.0 / (1.0 + jnp.exp(-a)))
    hi, lo = _split_bf16(a, 2)
    w = w_ref[0].astype(BF16)
    o_ref[0] = (jnp.dot(hi, w, preferred_element_type=F32)
                + jnp.dot(lo, w, preferred_element_type=F32) + b_ref[0])


def _mod_call(cc, w_mod, b_mod):
    depth, d, n = w_mod.shape
    tn = _WEIGHT_COL_TILE
    return pl.pallas_call(
        _mod_kernel,
        out_shape=jax.ShapeDtypeStruct((depth, 8, n), F32),
        grid=(depth, n // tn),
        in_specs=[pl.BlockSpec((8, d), lambda l, j: (0, 0)),
                  pl.BlockSpec((1, d, tn), lambda l, j: (l, 0, j)),
                  pl.BlockSpec((1, 1, tn), lambda l, j: (l, 0, j))],
        out_specs=pl.BlockSpec((1, 8, tn), lambda l, j: (l, 0, j)),
        compiler_params=_params(("arbitrary", "arbitrary"), 40),
        name="mod_vectors",
    )(cc, w_mod, b_mod.reshape(depth, 1, n))


def _norm_mod(x, g, shift, scale):
    y = x * lax.rsqrt(jnp.mean(x * x, axis=-1, keepdims=True) + EPS) * g
    return y * (1.0 + scale) + shift


def _stream_specs(d, nct, tile_off=0, lat_off=0):
    return [pl.BlockSpec((ROW_TILE, d), lambda i: (jnp.minimum(i + tile_off, nct - 1), 0)),
            pl.BlockSpec((ROW_TILE, d), lambda i: (jnp.maximum(i + tile_off - nct, 0) + lat_off, 0))]


def _stream_tile(c_ref, x_ref, nct, tile_off=0):
    return jnp.where(pl.program_id(0) + tile_off < nct, c_ref[...], x_ref[...])


def _prenorm_kernel(c_ref, x_ref, g_ref, m_ref, o_ref, *, nct):
    m = m_ref[0]
    o_ref[...] = _norm_mod(_stream_tile(c_ref, x_ref, nct), g_ref[...], m[0:1], m[1:2]).astype(BF16)


def _mod_spec(d, nct, tile_off=0):
    return pl.BlockSpec((1, N_MOD, d), lambda i: (jnp.where(i + tile_off < nct, 1, 0), 0, 0))


def _prenorm_call(xc, xl, g, mods, nct):
    d = xl.shape[1]
    t = xc.shape[0] + xl.shape[0]
    return pl.pallas_call(
        functools.partial(_prenorm_kernel, nct=nct),
        out_shape=jax.ShapeDtypeStruct((t, d), BF16),
        grid=(t // ROW_TILE,),
        in_specs=_stream_specs(d, nct) + [pl.BlockSpec((1, d), lambda i: (0, 0)), _mod_spec(d, nct)],
        out_specs=pl.BlockSpec((ROW_TILE, d), lambda i: (i, 0)),
        compiler_params=_params(("arbitrary",), 24),
        name="prenorm",
    )(xc, xl, g.reshape(1, d), mods)


def _mm_kernel(a_ref, b_ref, o_ref, b_bf):
    @pl.when(pl.program_id(1) == 0)
    def _():
        b_bf[...] = b_ref[0].astype(BF16)

    o_ref[...] = jnp.dot(a_ref[...], b_bf[...], preferred_element_type=F32)


def _matmul_call(a, b, layer, tm, tn):
    m, k = a.shape
    n = b.shape[2]
    return pl.pallas_call(
        _mm_kernel,
        out_shape=jax.ShapeDtypeStruct((m, n), F32),
        grid=(n // tn, m // tm),
        in_specs=[pl.BlockSpec((tm, k), lambda j, i: (i, 0)),
                  pl.BlockSpec((1, k, tn), lambda j, i: (layer, 0, j))],
        out_specs=pl.BlockSpec((tm, tn), lambda j, i: (i, j)),
        scratch_shapes=[pltpu.VMEM((k, tn), BF16)],
        compiler_params=_params(("arbitrary", "arbitrary"), 56),
        name="in_proj",
    )(a, b)


_HG_LEVELS = (1, 2, 4, 8, 16, 32)
_HG_TOT_ROW = HG_CHUNK * (len(_HG_LEVELS) + 1)
_HG_W_ROWS = _HG_TOT_ROW + 16
_HG_HEADS_PER_STEP = 4


def _hgrn_consts():
    c = HG_CHUNK
    w = np.zeros((2, _HG_W_ROWS, c), np.float32)
    msk = np.zeros((2, len(_HG_LEVELS) + 1, c, c), np.float32)
    for d in range(2):
        u = np.arange(c) if d == 0 else c - 1 - np.arange(c)
        ut, us = u[:, None], u[None, :]
        w[d, :c] = us <= ut
        for li, lv in enumerate(_HG_LEVELS):
            blk = u // (2 * lv)
            qside = (u % (2 * lv)) >= lv
            bnd = (blk * 2 * lv + lv - 1)[:, None]
            wq = (us > bnd) & (us <= ut)
            wk = (us > ut) & (us <= bnd)
            w[d, c * (li + 1):c * (li + 2)] = np.where(qside[:, None], wq, -1.0 * wk)
            msk[d, li] = (blk[:, None] == blk[None, :]) & qside[:, None] & ~qside[None, :]
        msk[d, len(_HG_LEVELS)] = np.eye(c)
        w[d, _HG_TOT_ROW:] = 1.0
    return jnp.asarray(np.concatenate([w, w, w], axis=2), BF16), jnp.asarray(msk, F32)


def _hgrn_kernel(lbl_ref, q_ref, z_ref, v_ref, w_ref, msk_ref, o_ref, st_ref, *, layer, chunks, hps):
    c = HG_CHUNK
    hd = HEAD_DIM
    d = pl.program_id(0)
    j = pl.program_id(2)

    @pl.when(j == 0)
    def _():
        st_ref[...] = jnp.zeros_like(st_ref)

    lbl = lbl_ref[...]
    rows = [lbl[i:i + 1] for i in range(lbl.shape[0])]
    mx = functools.reduce(jnp.maximum, rows)
    ex = [jnp.exp(r - mx) for r in rows]
    tot = functools.reduce(lambda a, b: a + b, ex)
    lb = jnp.zeros_like(mx)
    for i in range(1, layer + 1):
        lb = lb + ex[i] / tot
    log_lb = jnp.log(lb)
    log_1m_lb = jnp.log1p(-lb)

    wmat = w_ref[0]
    nlev = len(_HG_LEVELS)

    def finish(h, hs, r0, amat, v_bf, qe, kd, decay):
        st = st_ref[h]
        o = (jnp.dot(amat, v_bf[:, hs], preferred_element_type=F32)
             + lax.dot_general(qe[:, hs], st.astype(BF16), _NT, preferred_element_type=F32))
        o_ref[0, pl.ds(r0, c), hs] = o
        st_ref[h] = st * decay[:, hs] + lax.dot_general(v_bf[:, hs], kd[:, hs], _TN,
                                                        preferred_element_type=F32)

    pending = None
    for ci in range(chunks):
        cc = ci + d * (chunks - 1 - 2 * ci)
        r0 = pl.multiple_of(cc * c, c)
        q = q_ref[pl.ds(r0, c), :]
        z = z_ref[pl.ds(r0, c), :]
        v_bf = v_ref[pl.ds(r0, c), :].astype(BF16)
        l1p = jnp.log(1.0 + jnp.exp(-jnp.abs(z)))
        ls_pos = jnp.minimum(z, 0.0) - l1p
        ls_neg = jnp.minimum(-z, 0.0) - l1p
        a2 = log_1m_lb + ls_pos
        logf = jnp.maximum(log_lb, a2) + jnp.log(1.0 + jnp.exp(-jnp.abs(log_lb - a2)))
        k = (1.0 - lb) * jnp.exp(ls_neg)

        parts = jnp.concatenate(_split_bf16(logf, 3), axis=0)
        sums = jnp.dot(wmat, parts, preferred_element_type=F32)
        b = sums[0:c]
        btot = sums[_HG_TOT_ROW:_HG_TOT_ROW + 1]

        qls, kls = [q.astype(BF16)], [k.astype(BF16)]
        for li in range(nlev):
            fl = jnp.exp(-jnp.abs(sums[c * (li + 1):c * (li + 2)]))
            qls.append((q * fl).astype(BF16))
            kls.append((k * fl).astype(BF16))
        qe = (q * jnp.exp(b)).astype(BF16)
        kd = (k * jnp.exp(btot - b)).astype(BF16)
        decay = jnp.exp(btot)

        for h in range(hps):
            hs = slice(h * hd, (h + 1) * hd)
            amat = lax.dot_general(qls[0][:, hs], kls[0][:, hs], _NT,
                                   preferred_element_type=F32) * msk_ref[0, nlev]
            for li in range(nlev):
                amat = amat + lax.dot_general(qls[li + 1][:, hs], kls[li + 1][:, hs], _NT,
                                              preferred_element_type=F32) * msk_ref[0, li]
            if pending is not None:
                finish(*pending)
            pending = (h, hs, r0, amat.astype(BF16), v_bf, qe, kd, decay)
    finish(*pending)


def _hgrn_call(p, lb_logits, layer, nct):
    t = p.shape[0]
    nblk = t // ROW_TILE
    chunks = ROW_TILE // HG_CHUNK
    hps = _HG_HEADS_PER_STEP
    hw = hps * HEAD_DIM
    ng = HG_HEADS // hps
    wmat, msk = _hgrn_consts()

    def blk(d, j):
        back = jnp.where(j < nct, nct - 1 - j, nblk - 1 - (j - nct))
        return jnp.where(d == 0, j, back)

    return pl.pallas_call(
        functools.partial(_hgrn_kernel, layer=layer, chunks=chunks, hps=hps),
        out_shape=jax.ShapeDtypeStruct((2, t, HG_HEADS * HEAD_DIM), F32),
        grid=(2, ng, nblk),
        in_specs=[pl.BlockSpec((lb_logits.shape[0], hw), lambda d, h, j: (0, h)),
                  pl.BlockSpec((ROW_TILE, hw), lambda d, h, j: (blk(d, j), d * ng + h)),
                  pl.BlockSpec((ROW_TILE, hw), lambda d, h, j: (blk(d, j), (2 + d) * ng + h)),
                  pl.BlockSpec((ROW_TILE, hw), lambda d, h, j: (blk(d, j), 4 * ng + h)),
                  pl.BlockSpec((1, _HG_W_ROWS, 3 * HG_CHUNK), lambda d, h, j: (d, 0, 0)),
                  pl.BlockSpec((1, len(_HG_LEVELS) + 1, HG_CHUNK, HG_CHUNK), lambda d, h, j: (d, 0, 0, 0))],
        out_specs=pl.BlockSpec((1, ROW_TILE, hw), lambda d, h, j: (d, blk(d, j), h)),
        scratch_shapes=[pltpu.VMEM((hps, HEAD_DIM, HEAD_DIM), F32)],
        compiler_params=_params(("arbitrary", "arbitrary", "arbitrary"), 24),
        name="hgrn_scan",
    )(lb_logits, p, p, p, wmat, msk)


def _rope_tables(n_lat, n_ctx, dim):
    rows = n_lat // GRID_W
    row = jnp.repeat(jnp.arange(rows, dtype=F32), GRID_W)
    col = jnp.tile(jnp.arange(GRID_W, dtype=F32), rows)
    axis_dim = dim // 2
    inv_freq = ROPE_THETA ** (-jnp.arange(0, axis_dim, 2, dtype=F32) / axis_dim)
    ang = jnp.concatenate([row[:, None] * inv_freq, col[:, None] * inv_freq], axis=-1)
    cos = jnp.repeat(jnp.cos(ang), 2, axis=1)
    sin = jnp.repeat(jnp.sin(ang), 2, axis=1) * jnp.tile(jnp.asarray([-1.0, 1.0], F32), dim // 2)
    reps = LANES // dim
    cos = jnp.concatenate([jnp.ones((n_ctx, dim), F32), cos], axis=0)
    sin = jnp.concatenate([jnp.zeros((n_ctx, dim), F32), sin], axis=0)
    return jnp.tile(cos, (1, reps)), jnp.tile(sin, (1, reps))


def _rope(x, cos, sin):
    lane = lax.broadcasted_iota(I32, x.shape, 1)
    swapped = jnp.where((lane & 1) == 0, pltpu.roll(x, LANES - 1, 1), pltpu.roll(x, 1, 1))
    return x * cos + swapped * sin


def _head_norm(x, g):
    return x * lax.rsqrt(jnp.mean(x * x, axis=-1, keepdims=True) + EPS) * g


def _attn_prep_kernel(pq_ref, pkv_ref, pd_ref, ca_ref, sa_ref, cd_ref, sd_ref, qn_ref, kn_ref,
                      qa_ref, ka_ref, va_ref, qd_ref, kd_ref, vd_ref):
    hd = HEAD_DIM
    ca, sa, cd, sd = ca_ref[...], sa_ref[...], cd_ref[...], sd_ref[...]
    qn, kn = qn_ref[...], kn_ref[...]
    for h in range(GQA_HEADS):
        xq = _rope(_head_norm(pq_ref[:, h * hd:(h + 1) * hd], qn), ca, sa)
        qa_ref[:, h * hd:(h + 1) * hd] = (xq * (hd ** -0.5 * _LOG2E)).astype(BF16)
    for h in range(GQA_KV_HEADS):
        xk = _rope(_head_norm(pkv_ref[:, h * hd:(h + 1) * hd], kn), ca, sa)
        ka_ref[:, h * hd:(h + 1) * hd] = xk.astype(BF16)
    ones_col = jnp.where(lax.broadcasted_iota(I32, (pq_ref.shape[0], hd), 1) == 0, 1.0, 0.0).astype(BF16)
    for h in range(GQA_KV_HEADS):
        va_ref[:, (2 * h) * hd:(2 * h + 1) * hd] = pkv_ref[:, (GQA_KV_HEADS + h) * hd:
                                                            (GQA_KV_HEADS + h + 1) * hd].astype(BF16)
        va_ref[:, (2 * h + 1) * hd:(2 * h + 2) * hd] = ones_col
    first = lax.broadcasted_iota(I32, (pq_ref.shape[0], hd), 1) < hd // 2
    dw = DIFF_HEADS * hd
    for h in range(DIFF_HEADS):
        xq = _rope(pd_ref[:, h * hd:(h + 1) * hd], cd, sd) * ((hd // 2) ** -0.5 * _LOG2E)
        qd_ref[:, (2 * h) * hd:(2 * h + 1) * hd] = jnp.where(first, xq, 0.0).astype(BF16)
        qd_ref[:, (2 * h + 1) * hd:(2 * h + 2) * hd] = jnp.where(first, 0.0, xq).astype(BF16)
        xk = _rope(pd_ref[:, dw + h * hd:dw + (h + 1) * hd], cd, sd)
        kd_ref[:, h * hd:(h + 1) * hd] = xk.astype(BF16)
    for h in range(DIFF_HEADS):
        vd_ref[:, (2 * h) * hd:(2 * h + 1) * hd] = pd_ref[:, 2 * dw + h * hd:2 * dw + (h + 1) * hd].astype(BF16)
        vd_ref[:, (2 * h + 1) * hd:(2 * h + 2) * hd] = ones_col


def _attn_prep_call(p, tabs, qn, kn):
    t = p.shape[0]
    hd = HEAD_DIM
    gq, gkv, dw = GQA_HEADS * hd, 2 * GQA_KV_HEADS * hd, DIFF_HEADS * hd
    q_off = 6 * HG_HEADS * hd
    assert q_off % gq == 0 and (q_off + gq) % gkv == 0 and (q_off + gq + gkv) % (3 * dw) == 0
    tp = next(n for n in (768, ROW_TILE) if t % n == 0)
    row = lambda w: pl.BlockSpec((tp, w), lambda i: (i, 0))
    vec = pl.BlockSpec((1, hd), lambda i: (0, 0))
    return pl.pallas_call(
        _attn_prep_kernel,
        out_shape=[jax.ShapeDtypeStruct((t, gq), BF16),
                   jax.ShapeDtypeStruct((t, gkv // 2), BF16),
                   jax.ShapeDtypeStruct((t, gkv), BF16),
                   jax.ShapeDtypeStruct((t, 2 * dw), BF16),
                   jax.ShapeDtypeStruct((t, dw), BF16),
                   jax.ShapeDtypeStruct((t, 2 * dw), BF16)],
        grid=(t // tp,),
        in_specs=[pl.BlockSpec((tp, gq), lambda i: (i, q_off // gq)),
                  pl.BlockSpec((tp, gkv), lambda i: (i, (q_off + gq) // gkv)),
                  pl.BlockSpec((tp, 3 * dw), lambda i: (i, (q_off + gq + gkv) // (3 * dw))),
                  row(hd), row(hd), row(hd), row(hd), vec, vec],
        out_specs=[row(gq), row(gkv // 2), row(gkv), row(2 * dw), row(dw), row(2 * dw)],
        compiler_params=_params(("arbitrary",), 48),
        name="attn_prep",
    )(p, p, p, *tabs, qn.reshape(1, hd), kn.reshape(1, hd))


def _flash_kernel(*refs, g, nh, n_ctx, n_lat, tkl, nct, tile_off, diff, lam_init):
    if diff:
        q_ref, k_ref, v_ref, lq1, lk1, lq2, lk2, dn_ref, o_ref = refs[:9]
    else:
        q_ref, k_ref, v_ref, o_ref = refs[:4]
    nc = nh * g
    m_sc, acc_sc, p_sc = (refs[len(refs) - (3 - n) * nc:len(refs) - (2 - n) * nc] for n in range(3))
    hd = HEAD_DIM
    i = pl.program_id(1) + tile_off
    nblk = (n_ctx + n_lat) // tkl
    chains = [(c, c // g) for c in range(nh * g)]

    def head(ref, rows, h, width=1):
        return ref[rows, h * width * hd:(h + 1) * width * hd]

    def vhead(rows, h):
        return head(v_ref, rows, h, 2)

    def scores(c, kh, rows):
        return lax.dot_general(head(q_ref, slice(None), c), head(k_ref, rows, kh), _NT,
                               preferred_element_type=F32)

    def emit(outputs):
        norm = lambda a: a[:, :hd] / a[:, hd:hd + 1]
        if diff:
            lam = (jnp.exp(jnp.sum(lq1[...] * lk1[...], axis=1, keepdims=True))
                   - jnp.exp(jnp.sum(lq2[...] * lk2[...], axis=1, keepdims=True)) + lam_init)
            for kh in range(nh):
                dd = norm(outputs[2 * kh]) - lam * norm(outputs[2 * kh + 1])
                o_ref[:, kh * hd:(kh + 1) * hd] = (_head_norm(dd, dn_ref[...])
                                                   * (1.0 - lam_init)).astype(BF16)
        else:
            for c, _ in chains:
                o_ref[:, c * hd:(c + 1) * hd] = norm(outputs[c]).astype(BF16)

    @pl.when(i < nct)
    def _():
        ctx_rows = slice(0, n_ctx)
        outs = []
        for c, kh in chains:
            s = scores(c, kh, ctx_rows)
            p = jnp.exp2(s - jnp.max(s, axis=1, keepdims=True))
            outs.append(jnp.dot(p.astype(BF16), vhead(ctx_rows, kh), preferred_element_type=F32))
        emit(outs)

    @pl.when(i >= nct)
    def _():
        for jb in range(nblk):
            rows = pl.ds(jb * tkl, tkl)
            slot = jb % 2
            alphas = []
            for c, kh in chains:
                s = scores(c, kh, rows)
                m_cur = jnp.max(s, axis=1, keepdims=True)
                if jb == 0:
                    m_new = jnp.broadcast_to(m_cur, m_sc[c].shape)
                else:
                    m_prev = m_sc[c][...]
                    m_new = jnp.maximum(m_prev, m_cur)
                    alphas.append(jnp.tile(jnp.exp2(m_prev - m_new), (1, 2)))
                p_sc[c][slot] = jnp.exp2(s - jnp.tile(m_new, (1, tkl // LANES))).astype(BF16)
                m_sc[c][...] = m_new
            if jb == 0:
                continue
            prev = pl.ds((jb - 1) * tkl, tkl)
            for c, kh in chains:
                pv = jnp.dot(p_sc[c][1 - slot], vhead(prev, kh), preferred_element_type=F32)
                acc_sc[c][...] = alphas[c] * (pv if jb == 1 else acc_sc[c][...] + pv)

        last = pl.ds((nblk - 1) * tkl, tkl)
        outs = []
        for c, kh in chains:
            pv = jnp.dot(p_sc[c][(nblk - 1) % 2], vhead(last, kh), preferred_element_type=F32)
            outs.append(pv if nblk == 1 else acc_sc[c][...] + pv)
        emit(outs)


def _flash_call(q, k, v, n_ctx, g, nh, tile_off, extra=None, lam_init=0.0):
    t = k.shape[0]
    hd = HEAD_DIM
    n_kv = k.shape[1] // (nh * hd)
    n_lat = t - n_ctx
    nct = n_ctx // ROW_TILE
    nblk = next(n for n in range(1, t // LANES + 1)
                if t % (n * LANES) == 0 and t // n <= _ATT_KEY_BLOCK)
    tkl = t // nblk
    diff = extra is not None
    ow = nh * hd if diff else nh * g * hd
    resident = pl.Buffered(1)
    in_specs = [pl.BlockSpec((ROW_TILE, nh * g * hd), lambda kv, i: (i + tile_off, kv)),
                pl.BlockSpec((t, nh * hd), lambda kv, i: (0, kv), pipeline_mode=resident),
                pl.BlockSpec((t, 2 * nh * hd), lambda kv, i: (0, kv), pipeline_mode=resident)]
    args = [q, k, v]
    if diff:
        for a in extra:
            a = a.reshape(1, -1)
            in_specs.append(pl.BlockSpec(a.shape, lambda kv, i: (0, 0)))
            args.append(a)
    return pl.pallas_call(
        functools.partial(_flash_kernel, g=g, nh=nh, n_ctx=n_ctx, n_lat=n_lat, tkl=tkl, nct=nct,
                          tile_off=tile_off, diff=diff, lam_init=lam_init),
        out_shape=jax.ShapeDtypeStruct((t - tile_off * ROW_TILE, n_kv * ow), BF16),
        grid=(n_kv, t // ROW_TILE - tile_off),
        in_specs=in_specs,
        out_specs=pl.BlockSpec((ROW_TILE, ow), lambda kv, i: (i, kv)),
        scratch_shapes=([pltpu.VMEM((ROW_TILE, LANES), F32)] * (nh * g)
                        + [pltpu.VMEM((ROW_TILE, 2 * hd), F32)] * (nh * g)
                        + [pltpu.VMEM((2, ROW_TILE, tkl), BF16)] * (nh * g)),
        compiler_params=_params(("arbitrary", "arbitrary"), 48),
        name="diff_attention" if diff else "gqa_attention",
    )(*args)


def _outproj_kernel(of_ref, ob_ref, gate_ref, ng_ref, att_ref, dif_ref, wo_ref, xc_ref, xl_ref, m_ref,
                    n2_ref, rw_ref, xo_ref, h2_ref, lg_ref, *, nct, tile_off):
    hd = HEAD_DIM
    hgw = HG_HEADS * hd
    ng = ng_ref[...]
    gate = gate_ref[...]
    silu_gate = gate * (1.0 / (1.0 + jnp.exp(-gate)))
    acc = jnp.dot(att_ref[...], wo_ref[hgw:hgw + att_ref.shape[1], :], preferred_element_type=F32)
    acc = acc + jnp.dot(dif_ref[...], wo_ref[hgw + att_ref.shape[1]:, :], preferred_element_type=F32)
    hg = []
    for h in range(HG_HEADS):
        o = of_ref[0, :, h * hd:(h + 1) * hd] + ob_ref[0, :, h * hd:(h + 1) * hd]
        hg.append((_head_norm(o, ng) * silu_gate[:, h * hd:(h + 1) * hd]).astype(BF16))
    acc = acc + jnp.dot(jnp.concatenate(hg, axis=1), wo_ref[0:hgw, :], preferred_element_type=F32)
    m = m_ref[0]
    xn = _stream_tile(xc_ref, xl_ref, nct, tile_off) + m[2:3] * acc
    xo_ref[...] = xn
    h2 = _norm_mod(xn, n2_ref[...], m[3:4], m[4:5])
    _to_token_major(h2_ref, h2)
    hs = _split_bf16(h2, 2)
    ws = _split_bf16(rw_ref[...], 2)
    lg = jnp.dot(hs[1], ws[0], preferred_element_type=F32)
    lg = lg + jnp.dot(hs[0], ws[1], preferred_element_type=F32)
    lg_ref[...] = lg + jnp.dot(hs[0], ws[0], preferred_element_type=F32)


def _outproj_call(o, p, ng, att, dif, wo, xc, xl, lat_off, mods, n2, rw, nct, tile_off):
    d = xl.shape[1]
    hd = HEAD_DIM
    hgw = HG_HEADS * hd
    n_rows = att.shape[0]
    row = lambda w: pl.BlockSpec((ROW_TILE, w), lambda i: (i, 0))
    full = lambda a: pl.BlockSpec(a.shape, lambda i: (0,) * a.ndim)
    ng, n2 = ng.reshape(1, hd), n2.reshape(1, d)
    return pl.pallas_call(
        functools.partial(_outproj_kernel, nct=nct, tile_off=tile_off),
        out_shape=[jax.ShapeDtypeStruct((n_rows, d), F32),
                   jax.ShapeDtypeStruct((n_rows * (d // LANES), LANES), F32),
                   jax.ShapeDtypeStruct((n_rows, LANES), F32)],
        grid=(n_rows // ROW_TILE,),
        in_specs=[pl.BlockSpec((1, ROW_TILE, hgw), lambda i: (0, i + tile_off, 0)),
                  pl.BlockSpec((1, ROW_TILE, hgw), lambda i: (1, i + tile_off, 0)),
                  pl.BlockSpec((ROW_TILE, hgw), lambda i: (i + tile_off, 5)),
                  full(ng), row(att.shape[1]), row(dif.shape[1]), full(wo)]
                 + _stream_specs(d, nct, tile_off, lat_off)
                 + [_mod_spec(d, nct, tile_off), full(n2), full(rw)],
        out_specs=[row(d), pl.BlockSpec((ROW_TILE * (d // LANES), LANES), lambda i: (i, 0)), row(LANES)],
        compiler_params=_params(("arbitrary",), 56),
        name="out_proj",
    )(o, o, p, ng, att, dif, wo, xc, xl, mods, n2, rw)


def _router_kernel(lg_ref, bias_ref, r_ref, cnt_ref, run_ref):
    i = pl.program_id(0)
    tm = lg_ref.shape[0]

    @pl.when(i == 0)
    def _():
        run_ref[...] = jnp.zeros_like(run_ref)

    lane = lax.broadcasted_iota(I32, (tm, LANES), 1).astype(F32)
    lg = lg_ref[...] + bias_ref[...]
    ninf = -jnp.inf

    def first_max(vals):
        mx = jnp.max(vals, axis=1, keepdims=True)
        idx = jnp.min(jnp.where(vals == mx, lane, float(LANES)), axis=1, keepdims=True)
        return mx, idx

    gl = jnp.where(lane < N_GROUPS, lg, ninf)
    gmax, gidx = first_max(gl)
    g_top = 1.0 / jnp.sum(jnp.exp(gl - gmax), axis=1, keepdims=True)
    lo = N_GROUPS + EXPERTS_PER_GROUP * gidx
    el = jnp.where((lane >= lo) & (lane < lo + EXPERTS_PER_GROUP), lg, ninf)
    m1, e1 = first_max(el)
    m2, e2 = first_max(jnp.where(lane == e1, ninf, el))
    r = jnp.exp(m2 - m1)
    w1 = g_top / (1.0 + r)
    w2 = g_top * r / (1.0 + r)

    hit = ((lane == e1) | (lane == e2)).astype(BF16)
    ti = lax.broadcasted_iota(I32, (tm, tm), 0)
    si = lax.broadcasted_iota(I32, (tm, tm), 1)
    before = (si < ti).astype(BF16)
    pos = jnp.dot(before, hit, preferred_element_type=F32) + run_ref[0:1, :]
    p1 = jnp.sum(jnp.where(lane == e1, pos, 0.0), axis=1, keepdims=True)
    p2 = jnp.sum(jnp.where(lane == e2, pos, 0.0), axis=1, keepdims=True)
    total = run_ref[0:1, :] + jnp.sum(hit.astype(F32), axis=0, keepdims=True)
    run_ref[...] = jnp.broadcast_to(total, run_ref.shape)
    cnt_ref[...] = jnp.broadcast_to(total, cnt_ref.shape)

    fields = (e1 - N_GROUPS, e2 - N_GROUPS, w1, w2, p1, p2)
    out = jnp.zeros((tm, LANES), F32)
    for n, f in enumerate(fields):
        out = jnp.where(lane == n, f, out)
    r_ref[...] = out


def _router_call(logits, bias):
    n = logits.shape[0]
    return pl.pallas_call(
        _router_kernel,
        out_shape=[jax.ShapeDtypeStruct((n, LANES), F32), jax.ShapeDtypeStruct((8, LANES), F32)],
        grid=(n // ROW_TILE,),
        in_specs=[pl.BlockSpec((ROW_TILE, LANES), lambda i: (i, 0)),
                  pl.BlockSpec((1, LANES), lambda i: (0, 0))],
        out_specs=[pl.BlockSpec((ROW_TILE, LANES), lambda i: (i, 0)),
                   pl.BlockSpec((8, LANES), lambda i: (0, 0))],
        scratch_shapes=[pltpu.VMEM((8, LANES), F32)],
        compiler_params=_params(("arbitrary",), 16),
        name="router",
    )(logits, bias)


def _moe_kernel(src_ref, te_ref, nx_ref, ng_ref, nt_ref, h_hbm, wg_hbm, wu_hbm, wd_hbm, y_ref,
                xbuf, sem, wg_st, wu_st, wd_st, wsem, wg_bf, wu_bf, wd_bf, *, layer):
    i = pl.program_id(0)
    n_tiles = nt_ref[0]
    tm = MOE_TILE
    k = xbuf.shape[1] // tm
    grp_rows = _DMA_GROUP * k

    def weight_copies(e):
        pairs = ((wg_hbm, wg_st), (wu_hbm, wu_st), (wd_hbm, wd_st))
        return [pltpu.make_async_copy(w.at[layer, e], st, wsem.at[n]) for n, (w, st) in enumerate(pairs)]

    def start_tile(tile, slot):
        def body(grp, carry):
            for u in range(_DMA_GROUP):
                r = grp * _DMA_GROUP + u
                src_row = pl.multiple_of(src_ref[tile * tm + r] * k, k)
                pltpu.make_async_copy(h_hbm.at[pl.ds(src_row, k), :],
                                      xbuf.at[slot, pl.ds(pl.multiple_of(r * k, k), k), :],
                                      sem.at[slot]).start(priority=u % 2)
            return carry
        lax.fori_loop(0, ng_ref[tile], body, 0)

    def wait_tile(tile, slot):
        def body(grp, carry):
            pltpu.make_async_copy(h_hbm.at[pl.ds(0, grp_rows), :], xbuf.at[slot, pl.ds(0, grp_rows), :],
                                  sem.at[slot]).wait()
            return carry
        lax.fori_loop(0, ng_ref[tile], body, 0)

    @pl.when(i == 0)
    def _():
        xbuf[...] = jnp.zeros_like(xbuf)
        start_tile(0, 0)
        for cp in weight_copies(te_ref[0]):
            cp.start()

    @pl.when(i < n_tiles)
    def _():
        slot = i % 2

        @pl.when(i + 1 < n_tiles)
        def _():
            start_tile(i + 1, 1 - slot)

        changed = jnp.logical_or(i == 0, te_ref[i] != te_ref[jnp.maximum(i - 1, 0)])

        @pl.when(changed)
        def _():
            for cp in weight_copies(te_ref[i]):
                cp.wait()
            wg_bf[...] = wg_st[...].astype(BF16)
            wu_bf[...] = wu_st[...].astype(BF16)
            wd_bf[...] = wd_st[...].astype(BF16)

            @pl.when(nx_ref[i] >= 0)
            def _():
                for cp in weight_copies(nx_ref[i]):
                    cp.start()

        wait_tile(i, slot)
        xb = _from_token_major(xbuf.at[slot], tm).astype(BF16)
        gt = jnp.dot(xb, wg_bf[...], preferred_element_type=F32)
        up = jnp.dot(xb, wu_bf[...], preferred_element_type=F32)
        act = (gt * (1.0 / (1.0 + jnp.exp(-gt))) * up).astype(BF16)
        _to_token_major(y_ref, jnp.dot(act, wd_bf[...], preferred_element_type=F32))

    @pl.when(i >= n_tiles)
    def _():
        y_ref[...] = jnp.zeros_like(y_ref)


def _moe_call(h2, src, tile_expert, next_expert, tile_groups, n_tiles, wg, wu, wd, layer):
    d, ff = wg.shape[2], wg.shape[3]
    k = d // LANES
    max_tiles = tile_expert.shape[0]
    tm = MOE_TILE
    hbm = pl.BlockSpec(memory_space=pl.ANY)
    grid_spec = pltpu.PrefetchScalarGridSpec(
        num_scalar_prefetch=5,
        grid=(max_tiles,),
        in_specs=[hbm, hbm, hbm, hbm],
        out_specs=pl.BlockSpec((tm * k, LANES), lambda i, src, te, nx, ng, nt: (i, 0)),
        scratch_shapes=[pltpu.VMEM((2, tm * k, LANES), F32), pltpu.SemaphoreType.DMA((2,)),
                        pltpu.VMEM((d, ff), F32), pltpu.VMEM((d, ff), F32), pltpu.VMEM((ff, d), F32),
                        pltpu.SemaphoreType.DMA((3,)),
                        pltpu.VMEM((d, ff), BF16), pltpu.VMEM((d, ff), BF16), pltpu.VMEM((ff, d), BF16)])
    return pl.pallas_call(
        functools.partial(_moe_kernel, layer=layer),
        out_shape=jax.ShapeDtypeStruct((max_tiles * tm * k, LANES), F32),
        grid_spec=grid_spec,
        compiler_params=_params(("arbitrary",), 40),
        name="moe_experts",
    )(src, tile_expert, next_expert, tile_groups, n_tiles, h2, wg, wu, wd)


def _combine_kernel(dst_ref, x_ref, rt_ref, m_ref, g_ref, mn_ref, y_hbm, *rest, last, n_out):
    outs, (ybuf, sem) = rest[:n_out], rest[n_out:]
    i = pl.program_id(0)
    tm = x_ref.shape[0]
    k = ybuf.shape[2] // tm

    def start_tile(tile, slot):
        def body(grp, carry):
            for u in range(_DMA_GROUP):
                r = grp * _DMA_GROUP + u
                for j in range(2):
                    src_row = pl.multiple_of(dst_ref[(tile * tm + r) * 2 + j] * k, k)
                    pltpu.make_async_copy(y_hbm.at[pl.ds(src_row, k), :],
                                          ybuf.at[slot, j, pl.ds(pl.multiple_of(r * k, k), k), :],
                                          sem.at[slot]).start(priority=j)
            return carry
        lax.fori_loop(0, tm // _DMA_GROUP, body, 0)

    @pl.when(i == 0)
    def _():
        start_tile(0, 0)

    slot = i % 2

    @pl.when(i + 1 < pl.num_programs(0))
    def _():
        start_tile(i + 1, 1 - slot)

    for j in range(2):
        pltpu.make_async_copy(y_hbm.at[pl.ds(0, tm * k), :], ybuf.at[slot, j], sem.at[slot]).wait()
    rt = rt_ref[...]
    y = (rt[:, 2:3] * _from_token_major(ybuf.at[slot, 0], tm)
         + rt[:, 3:4] * _from_token_major(ybuf.at[slot, 1], tm))
    xn = x_ref[...] + m_ref[0][5:6] * y
    if last:
        outs[0][...] = xn * lax.rsqrt(jnp.mean(xn * xn, axis=-1, keepdims=True) + EPS) * g_ref[...]
    else:
        outs[0][...] = xn
        mn = mn_ref[0]
        outs[1][...] = _norm_mod(xn, g_ref[...], mn[0:1], mn[1:2]).astype(BF16)


def _combine_call(dst, x, route, mods, g, mods_next, ys, nct, last):
    n, d = x.shape
    tm = ROW_TILE
    mspec = pl.BlockSpec((1, N_MOD, d), lambda i, dst: (jnp.where(i < nct, 1, 0), 0, 0))
    out_shape = [jax.ShapeDtypeStruct((n, d), F32)]
    if not last:
        out_shape.append(jax.ShapeDtypeStruct((n, d), BF16))
    row = lambda w: pl.BlockSpec((tm, w), lambda i, dst: (i, 0))
    grid_spec = pltpu.PrefetchScalarGridSpec(
        num_scalar_prefetch=1,
        grid=(n // tm,),
        in_specs=[row(d), row(LANES), mspec, pl.BlockSpec((1, d), lambda i, dst: (0, 0)), mspec,
                  pl.BlockSpec(memory_space=pl.ANY)],
        out_specs=[row(d)] * len(out_shape),
        scratch_shapes=[pltpu.VMEM((2, 2, tm * (d // LANES), LANES), F32), pltpu.SemaphoreType.DMA((2,))])
    return pl.pallas_call(
        functools.partial(_combine_kernel, last=last, n_out=len(out_shape)),
        out_shape=out_shape,
        grid_spec=grid_spec,
        compiler_params=_params(("arbitrary",), 40),
        name="moe_combine",
    )(dst, x, route, mods, g.reshape(1, d), mods_next, ys)


def _dispatch_plan(route, counts, n_tokens):
    tm = MOE_TILE
    max_tiles = (2 * n_tokens) // tm + N_EXPERTS
    cnt = counts[0, N_GROUPS:N_GROUPS + N_EXPERTS].astype(I32)
    tiles_per = (cnt + tm - 1) // tm
    tile_end = jnp.cumsum(tiles_per)
    offs = (tile_end - tiles_per) * tm
    eid = route[:, 0:2].astype(I32)
    e_ids = jnp.arange(N_EXPERTS, dtype=I32)
    offs_of = jnp.sum(jnp.where(eid[..., None] == e_ids, offs, 0), axis=-1)
    dst = offs_of + route[:, 4:6].astype(I32)
    tok = jnp.broadcast_to(jnp.arange(n_tokens, dtype=I32)[:, None], (n_tokens, 2))
    src = jnp.zeros((max_tiles * tm,), I32).at[dst.reshape(-1)].set(tok.reshape(-1))
    n_tiles = tile_end[-1:]
    tile_ids = jnp.minimum(jnp.arange(max_tiles, dtype=I32), n_tiles[0] - 1)
    tile_expert = jnp.sum((tile_end[None, :] <= tile_ids[:, None]).astype(I32), axis=1)
    later = (e_ids[None, :] > e_ids[:, None]) & (tiles_per[None, :] > 0)
    nxt = jnp.min(jnp.where(later, e_ids[None, :], N_EXPERTS), axis=1)
    next_expert = jnp.where(nxt < N_EXPERTS, nxt, -1)[tile_expert]
    within = tile_ids - (tile_end - tiles_per)[tile_expert]
    rows = jnp.clip(cnt[tile_expert] - within * tm, 0, tm)
    tile_groups = (rows + _DMA_GROUP - 1) // _DMA_GROUP
    return dst.reshape(-1), src, tile_expert, next_expert, tile_groups.astype(I32), n_tiles.astype(I32)


def kernel(x, c, ctx, c_ctx, w_mod, b_mod, norm1_g, norm2_g, w_in, w_out, hg_lb_logits, hg_norm_g,
           q_norm_g, k_norm_g, lam_q1, lam_k1, lam_q2, lam_k2, diff_norm_g, router_group_w,
           router_group_b, router_expert_w, router_expert_b, w_gate, w_up, w_down, final_norm_g):
    depth = w_in.shape[0]
    n_lat, d = x.shape[1], x.shape[2]
    n_ctx = ctx.shape[1]
    nct = n_ctx // ROW_TILE
    assert x.shape[0] == 1 and n_ctx % ROW_TILE == 0 and n_lat % ROW_TILE == 0

    stream = (ctx[0], x[0], 0)
    cc = jnp.zeros((8, d), F32).at[0].set(c[0]).at[1].set(c_ctx)
    mods_all = _mod_call(cc, w_mod, b_mod)[:, :2].reshape(depth, 2, N_MOD, d)
    tabs = _rope_tables(n_lat, n_ctx, HEAD_DIM) + _rope_tables(n_lat, n_ctx, HEAD_DIM // 2)

    h1 = _prenorm_call(stream[0], stream[1], norm1_g[0], mods_all[0], nct)
    out = None
    for l in range(depth):
        last = l == depth - 1
        mods = mods_all[l]
        tile_off = nct if last else 0
        tm_in = next(tm for tm in _IN_PROJ_ROW_TILES if (n_ctx + n_lat) % tm == 0)
        p = _matmul_call(h1, w_in, l, tm_in, _WEIGHT_COL_TILE)
        o = _hgrn_call(p, hg_lb_logits, l, nct)
        qa, ka, va, qd, kd, vd = _attn_prep_call(p, tabs, q_norm_g[l], k_norm_g[l])
        att = _flash_call(qa, ka, va, n_ctx, GQA_HEADS // GQA_KV_HEADS, 1, tile_off)
        lam_init = 0.8 - 0.6 * math.exp(-0.3 * l)
        dif = _flash_call(qd, kd, vd, n_ctx, 2, 2, tile_off,
                          extra=(lam_q1[l], lam_k1[l], lam_q2[l], lam_k2[l], diff_norm_g[l]),
                          lam_init=lam_init)
        rw = jnp.concatenate([router_group_w[l], router_expert_w[l],
                              jnp.zeros((d, LANES - N_GROUPS - N_EXPERTS), F32)], axis=1)
        rb = jnp.concatenate([router_group_b[l], router_expert_b[l],
                              jnp.zeros((LANES - N_GROUPS - N_EXPERTS,), F32)]).reshape(1, LANES)
        xn, h2, logits = _outproj_call(o, p, hg_norm_g[l], att, dif, w_out[l].astype(BF16), *stream,
                                       mods, norm2_g[l], rw, nct, tile_off)
        n_tok = xn.shape[0]
        route, counts = _router_call(logits, rb)
        dst, src, tile_expert, next_expert, tile_groups, n_tiles = _dispatch_plan(route, counts, n_tok)
        ys = _moe_call(h2, src, tile_expert, next_expert, tile_groups, n_tiles, w_gate, w_up, w_down, l)
        if last:
            (out,) = _combine_call(dst, xn, route, mods, final_norm_g, mods, ys, 0, True)
        else:
            xall, h1 = _combine_call(dst, xn, route, mods, norm1_g[l + 1], mods_all[l + 1], ys, nct, False)
            stream = (xall, xall, nct)
    return out.reshape(1, n_lat, d)
```

```python
import functools
import math

import numpy as np
import jax
import jax.numpy as jnp
from jax import lax
from jax.experimental import pallas as pl
from jax.experimental.pallas import tpu as pltpu

F32 = jnp.float32
BF16 = jnp.bfloat16
I32 = jnp.int32

HEAD_DIM = 128
LANES = 128
GRID_W = 64
ROPE_THETA = 10000.0
EPS = 1e-6
HG_HEADS = 4
HG_CHUNK = 64
GQA_HEADS = 8
GQA_KV_HEADS = 2
DIFF_HEADS = 4
N_GROUPS = 4
EXPERTS_PER_GROUP = 8
N_EXPERTS = N_GROUPS * EXPERTS_PER_GROUP
N_MOD = 6
ROW_TILE = 256
MOE_TILE = 256
_IN_PROJ_ROW_TILES = (1408, 768, ROW_TILE)
_WEIGHT_COL_TILE = 1024
_DMA_GROUP = 8
MIB = 1024 * 1024

_LOG2E = math.log2(math.e)
_ATT_KEY_BLOCK = 2816
_NT = (((1,), (1,)), ((), ()))
_TN = (((0,), (0,)), ((), ()))


def _params(semantics, vmem_mib):
    return pltpu.CompilerParams(dimension_semantics=semantics, vmem_limit_bytes=vmem_mib * MIB)


def _to_token_major(ref, x):
    n, d = x.shape
    k = d // LANES
    for s in range(k):
        ref[pl.ds(s, n, stride=k), :] = x[:, s * LANES:(s + 1) * LANES]


def _from_token_major(ref, n):
    k = ref.shape[0] // n
    return jnp.concatenate([ref[pl.ds(s, n, stride=k), :] for s in range(k)], axis=1)


def _split_bf16(x, parts):
    out = []
    for _ in range(parts - 1):
        p = x.astype(BF16)
        out.append(p)
        x = x - p.astype(F32)
    out.append(x.astype(BF16))
    return out


def _mod_kernel(a_ref, w_ref, b_ref, o_ref):
    a = a_ref[...]
    a = a * (1.0 / (1.0 + jnp.exp(-a)))
    hi, lo = _split_bf16(a, 2)
    w = w_ref[0].astype(BF16)
    o_ref[0] = (jnp.dot(hi, w, preferred_element_type=F32)
                + jnp.dot(lo, w, preferred_element_type=F32) + b_ref[0])


def _mod_call(cc, w_mod, b_mod):
    depth, d, n = w_mod.shape
    tn = _WEIGHT_COL_TILE
    return pl.pallas_call(
        _mod_kernel,
        out_shape=jax.ShapeDtypeStruct((depth, 8, n), F32),
        grid=(depth, n // tn),
        in_specs=[pl.BlockSpec((8, d), lambda l, j: (0, 0)),
                  pl.BlockSpec((1, d, tn), lambda l, j: (l, 0, j)),
                  pl.BlockSpec((1, 1, tn), lambda l, j: (l, 0, j))],
        out_specs=pl.BlockSpec((1, 8, tn), lambda l, j: (l, 0, j)),
        compiler_params=_params(("arbitrary", "arbitrary"), 40),
        name="mod_vectors",
    )(cc, w_mod, b_mod.reshape(depth, 1, n))


def _norm_mod(x, g, shift, scale):
    y = x * lax.rsqrt(jnp.mean(x * x, axis=-1, keepdims=True) + EPS) * g
    return y * (1.0 + scale) + shift


def _stream_specs(d, nct, tile_off=0, lat_off=0):
    return [pl.BlockSpec((ROW_TILE, d), lambda i: (jnp.minimum(i + tile_off, nct - 1), 0)),
            pl.BlockSpec((ROW_TILE, d), lambda i: (jnp.maximum(i + tile_off - nct, 0) + lat_off, 0))]


def _stream_tile(c_ref, x_ref, nct, tile_off=0):
    return jnp.where(pl.program_id(0) + tile_off < nct, c_ref[...], x_ref[...])


def _prenorm_kernel(c_ref, x_ref, g_ref, m_ref, o_ref, *, nct):
    m = m_ref[0]
    o_ref[...] = _norm_mod(_stream_tile(c_ref, x_ref, nct), g_ref[...], m[0:1], m[1:2]).astype(BF16)


def _mod_spec(d, nct, tile_off=0):
    return pl.BlockSpec((1, N_MOD, d), lambda i: (jnp.where(i + tile_off < nct, 1, 0), 0, 0))


def _prenorm_call(xc, xl, g, mods, nct):
    d = xl.shape[1]
    t = xc.shape[0] + xl.shape[0]
    return pl.pallas_call(
        functools.partial(_prenorm_kernel, nct=nct),
        out_shape=jax.ShapeDtypeStruct((t, d), BF16),
        grid=(t // ROW_TILE,),
        in_specs=_stream_specs(d, nct) + [pl.BlockSpec((1, d), lambda i: (0, 0)), _mod_spec(d, nct)],
        out_specs=pl.BlockSpec((ROW_TILE, d), lambda i: (i, 0)),
        compiler_params=_params(("arbitrary",), 24),
        name="prenorm",
    )(xc, xl, g.reshape(1, d), mods)


def _mm_kernel(a_ref, b_ref, o_ref, b_bf):
    @pl.when(pl.program_id(1) == 0)
    def _():
        b_bf[...] = b_ref[0].astype(BF16)

    o_ref[...] = jnp.dot(a_ref[...], b_bf[...], preferred_element_type=F32)


def _matmul_call(a, b, layer, tm, tn):
    m, k = a.shape
    n = b.shape[2]
    return pl.pallas_call(
        _mm_kernel,
        out_shape=jax.ShapeDtypeStruct((m, n), F32),
        grid=(n // tn, m // tm),
        in_specs=[pl.BlockSpec((tm, k), lambda j, i: (i, 0)),
                  pl.BlockSpec((1, k, tn), lambda j, i: (layer, 0, j))],
        out_specs=pl.BlockSpec((tm, tn), lambda j, i: (i, j)),
        scratch_shapes=[pltpu.VMEM((k, tn), BF16)],
        compiler_params=_params(("arbitrary", "arbitrary"), 56),
        name="in_proj",
    )(a, b)


_HG_LEVELS = (1, 2, 4, 8, 16, 32)
_HG_TOT_ROW = HG_CHUNK * (len(_HG_LEVELS) + 1)
_HG_W_ROWS = _HG_TOT_ROW + 16
_HG_HEADS_PER_STEP = 4


def _hgrn_consts():
    c = HG_CHUNK
    w = np.zeros((2, _HG_W_ROWS, c), np.float32)
    msk = np.zeros((2, len(_HG_LEVELS) + 1, c, c), np.float32)
    for d in range(2):
        u = np.arange(c) if d == 0 else c - 1 - np.arange(c)
        ut, us = u[:, None], u[None, :]
        w[d, :c] = us <= ut
        for li, lv in enumerate(_HG_LEVELS):
            blk = u // (2 * lv)
            qside = (u % (2 * lv)) >= lv
            bnd = (blk * 2 * lv + lv - 1)[:, None]
            wq = (us > bnd) & (us <= ut)
            wk = (us > ut) & (us <= bnd)
            w[d, c * (li + 1):c * (li + 2)] = np.where(qside[:, None], wq, -1.0 * wk)
            msk[d, li] = (blk[:, None] == blk[None, :]) & qside[:, None] & ~qside[None, :]
        msk[d, len(_HG_LEVELS)] = np.eye(c)
        w[d, _HG_TOT_ROW:] = 1.0
    return jnp.asarray(np.concatenate([w, w, w], axis=2), BF16), jnp.asarray(msk, F32)


def _hgrn_kernel(lbl_ref, q_ref, z_ref, v_ref, w_ref, msk_ref, o_ref, st_ref, *, layer, chunks, hps):
    c = HG_CHUNK
    hd = HEAD_DIM
    d = pl.program_id(0)
    j = pl.program_id(2)

    @pl.when(j == 0)
    def _():
        st_ref[...] = jnp.zeros_like(st_ref)

    lbl = lbl_ref[...]
    rows = [lbl[i:i + 1] for i in range(lbl.shape[0])]
    mx = functools.reduce(jnp.maximum, rows)
    ex = [jnp.exp(r - mx) for r in rows]
    tot = functools.reduce(lambda a, b: a + b, ex)
    lb = jnp.zeros_like(mx)
    for i in range(1, layer + 1):
        lb = lb + ex[i] / tot
    log_lb = jnp.log(lb)
    log_1m_lb = jnp.log1p(-lb)

    wmat = w_ref[0]
    nlev = len(_HG_LEVELS)

    def finish(h, hs, r0, amat, v_bf, qe, kd, decay):
        st = st_ref[h]
        o = (jnp.dot(amat, v_bf[:, hs], preferred_element_type=F32)
             + lax.dot_general(qe[:, hs], st.astype(BF16), _NT, preferred_element_type=F32))
        o_ref[0, pl.ds(r0, c), hs] = o
        st_ref[h] = st * decay[:, hs] + lax.dot_general(v_bf[:, hs], kd[:, hs], _TN,
                                                        preferred_element_type=F32)

    pending = None
    for ci in range(chunks):
        cc = ci + d * (chunks - 1 - 2 * ci)
        r0 = pl.multiple_of(cc * c, c)
        q = q_ref[pl.ds(r0, c), :]
        z = z_ref[pl.ds(r0, c), :]
        v_bf = v_ref[pl.ds(r0, c), :].astype(BF16)
        l1p = jnp.log(1.0 + jnp.exp(-jnp.abs(z)))
        ls_pos = jnp.minimum(z, 0.0) - l1p
        ls_neg = jnp.minimum(-z, 0.0) - l1p
        a2 = log_1m_lb + ls_pos
        logf = jnp.maximum(log_lb, a2) + jnp.log(1.0 + jnp.exp(-jnp.abs(log_lb - a2)))
        k = (1.0 - lb) * jnp.exp(ls_neg)

        parts = jnp.concatenate(_split_bf16(logf, 3), axis=0)
        sums = jnp.dot(wmat, parts, preferred_element_type=F32)
        b = sums[0:c]
        btot = sums[_HG_TOT_ROW:_HG_TOT_ROW + 1]

        qls, kls = [q.astype(BF16)], [k.astype(BF16)]
        for li in range(nlev):
            fl = jnp.exp(-jnp.abs(sums[c * (li + 1):c * (li + 2)]))
            qls.append((q * fl).astype(BF16))
            kls.append((k * fl).astype(BF16))
        qe = (q * jnp.exp(b)).astype(BF16)
        kd = (k * jnp.exp(btot - b)).astype(BF16)
        decay = jnp.exp(btot)

        for h in range(hps):
            hs = slice(h * hd, (h + 1) * hd)
            amat = lax.dot_general(qls[0][:, hs], kls[0][:, hs], _NT,
                                   preferred_element_type=F32) * msk_ref[0, nlev]
            for li in range(nlev):
                amat = amat + lax.dot_general(qls[li + 1][:, hs], kls[li + 1][:, hs], _NT,
                                              preferred_element_type=F32) * msk_ref[0, li]
            if pending is not None:
                finish(*pending)
            pending = (h, hs, r0, amat.astype(BF16), v_bf, qe, kd, decay)
    finish(*pending)


def _hgrn_call(p, lb_logits, layer, nct):
    t = p.shape[0]
    nblk = t // ROW_TILE
    chunks = ROW_TILE // HG_CHUNK
    hps = _HG_HEADS_PER_STEP
    hw = hps * HEAD_DIM
    ng = HG_HEADS // hps
    wmat, msk = _hgrn_consts()

    def blk(d, j):
        back = jnp.where(j < nct, nct - 1 - j, nblk - 1 - (j - nct))
        return jnp.where(d == 0, j, back)

    return pl.pallas_call(
        functools.partial(_hgrn_kernel, layer=layer, chunks=chunks, hps=hps),
        out_shape=jax.ShapeDtypeStruct((2, t, HG_HEADS * HEAD_DIM), F32),
        grid=(2, ng, nblk),
        in_specs=[pl.BlockSpec((lb_logits.shape[0], hw), lambda d, h, j: (0, h)),
                  pl.BlockSpec((ROW_TILE, hw), lambda d, h, j: (blk(d, j), d * ng + h)),
                  pl.BlockSpec((ROW_TILE, hw), lambda d, h, j: (blk(d, j), (2 + d) * ng + h)),
                  pl.BlockSpec((ROW_TILE, hw), lambda d, h, j: (blk(d, j), 4 * ng + h)),
                  pl.BlockSpec((1, _HG_W_ROWS, 3 * HG_CHUNK), lambda d, h, j: (d, 0, 0)),
                  pl.BlockSpec((1, len(_HG_LEVELS) + 1, HG_CHUNK, HG_CHUNK), lambda d, h, j: (d, 0, 0, 0))],
        out_specs=pl.BlockSpec((1, ROW_TILE, hw), lambda d, h, j: (d, blk(d, j), h)),
        scratch_shapes=[pltpu.VMEM((hps, HEAD_DIM, HEAD_DIM), F32)],
        compiler_params=_params(("arbitrary", "arbitrary", "arbitrary"), 24),
        name="hgrn_scan",
    )(lb_logits, p, p, p, wmat, msk)


def _rope_tables(n_lat, n_ctx, dim):
    rows = n_lat // GRID_W
    row = jnp.repeat(jnp.arange(rows, dtype=F32), GRID_W)
    col = jnp.tile(jnp.arange(GRID_W, dtype=F32), rows)
    axis_dim = dim // 2
    inv_freq = ROPE_THETA ** (-jnp.arange(0, axis_dim, 2, dtype=F32) / axis_dim)
    ang = jnp.concatenate([row[:, None] * inv_freq, col[:, None] * inv_freq], axis=-1)
    cos = jnp.repeat(jnp.cos(ang), 2, axis=1)
    sin = jnp.repeat(jnp.sin(ang), 2, axis=1) * jnp.tile(jnp.asarray([-1.0, 1.0], F32), dim // 2)
    reps = LANES // dim
    cos = jnp.concatenate([jnp.ones((n_ctx, dim), F32), cos], axis=0)
    sin = jnp.concatenate([jnp.zeros((n_ctx, dim), F32), sin], axis=0)
    return jnp.tile(cos, (1, reps)), jnp.tile(sin, (1, reps))


def _rope(x, cos, sin):
    lane = lax.broadcasted_iota(I32, x.shape, 1)
    swapped = jnp.where((lane & 1) == 0, pltpu.roll(x, LANES - 1, 1), pltpu.roll(x, 1, 1))
    return x * cos + swapped * sin


def _head_norm(x, g):
    return x * lax.rsqrt(jnp.mean(x * x, axis=-1, keepdims=True) + EPS) * g


def _attn_prep_kernel(pq_ref, pkv_ref, pd_ref, ca_ref, sa_ref, cd_ref, sd_ref, qn_ref, kn_ref,
                      qa_ref, ka_ref, va_ref, qd_ref, kd_ref, vd_ref):
    hd = HEAD_DIM
    ca, sa, cd, sd = ca_ref[...], sa_ref[...], cd_ref[...], sd_ref[...]
    qn, kn = qn_ref[...], kn_ref[...]
    for h in range(GQA_HEADS):
        xq = _rope(_head_norm(pq_ref[:, h * hd:(h + 1) * hd], qn), ca, sa)
        qa_ref[:, h * hd:(h + 1) * hd] = (xq * (hd ** -0.5 * _LOG2E)).astype(BF16)
    for h in range(GQA_KV_HEADS):
        xk = _rope(_head_norm(pkv_ref[:, h * hd:(h + 1) * hd], kn), ca, sa)
        ka_ref[:, h * hd:(h + 1) * hd] = xk.astype(BF16)
    ones_col = jnp.where(lax.broadcasted_iota(I32, (pq_ref.shape[0], hd), 1) == 0, 1.0, 0.0).astype(BF16)
    for h in range(GQA_KV_HEADS):
        va_ref[:, (2 * h) * hd:(2 * h + 1) * hd] = pkv_ref[:, (GQA_KV_HEADS + h) * hd:
                                                            (GQA_KV_HEADS + h + 1) * hd].astype(BF16)
        va_ref[:, (2 * h + 1) * hd:(2 * h + 2) * hd] = ones_col
    first = lax.broadcasted_iota(I32, (pq_ref.shape[0], hd), 1) < hd // 2
    dw = DIFF_HEADS * hd
    for h in range(DIFF_HEADS):
        xq = _rope(pd_ref[:, h * hd:(h + 1) * hd], cd, sd) * ((hd // 2) ** -0.5 * _LOG2E)
        qd_ref[:, (2 * h) * hd:(2 * h + 1) * hd] = jnp.where(first, xq, 0.0).astype(BF16)
        qd_ref[:, (2 * h + 1) * hd:(2 * h + 2) * hd] = jnp.where(first, 0.0, xq).astype(BF16)
        xk = _rope(pd_ref[:, dw + h * hd:dw + (h + 1) * hd], cd, sd)
        kd_ref[:, h * hd:(h + 1) * hd] = xk.astype(BF16)
    for h in range(DIFF_HEADS):
        vd_ref[:, (2 * h) * hd:(2 * h + 1) * hd] = pd_ref[:, 2 * dw + h * hd:2 * dw + (h + 1) * hd].astype(BF16)
        vd_ref[:, (2 * h + 1) * hd:(2 * h + 2) * hd] = ones_col


def _attn_prep_call(p, tabs, qn, kn):
    t = p.shape[0]
    hd = HEAD_DIM
    gq, gkv, dw = GQA_HEADS * hd, 2 * GQA_KV_HEADS * hd, DIFF_HEADS * hd
    q_off = 6 * HG_HEADS * hd
    assert q_off % gq == 0 and (q_off + gq) % gkv == 0 and (q_off + gq + gkv) % (3 * dw) == 0
    tp = next(n for n in (768, ROW_TILE) if t % n == 0)
    row = lambda w: pl.BlockSpec((tp, w), lambda i: (i, 0))
    vec = pl.BlockSpec((1, hd), lambda i: (0, 0))
    return pl.pallas_call(
        _attn_prep_kernel,
        out_shape=[jax.ShapeDtypeStruct((t, gq), BF16),
                   jax.ShapeDtypeStruct((t, gkv // 2), BF16),
                   jax.ShapeDtypeStruct((t, gkv), BF16),
                   jax.ShapeDtypeStruct((t, 2 * dw), BF16),
                   jax.ShapeDtypeStruct((t, dw), BF16),
                   jax.ShapeDtypeStruct((t, 2 * dw), BF16)],
        grid=(t // tp,),
        in_specs=[pl.BlockSpec((tp, gq), lambda i: (i, q_off // gq)),
                  pl.BlockSpec((tp, gkv), lambda i: (i, (q_off + gq) // gkv)),
                  pl.BlockSpec((tp, 3 * dw), lambda i: (i, (q_off + gq + gkv) // (3 * dw))),
                  row(hd), row(hd), row(hd), row(hd), vec, vec],
        out_specs=[row(gq), row(gkv // 2), row(gkv), row(2 * dw), row(dw), row(2 * dw)],
        compiler_params=_params(("arbitrary",), 48),
        name="attn_prep",
    )(p, p, p, *tabs, qn.reshape(1, hd), kn.reshape(1, hd))


def _flash_kernel(*refs, g, nh, n_ctx, n_lat, tkl, nct, tile_off, diff, lam_init):
    if diff:
        q_ref, k_ref, v_ref, lq1, lk1, lq2, lk2, dn_ref, o_ref = refs[:9]
    else:
        q_ref, k_ref, v_ref, o_ref = refs[:4]
    nc = nh * g
    m_sc, acc_sc, p_sc = (refs[len(refs) - (3 - n) * nc:len(refs) - (2 - n) * nc] for n in range(3))
    hd = HEAD_DIM
    i = pl.program_id(1) + tile_off
    nblk = (n_ctx + n_lat) // tkl
    chains = [(c, c // g) for c in range(nh * g)]

    def head(ref, rows, h, width=1):
        return ref[rows, h * width * hd:(h + 1) * width * hd]

    def vhead(rows, h):
        return head(v_ref, rows, h, 2)

    def scores(c, kh, rows):
        return lax.dot_general(head(q_ref, slice(None), c), head(k_ref, rows, kh), _NT,
                               preferred_element_type=F32)

    def emit(outputs):
        norm = lambda a: a[:, :hd] / a[:, hd:hd + 1]
        if diff:
            lam = (jnp.exp(jnp.sum(lq1[...] * lk1[...], axis=1, keepdims=True))
                   - jnp.exp(jnp.sum(lq2[...] * lk2[...], axis=1, keepdims=True)) + lam_init)
            for kh in range(nh):
                dd = norm(outputs[2 * kh]) - lam * norm(outputs[2 * kh + 1])
                o_ref[:, kh * hd:(kh + 1) * hd] = (_head_norm(dd, dn_ref[...])
                                                   * (1.0 - lam_init)).astype(BF16)
        else:
            for c, _ in chains:
                o_ref[:, c * hd:(c + 1) * hd] = norm(outputs[c]).astype(BF16)

    @pl.when(i < nct)
    def _():
        ctx_rows = slice(0, n_ctx)
        outs = []
        for c, kh in chains:
            s = scores(c, kh, ctx_rows)
            p = jnp.exp2(s - jnp.max(s, axis=1, keepdims=True))
            outs.append(jnp.dot(p.astype(BF16), vhead(ctx_rows, kh), preferred_element_type=F32))
        emit(outs)

    @pl.when(i >= nct)
    def _():
        for jb in range(nblk):
            rows = pl.ds(jb * tkl, tkl)
            slot = jb % 2
            alphas = []
            for c, kh in chains:
                s = scores(c, kh, rows)
                m_cur = jnp.max(s, axis=1, keepdims=True)
                if jb == 0:
                    m_new = jnp.broadcast_to(m_cur, m_sc[c].shape)
                else:
                    m_prev = m_sc[c][...]
                    m_new = jnp.maximum(m_prev, m_cur)
                    alphas.append(jnp.tile(jnp.exp2(m_prev - m_new), (1, 2)))
                p_sc[c][slot] = jnp.exp2(s - jnp.tile(m_new, (1, tkl // LANES))).astype(BF16)
                m_sc[c][...] = m_new
            if jb == 0:
                continue
            prev = pl.ds((jb - 1) * tkl, tkl)
            for c, kh in chains:
                pv = jnp.dot(p_sc[c][1 - slot], vhead(prev, kh), preferred_element_type=F32)
                acc_sc[c][...] = alphas[c] * (pv if jb == 1 else acc_sc[c][...] + pv)

        last = pl.ds((nblk - 1) * tkl, tkl)
        outs = []
        for c, kh in chains:
            pv = jnp.dot(p_sc[c][(nblk - 1) % 2], vhead(last, kh), preferred_element_type=F32)
            outs.append(pv if nblk == 1 else acc_sc[c][...] + pv)
        emit(outs)


def _flash_call(q, k, v, n_ctx, g, nh, tile_off, extra=None, lam_init=0.0):
    t = k.shape[0]
    hd = HEAD_DIM
    n_kv = k.shape[1] // (nh * hd)
    n_lat = t - n_ctx
    nct = n_ctx // ROW_TILE
    nblk = next(n for n in range(1, t // LANES + 1)
                if t % (n * LANES) == 0 and t // n <= _ATT_KEY_BLOCK)
    tkl = t // nblk
    diff = extra is not None
    ow = nh * hd if diff else nh * g * hd
    resident = pl.Buffered(1)
    in_specs = [pl.BlockSpec((ROW_TILE, nh * g * hd), lambda kv, i: (i + tile_off, kv)),
                pl.BlockSpec((t, nh * hd), lambda kv, i: (0, kv), pipeline_mode=resident),
                pl.BlockSpec((t, 2 * nh * hd), lambda kv, i: (0, kv), pipeline_mode=resident)]
    args = [q, k, v]
    if diff:
        for a in extra:
            a = a.reshape(1, -1)
            in_specs.append(pl.BlockSpec(a.shape, lambda kv, i: (0, 0)))
            args.append(a)
    return pl.pallas_call(
        functools.partial(_flash_kernel, g=g, nh=nh, n_ctx=n_ctx, n_lat=n_lat, tkl=tkl, nct=nct,
                          tile_off=tile_off, diff=diff, lam_init=lam_init),
        out_shape=jax.ShapeDtypeStruct((t - tile_off * ROW_TILE, n_kv * ow), BF16),
        grid=(n_kv, t // ROW_TILE - tile_off),
        in_specs=in_specs,
        out_specs=pl.BlockSpec((ROW_TILE, ow), lambda kv, i: (i, kv)),
        scratch_shapes=([pltpu.VMEM((ROW_TILE, LANES), F32)] * (nh * g)
                        + [pltpu.VMEM((ROW_TILE, 2 * hd), F32)] * (nh * g)
                        + [pltpu.VMEM((2, ROW_TILE, tkl), BF16)] * (nh * g)),
        compiler_params=_params(("arbitrary", "arbitrary"), 48),
        name="diff_attention" if diff else "gqa_attention",
    )(*args)


def _outproj_kernel(of_ref, ob_ref, gate_ref, ng_ref, att_ref, dif_ref, wo_ref, xc_ref, xl_ref, m_ref,
                    n2_ref, rw_ref, xo_ref, h2_ref, lg_ref, *, nct, tile_off):
    hd = HEAD_DIM
    hgw = HG_HEADS * hd
    ng = ng_ref[...]
    gate = gate_ref[...]
    silu_gate = gate * (1.0 / (1.0 + jnp.exp(-gate)))
    acc = jnp.dot(att_ref[...], wo_ref[hgw:hgw + att_ref.shape[1], :], preferred_element_type=F32)
    acc = acc + jnp.dot(dif_ref[...], wo_ref[hgw + att_ref.shape[1]:, :], preferred_element_type=F32)
    hg = []
    for h in range(HG_HEADS):
        o = of_ref[0, :, h * hd:(h + 1) * hd] + ob_ref[0, :, h * hd:(h + 1) * hd]
        hg.append((_head_norm(o, ng) * silu_gate[:, h * hd:(h + 1) * hd]).astype(BF16))
    acc = acc + jnp.dot(jnp.concatenate(hg, axis=1), wo_ref[0:hgw, :], preferred_element_type=F32)
    m = m_ref[0]
    xn = _stream_tile(xc_ref, xl_ref, nct, tile_off) + m[2:3] * acc
    xo_ref[...] = xn
    h2 = _norm_mod(xn, n2_ref[...], m[3:4], m[4:5])
    _to_token_major(h2_ref, h2)
    hs = _split_bf16(h2, 2)
    ws = _split_bf16(rw_ref[...], 2)
    both = jnp.dot(hs[0], jnp.concatenate(ws, axis=1), preferred_element_type=F32)
    lg_ref[...] = (jnp.dot(hs[1], ws[0], preferred_element_type=F32) + both[:, LANES:]) + both[:, :LANES]


def _outproj_call(o, p, ng, att, dif, wo, xc, xl, lat_off, mods, n2, rw, nct, tile_off):
    d = xl.shape[1]
    hd = HEAD_DIM
    hgw = HG_HEADS * hd
    n_rows = att.shape[0]
    row = lambda w: pl.BlockSpec((ROW_TILE, w), lambda i: (i, 0))
    full = lambda a: pl.BlockSpec(a.shape, lambda i: (0,) * a.ndim)
    ng, n2 = ng.reshape(1, hd), n2.reshape(1, d)
    return pl.pallas_call(
        functools.partial(_outproj_kernel, nct=nct, tile_off=tile_off),
        out_shape=[jax.ShapeDtypeStruct((n_rows, d), F32),
                   jax.ShapeDtypeStruct((n_rows * (d // LANES), LANES), F32),
                   jax.ShapeDtypeStruct((n_rows, LANES), F32)],
        grid=(n_rows // ROW_TILE,),
        in_specs=[pl.BlockSpec((1, ROW_TILE, hgw), lambda i: (0, i + tile_off, 0)),
                  pl.BlockSpec((1, ROW_TILE, hgw), lambda i: (1, i + tile_off, 0)),
                  pl.BlockSpec((ROW_TILE, hgw), lambda i: (i + tile_off, 5)),
                  full(ng), row(att.shape[1]), row(dif.shape[1]), full(wo)]
                 + _stream_specs(d, nct, tile_off, lat_off)
                 + [_mod_spec(d, nct, tile_off), full(n2), full(rw)],
        out_specs=[row(d), pl.BlockSpec((ROW_TILE * (d // LANES), LANES), lambda i: (i, 0)), row(LANES)],
        compiler_params=_params(("arbitrary",), 56),
        name="out_proj",
    )(o, o, p, ng, att, dif, wo, xc, xl, mods, n2, rw)


def _router_kernel(lg_ref, bias_ref, r_ref, cnt_ref, run_ref):
    i = pl.program_id(0)
    tm = lg_ref.shape[0]

    @pl.when(i == 0)
    def _():
        run_ref[...] = jnp.zeros_like(run_ref)

    lane = lax.broadcasted_iota(I32, (tm, LANES), 1).astype(F32)
    lg = lg_ref[...] + bias_ref[...]
    ninf = -jnp.inf

    def first_max(vals):
        mx = jnp.max(vals, axis=1, keepdims=True)
        idx = jnp.min(jnp.where(vals == mx, lane, float(LANES)), axis=1, keepdims=True)
        return mx, idx

    gl = jnp.where(lane < N_GROUPS, lg, ninf)
    gmax, gidx = first_max(gl)
    g_top = 1.0 / jnp.sum(jnp.exp(gl - gmax), axis=1, keepdims=True)
    lo = N_GROUPS + EXPERTS_PER_GROUP * gidx
    el = jnp.where((lane >= lo) & (lane < lo + EXPERTS_PER_GROUP), lg, ninf)
    m1, e1 = first_max(el)
    m2, e2 = first_max(jnp.where(lane == e1, ninf, el))
    r = jnp.exp(m2 - m1)
    w1 = g_top / (1.0 + r)
    w2 = g_top * r / (1.0 + r)

    hit = ((lane == e1) | (lane == e2)).astype(BF16)
    ti = lax.broadcasted_iota(I32, (tm, tm), 0)
    si = lax.broadcasted_iota(I32, (tm, tm), 1)
    before = (si < ti).astype(BF16)
    pos = jnp.dot(before, hit, preferred_element_type=F32) + run_ref[0:1, :]
    p1 = jnp.sum(jnp.where(lane == e1, pos, 0.0), axis=1, keepdims=True)
    p2 = jnp.sum(jnp.where(lane == e2, pos, 0.0), axis=1, keepdims=True)
    total = run_ref[0:1, :] + jnp.sum(hit.astype(F32), axis=0, keepdims=True)
    run_ref[...] = jnp.broadcast_to(total, run_ref.shape)
    cnt_ref[...] = jnp.broadcast_to(total, cnt_ref.shape)

    fields = (e1 - N_GROUPS, e2 - N_GROUPS, w1, w2, p1, p2)
    out = jnp.zeros((tm, LANES), F32)
    for n, f in enumerate(fields):
        out = jnp.where(lane == n, f, out)
    r_ref[...] = out


def _router_call(logits, bias):
    n = logits.shape[0]
    return pl.pallas_call(
        _router_kernel,
        out_shape=[jax.ShapeDtypeStruct((n, LANES), F32), jax.ShapeDtypeStruct((8, LANES), F32)],
        grid=(n // ROW_TILE,),
        in_specs=[pl.BlockSpec((ROW_TILE, LANES), lambda i: (i, 0)),
                  pl.BlockSpec((1, LANES), lambda i: (0, 0))],
        out_specs=[pl.BlockSpec((ROW_TILE, LANES), lambda i: (i, 0)),
                   pl.BlockSpec((8, LANES), lambda i: (0, 0))],
        scratch_shapes=[pltpu.VMEM((8, LANES), F32)],
        compiler_params=_params(("arbitrary",), 16),
        name="router",
    )(logits, bias)


def _moe_kernel(src_ref, te_ref, nx_ref, ng_ref, nt_ref, h_hbm, wg_hbm, wu_hbm, wd_hbm, y_ref,
                xbuf, sem, wg_st, wu_st, wd_st, wsem, wg_bf, wu_bf, wd_bf, *, layer):
    i = pl.program_id(0)
    n_tiles = nt_ref[0]
    tm = MOE_TILE
    k = xbuf.shape[1] // tm
    grp_rows = _DMA_GROUP * k

    def weight_copies(e):
        pairs = ((wg_hbm, wg_st), (wu_hbm, wu_st), (wd_hbm, wd_st))
        return [pltpu.make_async_copy(w.at[layer, e], st, wsem.at[n]) for n, (w, st) in enumerate(pairs)]

    def start_tile(tile, slot):
        def body(grp, carry):
            for u in range(_DMA_GROUP):
                r = grp * _DMA_GROUP + u
                src_row = pl.multiple_of(src_ref[tile * tm + r] * k, k)
                pltpu.make_async_copy(h_hbm.at[pl.ds(src_row, k), :],
                                      xbuf.at[slot, pl.ds(pl.multiple_of(r * k, k), k), :],
                                      sem.at[slot]).start(priority=u % 2)
            return carry
        lax.fori_loop(0, ng_ref[tile], body, 0)

    def wait_tile(tile, slot):
        def body(grp, carry):
            pltpu.make_async_copy(h_hbm.at[pl.ds(0, grp_rows), :], xbuf.at[slot, pl.ds(0, grp_rows), :],
                                  sem.at[slot]).wait()
            return carry
        lax.fori_loop(0, ng_ref[tile], body, 0)

    @pl.when(i == 0)
    def _():
        xbuf[...] = jnp.zeros_like(xbuf)
        start_tile(0, 0)
        for cp in weight_copies(te_ref[0]):
            cp.start()

    @pl.when(i < n_tiles)
    def _():
        slot = i % 2

        @pl.when(i + 1 < n_tiles)
        def _():
            start_tile(i + 1, 1 - slot)

        changed = jnp.logical_or(i == 0, te_ref[i] != te_ref[jnp.maximum(i - 1, 0)])

        @pl.when(changed)
        def _():
            for cp in weight_copies(te_ref[i]):
                cp.wait()
            wg_bf[...] = wg_st[...].astype(BF16)
            wu_bf[...] = wu_st[...].astype(BF16)
            wd_bf[...] = wd_st[...].astype(BF16)

            @pl.when(nx_ref[i] >= 0)
            def _():
                for cp in weight_copies(nx_ref[i]):
                    cp.start()

        wait_tile(i, slot)
        xb = _from_token_major(xbuf.at[slot], tm).astype(BF16)
        gt = jnp.dot(xb, wg_bf[...], preferred_element_type=F32)
        up = jnp.dot(xb, wu_bf[...], preferred_element_type=F32)
        act = (gt * (1.0 / (1.0 + jnp.exp(-gt))) * up).astype(BF16)
        _to_token_major(y_ref, jnp.dot(act, wd_bf[...], preferred_element_type=F32))

    @pl.when(i >= n_tiles)
    def _():
        y_ref[...] = jnp.zeros_like(y_ref)


def _moe_call(h2, src, tile_expert, next_expert, tile_groups, n_tiles, wg, wu, wd, layer):
    d, ff = wg.shape[2], wg.shape[3]
    k = d // LANES
    max_tiles = tile_expert.shape[0]
    tm = MOE_TILE
    hbm = pl.BlockSpec(memory_space=pl.ANY)
    grid_spec = pltpu.PrefetchScalarGridSpec(
        num_scalar_prefetch=5,
        grid=(max_tiles,),
        in_specs=[hbm, hbm, hbm, hbm],
        out_specs=pl.BlockSpec((tm * k, LANES), lambda i, src, te, nx, ng, nt: (i, 0)),
        scratch_shapes=[pltpu.VMEM((2, tm * k, LANES), F32), pltpu.SemaphoreType.DMA((2,)),
                        pltpu.VMEM((d, ff), F32), pltpu.VMEM((d, ff), F32), pltpu.VMEM((ff, d), F32),
                        pltpu.SemaphoreType.DMA((3,)),
                        pltpu.VMEM((d, ff), BF16), pltpu.VMEM((d, ff), BF16), pltpu.VMEM((ff, d), BF16)])
    return pl.pallas_call(
        functools.partial(_moe_kernel, layer=layer),
        out_shape=jax.ShapeDtypeStruct((max_tiles * tm * k, LANES), F32),
        grid_spec=grid_spec,
        compiler_params=_params(("arbitrary",), 40),
        name="moe_experts",
    )(src, tile_expert, next_expert, tile_groups, n_tiles, h2, wg, wu, wd)


def _combine_kernel(dst_ref, x_ref, rt_ref, m_ref, g_ref, mn_ref, y_hbm, *rest, last, n_out):
    outs, (ybuf, sem) = rest[:n_out], rest[n_out:]
    i = pl.program_id(0)
    tm = x_ref.shape[0]
    k = ybuf.shape[2] // tm

    def start_tile(tile, slot):
        def body(grp, carry):
            for u in range(_DMA_GROUP):
                r = grp * _DMA_GROUP + u
                for j in range(2):
                    src_row = pl.multiple_of(dst_ref[(tile * tm + r) * 2 + j] * k, k)
                    pltpu.make_async_copy(y_hbm.at[pl.ds(src_row, k), :],
                                          ybuf.at[slot, j, pl.ds(pl.multiple_of(r * k, k), k), :],
                                          sem.at[slot]).start(priority=j)
            return carry
        lax.fori_loop(0, tm // _DMA_GROUP, body, 0)

    @pl.when(i == 0)
    def _():
        start_tile(0, 0)

    slot = i % 2

    @pl.when(i + 1 < pl.num_programs(0))
    def _():
        start_tile(i + 1, 1 - slot)

    for j in range(2):
        pltpu.make_async_copy(y_hbm.at[pl.ds(0, tm * k), :], ybuf.at[slot, j], sem.at[slot]).wait()
    rt = rt_ref[...]
    y = (rt[:, 2:3] * _from_token_major(ybuf.at[slot, 0], tm)
         + rt[:, 3:4] * _from_token_major(ybuf.at[slot, 1], tm))
    xn = x_ref[...] + m_ref[0][5:6] * y
    if last:
        outs[0][...] = xn * lax.rsqrt(jnp.mean(xn * xn, axis=-1, keepdims=True) + EPS) * g_ref[...]
    else:
        outs[0][...] = xn
        mn = mn_ref[0]
        outs[1][...] = _norm_mod(xn, g_ref[...], mn[0:1], mn[1:2]).astype(BF16)


def _combine_call(dst, x, route, mods, g, mods_next, ys, nct, last):
    n, d = x.shape
    tm = ROW_TILE
    mspec = pl.BlockSpec((1, N_MOD, d), lambda i, dst: (jnp.where(i < nct, 1, 0), 0, 0))
    out_shape = [jax.ShapeDtypeStruct((n, d), F32)]
    if not last:
        out_shape.append(jax.ShapeDtypeStruct((n, d), BF16))
    row = lambda w: pl.BlockSpec((tm, w), lambda i, dst: (i, 0))
    grid_spec = pltpu.PrefetchScalarGridSpec(
        num_scalar_prefetch=1,
        grid=(n // tm,),
        in_specs=[row(d), row(LANES), mspec, pl.BlockSpec((1, d), lambda i, dst: (0, 0)), mspec,
                  pl.BlockSpec(memory_space=pl.ANY)],
        out_specs=[row(d)] * len(out_shape),
        scratch_shapes=[pltpu.VMEM((2, 2, tm * (d // LANES), LANES), F32), pltpu.SemaphoreType.DMA((2,))])
    return pl.pallas_call(
        functools.partial(_combine_kernel, last=last, n_out=len(out_shape)),
        out_shape=out_shape,
        grid_spec=grid_spec,
        compiler_params=_params(("arbitrary",), 40),
        name="moe_combine",
    )(dst, x, route, mods, g.reshape(1, d), mods_next, ys)


def _dispatch_plan(route, counts, n_tokens):
    tm = MOE_TILE
    max_tiles = (2 * n_tokens) // tm + N_EXPERTS
    cnt = counts[0, N_GROUPS:N_GROUPS + N_EXPERTS].astype(I32)
    tiles_per = (cnt + tm - 1) // tm
    tile_end = jnp.cumsum(tiles_per)
    offs = (tile_end - tiles_per) * tm
    eid = route[:, 0:2].astype(I32)
    e_ids = jnp.arange(N_EXPERTS, dtype=I32)
    offs_of = jnp.sum(jnp.where(eid[..., None] == e_ids, offs, 0), axis=-1)
    dst = offs_of + route[:, 4:6].astype(I32)
    tok = jnp.broadcast_to(jnp.arange(n_tokens, dtype=I32)[:, None], (n_tokens, 2))
    src = jnp.zeros((max_tiles * tm,), I32).at[dst.reshape(-1)].set(tok.reshape(-1))
    n_tiles = tile_end[-1:]
    tile_ids = jnp.minimum(jnp.arange(max_tiles, dtype=I32), n_tiles[0] - 1)
    tile_expert = jnp.sum((tile_end[None, :] <= tile_ids[:, None]).astype(I32), axis=1)
    later = (e_ids[None, :] > e_ids[:, None]) & (tiles_per[None, :] > 0)
    nxt = jnp.min(jnp.where(later, e_ids[None, :], N_EXPERTS), axis=1)
    next_expert = jnp.where(nxt < N_EXPERTS, nxt, -1)[tile_expert]
    within = tile_ids - (tile_end - tiles_per)[tile_expert]
    rows = jnp.clip(cnt[tile_expert] - within * tm, 0, tm)
    tile_groups = (rows + _DMA_GROUP - 1) // _DMA_GROUP
    return dst.reshape(-1), src, tile_expert, next_expert, tile_groups.astype(I32), n_tiles.astype(I32)


def kernel(x, c, ctx, c_ctx, w_mod, b_mod, norm1_g, norm2_g, w_in, w_out, hg_lb_logits, hg_norm_g,
           q_norm_g, k_norm_g, lam_q1, lam_k1, lam_q2, lam_k2, diff_norm_g, router_group_w,
           router_group_b, router_expert_w, router_expert_b, w_gate, w_up, w_down, final_norm_g):
    depth = w_in.shape[0]
    n_lat, d = x.shape[1], x.shape[2]
    n_ctx = ctx.shape[1]
    nct = n_ctx // ROW_TILE
    assert x.shape[0] == 1 and n_ctx % ROW_TILE == 0 and n_lat % ROW_TILE == 0

    stream = (ctx[0], x[0], 0)
    cc = jnp.zeros((8, d), F32).at[0].set(c[0]).at[1].set(c_ctx)
    mods_all = _mod_call(cc, w_mod, b_mod)[:, :2].reshape(depth, 2, N_MOD, d)
    tabs = _rope_tables(n_lat, n_ctx, HEAD_DIM) + _rope_tables(n_lat, n_ctx, HEAD_DIM // 2)

    h1 = _prenorm_call(stream[0], stream[1], norm1_g[0], mods_all[0], nct)
    out = None
    for l in range(depth):
        last = l == depth - 1
        mods = mods_all[l]
        tile_off = nct if last else 0
        tm_in = next(tm for tm in _IN_PROJ_ROW_TILES if (n_ctx + n_lat) % tm == 0)
        p = _matmul_call(h1, w_in, l, tm_in, _WEIGHT_COL_TILE)
        o = _hgrn_call(p, hg_lb_logits, l, nct)
        qa, ka, va, qd, kd, vd = _attn_prep_call(p, tabs, q_norm_g[l], k_norm_g[l])
        att = _flash_call(qa, ka, va, n_ctx, GQA_HEADS // GQA_KV_HEADS, 1, tile_off)
        lam_init = 0.8 - 0.6 * math.exp(-0.3 * l)
        dif = _flash_call(qd, kd, vd, n_ctx, 2, 2, tile_off,
                          extra=(lam_q1[l], lam_k1[l], lam_q2[l], lam_k2[l], diff_norm_g[l]),
                          lam_init=lam_init)
        rw = jnp.concatenate([router_group_w[l], router_expert_w[l],
                              jnp.zeros((d, LANES - N_GROUPS - N_EXPERTS), F32)], axis=1)
        rb = jnp.concatenate([router_group_b[l], router_expert_b[l],
                              jnp.zeros((LANES - N_GROUPS - N_EXPERTS,), F32)]).reshape(1, LANES)
        xn, h2, logits = _outproj_call(o, p, hg_norm_g[l], att, dif, w_out[l].astype(BF16), *stream,
                                       mods, norm2_g[l], rw, nct, tile_off)
        n_tok = xn.shape[0]
        route, counts = _router_call(logits, rb)
        dst, src, tile_expert, next_expert, tile_groups, n_tiles = _dispatch_plan(route, counts, n_tok)
        ys = _moe_call(h2, src, tile_expert, next_expert, tile_groups, n_tiles, w_gate, w_up, w_down, l)
        if last:
            (out,) = _combine_call(dst, xn, route, mods, final_norm_g, mods, ys, 0, True)
        else:
            xall, h1 = _combine_call(dst, xn, route, mods, norm1_g[l + 1], mods_all[l + 1], ys, nct, False)
            stream = (xall, xall, nct)
    return out.reshape(1, n_lat, d)
```

```python
import functools
import math

import numpy as np
import jax
import jax.numpy as jnp
from jax import lax
from jax.experimental import pallas as pl
from jax.experimental.pallas import tpu as pltpu

F32 = jnp.float32
BF16 = jnp.bfloat16
I32 = jnp.int32

HEAD_DIM = 128
LANES = 128
GRID_W = 64
ROPE_THETA = 10000.0
EPS = 1e-6
HG_HEADS = 4
HG_CHUNK = 64
GQA_HEADS = 8
GQA_KV_HEADS = 2
DIFF_HEADS = 4
N_GROUPS = 4
EXPERTS_PER_GROUP = 8
N_EXPERTS = N_GROUPS * EXPERTS_PER_GROUP
N_MOD = 6
ROW_TILE = 256
MOE_TILE = 256
_IN_PROJ_ROW_TILES = (1408, 768, ROW_TILE)
_WEIGHT_COL_TILE = 1024
_DMA_GROUP = 8
MIB = 1024 * 1024

_LOG2E = math.log2(math.e)
_ATT_KEY_BLOCK = 2816
_NT = (((1,), (1,)), ((), ()))
_TN = (((0,), (0,)), ((), ()))


def _params(semantics, vmem_mib):
    return pltpu.CompilerParams(dimension_semantics=semantics, vmem_limit_bytes=vmem_mib * MIB)


def _to_token_major(ref, x):
    n, d = x.shape
    k = d // LANES
    for s in range(k):
        ref[pl.ds(s, n, stride=k), :] = x[:, s * LANES:(s + 1) * LANES]


def _from_token_major(ref, n):
    k = ref.shape[0] // n
    return jnp.concatenate([ref[pl.ds(s, n, stride=k), :] for s in range(k)], axis=1)


def _split_bf16(x, parts):
    out = []
    for _ in range(parts - 1):
        p = x.astype(BF16)
        out.append(p)
        x = x - p.astype(F32)
    out.append(x.astype(BF16))
    return out


def _mod_kernel(a_ref, w_ref, b_ref, o_ref):
    a = a_ref[...]
    a = a * (1.0 / (1.0 + jnp.exp(-a)))
    hi, lo = _split_bf16(a, 2)
    w = w_ref[0].astype(BF16)
    o_ref[0] = (jnp.dot(hi, w, preferred_element_type=F32)
                + jnp.dot(lo, w, preferred_element_type=F32) + b_ref[0])


def _mod_call(cc, w_mod, b_mod):
    depth, d, n = w_mod.shape
    tn = _WEIGHT_COL_TILE
    return pl.pallas_call(
        _mod_kernel,
        out_shape=jax.ShapeDtypeStruct((depth, 8, n), F32),
        grid=(depth, n // tn),
        in_specs=[pl.BlockSpec((8, d), lambda l, j: (0, 0)),
                  pl.BlockSpec((1, d, tn), lambda l, j: (l, 0, j)),
                  pl.BlockSpec((1, 1, tn), lambda l, j: (l, 0, j))],
        out_specs=pl.BlockSpec((1, 8, tn), lambda l, j: (l, 0, j)),
        compiler_params=_params(("arbitrary", "arbitrary"), 40),
        name="mod_vectors",
    )(cc, w_mod, b_mod.reshape(depth, 1, n))


def _norm_mod(x, g, shift, scale):
    y = x * lax.rsqrt(jnp.mean(x * x, axis=-1, keepdims=True) + EPS) * g
    return y * (1.0 + scale) + shift


def _stream_specs(d, nct, tile_off=0, lat_off=0):
    return [pl.BlockSpec((ROW_TILE, d), lambda i: (jnp.minimum(i + tile_off, nct - 1), 0)),
            pl.BlockSpec((ROW_TILE, d), lambda i: (jnp.maximum(i + tile_off - nct, 0) + lat_off, 0))]


def _stream_tile(c_ref, x_ref, nct, tile_off=0):
    return jnp.where(pl.program_id(0) + tile_off < nct, c_ref[...], x_ref[...])


def _prenorm_kernel(c_ref, x_ref, g_ref, m_ref, o_ref, *, nct):
    m = m_ref[0]
    o_ref[...] = _norm_mod(_stream_tile(c_ref, x_ref, nct), g_ref[...], m[0:1], m[1:2]).astype(BF16)


def _mod_spec(d, nct, tile_off=0):
    return pl.BlockSpec((1, N_MOD, d), lambda i: (jnp.where(i + tile_off < nct, 1, 0), 0, 0))


def _prenorm_call(xc, xl, g, mods, nct):
    d = xl.shape[1]
    t = xc.shape[0] + xl.shape[0]
    return pl.pallas_call(
        functools.partial(_prenorm_kernel, nct=nct),
        out_shape=jax.ShapeDtypeStruct((t, d), BF16),
        grid=(t // ROW_TILE,),
        in_specs=_stream_specs(d, nct) + [pl.BlockSpec((1, d), lambda i: (0, 0)), _mod_spec(d, nct)],
        out_specs=pl.BlockSpec((ROW_TILE, d), lambda i: (i, 0)),
        compiler_params=_params(("arbitrary",), 24),
        name="prenorm",
    )(xc, xl, g.reshape(1, d), mods)


def _mm_kernel(a_ref, b_ref, o_ref, b_bf):
    @pl.when(pl.program_id(1) == 0)
    def _():
        b_bf[...] = b_ref[0].astype(BF16)

    o_ref[...] = jnp.dot(a_ref[...], b_bf[...], preferred_element_type=F32)


def _matmul_call(a, b, layer, tm, tn):
    m, k = a.shape
    n = b.shape[2]
    return pl.pallas_call(
        _mm_kernel,
        out_shape=jax.ShapeDtypeStruct((m, n), F32),
        grid=(n // tn, m // tm),
        in_specs=[pl.BlockSpec((tm, k), lambda j, i: (i, 0)),
                  pl.BlockSpec((1, k, tn), lambda j, i: (layer, 0, j))],
        out_specs=pl.BlockSpec((tm, tn), lambda j, i: (i, j)),
        scratch_shapes=[pltpu.VMEM((k, tn), BF16)],
        compiler_params=_params(("arbitrary", "arbitrary"), 56),
        name="in_proj",
    )(a, b)


_HG_LEVELS = (1, 2, 4, 8, 16, 32)
_HG_TOT_ROW = HG_CHUNK * (len(_HG_LEVELS) + 1)
_HG_W_ROWS = _HG_TOT_ROW + 16
_HG_HEADS_PER_STEP = 4


def _hgrn_consts():
    c = HG_CHUNK
    w = np.zeros((2, _HG_W_ROWS, c), np.float32)
    msk = np.zeros((2, len(_HG_LEVELS) + 1, c, c), np.float32)
    for d in range(2):
        u = np.arange(c) if d == 0 else c - 1 - np.arange(c)
        ut, us = u[:, None], u[None, :]
        w[d, :c] = us <= ut
        for li, lv in enumerate(_HG_LEVELS):
            blk = u // (2 * lv)
            qside = (u % (2 * lv)) >= lv
            bnd = (blk * 2 * lv + lv - 1)[:, None]
            wq = (us > bnd) & (us <= ut)
            wk = (us > ut) & (us <= bnd)
            w[d, c * (li + 1):c * (li + 2)] = np.where(qside[:, None], wq, -1.0 * wk)
            msk[d, li] = (blk[:, None] == blk[None, :]) & qside[:, None] & ~qside[None, :]
        msk[d, len(_HG_LEVELS)] = np.eye(c)
        w[d, _HG_TOT_ROW:] = 1.0
    return jnp.asarray(np.concatenate([w, w, w], axis=2), BF16), jnp.asarray(msk, F32)


def _hgrn_kernel(lbl_ref, q_ref, z_ref, v_ref, w_ref, msk_ref, o_ref, st_ref, *, layer, chunks, hps):
    c = HG_CHUNK
    hd = HEAD_DIM
    d = pl.program_id(0)
    j = pl.program_id(2)

    @pl.when(j == 0)
    def _():
        st_ref[...] = jnp.zeros_like(st_ref)

    lbl = lbl_ref[...]
    rows = [lbl[i:i + 1] for i in range(lbl.shape[0])]
    mx = functools.reduce(jnp.maximum, rows)
    ex = [jnp.exp(r - mx) for r in rows]
    tot = functools.reduce(lambda a, b: a + b, ex)
    lb = jnp.zeros_like(mx)
    for i in range(1, layer + 1):
        lb = lb + ex[i] / tot
    log_lb = jnp.log(lb)
    log_1m_lb = jnp.log1p(-lb)

    wmat = w_ref[0]
    nlev = len(_HG_LEVELS)

    def finish(h, hs, r0, amat, v_bf, qe, kd, decay):
        st = st_ref[h]
        o = (jnp.dot(amat, v_bf[:, hs], preferred_element_type=F32)
             + lax.dot_general(qe[:, hs], st.astype(BF16), _NT, preferred_element_type=F32))
        o_ref[0, pl.ds(r0, c), hs] = o
        st_ref[h] = st * decay[:, hs] + lax.dot_general(v_bf[:, hs], kd[:, hs], _TN,
                                                        preferred_element_type=F32)

    pending = None
    for ci in range(chunks):
        cc = ci + d * (chunks - 1 - 2 * ci)
        r0 = pl.multiple_of(cc * c, c)
        q = q_ref[pl.ds(r0, c), :]
        z = z_ref[pl.ds(r0, c), :]
        v_bf = v_ref[pl.ds(r0, c), :].astype(BF16)
        l1p = jnp.log(1.0 + jnp.exp(-jnp.abs(z)))
        ls_pos = jnp.minimum(z, 0.0) - l1p
        ls_neg = jnp.minimum(-z, 0.0) - l1p
        a2 = log_1m_lb + ls_pos
        logf = jnp.maximum(log_lb, a2) + jnp.log(1.0 + jnp.exp(-jnp.abs(log_lb - a2)))
        k = (1.0 - lb) * jnp.exp(ls_neg)

        parts = jnp.concatenate(_split_bf16(logf, 3), axis=0)
        sums = jnp.dot(wmat, parts, preferred_element_type=F32)
        b = sums[0:c]
        btot = sums[_HG_TOT_ROW:_HG_TOT_ROW + 1]

        qls, kls = [q.astype(BF16)], [k.astype(BF16)]
        for li in range(nlev):
            fl = jnp.exp(-jnp.abs(sums[c * (li + 1):c * (li + 2)]))
            qls.append((q * fl).astype(BF16))
            kls.append((k * fl).astype(BF16))
        qe = (q * jnp.exp(b)).astype(BF16)
        kd = (k * jnp.exp(btot - b)).astype(BF16)
        decay = jnp.exp(btot)

        for h in range(hps):
            hs = slice(h * hd, (h + 1) * hd)
            amat = lax.dot_general(qls[0][:, hs], kls[0][:, hs], _NT,
                                   preferred_element_type=F32) * msk_ref[0, nlev]
            for li in range(nlev):
                amat = amat + lax.dot_general(qls[li + 1][:, hs], kls[li + 1][:, hs], _NT,
                                              preferred_element_type=F32) * msk_ref[0, li]
            if pending is not None:
                finish(*pending)
            pending = (h, hs, r0, amat.astype(BF16), v_bf, qe, kd, decay)
    finish(*pending)


def _hgrn_call(p, lb_logits, layer, nct):
    t = p.shape[0]
    nblk = t // ROW_TILE
    chunks = ROW_TILE // HG_CHUNK
    hps = _HG_HEADS_PER_STEP
    hw = hps * HEAD_DIM
    ng = HG_HEADS // hps
    wmat, msk = _hgrn_consts()

    def blk(d, j):
        back = jnp.where(j < nct, nct - 1 - j, nblk - 1 - (j - nct))
        return jnp.where(d == 0, j, back)

    return pl.pallas_call(
        functools.partial(_hgrn_kernel, layer=layer, chunks=chunks, hps=hps),
        out_shape=jax.ShapeDtypeStruct((2, t, HG_HEADS * HEAD_DIM), F32),
        grid=(2, ng, nblk),
        in_specs=[pl.BlockSpec((lb_logits.shape[0], hw), lambda d, h, j: (0, h)),
                  pl.BlockSpec((ROW_TILE, hw), lambda d, h, j: (blk(d, j), d * ng + h)),
                  pl.BlockSpec((ROW_TILE, hw), lambda d, h, j: (blk(d, j), (2 + d) * ng + h)),
                  pl.BlockSpec((ROW_TILE, hw), lambda d, h, j: (blk(d, j), 4 * ng + h)),
                  pl.BlockSpec((1, _HG_W_ROWS, 3 * HG_CHUNK), lambda d, h, j: (d, 0, 0)),
                  pl.BlockSpec((1, len(_HG_LEVELS) + 1, HG_CHUNK, HG_CHUNK), lambda d, h, j: (d, 0, 0, 0))],
        out_specs=pl.BlockSpec((1, ROW_TILE, hw), lambda d, h, j: (d, blk(d, j), h)),
        scratch_shapes=[pltpu.VMEM((hps, HEAD_DIM, HEAD_DIM), F32)],
        compiler_params=_params(("arbitrary", "arbitrary", "arbitrary"), 24),
        name="hgrn_scan",
    )(lb_logits, p, p, p, wmat, msk)


def _rope_tables(n_lat, n_ctx, dim):
    rows = n_lat // GRID_W
    row = jnp.repeat(jnp.arange(rows, dtype=F32), GRID_W)
    col = jnp.tile(jnp.arange(GRID_W, dtype=F32), rows)
    axis_dim = dim // 2
    inv_freq = ROPE_THETA ** (-jnp.arange(0, axis_dim, 2, dtype=F32) / axis_dim)
    ang = jnp.concatenate([row[:, None] * inv_freq, col[:, None] * inv_freq], axis=-1)
    cos = jnp.repeat(jnp.cos(ang), 2, axis=1)
    sin = jnp.repeat(jnp.sin(ang), 2, axis=1) * jnp.tile(jnp.asarray([-1.0, 1.0], F32), dim // 2)
    reps = LANES // dim
    cos = jnp.concatenate([jnp.ones((n_ctx, dim), F32), cos], axis=0)
    sin = jnp.concatenate([jnp.zeros((n_ctx, dim), F32), sin], axis=0)
    return jnp.tile(cos, (1, reps)), jnp.tile(sin, (1, reps))


def _rope(x, cos, sin):
    lane = lax.broadcasted_iota(I32, x.shape, 1)
    swapped = jnp.where((lane & 1) == 0, pltpu.roll(x, LANES - 1, 1), pltpu.roll(x, 1, 1))
    return x * cos + swapped * sin


def _head_norm(x, g):
    return x * lax.rsqrt(jnp.mean(x * x, axis=-1, keepdims=True) + EPS) * g


def _attn_prep_kernel(pq_ref, pkv_ref, pd_ref, ca_ref, sa_ref, cd_ref, sd_ref, qn_ref, kn_ref,
                      qa_ref, ka_ref, va_ref, qd_ref, kd_ref, vd_ref):
    hd = HEAD_DIM
    ca, sa, cd, sd = ca_ref[...], sa_ref[...], cd_ref[...], sd_ref[...]
    qn, kn = qn_ref[...], kn_ref[...]
    for h in range(GQA_HEADS):
        xq = _rope(_head_norm(pq_ref[:, h * hd:(h + 1) * hd], qn), ca, sa)
        qa_ref[:, h * hd:(h + 1) * hd] = (xq * (hd ** -0.5 * _LOG2E)).astype(BF16)
    for h in range(GQA_KV_HEADS):
        xk = _rope(_head_norm(pkv_ref[:, h * hd:(h + 1) * hd], kn), ca, sa)
        ka_ref[:, h * hd:(h + 1) * hd] = xk.astype(BF16)
    ones_col = jnp.where(lax.broadcasted_iota(I32, (pq_ref.shape[0], hd), 1) == 0, 1.0, 0.0).astype(BF16)
    for h in range(GQA_KV_HEADS):
        va_ref[:, (2 * h) * hd:(2 * h + 1) * hd] = pkv_ref[:, (GQA_KV_HEADS + h) * hd:
                                                            (GQA_KV_HEADS + h + 1) * hd].astype(BF16)
        va_ref[:, (2 * h + 1) * hd:(2 * h + 2) * hd] = ones_col
    first = lax.broadcasted_iota(I32, (pq_ref.shape[0], hd), 1) < hd // 2
    dw = DIFF_HEADS * hd
    for h in range(DIFF_HEADS):
        xq = _rope(pd_ref[:, h * hd:(h + 1) * hd], cd, sd) * ((hd // 2) ** -0.5 * _LOG2E)
        qd_ref[:, (2 * h) * hd:(2 * h + 1) * hd] = jnp.where(first, xq, 0.0).astype(BF16)
        qd_ref[:, (2 * h + 1) * hd:(2 * h + 2) * hd] = jnp.where(first, 0.0, xq).astype(BF16)
        xk = _rope(pd_ref[:, dw + h * hd:dw + (h + 1) * hd], cd, sd)
        kd_ref[:, h * hd:(h + 1) * hd] = xk.astype(BF16)
    for h in range(DIFF_HEADS):
        vd_ref[:, (2 * h) * hd:(2 * h + 1) * hd] = pd_ref[:, 2 * dw + h * hd:2 * dw + (h + 1) * hd].astype(BF16)
        vd_ref[:, (2 * h + 1) * hd:(2 * h + 2) * hd] = ones_col


def _attn_prep_call(p, tabs, qn, kn):
    t = p.shape[0]
    hd = HEAD_DIM
    gq, gkv, dw = GQA_HEADS * hd, 2 * GQA_KV_HEADS * hd, DIFF_HEADS * hd
    q_off = 6 * HG_HEADS * hd
    assert q_off % gq == 0 and (q_off + gq) % gkv == 0 and (q_off + gq + gkv) % (3 * dw) == 0
    tp = next(n for n in (768, ROW_TILE) if t % n == 0)
    row = lambda w: pl.BlockSpec((tp, w), lambda i: (i, 0))
    vec = pl.BlockSpec((1, hd), lambda i: (0, 0))
    return pl.pallas_call(
        _attn_prep_kernel,
        out_shape=[jax.ShapeDtypeStruct((t, gq), BF16),
                   jax.ShapeDtypeStruct((t, gkv // 2), BF16),
                   jax.ShapeDtypeStruct((t, gkv), BF16),
                   jax.ShapeDtypeStruct((t, 2 * dw), BF16),
                   jax.ShapeDtypeStruct((t, dw), BF16),
                   jax.ShapeDtypeStruct((t, 2 * dw), BF16)],
        grid=(t // tp,),
        in_specs=[pl.BlockSpec((tp, gq), lambda i: (i, q_off // gq)),
                  pl.BlockSpec((tp, gkv), lambda i: (i, (q_off + gq) // gkv)),
                  pl.BlockSpec((tp, 3 * dw), lambda i: (i, (q_off + gq + gkv) // (3 * dw))),
                  row(hd), row(hd), row(hd), row(hd), vec, vec],
        out_specs=[row(gq), row(gkv // 2), row(gkv), row(2 * dw), row(dw), row(2 * dw)],
        compiler_params=_params(("arbitrary",), 48),
        name="attn_prep",
    )(p, p, p, *tabs, qn.reshape(1, hd), kn.reshape(1, hd))


def _flash_kernel(*refs, g, nh, n_ctx, n_lat, tkl, nct, tile_off, diff, lam_init):
    if diff:
        q_ref, k_ref, v_ref, lq1, lk1, lq2, lk2, dn_ref, o_ref = refs[:9]
    else:
        q_ref, k_ref, v_ref, o_ref = refs[:4]
    nc = nh * g
    m_sc, acc_sc, p_sc = (refs[len(refs) - (3 - n) * nc:len(refs) - (2 - n) * nc] for n in range(3))
    hd = HEAD_DIM
    i = pl.program_id(1) + tile_off
    nblk = (n_ctx + n_lat) // tkl
    chains = [(c, c // g) for c in range(nh * g)]

    def head(ref, rows, h, width=1):
        return ref[rows, h * width * hd:(h + 1) * width * hd]

    def vhead(rows, h):
        return head(v_ref, rows, h, 2)

    def scores(c, kh, rows):
        return lax.dot_general(head(q_ref, slice(None), c), head(k_ref, rows, kh), _NT,
                               preferred_element_type=F32)

    def emit(outputs):
        norm = lambda a: a[:, :hd] / a[:, hd:hd + 1]
        if diff:
            lam = (jnp.exp(jnp.sum(lq1[...] * lk1[...], axis=1, keepdims=True))
                   - jnp.exp(jnp.sum(lq2[...] * lk2[...], axis=1, keepdims=True)) + lam_init)
            for kh in range(nh):
                dd = norm(outputs[2 * kh]) - lam * norm(outputs[2 * kh + 1])
                o_ref[:, kh * hd:(kh + 1) * hd] = (_head_norm(dd, dn_ref[...])
                                                   * (1.0 - lam_init)).astype(BF16)
        else:
            for c, _ in chains:
                o_ref[:, c * hd:(c + 1) * hd] = norm(outputs[c]).astype(BF16)

    @pl.when(i < nct)
    def _():
        ctx_rows = slice(0, n_ctx)
        outs = []
        for c, kh in chains:
            s = scores(c, kh, ctx_rows)
            p = jnp.exp2(s - jnp.max(s, axis=1, keepdims=True))
            outs.append(jnp.dot(p.astype(BF16), vhead(ctx_rows, kh), preferred_element_type=F32))
        emit(outs)

    @pl.when(i >= nct)
    def _():
        for jb in range(nblk):
            rows = pl.ds(jb * tkl, tkl)
            slot = jb % 2
            alphas = []
            for c, kh in chains:
                s = scores(c, kh, rows)
                m_cur = jnp.max(s, axis=1, keepdims=True)
                if jb == 0:
                    m_new = jnp.broadcast_to(m_cur, m_sc[c].shape)
                else:
                    m_prev = m_sc[c][...]
                    m_new = jnp.maximum(m_prev, m_cur)
                    alphas.append(jnp.tile(jnp.exp2(m_prev - m_new), (1, 2)))
                p_sc[c][slot] = jnp.exp2(s - jnp.tile(m_new, (1, tkl // LANES))).astype(BF16)
                m_sc[c][...] = m_new
            if jb == 0:
                continue
            prev = pl.ds((jb - 1) * tkl, tkl)
            for c, kh in chains:
                pv = jnp.dot(p_sc[c][1 - slot], vhead(prev, kh), preferred_element_type=F32)
                acc_sc[c][...] = alphas[c] * (pv if jb == 1 else acc_sc[c][...] + pv)

        last = pl.ds((nblk - 1) * tkl, tkl)
        outs = []
        for c, kh in chains:
            pv = jnp.dot(p_sc[c][(nblk - 1) % 2], vhead(last, kh), preferred_element_type=F32)
            outs.append(pv if nblk == 1 else acc_sc[c][...] + pv)
        emit(outs)


def _flash_call(q, k, v, n_ctx, g, nh, tile_off, extra=None, lam_init=0.0):
    t = k.shape[0]
    hd = HEAD_DIM
    n_kv = k.shape[1] // (nh * hd)
    n_lat = t - n_ctx
    nct = n_ctx // ROW_TILE
    nblk = next(n for n in range(1, t // LANES + 1)
                if t % (n * LANES) == 0 and t // n <= _ATT_KEY_BLOCK)
    tkl = t // nblk
    diff = extra is not None
    ow = nh * hd if diff else nh * g * hd
    resident = pl.Buffered(1)
    in_specs = [pl.BlockSpec((ROW_TILE, nh * g * hd), lambda kv, i: (i + tile_off, kv)),
                pl.BlockSpec((t, nh * hd), lambda kv, i: (0, kv), pipeline_mode=resident),
                pl.BlockSpec((t, 2 * nh * hd), lambda kv, i: (0, kv), pipeline_mode=resident)]
    args = [q, k, v]
    if diff:
        for a in extra:
            a = a.reshape(1, -1)
            in_specs.append(pl.BlockSpec(a.shape, lambda kv, i: (0, 0)))
            args.append(a)
    return pl.pallas_call(
        functools.partial(_flash_kernel, g=g, nh=nh, n_ctx=n_ctx, n_lat=n_lat, tkl=tkl, nct=nct,
                          tile_off=tile_off, diff=diff, lam_init=lam_init),
        out_shape=jax.ShapeDtypeStruct((t - tile_off * ROW_TILE, n_kv * ow), BF16),
        grid=(n_kv, t // ROW_TILE - tile_off),
        in_specs=in_specs,
        out_specs=pl.BlockSpec((ROW_TILE, ow), lambda kv, i: (i, kv)),
        scratch_shapes=([pltpu.VMEM((ROW_TILE, LANES), F32)] * (nh * g)
                        + [pltpu.VMEM((ROW_TILE, 2 * hd), F32)] * (nh * g)
                        + [pltpu.VMEM((2, ROW_TILE, tkl), BF16)] * (nh * g)),
        compiler_params=_params(("arbitrary", "arbitrary"), 48),
        name="diff_attention" if diff else "gqa_attention",
    )(*args)


def _outproj_kernel(of_ref, ob_ref, gate_ref, ng_ref, att_ref, dif_ref, wo_ref, xc_ref, xl_ref, m_ref,
                    n2_ref, rw_ref, xo_ref, h2_ref, lg_ref, *, nct, tile_off):
    hd = HEAD_DIM
    hgw = HG_HEADS * hd
    ng = ng_ref[...]
    gate = gate_ref[...]
    silu_gate = gate * (1.0 / (1.0 + jnp.exp(-gate)))
    acc = jnp.dot(att_ref[...], wo_ref[hgw:hgw + att_ref.shape[1], :], preferred_element_type=F32)
    acc = acc + jnp.dot(dif_ref[...], wo_ref[hgw + att_ref.shape[1]:, :], preferred_element_type=F32)
    hg = []
    for h in range(HG_HEADS):
        o = of_ref[0, :, h * hd:(h + 1) * hd] + ob_ref[0, :, h * hd:(h + 1) * hd]
        hg.append((_head_norm(o, ng) * silu_gate[:, h * hd:(h + 1) * hd]).astype(BF16))
    acc = acc + jnp.dot(jnp.concatenate(hg, axis=1), wo_ref[0:hgw, :], preferred_element_type=F32)
    m = m_ref[0]
    xn = _stream_tile(xc_ref, xl_ref, nct, tile_off) + m[2:3] * acc
    xo_ref[...] = xn
    h2 = _norm_mod(xn, n2_ref[...], m[3:4], m[4:5])
    _to_token_major(h2_ref, h2)
    hs = _split_bf16(h2, 2)
    ws = _split_bf16(rw_ref[...], 2)
    both = jnp.dot(hs[0], jnp.concatenate(ws, axis=1), preferred_element_type=F32)
    lg_ref[...] = (jnp.dot(hs[1], ws[0], preferred_element_type=F32) + both[:, LANES:]) + both[:, :LANES]


def _outproj_call(o, p, ng, att, dif, wo, xc, xl, lat_off, mods, n2, rw, nct, tile_off):
    d = xl.shape[1]
    hd = HEAD_DIM
    hgw = HG_HEADS * hd
    n_rows = att.shape[0]
    row = lambda w: pl.BlockSpec((ROW_TILE, w), lambda i: (i, 0))
    full = lambda a: pl.BlockSpec(a.shape, lambda i: (0,) * a.ndim)
    ng, n2 = ng.reshape(1, hd), n2.reshape(1, d)
    return pl.pallas_call(
        functools.partial(_outproj_kernel, nct=nct, tile_off=tile_off),
        out_shape=[jax.ShapeDtypeStruct((n_rows, d), F32),
                   jax.ShapeDtypeStruct((n_rows * (d // LANES), LANES), F32),
                   jax.ShapeDtypeStruct((n_rows, LANES), F32)],
        grid=(n_rows // ROW_TILE,),
        in_specs=[pl.BlockSpec((1, ROW_TILE, hgw), lambda i: (0, i + tile_off, 0)),
                  pl.BlockSpec((1, ROW_TILE, hgw), lambda i: (1, i + tile_off, 0)),
                  pl.BlockSpec((ROW_TILE, hgw), lambda i: (i + tile_off, 5)),
                  full(ng), row(att.shape[1]), row(dif.shape[1]), full(wo)]
                 + _stream_specs(d, nct, tile_off, lat_off)
                 + [_mod_spec(d, nct, tile_off), full(n2), full(rw)],
        out_specs=[row(d), pl.BlockSpec((ROW_TILE * (d // LANES), LANES), lambda i: (i, 0)), row(LANES)],
        compiler_params=_params(("arbitrary",), 56),
        name="out_proj",
    )(o, o, p, ng, att, dif, wo, xc, xl, mods, n2, rw)


def _router_kernel(lg_ref, bias_ref, r_ref, cnt_ref, run_ref):
    i = pl.program_id(0)
    tm = lg_ref.shape[0]

    @pl.when(i == 0)
    def _():
        run_ref[...] = jnp.zeros_like(run_ref)

    lane = lax.broadcasted_iota(I32, (tm, LANES), 1).astype(F32)
    lg = lg_ref[...] + bias_ref[...]
    ninf = -jnp.inf

    def first_max(vals):
        mx = jnp.max(vals, axis=1, keepdims=True)
        idx = jnp.min(jnp.where(vals == mx, lane, float(LANES)), axis=1, keepdims=True)
        return mx, idx

    gl = jnp.where(lane < N_GROUPS, lg, ninf)
    gmax, gidx = first_max(gl)
    g_top = 1.0 / jnp.sum(jnp.exp(gl - gmax), axis=1, keepdims=True)
    lo = N_GROUPS + EXPERTS_PER_GROUP * gidx
    el = jnp.where((lane >= lo) & (lane < lo + EXPERTS_PER_GROUP), lg, ninf)
    m1, e1 = first_max(el)
    m2, e2 = first_max(jnp.where(lane == e1, ninf, el))
    r = jnp.exp(m2 - m1)
    w1 = g_top / (1.0 + r)
    w2 = g_top * r / (1.0 + r)

    hit = ((lane == e1) | (lane == e2)).astype(BF16)
    ti = lax.broadcasted_iota(I32, (tm, tm), 0)
    si = lax.broadcasted_iota(I32, (tm, tm), 1)
    before = (si < ti).astype(BF16)
    pos = jnp.dot(before, hit, preferred_element_type=F32) + run_ref[0:1, :]
    p1 = jnp.sum(jnp.where(lane == e1, pos, 0.0), axis=1, keepdims=True)
    p2 = jnp.sum(jnp.where(lane == e2, pos, 0.0), axis=1, keepdims=True)
    total = run_ref[0:1, :] + jnp.sum(hit.astype(F32), axis=0, keepdims=True)
    run_ref[...] = jnp.broadcast_to(total, run_ref.shape)
    cnt_ref[...] = jnp.broadcast_to(total, cnt_ref.shape)

    fields = (e1 - N_GROUPS, e2 - N_GROUPS, w1, w2, p1, p2)
    out = jnp.zeros((tm, LANES), F32)
    for n, f in enumerate(fields):
        out = jnp.where(lane == n, f, out)
    r_ref[...] = out


def _router_call(logits, bias):
    n = logits.shape[0]
    return pl.pallas_call(
        _router_kernel,
        out_shape=[jax.ShapeDtypeStruct((n, LANES), F32), jax.ShapeDtypeStruct((8, LANES), F32)],
        grid=(n // ROW_TILE,),
        in_specs=[pl.BlockSpec((ROW_TILE, LANES), lambda i: (i, 0)),
                  pl.BlockSpec((1, LANES), lambda i: (0, 0))],
        out_specs=[pl.BlockSpec((ROW_TILE, LANES), lambda i: (i, 0)),
                   pl.BlockSpec((8, LANES), lambda i: (0, 0))],
        scratch_shapes=[pltpu.VMEM((8, LANES), F32)],
        compiler_params=_params(("arbitrary",), 16),
        name="router",
    )(logits, bias)


def _moe_kernel(src_ref, te_ref, nx_ref, ng_ref, nt_ref, h_hbm, wg_hbm, wu_hbm, wd_hbm, y_ref,
                xbuf, sem, wg_st, wu_st, wd_st, wsem, wg_bf, wu_bf, wd_bf, *, layer):
    i = pl.program_id(0)
    n_tiles = nt_ref[0]
    tm = MOE_TILE
    k = xbuf.shape[1] // tm
    grp_rows = _DMA_GROUP * k

    def weight_copies(e):
        pairs = ((wg_hbm, wg_st), (wu_hbm, wu_st), (wd_hbm, wd_st))
        return [pltpu.make_async_copy(w.at[layer, e], st, wsem.at[n]) for n, (w, st) in enumerate(pairs)]

    def start_tile(tile, slot):
        def body(grp, carry):
            for u in range(_DMA_GROUP):
                r = grp * _DMA_GROUP + u
                src_row = pl.multiple_of(src_ref[tile * tm + r] * k, k)
                pltpu.make_async_copy(h_hbm.at[pl.ds(src_row, k), :],
                                      xbuf.at[slot, pl.ds(pl.multiple_of(r * k, k), k), :],
                                      sem.at[slot]).start(priority=u % 2)
            return carry
        lax.fori_loop(0, ng_ref[tile], body, 0)

    def wait_tile(tile, slot):
        def body(grp, carry):
            pltpu.make_async_copy(h_hbm.at[pl.ds(0, grp_rows), :], xbuf.at[slot, pl.ds(0, grp_rows), :],
                                  sem.at[slot]).wait()
            return carry
        lax.fori_loop(0, ng_ref[tile], body, 0)

    @pl.when(i == 0)
    def _():
        xbuf[...] = jnp.zeros_like(xbuf)
        start_tile(0, 0)
        for cp in weight_copies(te_ref[0]):
            cp.start()

    @pl.when(i < n_tiles)
    def _():
        slot = i % 2

        @pl.when(i + 1 < n_tiles)
        def _():
            start_tile(i + 1, 1 - slot)

        changed = jnp.logical_or(i == 0, te_ref[i] != te_ref[jnp.maximum(i - 1, 0)])

        @pl.when(changed)
        def _():
            for cp in weight_copies(te_ref[i]):
                cp.wait()
            wg_bf[...] = wg_st[...].astype(BF16)
            wu_bf[...] = wu_st[...].astype(BF16)
            wd_bf[...] = wd_st[...].astype(BF16)

            @pl.when(nx_ref[i] >= 0)
            def _():
                for cp in weight_copies(nx_ref[i]):
                    cp.start()

        wait_tile(i, slot)
        xb = _from_token_major(xbuf.at[slot], tm).astype(BF16)
        gt = jnp.dot(xb, wg_bf[...], preferred_element_type=F32)
        up = jnp.dot(xb, wu_bf[...], preferred_element_type=F32)
        act = (gt * (1.0 / (1.0 + jnp.exp(-gt))) * up).astype(BF16)
        y_ref[...] = jnp.dot(act, wd_bf[...], preferred_element_type=F32)

    @pl.when(i >= n_tiles)
    def _():
        y_ref[...] = jnp.zeros_like(y_ref)


def _moe_call(h2, src, tile_expert, next_expert, tile_groups, n_tiles, wg, wu, wd, layer):
    d, ff = wg.shape[2], wg.shape[3]
    k = d // LANES
    max_tiles = tile_expert.shape[0]
    tm = MOE_TILE
    hbm = pl.BlockSpec(memory_space=pl.ANY)
    grid_spec = pltpu.PrefetchScalarGridSpec(
        num_scalar_prefetch=5,
        grid=(max_tiles,),
        in_specs=[hbm, hbm, hbm, hbm],
        out_specs=pl.BlockSpec((tm, d), lambda i, src, te, nx, ng, nt: (i, 0)),
        scratch_shapes=[pltpu.VMEM((2, tm * k, LANES), F32), pltpu.SemaphoreType.DMA((2,)),
                        pltpu.VMEM((d, ff), F32), pltpu.VMEM((d, ff), F32), pltpu.VMEM((ff, d), F32),
                        pltpu.SemaphoreType.DMA((3,)),
                        pltpu.VMEM((d, ff), BF16), pltpu.VMEM((d, ff), BF16), pltpu.VMEM((ff, d), BF16)])
    return pl.pallas_call(
        functools.partial(_moe_kernel, layer=layer),
        out_shape=jax.ShapeDtypeStruct((max_tiles * tm, d), F32),
        grid_spec=grid_spec,
        compiler_params=_params(("arbitrary",), 40),
        name="moe_experts",
    )(src, tile_expert, next_expert, tile_groups, n_tiles, h2, wg, wu, wd)


def _combine_kernel(dst_ref, x_ref, rt_ref, m_ref, g_ref, mn_ref, y_hbm, *rest, last, n_out):
    outs, (ybuf, sem) = rest[:n_out], rest[n_out:]
    i = pl.program_id(0)
    tm = x_ref.shape[0]

    def start_tile(tile, slot):
        def body(grp, carry):
            for u in range(_DMA_GROUP):
                r = grp * _DMA_GROUP + u
                for j in range(2):
                    pltpu.make_async_copy(y_hbm.at[pl.ds(dst_ref[(tile * tm + r) * 2 + j], 1), :],
                                          ybuf.at[slot, j, pl.ds(r, 1), :], sem.at[slot]).start(priority=j)
            return carry
        lax.fori_loop(0, tm // _DMA_GROUP, body, 0)

    @pl.when(i == 0)
    def _():
        start_tile(0, 0)

    slot = i % 2

    @pl.when(i + 1 < pl.num_programs(0))
    def _():
        start_tile(i + 1, 1 - slot)

    for j in range(2):
        pltpu.make_async_copy(y_hbm.at[pl.ds(0, tm), :], ybuf.at[slot, j], sem.at[slot]).wait()
    rt = rt_ref[...]
    y = rt[:, 2:3] * ybuf[slot, 0] + rt[:, 3:4] * ybuf[slot, 1]
    xn = x_ref[...] + m_ref[0][5:6] * y
    if last:
        outs[0][...] = xn * lax.rsqrt(jnp.mean(xn * xn, axis=-1, keepdims=True) + EPS) * g_ref[...]
    else:
        outs[0][...] = xn
        mn = mn_ref[0]
        outs[1][...] = _norm_mod(xn, g_ref[...], mn[0:1], mn[1:2]).astype(BF16)


def _combine_call(dst, x, route, mods, g, mods_next, ys, nct, last):
    n, d = x.shape
    tm = ROW_TILE
    mspec = pl.BlockSpec((1, N_MOD, d), lambda i, dst: (jnp.where(i < nct, 1, 0), 0, 0))
    out_shape = [jax.ShapeDtypeStruct((n, d), F32)]
    if not last:
        out_shape.append(jax.ShapeDtypeStruct((n, d), BF16))
    row = lambda w: pl.BlockSpec((tm, w), lambda i, dst: (i, 0))
    grid_spec = pltpu.PrefetchScalarGridSpec(
        num_scalar_prefetch=1,
        grid=(n // tm,),
        in_specs=[row(d), row(LANES), mspec, pl.BlockSpec((1, d), lambda i, dst: (0, 0)), mspec,
                  pl.BlockSpec(memory_space=pl.ANY)],
        out_specs=[row(d)] * len(out_shape),
        scratch_shapes=[pltpu.VMEM((2, 2, tm, d), F32), pltpu.SemaphoreType.DMA((2,))])
    return pl.pallas_call(
        functools.partial(_combine_kernel, last=last, n_out=len(out_shape)),
        out_shape=out_shape,
        grid_spec=grid_spec,
        compiler_params=_params(("arbitrary",), 40),
        name="moe_combine",
    )(dst, x, route, mods, g.reshape(1, d), mods_next, ys)


def _dispatch_plan(route, counts, n_tokens):
    tm = MOE_TILE
    max_tiles = (2 * n_tokens) // tm + N_EXPERTS
    cnt = counts[0, N_GROUPS:N_GROUPS + N_EXPERTS].astype(I32)
    tiles_per = (cnt + tm - 1) // tm
    tile_end = jnp.cumsum(tiles_per)
    offs = (tile_end - tiles_per) * tm
    eid = route[:, 0:2].astype(I32)
    e_ids = jnp.arange(N_EXPERTS, dtype=I32)
    offs_of = jnp.sum(jnp.where(eid[..., None] == e_ids, offs, 0), axis=-1)
    dst = offs_of + route[:, 4:6].astype(I32)
    tok = jnp.broadcast_to(jnp.arange(n_tokens, dtype=I32)[:, None], (n_tokens, 2))
    src = jnp.zeros((max_tiles * tm,), I32).at[dst.reshape(-1)].set(tok.reshape(-1))
    n_tiles = tile_end[-1:]
    tile_ids = jnp.minimum(jnp.arange(max_tiles, dtype=I32), n_tiles[0] - 1)
    tile_expert = jnp.sum((tile_end[None, :] <= tile_ids[:, None]).astype(I32), axis=1)
    later = (e_ids[None, :] > e_ids[:, None]) & (tiles_per[None, :] > 0)
    nxt = jnp.min(jnp.where(later, e_ids[None, :], N_EXPERTS), axis=1)
    next_expert = jnp.where(nxt < N_EXPERTS, nxt, -1)[tile_expert]
    within = tile_ids - (tile_end - tiles_per)[tile_expert]
    rows = jnp.clip(cnt[tile_expert] - within * tm, 0, tm)
    tile_groups = (rows + _DMA_GROUP - 1) // _DMA_GROUP
    return dst.reshape(-1), src, tile_expert, next_expert, tile_groups.astype(I32), n_tiles.astype(I32)


def kernel(x, c, ctx, c_ctx, w_mod, b_mod, norm1_g, norm2_g, w_in, w_out, hg_lb_logits, hg_norm_g,
           q_norm_g, k_norm_g, lam_q1, lam_k1, lam_q2, lam_k2, diff_norm_g, router_group_w,
           router_group_b, router_expert_w, router_expert_b, w_gate, w_up, w_down, final_norm_g):
    depth = w_in.shape[0]
    n_lat, d = x.shape[1], x.shape[2]
    n_ctx = ctx.shape[1]
    nct = n_ctx // ROW_TILE
    assert x.shape[0] == 1 and n_ctx % ROW_TILE == 0 and n_lat % ROW_TILE == 0

    stream = (ctx[0], x[0], 0)
    cc = jnp.zeros((8, d), F32).at[0].set(c[0]).at[1].set(c_ctx)
    mods_all = _mod_call(cc, w_mod, b_mod)[:, :2].reshape(depth, 2, N_MOD, d)
    tabs = _rope_tables(n_lat, n_ctx, HEAD_DIM) + _rope_tables(n_lat, n_ctx, HEAD_DIM // 2)

    h1 = _prenorm_call(stream[0], stream[1], norm1_g[0], mods_all[0], nct)
    out = None
    for l in range(depth):
        last = l == depth - 1
        mods = mods_all[l]
        tile_off = nct if last else 0
        tm_in = next(tm for tm in _IN_PROJ_ROW_TILES if (n_ctx + n_lat) % tm == 0)
        p = _matmul_call(h1, w_in, l, tm_in, _WEIGHT_COL_TILE)
        o = _hgrn_call(p, hg_lb_logits, l, nct)
        qa, ka, va, qd, kd, vd = _attn_prep_call(p, tabs, q_norm_g[l], k_norm_g[l])
        att = _flash_call(qa, ka, va, n_ctx, GQA_HEADS // GQA_KV_HEADS, 1, tile_off)
        lam_init = 0.8 - 0.6 * math.exp(-0.3 * l)
        dif = _flash_call(qd, kd, vd, n_ctx, 2, 2, tile_off,
                          extra=(lam_q1[l], lam_k1[l], lam_q2[l], lam_k2[l], diff_norm_g[l]),
                          lam_init=lam_init)
        rw = jnp.concatenate([router_group_w[l], router_expert_w[l],
                              jnp.zeros((d, LANES - N_GROUPS - N_EXPERTS), F32)], axis=1)
        rb = jnp.concatenate([router_group_b[l], router_expert_b[l],
                              jnp.zeros((LANES - N_GROUPS - N_EXPERTS,), F32)]).reshape(1, LANES)
        xn, h2, logits = _outproj_call(o, p, hg_norm_g[l], att, dif, w_out[l].astype(BF16), *stream,
                                       mods, norm2_g[l], rw, nct, tile_off)
        n_tok = xn.shape[0]
        route, counts = _router_call(logits, rb)
        dst, src, tile_expert, next_expert, tile_groups, n_tiles = _dispatch_plan(route, counts, n_tok)
        ys = _moe_call(h2, src, tile_expert, next_expert, tile_groups, n_tiles, w_gate, w_up, w_down, l)
        if last:
            (out,) = _combine_call(dst, xn, route, mods, final_norm_g, mods, ys, 0, True)
        else:
            xall, h1 = _combine_call(dst, xn, route, mods, norm1_g[l + 1], mods_all[l + 1], ys, nct, False)
            stream = (xall, xall, nct)
    return out.reshape(1, n_lat, d)
```
